```python
import math
import jax, jax.numpy as jnp
from jax import lax
import numpy as np

D_MODEL = 1024
BATCH = 16
SEQ = 2048
DEPTH = 1
DEC_BATCH = 128
DEC_SEQ = 8
PAST_LEN = 16384
PAGE_SIZE = 128

MIX_W = D_MODEL
N_HEADS_ATT = 8
N_KV_HEADS = 2
HEAD_DIM_ATT = 64
GQA_GROUP = N_HEADS_ATT // N_KV_HEADS
ATT_W = N_HEADS_ATT * HEAD_DIM_ATT
WINDOW = 128
ATT_BLOCK = WINDOW
NUM_BUCKETS = 32
MAX_DISTANCE = 128
N_HEADS_MLSTM = 4
DV_MLSTM = (MIX_W - ATT_W) // N_HEADS_MLSTM
DK_MLSTM = DV_MLSTM // 2
MLSTM_W = N_HEADS_MLSTM * DV_MLSTM
MLSTM_CHUNK = 64
D_FF = 4 * D_MODEL
EPS = 1e-6
SPLIT_SIZES = (ATT_W, N_KV_HEADS * HEAD_DIM_ATT, N_KV_HEADS * HEAD_DIM_ATT,
               N_HEADS_MLSTM * DK_MLSTM, N_HEADS_MLSTM * DK_MLSTM, MLSTM_W, MLSTM_W,
               2 * N_HEADS_MLSTM)
IN_COLS = sum(SPLIT_SIZES)

kernel_name = 'hymba_swa_sink_mlstm_decode_step'


def rmsnorm(x, g):
    xf = x.astype(jnp.float32)
    y = xf * lax.rsqrt(jnp.mean(xf * xf, axis=-1, keepdims=True) + EPS)
    return (y * g.astype(jnp.float32)).astype(x.dtype)


def t5_bucket(dist):
    max_exact = NUM_BUCKETS // 2
    d = jnp.maximum(dist, 0)
    ratio = jnp.maximum(d, 1).astype(jnp.float32) / max_exact
    large = max_exact + (jnp.log(ratio) / math.log(MAX_DISTANCE / max_exact)
                         * (NUM_BUCKETS - max_exact)).astype(jnp.int32)
    large = jnp.minimum(large, NUM_BUCKETS - 1)
    return jnp.where(d < max_exact, d, large)


def in_projection(x, g_norm, w_in, b_gates):
    h = rmsnorm(x, g_norm)
    z = jnp.einsum('btd,dc->btc', h, w_in)
    idx = list(np.cumsum(SPLIT_SIZES)[:-1])
    qa, ka, va, qm, km, vm, om, gates = jnp.split(z, idx, axis=-1)
    B, T = x.shape[:2]
    qa = qa.reshape(B, T, N_KV_HEADS, GQA_GROUP, HEAD_DIM_ATT)
    ka = ka.reshape(B, T, N_KV_HEADS, HEAD_DIM_ATT)
    va = va.reshape(B, T, N_KV_HEADS, HEAD_DIM_ATT)
    qm = qm.reshape(B, T, N_HEADS_MLSTM, DK_MLSTM)
    km = km.reshape(B, T, N_HEADS_MLSTM, DK_MLSTM) * (DK_MLSTM ** -0.5)
    vm = vm.reshape(B, T, N_HEADS_MLSTM, DV_MLSTM)
    om = jax.nn.sigmoid(om.reshape(B, T, N_HEADS_MLSTM, DV_MLSTM))
    g = gates.astype(jnp.float32) + b_gates.astype(jnp.float32)
    ig = g[..., :N_HEADS_MLSTM]
    lf = jax.nn.log_sigmoid(g[..., N_HEADS_MLSTM:])
    return qa, ka, va, qm, km, vm, om, ig, lf


def attend_sinks(q, k, v, dist, valid, sinks, rel_bias):
    tq, tk = dist.shape
    s = jnp.einsum('...qhgd,...khd->...hgqk', q, k).astype(jnp.float32) * (HEAD_DIM_ATT ** -0.5)
    bias = rel_bias[t5_bucket(dist)].astype(jnp.float32)
    bias = bias.transpose(2, 0, 1).reshape(N_KV_HEADS, GQA_GROUP, tq, tk)
    s = jnp.where(valid, s + bias, -jnp.inf)
    sink = sinks.astype(jnp.float32).reshape(N_KV_HEADS, GQA_GROUP, 1, 1)
    m = jnp.maximum(s.max(axis=-1, keepdims=True), sink)
    p = jnp.exp(s - m)
    p = p / (p.sum(axis=-1, keepdims=True) + jnp.exp(sink - m))
    return jnp.einsum('...hgqk,...khd->...qhgd', p.astype(v.dtype), v)


def swa_prompt(qa, ka, va, sinks, rel_bias):
    B, S = qa.shape[:2]
    nb = S // ATT_BLOCK
    qb = qa.reshape(B, nb, ATT_BLOCK, N_KV_HEADS, GQA_GROUP, HEAD_DIM_ATT)

    def band(a):
        ab = a.reshape(B, nb, ATT_BLOCK, N_KV_HEADS, HEAD_DIM_ATT)
        prev = jnp.concatenate([jnp.zeros_like(ab[:, :1]), ab[:, :-1]], axis=1)
        return jnp.concatenate([prev, ab], axis=2)

    kk, vv = band(ka), band(va)
    qi = jnp.arange(ATT_BLOCK)[:, None]
    kj = jnp.arange(2 * ATT_BLOCK)[None, :]
    dist = qi + ATT_BLOCK - kj
    in_win = (dist >= 0) & (dist <= WINDOW)
    has_prev = (jnp.arange(nb)[:, None, None] > 0) | (kj >= ATT_BLOCK)[None]
    valid = (in_win[None] & has_prev)[None, :, None, None]
    o = attend_sinks(qb, kk, vv, dist, valid, sinks, rel_bias)
    return o.reshape(B, S, ATT_W)


def swa_sample(qa, ka, va, ck, cv, sinks, rel_bias):
    B, T = qa.shape[:2]
    W = ck.shape[1]
    kk = jnp.concatenate([ck.astype(ka.dtype), ka], axis=1)
    vv = jnp.concatenate([cv.astype(va.dtype), va], axis=1)
    dist = (W + jnp.arange(T))[:, None] - jnp.arange(W + T)[None, :]
    valid = (dist >= 0) & (dist <= WINDOW)
    o = attend_sinks(qa, kk, vv, dist, valid, sinks, rel_bias)
    return o.reshape(B, T, ATT_W), kk[:, -W:], vv[:, -W:]


def mlstm_chunk(carry, inp):
    C, n, m = carry
    q, k, v, ig, lf = inp
    L = q.shape[-2]
    tri = jnp.tril(jnp.ones((L, L), dtype=bool))
    b = jnp.cumsum(lf, axis=-1)
    logD = jnp.where(tri, b[..., :, None] - b[..., None, :] + ig[..., None, :], -jnp.inf)
    m_t = jnp.maximum(b + m[..., None], logD.max(axis=-1))
    D = jnp.exp(logD - m_t[..., None])
    inter = jnp.exp(b + m[..., None] - m_t)
    S = jnp.einsum('bhtd,bhsd->bhts', q, k) * D
    num = inter[..., None] * jnp.einsum('bhtd,bhde->bhte', q, C) + jnp.einsum('bhts,bhse->bhte', S, v)
    den = inter * jnp.einsum('bhtd,bhd->bht', q, n) + S.sum(axis=-1)
    h = num / jnp.maximum(jnp.abs(den), jnp.exp(-m_t))[..., None]
    m_new = m_t[..., -1]
    w = jnp.exp(b[..., -1:] - b + ig - m_new[..., None])
    decay = jnp.exp(b[..., -1] + m - m_new)
    C_new = decay[..., None, None] * C + jnp.einsum('bhs,bhsd,bhse->bhde', w, k, v)
    n_new = decay[..., None] * n + jnp.einsum('bhs,bhsd->bhd', w, k)
    return (C_new, n_new, m_new), h


def mlstm_sequence(q, k, v, ig, lf, C0, n0, m0):
    B, T = q.shape[:2]
    L = math.gcd(T, MLSTM_CHUNK)
    nc = T // L

    def to_chunks(a):
        a = a.astype(jnp.float32).reshape((B, nc, L, N_HEADS_MLSTM) + a.shape[3:])
        return jnp.moveaxis(a, (1, 3), (0, 2))

    xs = (to_chunks(q), to_chunks(k), to_chunks(v), to_chunks(ig), to_chunks(lf))
    carry0 = (C0.astype(jnp.float32), n0.astype(jnp.float32), m0.astype(jnp.float32))
    carry, h = lax.scan(mlstm_chunk, carry0, xs)
    h = jnp.moveaxis(h, (0, 2), (1, 3)).reshape(B, T, N_HEADS_MLSTM, DV_MLSTM)
    return h, carry


def merge_heads(att, h_m, o_m, g_att, g_m, w_out):
    B, T = att.shape[:2]
    a = rmsnorm(att, g_att)
    hm = o_m.astype(jnp.float32) * h_m
    hm = rmsnorm(hm, g_m.reshape(N_HEADS_MLSTM, DV_MLSTM)).reshape(B, T, MLSTM_W).astype(a.dtype)
    mix = jnp.concatenate([a, hm], axis=-1)
    return jnp.einsum('btc,cd->btd', mix, w_out)


def channel_mlp(x, g, w_up, w_down):
    h = rmsnorm(x, g)
    u = jnp.maximum(jnp.einsum('btd,df->btf', h, w_up), 0)
    return jnp.einsum('btf,fd->btd', u * u, w_down)


def setup_inputs(seed: int = 0) -> dict:
    key = jax.random.key(seed)
    ks = jax.random.split(key, 24)
    f32 = jnp.float32
    nrm = lambda k, shape, scale: scale * jax.random.normal(k, shape, f32)
    win_cache = min(WINDOW, PAST_LEN)
    return {
        'x_prompt': nrm(ks[0], (BATCH, SEQ, D_MODEL), 1.0),
        'x_sample': nrm(ks[1], (DEC_BATCH, DEC_SEQ, D_MODEL), 1.0),
        'cache_k_win': nrm(ks[2], (DEPTH, DEC_BATCH, win_cache, N_KV_HEADS, HEAD_DIM_ATT), 1.0),
        'cache_v_win': nrm(ks[3], (DEPTH, DEC_BATCH, win_cache, N_KV_HEADS, HEAD_DIM_ATT), 1.0),
        'state_C': nrm(ks[4], (DEPTH, DEC_BATCH, N_HEADS_MLSTM, DK_MLSTM, DV_MLSTM), 0.3),
        'state_n': nrm(ks[5], (DEPTH, DEC_BATCH, N_HEADS_MLSTM, DK_MLSTM), 0.3),
        'state_m': 1.0 + nrm(ks[6], (DEPTH, DEC_BATCH, N_HEADS_MLSTM), 0.5),
        'rel_bias': nrm(ks[7], (NUM_BUCKETS, N_HEADS_ATT), 0.5),
        'norm_mix': 1.0 + nrm(ks[8], (DEPTH, D_MODEL), 0.02),
        'w_in': nrm(ks[9], (DEPTH, D_MODEL, IN_COLS), D_MODEL ** -0.5),
        'b_gates': jnp.concatenate([nrm(ks[10], (DEPTH, N_HEADS_MLSTM), 0.1),
                                    3.0 + nrm(ks[11], (DEPTH, N_HEADS_MLSTM), 0.1)], axis=-1),
        'attn_sinks': nrm(ks[12], (DEPTH, N_HEADS_ATT), 0.5),
        'norm_attn_out': 1.0 + nrm(ks[13], (DEPTH, ATT_W), 0.02),
        'norm_mlstm_out': 1.0 + nrm(ks[14], (DEPTH, MLSTM_W), 0.02),
        'w_out': nrm(ks[15], (DEPTH, MIX_W, D_MODEL), MIX_W ** -0.5),
        'norm_mlp': 1.0 + nrm(ks[16], (DEPTH, D_MODEL), 0.02),
        'w_up': nrm(ks[17], (DEPTH, D_MODEL, D_FF), D_MODEL ** -0.5),
        'w_down': nrm(ks[18], (DEPTH, D_FF, D_MODEL), D_FF ** -0.5),
        'norm_final': 1.0 + nrm(ks[19], (D_MODEL,), 0.02),
    }


def reference(x_prompt, x_sample, cache_k_win, cache_v_win, state_C, state_n, state_m,
              rel_bias, norm_mix, w_in, b_gates, attn_sinks, norm_attn_out, norm_mlstm_out,
              w_out, norm_mlp, w_up, w_down, norm_final):
    yp, ys = x_prompt, x_sample
    Bp, S = yp.shape[:2]
    win_p = min(WINDOW, S)
    kwp, vwp, Cp_l, np_l, mp_l = [], [], [], [], []
    kws, vws, Cs_l, ns_l, ms_l = [], [], [], [], []
    for l in range(DEPTH):
        qa, ka, va, qm, km, vm, om, ig, lf = in_projection(yp, norm_mix[l], w_in[l], b_gates[l])
        att = swa_prompt(qa, ka, va, attn_sinks[l], rel_bias)
        C0 = jnp.zeros((Bp, N_HEADS_MLSTM, DK_MLSTM, DV_MLSTM), jnp.float32)
        n0 = jnp.zeros((Bp, N_HEADS_MLSTM, DK_MLSTM), jnp.float32)
        m0 = jnp.zeros((Bp, N_HEADS_MLSTM), jnp.float32)
        hm, (Cp, npv, mp) = mlstm_sequence(qm, km, vm, ig, lf, C0, n0, m0)
        yp = yp + merge_heads(att, hm, om, norm_attn_out[l], norm_mlstm_out[l], w_out[l])
        yp = yp + channel_mlp(yp, norm_mlp[l], w_up[l], w_down[l])
        kwp.append(ka[:, -win_p:])
        vwp.append(va[:, -win_p:])
        Cp_l.append(Cp)
        np_l.append(npv)
        mp_l.append(mp)
        qa, ka, va, qm, km, vm, om, ig, lf = in_projection(ys, norm_mix[l], w_in[l], b_gates[l])
        att, kw_new, vw_new = swa_sample(qa, ka, va, cache_k_win[l], cache_v_win[l], attn_sinks[l], rel_bias)
        hm, (Cs, nsv, msv) = mlstm_sequence(qm, km, vm, ig, lf, state_C[l], state_n[l], state_m[l])
        ys = ys + merge_heads(att, hm, om, norm_attn_out[l], norm_mlstm_out[l], w_out[l])
        ys = ys + channel_mlp(ys, norm_mlp[l], w_up[l], w_down[l])
        kws.append(kw_new)
        vws.append(vw_new)
        Cs_l.append(Cs)
        ns_l.append(nsv)
        ms_l.append(msv)
    y_prompt = rmsnorm(yp, norm_final)
    y_sample = rmsnorm(ys, norm_final)
    return (y_prompt, y_sample,
            jnp.stack(kwp), jnp.stack(vwp), jnp.stack(Cp_l), jnp.stack(np_l), jnp.stack(mp_l),
            jnp.stack(kws), jnp.stack(vws), jnp.stack(Cs_l), jnp.stack(ns_l), jnp.stack(ms_l))
```

```python
import functools
import math

import numpy as np
import jax
import jax.numpy as jnp
from jax import lax
from jax.experimental import pallas as pl
from jax.experimental.pallas import tpu as pltpu

D_MODEL = 1024
N_HEADS_ATT = 8
N_KV_HEADS = 2
GQA_GROUP = N_HEADS_ATT // N_KV_HEADS
HEAD_DIM_ATT = 64
ATT_W = N_HEADS_ATT * HEAD_DIM_ATT
KV_W = N_KV_HEADS * HEAD_DIM_ATT
WINDOW = 128
NUM_BUCKETS = 32
MAX_DISTANCE = 128
N_HEADS_MLSTM = 4
DV_MLSTM = 128
DK_MLSTM = 64
MLSTM_W = N_HEADS_MLSTM * DV_MLSTM
QK_M_W = N_HEADS_MLSTM * DK_MLSTM
D_FF = 4 * D_MODEL
EPS = 1e-6
N_GATES = 2 * N_HEADS_MLSTM
LANES = 128
MAIN_COLS = ATT_W + 2 * KV_W + 2 * QK_M_W + 2 * MLSTM_W
VMEM_LIMIT = 56 * 1024 * 1024

F32 = jnp.float32
BF16 = jnp.bfloat16

_O_QA = 0
_O_KV = _O_QA + ATT_W
_O_QKM = _O_KV + 2 * KV_W
_O_VM = _O_QKM + 2 * QK_M_W
_O_OM = _O_VM + MLSTM_W
_O_G = _O_OM + MLSTM_W


def _const_spec(shape):
    nd = len(shape)
    return pl.BlockSpec(shape, lambda *_: (0,) * nd, pipeline_mode=pl.Buffered(1))


def _in_proj_kernel(x_ref, g_ref, w_ref, bg_ref, qa_ref, kvb_ref, kvf_ref, qkm_ref, vm_ref, om_ref, gt_ref):
    xf = x_ref[...]
    h = xf * lax.rsqrt(jnp.mean(xf * xf, axis=-1, keepdims=True) + EPS) * g_ref[...]
    hb = h.astype(BF16)

    def proj(lo, width):
        return jnp.dot(hb, w_ref[:, lo:lo + width], preferred_element_type=F32)

    qa_ref[...] = (proj(_O_QA, ATT_W) * (HEAD_DIM_ATT ** -0.5)).astype(BF16)
    kv = proj(_O_KV, 2 * KV_W)
    kvf_ref[...] = kv
    kvb_ref[...] = kv.astype(BF16)
    qkm = proj(_O_QKM, 2 * QK_M_W)
    qkm_ref[:, :QK_M_W] = qkm[:, :QK_M_W].astype(BF16)
    qkm_ref[:, QK_M_W:] = (qkm[:, QK_M_W:] * (DK_MLSTM ** -0.5)).astype(BF16)
    vm_ref[...] = proj(_O_VM, MLSTM_W).astype(BF16)
    om_ref[...] = jax.nn.sigmoid(proj(_O_OM, MLSTM_W))
    g = proj(_O_G, LANES) + bg_ref[...]
    lf = jnp.minimum(g, 0.0) - jnp.log1p(jnp.exp(-jnp.abs(g)))
    col = lax.broadcasted_iota(jnp.int32, g.shape, 1)
    gt_ref[...] = jnp.where(col < N_HEADS_MLSTM, g, lf)


def _in_proj(x2d, g_norm, w_pad, bg_pad, tm):
    n = x2d.shape[0]
    row = lambda w: pl.BlockSpec((tm, w), lambda i: (i, 0))
    return pl.pallas_call(
        _in_proj_kernel,
        grid=(n // tm,),
        in_specs=[row(D_MODEL), _const_spec((1, D_MODEL)), _const_spec(w_pad.shape), _const_spec((1, LANES))],
        out_specs=[row(ATT_W), row(2 * KV_W), row(2 * KV_W), row(2 * QK_M_W), row(MLSTM_W), row(MLSTM_W), row(LANES)],
        out_shape=[jax.ShapeDtypeStruct((n, ATT_W), BF16), jax.ShapeDtypeStruct((n, 2 * KV_W), BF16),
                   jax.ShapeDtypeStruct((n, 2 * KV_W), F32), jax.ShapeDtypeStruct((n, 2 * QK_M_W), BF16),
                   jax.ShapeDtypeStruct((n, MLSTM_W), BF16), jax.ShapeDtypeStruct((n, MLSTM_W), F32),
                   jax.ShapeDtypeStruct((n, LANES), F32)],
        compiler_params=pltpu.CompilerParams(dimension_semantics=("arbitrary",), vmem_limit_bytes=VMEM_LIMIT),
        name="in_proj",
    )(x2d, g_norm, w_pad, bg_pad)


def _t5_bucket_np(dist):
    max_exact = NUM_BUCKETS // 2
    d = np.maximum(dist, 0)
    ratio = np.maximum(d, 1).astype(np.float32) / np.float32(max_exact)
    large = max_exact + (np.log(ratio) / np.float32(math.log(MAX_DISTANCE / max_exact))
                         * np.float32(NUM_BUCKETS - max_exact)).astype(np.int32)
    large = np.minimum(large, NUM_BUCKETS - 1)
    return np.where(d < max_exact, d, large).astype(np.int32)


def _bias_table_kernel(rb_ref, bucket_ref, out_ref):
    h = pl.program_id(0)
    bucket = bucket_ref[...]
    acc = jnp.full(bucket.shape, -jnp.inf, F32)
    for j in range(NUM_BUCKETS):
        acc = jnp.where(bucket == j, rb_ref[j, h], acc)
    out_ref[0] = acc


def _bias_table(rel_bias, bucket):
    r, c = bucket.shape
    return pl.pallas_call(
        _bias_table_kernel,
        grid=(N_HEADS_ATT,),
        in_specs=[pl.BlockSpec(memory_space=pltpu.SMEM), pl.BlockSpec((r, c), lambda h: (0, 0))],
        out_specs=pl.BlockSpec((1, r, c), lambda h: (h, 0, 0)),
        out_shape=jax.ShapeDtypeStruct((N_HEADS_ATT, r, c), F32),
        compiler_params=pltpu.CompilerParams(dimension_semantics=("arbitrary",)),
        name="bias_table",
    )(rel_bias, jnp.asarray(bucket))


def _prompt_buckets():
    qi = np.arange(WINDOW)[:, None]
    kj = np.arange(2 * WINDOW)[None, :]
    dist = qi + WINDOW - kj
    in_win = (dist >= 0) & (dist <= WINDOW)
    b = _t5_bucket_np(dist)
    general = np.where(in_win, b, -1)
    first = np.where(in_win & (kj >= WINDOW), b, -1)
    return np.concatenate([general, first], axis=0).astype(np.int32)


def _sample_buckets(t, w, padded):
    dist = (w + np.arange(t))[:, None] - np.arange(padded)[None, :]
    valid = (dist >= 0) & (dist <= WINDOW) & (np.arange(padded)[None, :] < w + t)
    return np.where(valid, _t5_bucket_np(dist), -1).astype(np.int32)


def _attn_prompt_kernel(q_ref, kvp_ref, kvc_ref, bias_ref, sink_ref, g_ref, out_ref):
    first = pl.program_id(1) == 0
    off = pl.multiple_of(jnp.where(first, WINDOW, 0), WINDOW)
    kv = jnp.concatenate([kvp_ref[...], kvc_ref[...]], axis=0)
    outs = []
    ssq = jnp.zeros((WINDOW, 1), F32)
    for h in range(N_HEADS_ATT):
        g = h // GQA_GROUP
        q = q_ref[:, h * HEAD_DIM_ATT:(h + 1) * HEAD_DIM_ATT]
        k = kv[:, g * HEAD_DIM_ATT:(g + 1) * HEAD_DIM_ATT]
        v = kv[:, KV_W + g * HEAD_DIM_ATT:KV_W + (g + 1) * HEAD_DIM_ATT]
        s = lax.dot_general(q, k, (((1,), (1,)), ((), ())), preferred_element_type=F32)
        s = s + bias_ref[h, pl.ds(off, WINDOW), :]
        sink = sink_ref[h]
        m = jnp.maximum(jnp.max(s, axis=1, keepdims=True), sink)
        p = jnp.exp(s - m)
        denom = jnp.sum(p, axis=1, keepdims=True) + jnp.exp(sink - m)
        p = p * (1.0 / denom)
        o = jnp.dot(p.astype(BF16), v, preferred_element_type=F32)
        outs.append(o)
        ssq = ssq + jnp.sum(o * o, axis=1, keepdims=True)
    inv = lax.rsqrt(ssq * (1.0 / ATT_W) + EPS)
    for h in range(N_HEADS_ATT):
        sl = slice(h * HEAD_DIM_ATT, (h + 1) * HEAD_DIM_ATT)
        out_ref[:, sl] = (outs[h] * inv * g_ref[:, sl]).astype(BF16)


def _attn_prompt(qa, kvb, bias, sinks, g_att, batch, seq):
    nb = seq // WINDOW
    return pl.pallas_call(
        _attn_prompt_kernel,
        grid=(batch, nb),
        in_specs=[pl.BlockSpec((WINDOW, ATT_W), lambda b, j: (b * nb + j, 0)),
                  pl.BlockSpec((WINDOW, 2 * KV_W), lambda b, j: (b * nb + jnp.maximum(j - 1, 0), 0)),
                  pl.BlockSpec((WINDOW, 2 * KV_W), lambda b, j: (b * nb + j, 0)),
                  _const_spec(bias.shape),
                  pl.BlockSpec(memory_space=pltpu.SMEM),
                  _const_spec((1, ATT_W))],
        out_specs=pl.BlockSpec((WINDOW, ATT_W), lambda b, j: (b * nb + j, 0)),
        out_shape=jax.ShapeDtypeStruct((batch * seq, ATT_W), BF16),
        compiler_params=pltpu.CompilerParams(dimension_semantics=("arbitrary", "arbitrary"),
                                             vmem_limit_bytes=VMEM_LIMIT),
        name="attn_prompt",
    )(qa, kvb, kvb, bias, sinks, g_att)


def _attn_sample_kernel(q_ref, kvf_ref, ck_ref, cv_ref, bias_ref, sink_ref, g_ref, out_ref, kw_ref, vw_ref,
                        *, bb, t, w, kpad):
    rows = GQA_GROUP * t
    for b in range(bb):
        ck = ck_ref[b]
        cv = cv_ref[b]
        kv_new = kvf_ref[b * t:(b + 1) * t, :]
        k_new = kv_new[:, :KV_W]
        v_new = kv_new[:, KV_W:]
        kw_ref[b, :w - t, :] = ck[t:, :]
        kw_ref[b, w - t:, :] = k_new
        vw_ref[b, :w - t, :] = cv[t:, :]
        vw_ref[b, w - t:, :] = v_new
        zpad = jnp.zeros((kpad - w - t, KV_W), F32)
        kk = jnp.concatenate([ck, k_new, zpad], axis=0).astype(BF16)
        vv = jnp.concatenate([cv, v_new, zpad], axis=0).astype(BF16)
        qb = q_ref[b * t:(b + 1) * t, :].astype(F32)
        outs = []
        ssq = jnp.zeros((t, 1), F32)
        for g in range(N_KV_HEADS):
            qg = jnp.concatenate(
                [qb[:, (g * GQA_GROUP + i) * HEAD_DIM_ATT:(g * GQA_GROUP + i + 1) * HEAD_DIM_ATT]
                 for i in range(GQA_GROUP)], axis=0).astype(BF16)
            k = kk[:, g * HEAD_DIM_ATT:(g + 1) * HEAD_DIM_ATT]
            v = vv[:, g * HEAD_DIM_ATT:(g + 1) * HEAD_DIM_ATT]
            s = lax.dot_general(qg, k, (((1,), (1,)), ((), ())), preferred_element_type=F32) + bias_ref[g]
            sink = sink_ref[g]
            m = jnp.maximum(jnp.max(s, axis=1, keepdims=True), sink)
            p = jnp.exp(s - m)
            denom = jnp.sum(p, axis=1, keepdims=True) + jnp.exp(sink - m)
            p = p * (1.0 / denom)
            o = jnp.dot(p.astype(BF16), v, preferred_element_type=F32)
            for i in range(GQA_GROUP):
                oi = o[i * t:(i + 1) * t, :]
                outs.append(oi)
                ssq = ssq + jnp.sum(oi * oi, axis=1, keepdims=True)
        inv = lax.rsqrt(ssq * (1.0 / ATT_W) + EPS)
        for h in range(N_HEADS_ATT):
            sl = slice(h * HEAD_DIM_ATT, (h + 1) * HEAD_DIM_ATT)
            out_ref[b * t:(b + 1) * t, sl] = (outs[h] * inv * g_ref[:, sl]).astype(BF16)


def _attn_sample(qa, kvf, ck, cv, bias, sink_rows, g_att, nseq, t, bb):
    w = ck.shape[1]
    kpad = bias.shape[-1]
    rows = GQA_GROUP * t
    kern = functools.partial(_attn_sample_kernel, bb=bb, t=t, w=w, kpad=kpad)
    tok = lambda wd: pl.BlockSpec((bb * t, wd), lambda i: (i, 0))
    cache = pl.BlockSpec((bb, w, KV_W), lambda i: (i, 0, 0))
    return pl.pallas_call(
        kern,
        grid=(nseq // bb,),
        in_specs=[tok(ATT_W), tok(2 * KV_W), cache, cache,
                  _const_spec((N_KV_HEADS, rows, kpad)), _const_spec((N_KV_HEADS, rows, 1)), _const_spec((1, ATT_W))],
        out_specs=[tok(ATT_W), cache, cache],
        out_shape=[jax.ShapeDtypeStruct((nseq * t, ATT_W), BF16),
                   jax.ShapeDtypeStruct((nseq, w, KV_W), F32), jax.ShapeDtypeStruct((nseq, w, KV_W), F32)],
        compiler_params=pltpu.CompilerParams(dimension_semantics=("arbitrary",), vmem_limit_bytes=VMEM_LIMIT),
        name="attn_sample",
    )(qa, kvf, ck, cv, bias, sink_rows, g_att)


def _mlstm_kernel(qk_ref, v_ref, om_ref, gt_ref, gm_ref, c0_ref, n0_ref, m0_ref,
                  mix_ref, c_out, n_out, m_out, c_s, n_s, m_s, *, chunk, bb):
    c = pl.program_id(1)

    @pl.when(c == 0)
    def _():
        c_s[...] = c0_ref[...]
        n_s[...] = n0_ref[...]
        m_s[...] = m0_ref[...]

    L = chunk
    t_idx = lax.broadcasted_iota(jnp.int32, (L, L), 0)
    s_idx = lax.broadcasted_iota(jnp.int32, (L, L), 1)
    causal = s_idx <= t_idx
    eye = s_idx == t_idx
    for b in range(bb):
        for h in range(N_HEADS_MLSTM):
            q = qk_ref[b, :, h * DK_MLSTM:(h + 1) * DK_MLSTM]
            k = qk_ref[b, :, QK_M_W + h * DK_MLSTM:QK_M_W + (h + 1) * DK_MLSTM]
            v = v_ref[b, :, h * DV_MLSTM:(h + 1) * DV_MLSTM]
            ig_col = gt_ref[b, :, h:h + 1]
            lf_col = gt_ref[b, :, N_HEADS_MLSTM + h:N_HEADS_MLSTM + h + 1]
            c_prev = c_s[b, h]
            n_prev = n_s[b, h]
            m_prev = m_s[b, h]
            b_row = jnp.sum(jnp.where(t_idx <= s_idx, lf_col, 0.0), axis=0, keepdims=True)
            b_col = jnp.sum(jnp.where(eye, b_row, 0.0), axis=1, keepdims=True)
            ig_row = jnp.sum(jnp.where(eye, ig_col, 0.0), axis=0, keepdims=True)
            log_d = jnp.where(causal, b_col - b_row + ig_row, -jnp.inf)
            m_t = jnp.maximum(b_col + m_prev, jnp.max(log_d, axis=1, keepdims=True))
            d = jnp.exp(log_d - m_t)
            inter = jnp.exp(b_col + m_prev - m_t)
            s = lax.dot_general(q, k, (((1,), (1,)), ((), ())), preferred_element_type=F32) * d
            num = (inter * jnp.dot(q, c_prev.astype(BF16), preferred_element_type=F32)
                   + jnp.dot(s.astype(BF16), v, preferred_element_type=F32))
            qn = jnp.sum(q.astype(F32) * n_prev, axis=1, keepdims=True)
            den = inter * qn + jnp.sum(s, axis=1, keepdims=True)
            hh = num / jnp.maximum(jnp.abs(den), jnp.exp(-m_t))
            hm = om_ref[b, :, h * DV_MLSTM:(h + 1) * DV_MLSTM] * hh
            y = hm * lax.rsqrt(jnp.mean(hm * hm, axis=1, keepdims=True) + EPS)
            y = y * gm_ref[:, h * DV_MLSTM:(h + 1) * DV_MLSTM]
            mix_ref[b, :, h * DV_MLSTM:(h + 1) * DV_MLSTM] = y.astype(BF16)
            m_new = m_t[L - 1:L, :]
            b_end = b_col[L - 1:L, :]
            w_col = jnp.exp(b_end - b_col + ig_col - m_new)
            decay = jnp.exp(b_end + m_prev - m_new)
            kw = k.astype(F32) * w_col
            u = lax.dot_general(kw.astype(BF16), v, (((0,), (0,)), ((), ())), preferred_element_type=F32)
            c_s[b, h] = decay * c_prev + u
            n_s[b, h] = decay * n_prev + jnp.sum(kw, axis=0, keepdims=True)
            m_s[b, h] = m_new

    @pl.when(c == pl.num_programs(1) - 1)
    def _():
        c_out[...] = c_s[...]
        n_out[...] = n_s[...]
        m_out[...] = m_s[...]


def _mlstm(qkm, vm, om, gt, g_m, c0, n0, m0, chunk, bb):
    nseq, t = qkm.shape[:2]
    nc = t // chunk
    kern = functools.partial(_mlstm_kernel, chunk=chunk, bb=bb)
    tok = lambda wd: pl.BlockSpec((bb, chunk, wd), lambda i, c: (i, c, 0))
    st = lambda a, d: pl.BlockSpec((bb, N_HEADS_MLSTM, a, d), lambda i, c: (i, 0, 0, 0))
    return pl.pallas_call(
        kern,
        grid=(nseq // bb, nc),
        in_specs=[tok(2 * QK_M_W), tok(MLSTM_W), tok(MLSTM_W), tok(LANES), _const_spec((1, MLSTM_W)),
                  st(DK_MLSTM, DV_MLSTM), st(1, DK_MLSTM), st(1, 1)],
        out_specs=[tok(MLSTM_W), st(DK_MLSTM, DV_MLSTM), st(1, DK_MLSTM), st(1, 1)],
        out_shape=[jax.ShapeDtypeStruct((nseq, t, MLSTM_W), BF16),
                   jax.ShapeDtypeStruct((nseq, N_HEADS_MLSTM, DK_MLSTM, DV_MLSTM), F32),
                   jax.ShapeDtypeStruct((nseq, N_HEADS_MLSTM, 1, DK_MLSTM), F32),
                   jax.ShapeDtypeStruct((nseq, N_HEADS_MLSTM, 1, 1), F32)],
        scratch_shapes=[pltpu.VMEM((bb, N_HEADS_MLSTM, DK_MLSTM, DV_MLSTM), F32),
                        pltpu.VMEM((bb, N_HEADS_MLSTM, 1, DK_MLSTM), F32),
                        pltpu.VMEM((bb, N_HEADS_MLSTM, 1, 1), F32)],
        compiler_params=pltpu.CompilerParams(dimension_semantics=("arbitrary", "arbitrary"),
                                             vmem_limit_bytes=VMEM_LIMIT),
        name="mlstm",
    )(qkm, vm, om, gt, g_m, c0, n0, m0)


def _out_mlp_kernel(x_ref, a_ref, hm_ref, woa_ref, wom_ref, gmlp_ref, wup_ref, wdn_ref, gfin_ref, y_ref, acc_ref,
                    *, fchunk):
    x1 = (x_ref[...]
          + jnp.dot(a_ref[...], woa_ref[...], preferred_element_type=F32)
          + jnp.dot(hm_ref[...], wom_ref[...], preferred_element_type=F32))
    h = (x1 * lax.rsqrt(jnp.mean(x1 * x1, axis=-1, keepdims=True) + EPS) * gmlp_ref[...]).astype(BF16)
    acc_ref[...] = x1

    def ff_chunk(f, carry):
        lo = pl.multiple_of(f * fchunk, fchunk)
        u = jnp.maximum(jnp.dot(h, wup_ref[:, pl.ds(lo, fchunk)], preferred_element_type=F32), 0.0)
        acc_ref[...] += jnp.dot((u * u).astype(BF16), wdn_ref[pl.ds(lo, fchunk), :], preferred_element_type=F32)
        return carry

    lax.fori_loop(0, D_FF // fchunk, ff_chunk, 0)
    y = acc_ref[...]
    y_ref[...] = y * lax.rsqrt(jnp.mean(y * y, axis=-1, keepdims=True) + EPS) * gfin_ref[...]


def _out_mlp(x2d, mix_a, mix_m, wo_a, wo_m, g_mlp, w_up, w_dn, g_fin, tm, fchunk):
    n = x2d.shape[0]
    row = lambda w: pl.BlockSpec((tm, w), lambda i: (i, 0))
    return pl.pallas_call(
        functools.partial(_out_mlp_kernel, fchunk=fchunk),
        grid=(n // tm,),
        in_specs=[row(D_MODEL), row(ATT_W), row(MLSTM_W), _const_spec(wo_a.shape), _const_spec(wo_m.shape),
                  _const_spec((1, D_MODEL)), _const_spec(w_up.shape), _const_spec(w_dn.shape),
                  _const_spec((1, D_MODEL))],
        out_specs=row(D_MODEL),
        out_shape=jax.ShapeDtypeStruct((n, D_MODEL), F32),
        scratch_shapes=[pltpu.VMEM((tm, D_MODEL), F32)],
        compiler_params=pltpu.CompilerParams(dimension_semantics=("arbitrary",), vmem_limit_bytes=VMEM_LIMIT),
        name="out_mlp",
    )(x2d, mix_a, mix_m, wo_a, wo_m, g_mlp, w_up, w_dn, g_fin)


def kernel(x_prompt, x_sample, cache_k_win, cache_v_win, state_C, state_n, state_m, rel_bias, norm_mix, w_in,
           b_gates, attn_sinks, norm_attn_out, norm_mlstm_out, w_out, norm_mlp, w_up, w_down, norm_final):
    depth = w_in.shape[0]
    assert depth == 1, "single-layer trunk"
    bp, sp = x_prompt.shape[:2]
    bs, ts = x_sample.shape[:2]
    wc = cache_k_win.shape[2]
    l = 0

    w_pad = jnp.pad(w_in[l], ((0, 0), (0, MAIN_COLS + LANES - w_in.shape[2]))).astype(BF16)
    bg_pad = jnp.pad(b_gates[l], (0, LANES - N_GATES)).reshape(1, LANES)
    wo_a = w_out[l, :ATT_W].astype(BF16)
    wo_m = w_out[l, ATT_W:].astype(BF16)
    wup = w_up[l].astype(BF16)
    wdn = w_down[l].astype(BF16)
    g_mix = norm_mix[l].reshape(1, D_MODEL)
    g_att = norm_attn_out[l].reshape(1, ATT_W)
    g_m = norm_mlstm_out[l].reshape(1, MLSTM_W)
    g_mlp = norm_mlp[l].reshape(1, D_MODEL)
    g_fin = norm_final.reshape(1, D_MODEL)
    sinks = attn_sinks[l]

    bias_p = _bias_table(rel_bias, _prompt_buckets())
    kpad = ((wc + ts + 15) // 16) * 16
    bias_s = _bias_table(rel_bias, _sample_buckets(ts, wc, kpad))
    bias_s = bias_s.reshape(N_KV_HEADS, GQA_GROUP * ts, kpad)
    sink_rows = jnp.repeat(sinks, ts).reshape(N_KV_HEADS, GQA_GROUP * ts, 1)

    xp = x_prompt.reshape(bp * sp, D_MODEL)
    qa, kvb, kvf, qkm, vm, om, gt = _in_proj(xp, g_mix, w_pad, bg_pad, tm=512)
    mix_a = _attn_prompt(qa, kvb, bias_p, sinks, g_att, bp, sp)
    zc = jnp.zeros((bp, N_HEADS_MLSTM, DK_MLSTM, DV_MLSTM), F32)
    zn = jnp.zeros((bp, N_HEADS_MLSTM, 1, DK_MLSTM), F32)
    zm = jnp.zeros((bp, N_HEADS_MLSTM, 1, 1), F32)
    r3 = lambda a: a.reshape(bp, sp, a.shape[-1])
    mix_m, c_p, n_p, m_p = _mlstm(r3(qkm), r3(vm), r3(om), r3(gt), g_m, zc, zn, zm, chunk=256, bb=1)
    y_p = _out_mlp(xp, mix_a, mix_m.reshape(bp * sp, MLSTM_W), wo_a, wo_m, g_mlp, wup, wdn, g_fin, tm=512, fchunk=512)
    win_p = min(WINDOW, sp)
    kv_win = kvf.reshape(bp, sp, 2 * KV_W)[:, sp - win_p:, :]
    k_win_p = kv_win[:, :, :KV_W].reshape(1, bp, win_p, N_KV_HEADS, HEAD_DIM_ATT)
    v_win_p = kv_win[:, :, KV_W:].reshape(1, bp, win_p, N_KV_HEADS, HEAD_DIM_ATT)

    xs = x_sample.reshape(bs * ts, D_MODEL)
    qa, kvb, kvf, qkm, vm, om, gt = _in_proj(xs, g_mix, w_pad, bg_pad, tm=256)
    ck = cache_k_win[l].reshape(bs, wc, KV_W)
    cv = cache_v_win[l].reshape(bs, wc, KV_W)
    mix_a, kw_s, vw_s = _attn_sample(qa, kvf, ck, cv, bias_s, sink_rows, g_att, bs, ts, bb=8)
    r3 = lambda a: a.reshape(bs, ts, a.shape[-1])
    mix_m, c_s, n_s, m_s = _mlstm(r3(qkm), r3(vm), r3(om), r3(gt), g_m, state_C[l],
                                  state_n[l].reshape(bs, N_HEADS_MLSTM, 1, DK_MLSTM),
                                  state_m[l].reshape(bs, N_HEADS_MLSTM, 1, 1), chunk=math.gcd(ts, 64), bb=8)
    y_s = _out_mlp(xs, mix_a, mix_m.reshape(bs * ts, MLSTM_W), wo_a, wo_m, g_mlp, wup, wdn, g_fin, tm=256, fchunk=512)

    return (y_p.reshape(bp, sp, D_MODEL), y_s.reshape(bs, ts, D_MODEL),
            k_win_p, v_win_p,
            c_p[None], n_p.reshape(1, bp, N_HEADS_MLSTM, DK_MLSTM), m_p.reshape(1, bp, N_HEADS_MLSTM),
            kw_s.reshape(1, bs, wc, N_KV_HEADS, HEAD_DIM_ATT), vw_s.reshape(1, bs, wc, N_KV_HEADS, HEAD_DIM_ATT),
            c_s[None], n_s.reshape(1, bs, N_HEADS_MLSTM, DK_MLSTM), m_s.reshape(1, bs, N_HEADS_MLSTM))
```

```python
import functools
import math

import numpy as np
import jax
import jax.numpy as jnp
from jax import lax
from jax.experimental import pallas as pl
from jax.experimental.pallas import tpu as pltpu

D_MODEL = 1024
N_HEADS_ATT = 8
N_KV_HEADS = 2
GQA_GROUP = N_HEADS_ATT // N_KV_HEADS
HEAD_DIM_ATT = 64
ATT_W = N_HEADS_ATT * HEAD_DIM_ATT
KV_W = N_KV_HEADS * HEAD_DIM_ATT
WINDOW = 128
NUM_BUCKETS = 32
MAX_DISTANCE = 128
N_HEADS_MLSTM = 4
DV_MLSTM = 128
DK_MLSTM = 64
MLSTM_W = N_HEADS_MLSTM * DV_MLSTM
QK_M_W = N_HEADS_MLSTM * DK_MLSTM
D_FF = 4 * D_MODEL
EPS = 1e-6
N_GATES = 2 * N_HEADS_MLSTM
LANES = 128
MAIN_COLS = ATT_W + 2 * KV_W + 2 * QK_M_W + 2 * MLSTM_W
VMEM_LIMIT = 56 * 1024 * 1024

F32 = jnp.float32
BF16 = jnp.bfloat16

_O_QA = 0
_O_KV = _O_QA + ATT_W
_O_QKM = _O_KV + 2 * KV_W
_O_VM = _O_QKM + 2 * QK_M_W
_O_OM = _O_VM + MLSTM_W
_O_G = _O_OM + MLSTM_W


def _const_spec(shape):
    nd = len(shape)
    return pl.BlockSpec(shape, lambda *_: (0,) * nd, pipeline_mode=pl.Buffered(1))


def _in_proj_kernel(x_ref, g_ref, w_ref, bg_ref, qa_ref, kvb_ref, kvf_ref, qkm_ref, vm_ref, om_ref, gt_ref):
    xf = x_ref[...]
    h = xf * lax.rsqrt(jnp.mean(xf * xf, axis=-1, keepdims=True) + EPS) * g_ref[...]
    hb = h.astype(BF16)

    def proj(lo, width):
        return jnp.dot(hb, w_ref[:, lo:lo + width], preferred_element_type=F32)

    qa_ref[...] = (proj(_O_QA, ATT_W) * (HEAD_DIM_ATT ** -0.5)).astype(BF16)
    kv = proj(_O_KV, 2 * KV_W)
    kvf_ref[...] = kv
    kvb_ref[...] = kv.astype(BF16)
    qkm = proj(_O_QKM, 2 * QK_M_W)
    qkm_ref[:, :QK_M_W] = qkm[:, :QK_M_W].astype(BF16)
    qkm_ref[:, QK_M_W:] = (qkm[:, QK_M_W:] * (DK_MLSTM ** -0.5)).astype(BF16)
    vm_ref[...] = proj(_O_VM, MLSTM_W).astype(BF16)
    om_ref[...] = jax.nn.sigmoid(proj(_O_OM, MLSTM_W))
    g = proj(_O_G, LANES) + bg_ref[...]
    lf = jnp.minimum(g, 0.0) - jnp.log1p(jnp.exp(-jnp.abs(g)))
    col = lax.broadcasted_iota(jnp.int32, g.shape, 1)
    gt_ref[...] = jnp.where(col < N_HEADS_MLSTM, g, lf)


def _in_proj(x2d, g_norm, w_pad, bg_pad, tm):
    n = x2d.shape[0]
    row = lambda w: pl.BlockSpec((tm, w), lambda i: (i, 0))
    return pl.pallas_call(
        _in_proj_kernel,
        grid=(n // tm,),
        in_specs=[row(D_MODEL), _const_spec((1, D_MODEL)), _const_spec(w_pad.shape), _const_spec((1, LANES))],
        out_specs=[row(ATT_W), row(2 * KV_W), row(2 * KV_W), row(2 * QK_M_W), row(MLSTM_W), row(MLSTM_W), row(LANES)],
        out_shape=[jax.ShapeDtypeStruct((n, ATT_W), BF16), jax.ShapeDtypeStruct((n, 2 * KV_W), BF16),
                   jax.ShapeDtypeStruct((n, 2 * KV_W), F32), jax.ShapeDtypeStruct((n, 2 * QK_M_W), BF16),
                   jax.ShapeDtypeStruct((n, MLSTM_W), BF16), jax.ShapeDtypeStruct((n, MLSTM_W), F32),
                   jax.ShapeDtypeStruct((n, LANES), F32)],
        compiler_params=pltpu.CompilerParams(dimension_semantics=("arbitrary",), vmem_limit_bytes=VMEM_LIMIT),
        name="in_proj",
    )(x2d, g_norm, w_pad, bg_pad)


def _t5_bucket_np(dist):
    max_exact = NUM_BUCKETS // 2
    d = np.maximum(dist, 0)
    ratio = np.maximum(d, 1).astype(np.float32) / np.float32(max_exact)
    large = max_exact + (np.log(ratio) / np.float32(math.log(MAX_DISTANCE / max_exact))
                         * np.float32(NUM_BUCKETS - max_exact)).astype(np.int32)
    large = np.minimum(large, NUM_BUCKETS - 1)
    return np.where(d < max_exact, d, large).astype(np.int32)


def _bias_table_kernel(rb_ref, bucket_ref, out_ref):
    h = pl.program_id(0)
    bucket = bucket_ref[...]
    acc = jnp.full(bucket.shape, -jnp.inf, F32)
    for j in range(NUM_BUCKETS):
        acc = jnp.where(bucket == j, rb_ref[j, h], acc)
    out_ref[0] = acc


def _bias_table(rel_bias, bucket):
    r, c = bucket.shape
    return pl.pallas_call(
        _bias_table_kernel,
        grid=(N_HEADS_ATT,),
        in_specs=[pl.BlockSpec(memory_space=pltpu.SMEM), pl.BlockSpec((r, c), lambda h: (0, 0))],
        out_specs=pl.BlockSpec((1, r, c), lambda h: (h, 0, 0)),
        out_shape=jax.ShapeDtypeStruct((N_HEADS_ATT, r, c), F32),
        compiler_params=pltpu.CompilerParams(dimension_semantics=("arbitrary",)),
        name="bias_table",
    )(rel_bias, jnp.asarray(bucket))


def _prompt_buckets():
    qi = np.arange(WINDOW)[:, None]
    kj = np.arange(2 * WINDOW)[None, :]
    dist = qi + WINDOW - kj
    in_win = (dist >= 0) & (dist <= WINDOW)
    b = _t5_bucket_np(dist)
    general = np.where(in_win, b, -1)
    first = np.where(in_win & (kj >= WINDOW), b, -1)
    return np.concatenate([general, first], axis=0).astype(np.int32)


def _sample_buckets(t, w, padded):
    dist = (w + np.arange(t))[:, None] - np.arange(padded)[None, :]
    valid = (dist >= 0) & (dist <= WINDOW) & (np.arange(padded)[None, :] < w + t)
    return np.where(valid, _t5_bucket_np(dist), -1).astype(np.int32)


def _attn_prompt_kernel(q_ref, kvp_ref, kvc_ref, bias_ref, sink_ref, g_ref, out_ref):
    first = pl.program_id(1) == 0
    off = pl.multiple_of(jnp.where(first, WINDOW, 0), WINDOW)
    kv = jnp.concatenate([kvp_ref[...], kvc_ref[...]], axis=0)
    outs = []
    ssq = jnp.zeros((WINDOW, 1), F32)
    for h in range(N_HEADS_ATT):
        g = h // GQA_GROUP
        q = q_ref[:, h * HEAD_DIM_ATT:(h + 1) * HEAD_DIM_ATT]
        k = kv[:, g * HEAD_DIM_ATT:(g + 1) * HEAD_DIM_ATT]
        v = kv[:, KV_W + g * HEAD_DIM_ATT:KV_W + (g + 1) * HEAD_DIM_ATT]
        s = lax.dot_general(q, k, (((1,), (1,)), ((), ())), preferred_element_type=F32)
        s = s + bias_ref[h, pl.ds(off, WINDOW), :]
        sink = sink_ref[h]
        m = jnp.maximum(jnp.max(s, axis=1, keepdims=True), sink)
        p = jnp.exp(s - m)
        denom = jnp.sum(p, axis=1, keepdims=True) + jnp.exp(sink - m)
        p = p * (1.0 / denom)
        o = jnp.dot(p.astype(BF16), v, preferred_element_type=F32)
        outs.append(o)
        ssq = ssq + jnp.sum(o * o, axis=1, keepdims=True)
    inv = lax.rsqrt(ssq * (1.0 / ATT_W) + EPS)
    for h in range(N_HEADS_ATT):
        sl = slice(h * HEAD_DIM_ATT, (h + 1) * HEAD_DIM_ATT)
        out_ref[:, sl] = (outs[h] * inv * g_ref[:, sl]).astype(BF16)


def _attn_prompt(qa, kvb, bias, sinks, g_att, batch, seq):
    nb = seq // WINDOW
    return pl.pallas_call(
        _attn_prompt_kernel,
        grid=(batch, nb),
        in_specs=[pl.BlockSpec((WINDOW, ATT_W), lambda b, j: (b * nb + j, 0)),
                  pl.BlockSpec((WINDOW, 2 * KV_W), lambda b, j: (b * nb + jnp.maximum(j - 1, 0), 0)),
                  pl.BlockSpec((WINDOW, 2 * KV_W), lambda b, j: (b * nb + j, 0)),
                  _const_spec(bias.shape),
                  pl.BlockSpec(memory_space=pltpu.SMEM),
                  _const_spec((1, ATT_W))],
        out_specs=pl.BlockSpec((WINDOW, ATT_W), lambda b, j: (b * nb + j, 0)),
        out_shape=jax.ShapeDtypeStruct((batch * seq, ATT_W), BF16),
        compiler_params=pltpu.CompilerParams(dimension_semantics=("arbitrary", "arbitrary"),
                                             vmem_limit_bytes=VMEM_LIMIT),
        name="attn_prompt",
    )(qa, kvb, kvb, bias, sinks, g_att)


def _attn_sample_kernel(q_ref, kvf_ref, ck_ref, cv_ref, bias_ref, sink_ref, g_ref, out_ref, kw_ref, vw_ref,
                        *, bb, t, w, kpad):
    rows = GQA_GROUP * t
    for b in range(bb):
        ck = ck_ref[b]
        cv = cv_ref[b]
        kv_new = kvf_ref[b * t:(b + 1) * t, :]
        k_new = kv_new[:, :KV_W]
        v_new = kv_new[:, KV_W:]
        kw_ref[b, :w - t, :] = ck[t:, :]
        kw_ref[b, w - t:, :] = k_new
        vw_ref[b, :w - t, :] = cv[t:, :]
        vw_ref[b, w - t:, :] = v_new
        zpad = jnp.zeros((kpad - w - t, KV_W), F32)
        kk = jnp.concatenate([ck, k_new, zpad], axis=0).astype(BF16)
        vv = jnp.concatenate([cv, v_new, zpad], axis=0).astype(BF16)
        qb = q_ref[b * t:(b + 1) * t, :].astype(F32)
        outs = []
        ssq = jnp.zeros((t, 1), F32)
        for g in range(N_KV_HEADS):
            qg = jnp.concatenate(
                [qb[:, (g * GQA_GROUP + i) * HEAD_DIM_ATT:(g * GQA_GROUP + i + 1) * HEAD_DIM_ATT]
                 for i in range(GQA_GROUP)], axis=0).astype(BF16)
            k = kk[:, g * HEAD_DIM_ATT:(g + 1) * HEAD_DIM_ATT]
            v = vv[:, g * HEAD_DIM_ATT:(g + 1) * HEAD_DIM_ATT]
            s = lax.dot_general(qg, k, (((1,), (1,)), ((), ())), preferred_element_type=F32) + bias_ref[g]
            sink = sink_ref[g]
            m = jnp.maximum(jnp.max(s, axis=1, keepdims=True), sink)
            p = jnp.exp(s - m)
            denom = jnp.sum(p, axis=1, keepdims=True) + jnp.exp(sink - m)
            p = p * (1.0 / denom)
            o = jnp.dot(p.astype(BF16), v, preferred_element_type=F32)
            for i in range(GQA_GROUP):
                oi = o[i * t:(i + 1) * t, :]
                outs.append(oi)
                ssq = ssq + jnp.sum(oi * oi, axis=1, keepdims=True)
        inv = lax.rsqrt(ssq * (1.0 / ATT_W) + EPS)
        for h in range(N_HEADS_ATT):
            sl = slice(h * HEAD_DIM_ATT, (h + 1) * HEAD_DIM_ATT)
            out_ref[b * t:(b + 1) * t, sl] = (outs[h] * inv * g_ref[:, sl]).astype(BF16)


def _attn_sample(qa, kvf, ck, cv, bias, sink_rows, g_att, nseq, t, bb):
    w = ck.shape[1]
    kpad = bias.shape[-1]
    rows = GQA_GROUP * t
    kern = functools.partial(_attn_sample_kernel, bb=bb, t=t, w=w, kpad=kpad)
    tok = lambda wd: pl.BlockSpec((bb * t, wd), lambda i: (i, 0))
    cache = pl.BlockSpec((bb, w, KV_W), lambda i: (i, 0, 0))
    return pl.pallas_call(
        kern,
        grid=(nseq // bb,),
        in_specs=[tok(ATT_W), tok(2 * KV_W), cache, cache,
                  _const_spec((N_KV_HEADS, rows, kpad)), _const_spec((N_KV_HEADS, rows, 1)), _const_spec((1, ATT_W))],
        out_specs=[tok(ATT_W), cache, cache],
        out_shape=[jax.ShapeDtypeStruct((nseq * t, ATT_W), BF16),
                   jax.ShapeDtypeStruct((nseq, w, KV_W), F32), jax.ShapeDtypeStruct((nseq, w, KV_W), F32)],
        compiler_params=pltpu.CompilerParams(dimension_semantics=("arbitrary",), vmem_limit_bytes=VMEM_LIMIT),
        name="attn_sample",
    )(qa, kvf, ck, cv, bias, sink_rows, g_att)


def _mlstm_kernel(qk_ref, v_ref, om_ref, gt_ref, gm_ref, c0_ref, n0_ref, m0_ref,
                  mix_ref, c_out, n_out, m_out, c_s, n_s, m_s, *, chunk, bb):
    c = pl.program_id(1)

    @pl.when(c == 0)
    def _():
        c_s[...] = c0_ref[...]
        n_s[...] = n0_ref[...]
        m_s[...] = m0_ref[...]

    L = chunk
    t_idx = lax.broadcasted_iota(jnp.int32, (L, L), 0)
    s_idx = lax.broadcasted_iota(jnp.int32, (L, L), 1)
    causal = s_idx <= t_idx
    eye = s_idx == t_idx
    for b in range(bb):
        for h in range(N_HEADS_MLSTM):
            q = qk_ref[b, :, h * DK_MLSTM:(h + 1) * DK_MLSTM]
            k = qk_ref[b, :, QK_M_W + h * DK_MLSTM:QK_M_W + (h + 1) * DK_MLSTM]
            v = v_ref[b, :, h * DV_MLSTM:(h + 1) * DV_MLSTM]
            ig_col = gt_ref[b, :, h:h + 1]
            lf_col = gt_ref[b, :, N_HEADS_MLSTM + h:N_HEADS_MLSTM + h + 1]
            c_prev = c_s[b, h]
            n_prev = n_s[b, h]
            m_prev = m_s[b, h]
            b_row = jnp.sum(jnp.where(t_idx <= s_idx, lf_col, 0.0), axis=0, keepdims=True)
            b_col = jnp.sum(jnp.where(eye, b_row, 0.0), axis=1, keepdims=True)
            ig_row = jnp.sum(jnp.where(eye, ig_col, 0.0), axis=0, keepdims=True)
            log_d = jnp.where(causal, b_col - b_row + ig_row, -jnp.inf)
            m_t = jnp.maximum(b_col + m_prev, jnp.max(log_d, axis=1, keepdims=True))
            d = jnp.exp(log_d - m_t)
            inter = jnp.exp(b_col + m_prev - m_t)
            s = lax.dot_general(q, k, (((1,), (1,)), ((), ())), preferred_element_type=F32) * d
            num = (inter * jnp.dot(q, c_prev.astype(BF16), preferred_element_type=F32)
                   + jnp.dot(s.astype(BF16), v, preferred_element_type=F32))
            qn = jnp.sum(q.astype(F32) * n_prev, axis=1, keepdims=True)
            den = inter * qn + jnp.sum(s, axis=1, keepdims=True)
            hh = num / jnp.maximum(jnp.abs(den), jnp.exp(-m_t))
            hm = om_ref[b, :, h * DV_MLSTM:(h + 1) * DV_MLSTM] * hh
            y = hm * lax.rsqrt(jnp.mean(hm * hm, axis=1, keepdims=True) + EPS)
            y = y * gm_ref[:, h * DV_MLSTM:(h + 1) * DV_MLSTM]
            mix_ref[b, :, h * DV_MLSTM:(h + 1) * DV_MLSTM] = y.astype(BF16)
            m_new = m_t[L - 1:L, :]
            b_end = b_col[L - 1:L, :]
            w_col = jnp.exp(b_end - b_col + ig_col - m_new)
            decay = jnp.exp(b_end + m_prev - m_new)
            kw = k.astype(F32) * w_col
            u = lax.dot_general(kw.astype(BF16), v, (((0,), (0,)), ((), ())), preferred_element_type=F32)
            c_s[b, h] = decay * c_prev + u
            n_s[b, h] = decay * n_prev + jnp.sum(kw, axis=0, keepdims=True)
            m_s[b, h] = m_new

    @pl.when(c == pl.num_programs(1) - 1)
    def _():
        c_out[...] = c_s[...]
        n_out[...] = n_s[...]
        m_out[...] = m_s[...]


def _mlstm(qkm, vm, om, gt, g_m, c0, n0, m0, chunk, bb):
    nseq, t = qkm.shape[:2]
    nc = t // chunk
    kern = functools.partial(_mlstm_kernel, chunk=chunk, bb=bb)
    tok = lambda wd: pl.BlockSpec((bb, chunk, wd), lambda i, c: (i, c, 0))
    st = lambda a, d: pl.BlockSpec((bb, N_HEADS_MLSTM, a, d), lambda i, c: (i, 0, 0, 0))
    return pl.pallas_call(
        kern,
        grid=(nseq // bb, nc),
        in_specs=[tok(2 * QK_M_W), tok(MLSTM_W), tok(MLSTM_W), tok(LANES), _const_spec((1, MLSTM_W)),
                  st(DK_MLSTM, DV_MLSTM), st(1, DK_MLSTM), st(1, 1)],
        out_specs=[tok(MLSTM_W), st(DK_MLSTM, DV_MLSTM), st(1, DK_MLSTM), st(1, 1)],
        out_shape=[jax.ShapeDtypeStruct((nseq, t, MLSTM_W), BF16),
                   jax.ShapeDtypeStruct((nseq, N_HEADS_MLSTM, DK_MLSTM, DV_MLSTM), F32),
                   jax.ShapeDtypeStruct((nseq, N_HEADS_MLSTM, 1, DK_MLSTM), F32),
                   jax.ShapeDtypeStruct((nseq, N_HEADS_MLSTM, 1, 1), F32)],
        scratch_shapes=[pltpu.VMEM((bb, N_HEADS_MLSTM, DK_MLSTM, DV_MLSTM), F32),
                        pltpu.VMEM((bb, N_HEADS_MLSTM, 1, DK_MLSTM), F32),
                        pltpu.VMEM((bb, N_HEADS_MLSTM, 1, 1), F32)],
        compiler_params=pltpu.CompilerParams(dimension_semantics=("arbitrary", "arbitrary"),
                                             vmem_limit_bytes=VMEM_LIMIT),
        name="mlstm",
    )(qkm, vm, om, gt, g_m, c0, n0, m0)


_GATE_LANE0 = N_HEADS_MLSTM


def _prefix_max_rows(x):
    n = x.shape[0]
    row = lax.broadcasted_iota(jnp.int32, x.shape, 0)
    k = 1
    while k < n:
        x = jnp.maximum(x, jnp.where(row >= k, pltpu.roll(x, k, axis=0), -jnp.inf))
        k *= 2
    return x


def _mlstm_seq_kernel(qk_ref, v_ref, om_ref, gt_ref, gm_ref, mix_ref, c_out, n_out, m_out, cn_s, m_s,
                      *, chunk, nsub, bb):
    c = pl.program_id(1)

    @pl.when(c == 0)
    def _():
        cn_s[...] = jnp.zeros_like(cn_s)
        m_s[...] = jnp.zeros_like(m_s)

    L = chunk
    t_idx = lax.broadcasted_iota(jnp.int32, (L, L), 0)
    s_idx = lax.broadcasted_iota(jnp.int32, (L, L), 1)
    causal = s_idx <= t_idx
    tril = causal.astype(F32)
    ones_col = (lax.broadcasted_iota(jnp.int32, (L, DV_MLSTM), 1) == 0).astype(BF16)
    for bi in range(bb):
        m_prev = m_s[bi]
        cn = [cn_s[bi, h] for h in range(N_HEADS_MLSTM)]
        for j in range(nsub):
            rows = slice(j * L, (j + 1) * L)
            gt = gt_ref[bi, rows, :]
            b = jnp.dot(tril, gt, precision=lax.Precision.HIGHEST, preferred_element_type=F32)
            a = pltpu.roll(gt, N_HEADS_MLSTM, axis=1) - b
            big_m = jnp.maximum(m_prev, _prefix_max_rows(a))
            inter = jnp.exp(m_prev - big_m)
            em = jnp.exp(-(b + big_m))
            m_end = big_m[L - 1:L, :]
            decay = jnp.exp(m_prev - m_end)
            m_prev = b[L - 1:L, :] + m_end
            a_t = a.T
            w_t = jnp.exp(a_t - m_end.T)
            for h in range(N_HEADS_MLSTM):
                ln = _GATE_LANE0 + h
                hv = slice(h * DV_MLSTM, (h + 1) * DV_MLSTM)
                q = qk_ref[bi, rows, h * DK_MLSTM:(h + 1) * DK_MLSTM]
                k = qk_ref[bi, rows, QK_M_W + h * DK_MLSTM:QK_M_W + (h + 1) * DK_MLSTM]
                v_ext = jnp.concatenate([v_ref[bi, rows, hv], ones_col], axis=1)
                d = jnp.exp(jnp.where(causal, a_t[ln:ln + 1, :] - big_m[:, ln:ln + 1], -jnp.inf))
                s = (lax.dot_general(q, k, (((1,), (1,)), ((), ())), preferred_element_type=F32) * d).astype(BF16)
                nd = (inter[:, ln:ln + 1] * jnp.dot(q, cn[h].astype(BF16), preferred_element_type=F32)
                      + jnp.dot(s, v_ext, preferred_element_type=F32))
                r = 1.0 / jnp.maximum(jnp.abs(nd[:, DV_MLSTM:DV_MLSTM + 1]), em[:, ln:ln + 1])
                t = om_ref[bi, rows, hv] * nd[:, :DV_MLSTM]
                msq = jnp.mean(t * t, axis=1, keepdims=True)
                scale = r * lax.rsqrt(r * r * msq + EPS)
                mix_ref[bi, rows, hv] = (t * scale * gm_ref[:, hv]).astype(BF16)
                kw_t = (k.astype(F32).T * w_t[ln:ln + 1, :]).astype(BF16)
                cn[h] = decay[:, ln:ln + 1] * cn[h] + jnp.dot(kw_t, v_ext, preferred_element_type=F32)
        m_s[bi] = m_prev
        for h in range(N_HEADS_MLSTM):
            cn_s[bi, h] = cn[h]

    @pl.when(c == pl.num_programs(1) - 1)
    def _():
        c_out[...] = cn_s[:, :, :, :DV_MLSTM]
        n_out[...] = cn_s[:, :, :, DV_MLSTM:DV_MLSTM + 1]
        m_out[...] = m_s[...]


def _mlstm_seq(qkm, vm, om, gt, g_m, chunk, nsub, bb):
    nseq, t = qkm.shape[:2]
    step = chunk * nsub
    kern = functools.partial(_mlstm_seq_kernel, chunk=chunk, nsub=nsub, bb=bb)
    tok = lambda wd: pl.BlockSpec((bb, step, wd), lambda i, c: (i, c, 0))
    st = lambda a, d: pl.BlockSpec((bb, N_HEADS_MLSTM, a, d), lambda i, c: (i, 0, 0, 0))
    return pl.pallas_call(
        kern,
        grid=(nseq // bb, t // step),
        in_specs=[tok(2 * QK_M_W), tok(MLSTM_W), tok(MLSTM_W), tok(LANES), _const_spec((1, MLSTM_W))],
        out_specs=[tok(MLSTM_W), st(DK_MLSTM, DV_MLSTM), st(DK_MLSTM, 1),
                   pl.BlockSpec((bb, 1, LANES), lambda i, c: (i, 0, 0))],
        out_shape=[jax.ShapeDtypeStruct((nseq, t, MLSTM_W), BF16),
                   jax.ShapeDtypeStruct((nseq, N_HEADS_MLSTM, DK_MLSTM, DV_MLSTM), F32),
                   jax.ShapeDtypeStruct((nseq, N_HEADS_MLSTM, DK_MLSTM, 1), F32),
                   jax.ShapeDtypeStruct((nseq, 1, LANES), F32)],
        scratch_shapes=[pltpu.VMEM((bb, N_HEADS_MLSTM, DK_MLSTM, 2 * DV_MLSTM), F32),
                        pltpu.VMEM((bb, 1, LANES), F32)],
        compiler_params=pltpu.CompilerParams(dimension_semantics=("arbitrary", "arbitrary"),
                                             vmem_limit_bytes=VMEM_LIMIT),
        name="mlstm_seq",
    )(qkm, vm, om, gt, g_m)


def _out_mlp_kernel(x_ref, a_ref, hm_ref, woa_ref, wom_ref, gmlp_ref, wup_ref, wdn_ref, gfin_ref, y_ref, acc_ref,
                    *, fchunk):
    x1 = (x_ref[...]
          + jnp.dot(a_ref[...], woa_ref[...], preferred_element_type=F32)
          + jnp.dot(hm_ref[...], wom_ref[...], preferred_element_type=F32))
    h = (x1 * lax.rsqrt(jnp.mean(x1 * x1, axis=-1, keepdims=True) + EPS) * gmlp_ref[...]).astype(BF16)
    acc_ref[...] = x1

    def ff_chunk(f, carry):
        lo = pl.multiple_of(f * fchunk, fchunk)
        u = jnp.maximum(jnp.dot(h, wup_ref[:, pl.ds(lo, fchunk)], preferred_element_type=F32), 0.0)
        acc_ref[...] += jnp.dot((u * u).astype(BF16), wdn_ref[pl.ds(lo, fchunk), :], preferred_element_type=F32)
        return carry

    lax.fori_loop(0, D_FF // fchunk, ff_chunk, 0)
    y = acc_ref[...]
    y_ref[...] = y * lax.rsqrt(jnp.mean(y * y, axis=-1, keepdims=True) + EPS) * gfin_ref[...]


def _out_mlp(x2d, mix_a, mix_m, wo_a, wo_m, g_mlp, w_up, w_dn, g_fin, tm, fchunk):
    n = x2d.shape[0]
    row = lambda w: pl.BlockSpec((tm, w), lambda i: (i, 0))
    return pl.pallas_call(
        functools.partial(_out_mlp_kernel, fchunk=fchunk),
        grid=(n // tm,),
        in_specs=[row(D_MODEL), row(ATT_W), row(MLSTM_W), _const_spec(wo_a.shape), _const_spec(wo_m.shape),
                  _const_spec((1, D_MODEL)), _const_spec(w_up.shape), _const_spec(w_dn.shape),
                  _const_spec((1, D_MODEL))],
        out_specs=row(D_MODEL),
        out_shape=jax.ShapeDtypeStruct((n, D_MODEL), F32),
        scratch_shapes=[pltpu.VMEM((tm, D_MODEL), F32)],
        compiler_params=pltpu.CompilerParams(dimension_semantics=("arbitrary",), vmem_limit_bytes=VMEM_LIMIT),
        name="out_mlp",
    )(x2d, mix_a, mix_m, wo_a, wo_m, g_mlp, w_up, w_dn, g_fin)


def kernel(x_prompt, x_sample, cache_k_win, cache_v_win, state_C, state_n, state_m, rel_bias, norm_mix, w_in,
           b_gates, attn_sinks, norm_attn_out, norm_mlstm_out, w_out, norm_mlp, w_up, w_down, norm_final):
    depth = w_in.shape[0]
    assert depth == 1, "single-layer trunk"
    bp, sp = x_prompt.shape[:2]
    bs, ts = x_sample.shape[:2]
    wc = cache_k_win.shape[2]
    l = 0

    w_pad = jnp.pad(w_in[l], ((0, 0), (0, MAIN_COLS + LANES - w_in.shape[2]))).astype(BF16)
    bg_pad = jnp.pad(b_gates[l], (0, LANES - N_GATES)).reshape(1, LANES)
    wo_a = w_out[l, :ATT_W].astype(BF16)
    wo_m = w_out[l, ATT_W:].astype(BF16)
    wup = w_up[l].astype(BF16)
    wdn = w_down[l].astype(BF16)
    g_mix = norm_mix[l].reshape(1, D_MODEL)
    g_att = norm_attn_out[l].reshape(1, ATT_W)
    g_m = norm_mlstm_out[l].reshape(1, MLSTM_W)
    g_mlp = norm_mlp[l].reshape(1, D_MODEL)
    g_fin = norm_final.reshape(1, D_MODEL)
    sinks = attn_sinks[l]

    bias_p = _bias_table(rel_bias, _prompt_buckets())
    kpad = ((wc + ts + 15) // 16) * 16
    bias_s = _bias_table(rel_bias, _sample_buckets(ts, wc, kpad))
    bias_s = bias_s.reshape(N_KV_HEADS, GQA_GROUP * ts, kpad)
    sink_rows = jnp.repeat(sinks, ts).reshape(N_KV_HEADS, GQA_GROUP * ts, 1)

    xp = x_prompt.reshape(bp * sp, D_MODEL)
    qa, kvb, kvf, qkm, vm, om, gt = _in_proj(xp, g_mix, w_pad, bg_pad, tm=512)
    mix_a = _attn_prompt(qa, kvb, bias_p, sinks, g_att, bp, sp)
    r3 = lambda a: a.reshape(bp, sp, a.shape[-1])
    mix_m, c_p, n_p, m_p = _mlstm_seq(r3(qkm), r3(vm), r3(om), r3(gt), g_m, chunk=128, nsub=2, bb=2)
    m_p = m_p[:, 0, _GATE_LANE0:_GATE_LANE0 + N_HEADS_MLSTM]
    y_p = _out_mlp(xp, mix_a, mix_m.reshape(bp * sp, MLSTM_W), wo_a, wo_m, g_mlp, wup, wdn, g_fin, tm=512, fchunk=512)
    win_p = min(WINDOW, sp)
    kv_win = kvf.reshape(bp, sp, 2 * KV_W)[:, sp - win_p:, :]
    k_win_p = kv_win[:, :, :KV_W].reshape(1, bp, win_p, N_KV_HEADS, HEAD_DIM_ATT)
    v_win_p = kv_win[:, :, KV_W:].reshape(1, bp, win_p, N_KV_HEADS, HEAD_DIM_ATT)

    xs = x_sample.reshape(bs * ts, D_MODEL)
    qa, kvb, kvf, qkm, vm, om, gt = _in_proj(xs, g_mix, w_pad, bg_pad, tm=256)
    ck = cache_k_win[l].reshape(bs, wc, KV_W)
    cv = cache_v_win[l].reshape(bs, wc, KV_W)
    mix_a, kw_s, vw_s = _attn_sample(qa, kvf, ck, cv, bias_s, sink_rows, g_att, bs, ts, bb=8)
    r3 = lambda a: a.reshape(bs, ts, a.shape[-1])
    mix_m, c_s, n_s, m_s = _mlstm(r3(qkm), r3(vm), r3(om), r3(gt), g_m, state_C[l],
                                  state_n[l].reshape(bs, N_HEADS_MLSTM, 1, DK_MLSTM),
                                  state_m[l].reshape(bs, N_HEADS_MLSTM, 1, 1), chunk=math.gcd(ts, 64), bb=8)
    y_s = _out_mlp(xs, mix_a, mix_m.reshape(bs * ts, MLSTM_W), wo_a, wo_m, g_mlp, wup, wdn, g_fin, tm=256, fchunk=512)

    return (y_p.reshape(bp, sp, D_MODEL), y_s.reshape(bs, ts, D_MODEL),
            k_win_p, v_win_p,
            c_p[None], n_p.reshape(1, bp, N_HEADS_MLSTM, DK_MLSTM), m_p.reshape(1, bp, N_HEADS_MLSTM),
            kw_s.reshape(1, bs, wc, N_KV_HEADS, HEAD_DIM_ATT), vw_s.reshape(1, bs, wc, N_KV_HEADS, HEAD_DIM_ATT),
            c_s[None], n_s.reshape(1, bs, N_HEADS_MLSTM, DK_MLSTM), m_s.reshape(1, bs, N_HEADS_MLSTM))
```

```python
import functools
import math

import numpy as np
import jax
import jax.numpy as jnp
from jax import lax
from jax.experimental import pallas as pl
from jax.experimental.pallas import tpu as pltpu

D_MODEL = 1024
N_HEADS_ATT = 8
N_KV_HEADS = 2
GQA_GROUP = N_HEADS_ATT // N_KV_HEADS
HEAD_DIM_ATT = 64
ATT_W = N_HEADS_ATT * HEAD_DIM_ATT
KV_W = N_KV_HEADS * HEAD_DIM_ATT
WINDOW = 128
NUM_BUCKETS = 32
MAX_DISTANCE = 128
N_HEADS_MLSTM = 4
DV_MLSTM = 128
DK_MLSTM = 64
MLSTM_W = N_HEADS_MLSTM * DV_MLSTM
QK_M_W = N_HEADS_MLSTM * DK_MLSTM
D_FF = 4 * D_MODEL
EPS = 1e-6
N_GATES = 2 * N_HEADS_MLSTM
LANES = 128
MAIN_COLS = ATT_W + 2 * KV_W + 2 * QK_M_W + 2 * MLSTM_W
VMEM_LIMIT = 56 * 1024 * 1024

F32 = jnp.float32
BF16 = jnp.bfloat16

_O_QA = 0
_O_KV = _O_QA + ATT_W
_O_QKM = _O_KV + 2 * KV_W
_O_VM = _O_QKM + 2 * QK_M_W
_O_OM = _O_VM + MLSTM_W
_O_G = _O_OM + MLSTM_W


def _const_spec(shape):
    nd = len(shape)
    return pl.BlockSpec(shape, lambda *_: (0,) * nd, pipeline_mode=pl.Buffered(1))


def _in_proj_kernel(x_ref, g_ref, w_ref, bg_ref, qa_ref, kvb_ref, kvf_ref, qkm_ref, vm_ref, om_ref, gt_ref):
    xf = x_ref[...]
    h = xf * lax.rsqrt(jnp.mean(xf * xf, axis=-1, keepdims=True) + EPS) * g_ref[...]
    hb = h.astype(BF16)

    def proj(lo, width):
        return jnp.dot(hb, w_ref[:, lo:lo + width], preferred_element_type=F32)

    qa_ref[...] = (proj(_O_QA, ATT_W) * (HEAD_DIM_ATT ** -0.5)).astype(BF16)
    kv = proj(_O_KV, 2 * KV_W)
    kvf_ref[...] = kv
    kvb_ref[...] = kv.astype(BF16)
    qkm = proj(_O_QKM, 2 * QK_M_W)
    qkm_ref[:, :QK_M_W] = qkm[:, :QK_M_W].astype(BF16)
    qkm_ref[:, QK_M_W:] = (qkm[:, QK_M_W:] * (DK_MLSTM ** -0.5)).astype(BF16)
    vm_ref[...] = proj(_O_VM, MLSTM_W).astype(BF16)
    om_ref[...] = jax.nn.sigmoid(proj(_O_OM, MLSTM_W))
    g = proj(_O_G, LANES) + bg_ref[...]
    lf = jnp.minimum(g, 0.0) - jnp.log1p(jnp.exp(-jnp.abs(g)))
    col = lax.broadcasted_iota(jnp.int32, g.shape, 1)
    gt_ref[...] = jnp.where(col < N_HEADS_MLSTM, g, lf)


def _in_proj(x2d, g_norm, w_pad, bg_pad, tm):
    n = x2d.shape[0]
    row = lambda w: pl.BlockSpec((tm, w), lambda i: (i, 0))
    return pl.pallas_call(
        _in_proj_kernel,
        grid=(n // tm,),
        in_specs=[row(D_MODEL), _const_spec((1, D_MODEL)), _const_spec(w_pad.shape), _const_spec((1, LANES))],
        out_specs=[row(ATT_W), row(2 * KV_W), row(2 * KV_W), row(2 * QK_M_W), row(MLSTM_W), row(MLSTM_W), row(LANES)],
        out_shape=[jax.ShapeDtypeStruct((n, ATT_W), BF16), jax.ShapeDtypeStruct((n, 2 * KV_W), BF16),
                   jax.ShapeDtypeStruct((n, 2 * KV_W), F32), jax.ShapeDtypeStruct((n, 2 * QK_M_W), BF16),
                   jax.ShapeDtypeStruct((n, MLSTM_W), BF16), jax.ShapeDtypeStruct((n, MLSTM_W), F32),
                   jax.ShapeDtypeStruct((n, LANES), F32)],
        compiler_params=pltpu.CompilerParams(dimension_semantics=("arbitrary",), vmem_limit_bytes=VMEM_LIMIT),
        name="in_proj",
    )(x2d, g_norm, w_pad, bg_pad)


def _t5_bucket_np(dist):
    max_exact = NUM_BUCKETS // 2
    d = np.maximum(dist, 0)
    ratio = np.maximum(d, 1).astype(np.float32) / np.float32(max_exact)
    large = max_exact + (np.log(ratio) / np.float32(math.log(MAX_DISTANCE / max_exact))
                         * np.float32(NUM_BUCKETS - max_exact)).astype(np.int32)
    large = np.minimum(large, NUM_BUCKETS - 1)
    return np.where(d < max_exact, d, large).astype(np.int32)


def _bias_table_kernel(rb_ref, bucket_ref, out_ref):
    h = pl.program_id(0)
    bucket = bucket_ref[...]
    acc = jnp.full(bucket.shape, -jnp.inf, F32)
    for j in range(NUM_BUCKETS):
        acc = jnp.where(bucket == j, rb_ref[j, h], acc)
    out_ref[0] = acc


def _bias_table(rel_bias, bucket):
    r, c = bucket.shape
    return pl.pallas_call(
        _bias_table_kernel,
        grid=(N_HEADS_ATT,),
        in_specs=[pl.BlockSpec(memory_space=pltpu.SMEM), pl.BlockSpec((r, c), lambda h: (0, 0))],
        out_specs=pl.BlockSpec((1, r, c), lambda h: (h, 0, 0)),
        out_shape=jax.ShapeDtypeStruct((N_HEADS_ATT, r, c), F32),
        compiler_params=pltpu.CompilerParams(dimension_semantics=("arbitrary",)),
        name="bias_table",
    )(rel_bias, jnp.asarray(bucket))


def _prompt_buckets():
    qi = np.arange(WINDOW)[:, None]
    kj = np.arange(2 * WINDOW)[None, :]
    dist = qi + WINDOW - kj
    in_win = (dist >= 0) & (dist <= WINDOW)
    b = _t5_bucket_np(dist)
    general = np.where(in_win, b, -1)
    first = np.where(in_win & (kj >= WINDOW), b, -1)
    return np.concatenate([general, first], axis=0).astype(np.int32)


def _sample_buckets(t, w, padded):
    dist = (w + np.arange(t))[:, None] - np.arange(padded)[None, :]
    valid = (dist >= 0) & (dist <= WINDOW) & (np.arange(padded)[None, :] < w + t)
    return np.where(valid, _t5_bucket_np(dist), -1).astype(np.int32)


def _attn_prompt_kernel(q_ref, kvp_ref, kvc_ref, bias_ref, sink_ref, g_ref, out_ref):
    first = pl.program_id(1) == 0
    off = pl.multiple_of(jnp.where(first, WINDOW, 0), WINDOW)
    kv = jnp.concatenate([kvp_ref[...], kvc_ref[...]], axis=0)
    outs = []
    ssq = jnp.zeros((WINDOW, 1), F32)
    for h in range(N_HEADS_ATT):
        g = h // GQA_GROUP
        q = q_ref[:, h * HEAD_DIM_ATT:(h + 1) * HEAD_DIM_ATT]
        k = kv[:, g * HEAD_DIM_ATT:(g + 1) * HEAD_DIM_ATT]
        v = kv[:, KV_W + g * HEAD_DIM_ATT:KV_W + (g + 1) * HEAD_DIM_ATT]
        s = lax.dot_general(q, k, (((1,), (1,)), ((), ())), preferred_element_type=F32)
        s = s + bias_ref[h, pl.ds(off, WINDOW), :]
        sink = sink_ref[h]
        m = jnp.maximum(jnp.max(s, axis=1, keepdims=True), sink)
        p = jnp.exp(s - m)
        denom = jnp.sum(p, axis=1, keepdims=True) + jnp.exp(sink - m)
        p = p * (1.0 / denom)
        o = jnp.dot(p.astype(BF16), v, preferred_element_type=F32)
        outs.append(o)
        ssq = ssq + jnp.sum(o * o, axis=1, keepdims=True)
    inv = lax.rsqrt(ssq * (1.0 / ATT_W) + EPS)
    for h in range(N_HEADS_ATT):
        sl = slice(h * HEAD_DIM_ATT, (h + 1) * HEAD_DIM_ATT)
        out_ref[:, sl] = (outs[h] * inv * g_ref[:, sl]).astype(BF16)


def _attn_prompt(qa, kvb, bias, sinks, g_att, batch, seq):
    nb = seq // WINDOW
    return pl.pallas_call(
        _attn_prompt_kernel,
        grid=(batch, nb),
        in_specs=[pl.BlockSpec((WINDOW, ATT_W), lambda b, j: (b * nb + j, 0)),
                  pl.BlockSpec((WINDOW, 2 * KV_W), lambda b, j: (b * nb + jnp.maximum(j - 1, 0), 0)),
                  pl.BlockSpec((WINDOW, 2 * KV_W), lambda b, j: (b * nb + j, 0)),
                  _const_spec(bias.shape),
                  pl.BlockSpec(memory_space=pltpu.SMEM),
                  _const_spec((1, ATT_W))],
        out_specs=pl.BlockSpec((WINDOW, ATT_W), lambda b, j: (b * nb + j, 0)),
        out_shape=jax.ShapeDtypeStruct((batch * seq, ATT_W), BF16),
        compiler_params=pltpu.CompilerParams(dimension_semantics=("arbitrary", "arbitrary"),
                                             vmem_limit_bytes=VMEM_LIMIT),
        name="attn_prompt",
    )(qa, kvb, kvb, bias, sinks, g_att)


def _attn_sample_kernel(q_ref, kvf_ref, ck_ref, cv_ref, bias_ref, sink_ref, g_ref, out_ref, kw_ref, vw_ref,
                        *, bb, t, w, kpad):
    rows = GQA_GROUP * t
    for b in range(bb):
        ck = ck_ref[b]
        cv = cv_ref[b]
        kv_new = kvf_ref[b * t:(b + 1) * t, :]
        k_new = kv_new[:, :KV_W]
        v_new = kv_new[:, KV_W:]
        kw_ref[b, :w - t, :] = ck[t:, :]
        kw_ref[b, w - t:, :] = k_new
        vw_ref[b, :w - t, :] = cv[t:, :]
        vw_ref[b, w - t:, :] = v_new
        zpad = jnp.zeros((kpad - w - t, KV_W), F32)
        kk = jnp.concatenate([ck, k_new, zpad], axis=0).astype(BF16)
        vv = jnp.concatenate([cv, v_new, zpad], axis=0).astype(BF16)
        qb = q_ref[b * t:(b + 1) * t, :].astype(F32)
        outs = []
        ssq = jnp.zeros((t, 1), F32)
        for g in range(N_KV_HEADS):
            qg = jnp.concatenate(
                [qb[:, (g * GQA_GROUP + i) * HEAD_DIM_ATT:(g * GQA_GROUP + i + 1) * HEAD_DIM_ATT]
                 for i in range(GQA_GROUP)], axis=0).astype(BF16)
            k = kk[:, g * HEAD_DIM_ATT:(g + 1) * HEAD_DIM_ATT]
            v = vv[:, g * HEAD_DIM_ATT:(g + 1) * HEAD_DIM_ATT]
            s = lax.dot_general(qg, k, (((1,), (1,)), ((), ())), preferred_element_type=F32) + bias_ref[g]
            sink = sink_ref[g]
            m = jnp.maximum(jnp.max(s, axis=1, keepdims=True), sink)
            p = jnp.exp(s - m)
            denom = jnp.sum(p, axis=1, keepdims=True) + jnp.exp(sink - m)
            p = p * (1.0 / denom)
            o = jnp.dot(p.astype(BF16), v, preferred_element_type=F32)
            for i in range(GQA_GROUP):
                oi = o[i * t:(i + 1) * t, :]
                outs.append(oi)
                ssq = ssq + jnp.sum(oi * oi, axis=1, keepdims=True)
        inv = lax.rsqrt(ssq * (1.0 / ATT_W) + EPS)
        for h in range(N_HEADS_ATT):
            sl = slice(h * HEAD_DIM_ATT, (h + 1) * HEAD_DIM_ATT)
            out_ref[b * t:(b + 1) * t, sl] = (outs[h] * inv * g_ref[:, sl]).astype(BF16)


def _attn_sample(qa, kvf, ck, cv, bias, sink_rows, g_att, nseq, t, bb):
    w = ck.shape[1]
    kpad = bias.shape[-1]
    rows = GQA_GROUP * t
    kern = functools.partial(_attn_sample_kernel, bb=bb, t=t, w=w, kpad=kpad)
    tok = lambda wd: pl.BlockSpec((bb * t, wd), lambda i: (i, 0))
    cache = pl.BlockSpec((bb, w, KV_W), lambda i: (i, 0, 0))
    return pl.pallas_call(
        kern,
        grid=(nseq // bb,),
        in_specs=[tok(ATT_W), tok(2 * KV_W), cache, cache,
                  _const_spec((N_KV_HEADS, rows, kpad)), _const_spec((N_KV_HEADS, rows, 1)), _const_spec((1, ATT_W))],
        out_specs=[tok(ATT_W), cache, cache],
        out_shape=[jax.ShapeDtypeStruct((nseq * t, ATT_W), BF16),
                   jax.ShapeDtypeStruct((nseq, w, KV_W), F32), jax.ShapeDtypeStruct((nseq, w, KV_W), F32)],
        compiler_params=pltpu.CompilerParams(dimension_semantics=("arbitrary",), vmem_limit_bytes=VMEM_LIMIT),
        name="attn_sample",
    )(qa, kvf, ck, cv, bias, sink_rows, g_att)


def _mlstm_kernel(qk_ref, v_ref, om_ref, gt_ref, gm_ref, c0_ref, n0_ref, m0_ref,
                  mix_ref, c_out, n_out, m_out, c_s, n_s, m_s, *, chunk, bb):
    c = pl.program_id(1)

    @pl.when(c == 0)
    def _():
        c_s[...] = c0_ref[...]
        n_s[...] = n0_ref[...]
        m_s[...] = m0_ref[...]

    L = chunk
    t_idx = lax.broadcasted_iota(jnp.int32, (L, L), 0)
    s_idx = lax.broadcasted_iota(jnp.int32, (L, L), 1)
    causal = s_idx <= t_idx
    eye = s_idx == t_idx
    for b in range(bb):
        for h in range(N_HEADS_MLSTM):
            q = qk_ref[b, :, h * DK_MLSTM:(h + 1) * DK_MLSTM]
            k = qk_ref[b, :, QK_M_W + h * DK_MLSTM:QK_M_W + (h + 1) * DK_MLSTM]
            v = v_ref[b, :, h * DV_MLSTM:(h + 1) * DV_MLSTM]
            ig_col = gt_ref[b, :, h:h + 1]
            lf_col = gt_ref[b, :, N_HEADS_MLSTM + h:N_HEADS_MLSTM + h + 1]
            c_prev = c_s[b, h]
            n_prev = n_s[b, h]
            m_prev = m_s[b, h]
            b_row = jnp.sum(jnp.where(t_idx <= s_idx, lf_col, 0.0), axis=0, keepdims=True)
            b_col = jnp.sum(jnp.where(eye, b_row, 0.0), axis=1, keepdims=True)
            ig_row = jnp.sum(jnp.where(eye, ig_col, 0.0), axis=0, keepdims=True)
            log_d = jnp.where(causal, b_col - b_row + ig_row, -jnp.inf)
            m_t = jnp.maximum(b_col + m_prev, jnp.max(log_d, axis=1, keepdims=True))
            d = jnp.exp(log_d - m_t)
            inter = jnp.exp(b_col + m_prev - m_t)
            s = lax.dot_general(q, k, (((1,), (1,)), ((), ())), preferred_element_type=F32) * d
            num = (inter * jnp.dot(q, c_prev.astype(BF16), preferred_element_type=F32)
                   + jnp.dot(s.astype(BF16), v, preferred_element_type=F32))
            qn = jnp.sum(q.astype(F32) * n_prev, axis=1, keepdims=True)
            den = inter * qn + jnp.sum(s, axis=1, keepdims=True)
            hh = num / jnp.maximum(jnp.abs(den), jnp.exp(-m_t))
            hm = om_ref[b, :, h * DV_MLSTM:(h + 1) * DV_MLSTM] * hh
            y = hm * lax.rsqrt(jnp.mean(hm * hm, axis=1, keepdims=True) + EPS)
            y = y * gm_ref[:, h * DV_MLSTM:(h + 1) * DV_MLSTM]
            mix_ref[b, :, h * DV_MLSTM:(h + 1) * DV_MLSTM] = y.astype(BF16)
            m_new = m_t[L - 1:L, :]
            b_end = b_col[L - 1:L, :]
            w_col = jnp.exp(b_end - b_col + ig_col - m_new)
            decay = jnp.exp(b_end + m_prev - m_new)
            kw = k.astype(F32) * w_col
            u = lax.dot_general(kw.astype(BF16), v, (((0,), (0,)), ((), ())), preferred_element_type=F32)
            c_s[b, h] = decay * c_prev + u
            n_s[b, h] = decay * n_prev + jnp.sum(kw, axis=0, keepdims=True)
            m_s[b, h] = m_new

    @pl.when(c == pl.num_programs(1) - 1)
    def _():
        c_out[...] = c_s[...]
        n_out[...] = n_s[...]
        m_out[...] = m_s[...]


def _mlstm(qkm, vm, om, gt, g_m, c0, n0, m0, chunk, bb):
    nseq, t = qkm.shape[:2]
    nc = t // chunk
    kern = functools.partial(_mlstm_kernel, chunk=chunk, bb=bb)
    tok = lambda wd: pl.BlockSpec((bb, chunk, wd), lambda i, c: (i, c, 0))
    st = lambda a, d: pl.BlockSpec((bb, N_HEADS_MLSTM, a, d), lambda i, c: (i, 0, 0, 0))
    return pl.pallas_call(
        kern,
        grid=(nseq // bb, nc),
        in_specs=[tok(2 * QK_M_W), tok(MLSTM_W), tok(MLSTM_W), tok(LANES), _const_spec((1, MLSTM_W)),
                  st(DK_MLSTM, DV_MLSTM), st(1, DK_MLSTM), st(1, 1)],
        out_specs=[tok(MLSTM_W), st(DK_MLSTM, DV_MLSTM), st(1, DK_MLSTM), st(1, 1)],
        out_shape=[jax.ShapeDtypeStruct((nseq, t, MLSTM_W), BF16),
                   jax.ShapeDtypeStruct((nseq, N_HEADS_MLSTM, DK_MLSTM, DV_MLSTM), F32),
                   jax.ShapeDtypeStruct((nseq, N_HEADS_MLSTM, 1, DK_MLSTM), F32),
                   jax.ShapeDtypeStruct((nseq, N_HEADS_MLSTM, 1, 1), F32)],
        scratch_shapes=[pltpu.VMEM((bb, N_HEADS_MLSTM, DK_MLSTM, DV_MLSTM), F32),
                        pltpu.VMEM((bb, N_HEADS_MLSTM, 1, DK_MLSTM), F32),
                        pltpu.VMEM((bb, N_HEADS_MLSTM, 1, 1), F32)],
        compiler_params=pltpu.CompilerParams(dimension_semantics=("arbitrary", "arbitrary"),
                                             vmem_limit_bytes=VMEM_LIMIT),
        name="mlstm",
    )(qkm, vm, om, gt, g_m, c0, n0, m0)


_GATE_LANE0 = N_HEADS_MLSTM


def _prefix_max_rows(x):
    n = x.shape[0]
    row = lax.broadcasted_iota(jnp.int32, x.shape, 0)
    k = 1
    while k < n:
        x = jnp.maximum(x, jnp.where(row >= k, pltpu.roll(x, k, axis=0), -jnp.inf))
        k *= 2
    return x


def _mlstm_seq_kernel(qk_ref, v_ref, om_ref, gt_ref, gm_ref, mix_ref, c_out, n_out, m_out, cn_s, m_s,
                      *, chunk, nsub, bb):
    c = pl.program_id(1)

    @pl.when(c == 0)
    def _():
        cn_s[...] = jnp.zeros_like(cn_s)
        m_s[...] = jnp.zeros_like(m_s)

    L = chunk
    t_idx = lax.broadcasted_iota(jnp.int32, (L, L), 0)
    s_idx = lax.broadcasted_iota(jnp.int32, (L, L), 1)
    causal = s_idx <= t_idx
    tril = causal.astype(F32)
    ones_col = (lax.broadcasted_iota(jnp.int32, (L, DV_MLSTM), 1) == 0).astype(BF16)
    for bi in range(bb):
        m_prev = m_s[bi]
        cn = [cn_s[bi, h] for h in range(N_HEADS_MLSTM)]
        for j in range(nsub):
            rows = slice(j * L, (j + 1) * L)
            gt = gt_ref[bi, rows, :]
            b = jnp.dot(tril, gt, precision=lax.Precision.HIGHEST, preferred_element_type=F32)
            a = pltpu.roll(gt, N_HEADS_MLSTM, axis=1) - b
            big_m = jnp.maximum(m_prev, _prefix_max_rows(a))
            inter = jnp.exp(m_prev - big_m)
            em = jnp.exp(-(b + big_m))
            m_end = big_m[L - 1:L, :]
            decay = jnp.exp(m_prev - m_end)
            m_prev = b[L - 1:L, :] + m_end
            a_t = a.T
            w_t = jnp.exp(a_t - m_end.T)
            for h in range(N_HEADS_MLSTM):
                ln = _GATE_LANE0 + h
                hv = slice(h * DV_MLSTM, (h + 1) * DV_MLSTM)
                q = qk_ref[bi, rows, h * DK_MLSTM:(h + 1) * DK_MLSTM]
                k = qk_ref[bi, rows, QK_M_W + h * DK_MLSTM:QK_M_W + (h + 1) * DK_MLSTM]
                v_ext = jnp.concatenate([v_ref[bi, rows, hv], ones_col], axis=1)
                d = jnp.exp(jnp.where(causal, a_t[ln:ln + 1, :] - big_m[:, ln:ln + 1], -jnp.inf))
                s = (lax.dot_general(q, k, (((1,), (1,)), ((), ())), preferred_element_type=F32) * d).astype(BF16)
                nd = (inter[:, ln:ln + 1] * jnp.dot(q, cn[h].astype(BF16), preferred_element_type=F32)
                      + jnp.dot(s, v_ext, preferred_element_type=F32))
                r = 1.0 / jnp.maximum(jnp.abs(nd[:, DV_MLSTM:DV_MLSTM + 1]), em[:, ln:ln + 1])
                t = om_ref[bi, rows, hv] * nd[:, :DV_MLSTM]
                msq = jnp.mean(t * t, axis=1, keepdims=True)
                scale = r * lax.rsqrt(r * r * msq + EPS)
                mix_ref[bi, rows, hv] = (t * scale * gm_ref[:, hv]).astype(BF16)
                kw_t = (k.astype(F32).T * w_t[ln:ln + 1, :]).astype(BF16)
                cn[h] = decay[:, ln:ln + 1] * cn[h] + jnp.dot(kw_t, v_ext, preferred_element_type=F32)
        m_s[bi] = m_prev
        for h in range(N_HEADS_MLSTM):
            cn_s[bi, h] = cn[h]

    @pl.when(c == pl.num_programs(1) - 1)
    def _():
        c_out[...] = cn_s[:, :, :, :DV_MLSTM]
        n_out[...] = cn_s[:, :, :, DV_MLSTM:DV_MLSTM + 1]
        m_out[...] = m_s[...]


def _mlstm_seq(qkm, vm, om, gt, g_m, chunk, nsub, bb):
    nseq, t = qkm.shape[:2]
    step = chunk * nsub
    kern = functools.partial(_mlstm_seq_kernel, chunk=chunk, nsub=nsub, bb=bb)
    tok = lambda wd: pl.BlockSpec((bb, step, wd), lambda i, c: (i, c, 0))
    st = lambda a, d: pl.BlockSpec((bb, N_HEADS_MLSTM, a, d), lambda i, c: (i, 0, 0, 0))
    return pl.pallas_call(
        kern,
        grid=(nseq // bb, t // step),
        in_specs=[tok(2 * QK_M_W), tok(MLSTM_W), tok(MLSTM_W), tok(LANES), _const_spec((1, MLSTM_W))],
        out_specs=[tok(MLSTM_W), st(DK_MLSTM, DV_MLSTM), st(DK_MLSTM, 1),
                   pl.BlockSpec((bb, 1, LANES), lambda i, c: (i, 0, 0))],
        out_shape=[jax.ShapeDtypeStruct((nseq, t, MLSTM_W), BF16),
                   jax.ShapeDtypeStruct((nseq, N_HEADS_MLSTM, DK_MLSTM, DV_MLSTM), F32),
                   jax.ShapeDtypeStruct((nseq, N_HEADS_MLSTM, DK_MLSTM, 1), F32),
                   jax.ShapeDtypeStruct((nseq, 1, LANES), F32)],
        scratch_shapes=[pltpu.VMEM((bb, N_HEADS_MLSTM, DK_MLSTM, 2 * DV_MLSTM), F32),
                        pltpu.VMEM((bb, 1, LANES), F32)],
        compiler_params=pltpu.CompilerParams(dimension_semantics=("arbitrary", "arbitrary"),
                                             vmem_limit_bytes=VMEM_LIMIT),
        name="mlstm_seq",
    )(qkm, vm, om, gt, g_m)


_T_QA = 0
_T_VA = _T_QA + ATT_W
_T_QM = _T_VA + KV_W
_T_VM = _T_QM + QK_M_W
_T_OM = _T_VM + MLSTM_W
_T_G = _T_OM + MLSTM_W
_T_ROWS = _T_G + 16
_N_KV = 0
_N_KM = _N_KV + 2 * KV_W
_N_G = _N_KM + QK_M_W
_N_COLS = _N_G + LANES


def _log_sigmoid(g):
    return jnp.minimum(g, 0.0) - jnp.log1p(jnp.exp(-jnp.abs(g)))


def _in_proj_t_kernel(x_ref, g_ref, wn_ref, wt_ref, bgr_ref, bgc_ref,
                      k_ref, kvf_ref, km_ref, gt_ref, qT_ref, vT_ref, qmT_ref, vmT_ref, omT_ref, gT_ref):
    xf = x_ref[...]
    hb = (xf * lax.rsqrt(jnp.mean(xf * xf, axis=-1, keepdims=True) + EPS) * g_ref[...]).astype(BF16)

    def nn(lo, width):
        return jnp.dot(hb, wn_ref[:, lo:lo + width], preferred_element_type=F32)

    def nt(lo, width):
        return lax.dot_general(wt_ref[lo:lo + width, :], hb, (((1,), (1,)), ((), ())), preferred_element_type=F32)

    kv = nn(_N_KV, 2 * KV_W)
    kvf_ref[...] = kv
    k_ref[...] = kv[:, :KV_W].astype(BF16)
    km_ref[...] = (nn(_N_KM, QK_M_W) * (DK_MLSTM ** -0.5)).astype(BF16)
    g = nn(_N_G, LANES) + bgr_ref[...]
    col = lax.broadcasted_iota(jnp.int32, g.shape, 1)
    gt_ref[...] = jnp.where(col < N_HEADS_MLSTM, g, _log_sigmoid(g))
    qT_ref[...] = (nt(_T_QA, ATT_W) * (HEAD_DIM_ATT ** -0.5)).astype(BF16)
    vT_ref[...] = nt(_T_VA, KV_W).astype(BF16)
    qmT_ref[...] = nt(_T_QM, QK_M_W).astype(BF16)
    vmT_ref[...] = nt(_T_VM, MLSTM_W).astype(BF16)
    omT_ref[...] = jax.nn.sigmoid(nt(_T_OM, MLSTM_W))
    g_t = nt(_T_G, 16)[:N_GATES, :] + bgc_ref[...]
    row = lax.broadcasted_iota(jnp.int32, g_t.shape, 0)
    gT_ref[...] = jnp.where(row < N_HEADS_MLSTM, g_t, _log_sigmoid(g_t))


def _in_proj_t(x2d, g_norm, w_n, w_t, bg_row, bg_col, tm):
    n = x2d.shape[0]
    row = lambda w: pl.BlockSpec((tm, w), lambda i: (i, 0))
    colb = lambda r: pl.BlockSpec((r, tm), lambda i: (0, i))
    sds = jax.ShapeDtypeStruct
    return pl.pallas_call(
        _in_proj_t_kernel,
        grid=(n // tm,),
        in_specs=[row(D_MODEL), _const_spec((1, D_MODEL)), _const_spec(w_n.shape), _const_spec(w_t.shape),
                  _const_spec((1, LANES)), _const_spec((N_GATES, 1))],
        out_specs=[row(KV_W), row(2 * KV_W), row(QK_M_W), row(LANES),
                   colb(ATT_W), colb(KV_W), colb(QK_M_W), colb(MLSTM_W), colb(MLSTM_W), colb(N_GATES)],
        out_shape=[sds((n, KV_W), BF16), sds((n, 2 * KV_W), F32), sds((n, QK_M_W), BF16), sds((n, LANES), F32),
                   sds((ATT_W, n), BF16), sds((KV_W, n), BF16), sds((QK_M_W, n), BF16), sds((MLSTM_W, n), BF16),
                   sds((MLSTM_W, n), F32), sds((N_GATES, n), F32)],
        compiler_params=pltpu.CompilerParams(dimension_semantics=("arbitrary",), vmem_limit_bytes=VMEM_LIMIT),
        name="in_proj_t",
    )(x2d, g_norm, w_n, w_t, bg_row, bg_col)


def _prompt_buckets_t():
    b = _prompt_buckets()
    return np.concatenate([b[:WINDOW].T, b[WINDOW:].T], axis=0).astype(np.int32)


def _attn_prompt_t_kernel(qT_ref, kp_ref, kc_ref, vTp_ref, vTc_ref, bias_ref, sink_ref, gb_ref, out_ref, *, qb):
    first = pl.program_id(1) == 0
    k_all = jnp.concatenate([kp_ref[...], kc_ref[...]], axis=0)
    vT_all = jnp.concatenate([vTp_ref[...], vTc_ref[...]], axis=1)
    for i in range(qb):
        cols = slice(i * WINDOW, (i + 1) * WINDOW)
        off = pl.multiple_of(jnp.where(first, 2 * WINDOW, 0), 2 * WINDOW) if i == 0 else 0
        k2 = k_all[i * WINDOW:(i + 2) * WINDOW, :]
        vT2 = vT_all[:, i * WINDOW:(i + 2) * WINDOW]
        hd = lambda h: slice((h // GQA_GROUP) * HEAD_DIM_ATT, (h // GQA_GROUP + 1) * HEAD_DIM_ATT)
        scores = [jnp.dot(k2[:, hd(h)], qT_ref[h * HEAD_DIM_ATT:(h + 1) * HEAD_DIM_ATT, cols],
                          preferred_element_type=F32) for h in range(N_HEADS_ATT)]
        probs, rden = [], []
        for h in range(N_HEADS_ATT):
            s = scores[h] + bias_ref[h, pl.ds(off, 2 * WINDOW), :]
            sink = sink_ref[h]
            m = jnp.maximum(jnp.max(s, axis=0, keepdims=True), sink)
            p = jnp.exp(s - m)
            rden.append(1.0 / (jnp.sum(p, axis=0, keepdims=True) + jnp.exp(sink - m)))
            probs.append(p.astype(BF16))
        outs = []
        ssq = jnp.zeros((1, WINDOW), F32)
        for h in range(N_HEADS_ATT):
            o = jnp.dot(vT2[hd(h), :], probs[h], preferred_element_type=F32) * rden[h]
            outs.append(o)
            ssq = ssq + jnp.sum(o * o, axis=0, keepdims=True)
        inv = lax.rsqrt(ssq * (1.0 / ATT_W) + EPS)
        for h in range(N_HEADS_ATT):
            rows = slice(h * HEAD_DIM_ATT, (h + 1) * HEAD_DIM_ATT)
            out_ref[rows, cols] = (outs[h] * inv * gb_ref[rows, :]).astype(BF16)


def _attn_prompt_t(qT, k, vT, bias, sinks, g_b, batch, seq, qb):
    nb = seq // WINDOW
    ns = nb // qb
    prev = lambda b, j: jnp.maximum(b * nb + j * qb - 1, 0)
    return pl.pallas_call(
        functools.partial(_attn_prompt_t_kernel, qb=qb),
        grid=(batch, ns),
        in_specs=[pl.BlockSpec((ATT_W, qb * WINDOW), lambda b, j: (0, b * ns + j)),
                  pl.BlockSpec((WINDOW, KV_W), lambda b, j: (prev(b, j), 0)),
                  pl.BlockSpec((qb * WINDOW, KV_W), lambda b, j: (b * ns + j, 0)),
                  pl.BlockSpec((KV_W, WINDOW), lambda b, j: (0, prev(b, j))),
                  pl.BlockSpec((KV_W, qb * WINDOW), lambda b, j: (0, b * ns + j)),
                  _const_spec(bias.shape),
                  pl.BlockSpec(memory_space=pltpu.SMEM),
                  _const_spec(g_b.shape)],
        out_specs=pl.BlockSpec((ATT_W, qb * WINDOW), lambda b, j: (0, b * ns + j)),
        out_shape=jax.ShapeDtypeStruct((ATT_W, batch * seq), BF16),
        compiler_params=pltpu.CompilerParams(dimension_semantics=("arbitrary", "arbitrary"),
                                             vmem_limit_bytes=VMEM_LIMIT),
        name="attn_prompt_t",
    )(qT, k, k, vT, vT, bias, sinks, g_b)


def _prefix_max_lanes(x):
    n = x.shape[1]
    col = lax.broadcasted_iota(jnp.int32, x.shape, 1)
    k = 1
    while k < n:
        x = jnp.maximum(x, jnp.where(col >= k, pltpu.roll(x, k, axis=1), -jnp.inf))
        k *= 2
    return x


def _mlstm_seq_t_kernel(qT_ref, k_ref, vT_ref, omT_ref, gt_ref, gT_ref, gb_ref, mix_ref, c_out, n_out, m_out,
                        cn_s, mr_s, mc_s, *, chunk, nsub):
    c = pl.program_id(1)

    @pl.when(c == 0)
    def _():
        cn_s[...] = jnp.zeros_like(cn_s)
        mr_s[...] = jnp.zeros_like(mr_s)
        mc_s[...] = jnp.zeros_like(mc_s)

    L = chunk
    i0 = lax.broadcasted_iota(jnp.int32, (L, L), 0)
    i1 = lax.broadcasted_iota(jnp.int32, (L, L), 1)
    causal_t = i0 <= i1
    triu = causal_t.astype(F32)
    tril = (i1 <= i0).astype(F32)
    ones_row = (lax.broadcasted_iota(jnp.int32, (DV_MLSTM, L), 0) == 0).astype(BF16)
    cn = [cn_s[h] for h in range(N_HEADS_MLSTM)]
    m_r = mr_s[...]
    m_c = mc_s[...]
    hi = lax.Precision.HIGHEST
    heads = range(N_HEADS_MLSTM)
    subs = range(nsub)
    col_of = lambda j: slice(j * L, (j + 1) * L)
    hv = lambda h: slice(h * DV_MLSTM, (h + 1) * DV_MLSTM)
    hk = lambda h: slice(h * DK_MLSTM, (h + 1) * DK_MLSTM)
    a_c, a_r, b_c, b_r, pm_r, amax_c = [], [], [], [], [], []
    for j in subs:
        gt = gt_ref[col_of(j), :]
        g_t = gT_ref[:, col_of(j)]
        b_c.append(jnp.dot(tril, gt, precision=hi, preferred_element_type=F32))
        a_c.append(pltpu.roll(gt, N_HEADS_MLSTM, axis=1) - b_c[j])
        b_r.append(jnp.dot(g_t, triu, precision=hi, preferred_element_type=F32))
        a_r.append(pltpu.roll(g_t, N_HEADS_MLSTM, axis=0) - b_r[j])
    kq = [[jnp.dot(k_ref[col_of(j), hk(h)], qT_ref[hk(h), col_of(j)], preferred_element_type=F32)
           for h in heads] for j in subs]
    vT_ext = [[jnp.concatenate([vT_ref[hv(h), col_of(j)], ones_row], axis=0) for h in heads] for j in subs]
    for j in subs:
        pm_r.append(_prefix_max_lanes(a_r[j]))
        amax_c.append(jnp.max(a_c[j], axis=0, keepdims=True))
    big_m, inter, em, w_c, decay_c = [], [], [], [], []
    for j in subs:
        big_m.append(jnp.maximum(m_r, pm_r[j]))
        inter.append(jnp.exp(m_r - big_m[j]))
        em.append(jnp.exp(-(b_r[j] + big_m[j])))
        m_end_c = jnp.maximum(m_c, amax_c[j])
        w_c.append(jnp.exp(a_c[j] - m_end_c))
        decay_c.append(jnp.exp(m_c - m_end_c))
        m_c = b_c[j][L - 1:L, :] + m_end_c
        m_r = jnp.broadcast_to(b_r[j][:, L - 1:L] + big_m[j][:, L - 1:L], (N_GATES, L))
    sT = [[None] * N_HEADS_MLSTM for _ in subs]
    kw = [[None] * N_HEADS_MLSTM for _ in subs]
    for j in subs:
        for h in heads:
            r = _GATE_LANE0 + h
            d = jnp.exp(jnp.where(causal_t, a_c[j][:, r:r + 1] - big_m[j][r:r + 1, :], -jnp.inf))
            sT[j][h] = (kq[j][h] * d).astype(BF16)
            kw[j][h] = (k_ref[col_of(j), hk(h)].astype(F32) * w_c[j][:, r:r + 1]).astype(BF16)
    intra = [[jnp.dot(vT_ext[j][h], sT[j][h], preferred_element_type=F32) for h in heads] for j in subs]
    upd = [[jnp.dot(vT_ext[j][h], kw[j][h], preferred_element_type=F32) for h in heads] for j in subs]
    for j in subs:
        qc = [jnp.dot(cn[h].astype(BF16), qT_ref[hk(h), col_of(j)], preferred_element_type=F32) for h in heads]
        for h in heads:
            r = _GATE_LANE0 + h
            nd = inter[j][r:r + 1, :] * qc[h] + intra[j][h]
            rr = 1.0 / jnp.maximum(jnp.abs(nd[DV_MLSTM:DV_MLSTM + 1, :]), em[j][r:r + 1, :])
            t = omT_ref[hv(h), col_of(j)] * nd[:DV_MLSTM, :]
            msq = jnp.mean(t * t, axis=0, keepdims=True)
            scale = rr * lax.rsqrt(rr * rr * msq + EPS)
            mix_ref[hv(h), col_of(j)] = (t * scale * gb_ref[hv(h), :]).astype(BF16)
            cn[h] = decay_c[j][:, r:r + 1] * cn[h] + upd[j][h]
    for h in range(N_HEADS_MLSTM):
        cn_s[h] = cn[h]
    mr_s[...] = m_r
    mc_s[...] = m_c

    @pl.when(c == pl.num_programs(1) - 1)
    def _():
        for h in range(N_HEADS_MLSTM):
            c_out[0, h] = cn_s[h, :DV_MLSTM, :].T
            n_out[0, h] = cn_s[h, DV_MLSTM:DV_MLSTM + 1, :]
        m_out[0] = mc_s[...]


def _mlstm_seq_t(qmT, km, vmT, omT, gt, gT, g_b, nseq, t, chunk, nsub):
    step = chunk * nsub
    cps = t // step
    kern = functools.partial(_mlstm_seq_t_kernel, chunk=chunk, nsub=nsub)
    colb = lambda r: pl.BlockSpec((r, step), lambda i, c: (0, i * cps + c))
    rowb = lambda w: pl.BlockSpec((step, w), lambda i, c: (i * cps + c, 0))
    st = lambda a, d: pl.BlockSpec((1, N_HEADS_MLSTM, a, d), lambda i, c: (i, 0, 0, 0))
    sds = jax.ShapeDtypeStruct
    return pl.pallas_call(
        kern,
        grid=(nseq, cps),
        in_specs=[colb(QK_M_W), rowb(QK_M_W), colb(MLSTM_W), colb(MLSTM_W), rowb(LANES), colb(N_GATES),
                  _const_spec(g_b.shape)],
        out_specs=[colb(MLSTM_W), st(DK_MLSTM, DV_MLSTM), st(1, DK_MLSTM),
                   pl.BlockSpec((1, 1, LANES), lambda i, c: (i, 0, 0))],
        out_shape=[sds((MLSTM_W, nseq * t), BF16), sds((nseq, N_HEADS_MLSTM, DK_MLSTM, DV_MLSTM), F32),
                   sds((nseq, N_HEADS_MLSTM, 1, DK_MLSTM), F32), sds((nseq, 1, LANES), F32)],
        scratch_shapes=[pltpu.VMEM((N_HEADS_MLSTM, 2 * DV_MLSTM, DK_MLSTM), F32),
                        pltpu.VMEM((N_GATES, chunk), F32), pltpu.VMEM((1, LANES), F32)],
        compiler_params=pltpu.CompilerParams(dimension_semantics=("arbitrary", "arbitrary"),
                                             vmem_limit_bytes=VMEM_LIMIT),
        name="mlstm_seq_t",
    )(qmT, km, vmT, omT, gt, gT, g_b)


def _out_mlp_kernel(x_ref, a_ref, hm_ref, woa_ref, wom_ref, gmlp_ref, wup_ref, wdn_ref, gfin_ref, y_ref, acc_ref,
                    *, fchunk, mix_t):
    mix_dims = (((0,), (0,)), ((), ())) if mix_t else (((1,), (0,)), ((), ()))
    x1 = (x_ref[...]
          + lax.dot_general(a_ref[...], woa_ref[...], mix_dims, preferred_element_type=F32)
          + lax.dot_general(hm_ref[...], wom_ref[...], mix_dims, preferred_element_type=F32))
    h = (x1 * lax.rsqrt(jnp.mean(x1 * x1, axis=-1, keepdims=True) + EPS) * gmlp_ref[...]).astype(BF16)
    acc_ref[...] = x1

    def ff_chunk(f, carry):
        lo = pl.multiple_of(f * fchunk, fchunk)
        u = jnp.maximum(jnp.dot(h, wup_ref[:, pl.ds(lo, fchunk)], preferred_element_type=F32), 0.0)
        acc_ref[...] += jnp.dot((u * u).astype(BF16), wdn_ref[pl.ds(lo, fchunk), :], preferred_element_type=F32)
        return carry

    lax.fori_loop(0, D_FF // fchunk, ff_chunk, 0)
    y = acc_ref[...]
    y_ref[...] = y * lax.rsqrt(jnp.mean(y * y, axis=-1, keepdims=True) + EPS) * gfin_ref[...]


def _out_mlp(x2d, mix_a, mix_m, wo_a, wo_m, g_mlp, w_up, w_dn, g_fin, tm, fchunk, mix_t):
    n = x2d.shape[0]
    row = lambda w: pl.BlockSpec((tm, w), lambda i: (i, 0))
    mix = (lambda w: pl.BlockSpec((w, tm), lambda i: (0, i))) if mix_t else row
    return pl.pallas_call(
        functools.partial(_out_mlp_kernel, fchunk=fchunk, mix_t=mix_t),
        grid=(n // tm,),
        in_specs=[row(D_MODEL), mix(ATT_W), mix(MLSTM_W), _const_spec(wo_a.shape), _const_spec(wo_m.shape),
                  _const_spec((1, D_MODEL)), _const_spec(w_up.shape), _const_spec(w_dn.shape),
                  _const_spec((1, D_MODEL))],
        out_specs=row(D_MODEL),
        out_shape=jax.ShapeDtypeStruct((n, D_MODEL), F32),
        scratch_shapes=[pltpu.VMEM((tm, D_MODEL), F32)],
        compiler_params=pltpu.CompilerParams(dimension_semantics=("arbitrary",), vmem_limit_bytes=VMEM_LIMIT),
        name="out_mlp",
    )(x2d, mix_a, mix_m, wo_a, wo_m, g_mlp, w_up, w_dn, g_fin)


def kernel(x_prompt, x_sample, cache_k_win, cache_v_win, state_C, state_n, state_m, rel_bias, norm_mix, w_in,
           b_gates, attn_sinks, norm_attn_out, norm_mlstm_out, w_out, norm_mlp, w_up, w_down, norm_final):
    depth = w_in.shape[0]
    assert depth == 1, "single-layer trunk"
    bp, sp = x_prompt.shape[:2]
    bs, ts = x_sample.shape[:2]
    wc = cache_k_win.shape[2]
    l = 0

    w_pad = jnp.pad(w_in[l], ((0, 0), (0, MAIN_COLS + LANES - w_in.shape[2]))).astype(BF16)
    bg_pad = jnp.pad(b_gates[l], (0, LANES - N_GATES)).reshape(1, LANES)
    wo_a = w_out[l, :ATT_W].astype(BF16)
    wo_m = w_out[l, ATT_W:].astype(BF16)
    wup = w_up[l].astype(BF16)
    wdn = w_down[l].astype(BF16)
    g_mix = norm_mix[l].reshape(1, D_MODEL)
    g_att = norm_attn_out[l].reshape(1, ATT_W)
    g_m = norm_mlstm_out[l].reshape(1, MLSTM_W)
    g_mlp = norm_mlp[l].reshape(1, D_MODEL)
    g_fin = norm_final.reshape(1, D_MODEL)
    sinks = attn_sinks[l]

    kpad = ((wc + ts + 15) // 16) * 16
    bias_s = _bias_table(rel_bias, _sample_buckets(ts, wc, kpad))
    bias_s = bias_s.reshape(N_KV_HEADS, GQA_GROUP * ts, kpad)
    sink_rows = jnp.repeat(sinks, ts).reshape(N_KV_HEADS, GQA_GROUP * ts, 1)

    xp = x_prompt.reshape(bp * sp, D_MODEL)
    wl = w_in[l]
    cols = lambda lo, n: wl[:, lo:lo + n]
    w_n = jnp.concatenate([cols(_O_KV, 2 * KV_W), cols(_O_QKM + QK_M_W, QK_M_W),
                           jnp.pad(cols(_O_G, N_GATES), ((0, 0), (0, LANES - N_GATES)))], axis=1).astype(BF16)
    w_t = jnp.concatenate([cols(_O_QA, ATT_W), cols(_O_KV + KV_W, KV_W), cols(_O_QKM, QK_M_W), cols(_O_VM, MLSTM_W),
                           cols(_O_OM, MLSTM_W), jnp.pad(cols(_O_G, N_GATES), ((0, 0), (0, 16 - N_GATES)))],
                          axis=1).T.astype(BF16)
    chunk = 128
    gb_att = jnp.broadcast_to(norm_attn_out[l].reshape(ATT_W, 1), (ATT_W, WINDOW))
    gb_m = jnp.broadcast_to(norm_mlstm_out[l].reshape(MLSTM_W, 1), (MLSTM_W, chunk))
    k_a, kvf, km, gt, qT, vT, qmT, vmT, omT, gT = _in_proj_t(xp, g_mix, w_n, w_t, bg_pad,
                                                             b_gates[l].reshape(N_GATES, 1), tm=512)
    bias_t = _bias_table(rel_bias, _prompt_buckets_t())
    mix_a = _attn_prompt_t(qT, k_a, vT, bias_t, sinks, gb_att, bp, sp, qb=8)
    mix_m, c_p, n_p, m_p = _mlstm_seq_t(qmT, km, vmT, omT, gt, gT, gb_m, bp, sp, chunk=chunk, nsub=4)
    m_p = m_p[:, 0, _GATE_LANE0:_GATE_LANE0 + N_HEADS_MLSTM]
    y_p = _out_mlp(xp, mix_a, mix_m, wo_a, wo_m, g_mlp, wup, wdn, g_fin, tm=512, fchunk=512, mix_t=True)
    win_p = min(WINDOW, sp)
    kv_win = kvf.reshape(bp, sp, 2 * KV_W)[:, sp - win_p:, :]
    k_win_p = kv_win[:, :, :KV_W].reshape(1, bp, win_p, N_KV_HEADS, HEAD_DIM_ATT)
    v_win_p = kv_win[:, :, KV_W:].reshape(1, bp, win_p, N_KV_HEADS, HEAD_DIM_ATT)

    xs = x_sample.reshape(bs * ts, D_MODEL)
    qa, kvb, kvf, qkm, vm, om, gt = _in_proj(xs, g_mix, w_pad, bg_pad, tm=256)
    ck = cache_k_win[l].reshape(bs, wc, KV_W)
    cv = cache_v_win[l].reshape(bs, wc, KV_W)
    mix_a, kw_s, vw_s = _attn_sample(qa, kvf, ck, cv, bias_s, sink_rows, g_att, bs, ts, bb=8)
    r3 = lambda a: a.reshape(bs, ts, a.shape[-1])
    mix_m, c_s, n_s, m_s = _mlstm(r3(qkm), r3(vm), r3(om), r3(gt), g_m, state_C[l],
                                  state_n[l].reshape(bs, N_HEADS_MLSTM, 1, DK_MLSTM),
                                  state_m[l].reshape(bs, N_HEADS_MLSTM, 1, 1), chunk=math.gcd(ts, 64), bb=8)
    y_s = _out_mlp(xs, mix_a, mix_m.reshape(bs * ts, MLSTM_W), wo_a, wo_m, g_mlp, wup, wdn, g_fin, tm=256, fchunk=512,
                   mix_t=False)

    return (y_p.reshape(bp, sp, D_MODEL), y_s.reshape(bs, ts, D_MODEL),
            k_win_p, v_win_p,
            c_p[None], n_p.reshape(1, bp, N_HEADS_MLSTM, DK_MLSTM), m_p.reshape(1, bp, N_HEADS_MLSTM),
            kw_s.reshape(1, bs, wc, N_KV_HEADS, HEAD_DIM_ATT), vw_s.reshape(1, bs, wc, N_KV_HEADS, HEAD_DIM_ATT),
            c_s[None], n_s.reshape(1, bs, N_HEADS_MLSTM, DK_MLSTM), m_s.reshape(1, bs, N_HEADS_MLSTM))
```

```python
import functools
import math

import numpy as np
import jax
import jax.numpy as jnp
from jax import lax
from jax.experimental import pallas as pl
from jax.experimental.pallas import tpu as pltpu

D_MODEL = 1024
N_HEADS_ATT = 8
N_KV_HEADS = 2
GQA_GROUP = N_HEADS_ATT // N_KV_HEADS
HEAD_DIM_ATT = 64
ATT_W = N_HEADS_ATT * HEAD_DIM_ATT
KV_W = N_KV_HEADS * HEAD_DIM_ATT
WINDOW = 128
NUM_BUCKETS = 32
MAX_DISTANCE = 128
N_HEADS_MLSTM = 4
DV_MLSTM = 128
DK_MLSTM = 64
MLSTM_W = N_HEADS_MLSTM * DV_MLSTM
QK_M_W = N_HEADS_MLSTM * DK_MLSTM
D_FF = 4 * D_MODEL
EPS = 1e-6
N_GATES = 2 * N_HEADS_MLSTM
LANES = 128
MAIN_COLS = ATT_W + 2 * KV_W + 2 * QK_M_W + 2 * MLSTM_W
VMEM_LIMIT = 56 * 1024 * 1024

F32 = jnp.float32
BF16 = jnp.bfloat16

_O_QA = 0
_O_KV = _O_QA + ATT_W
_O_QKM = _O_KV + 2 * KV_W
_O_VM = _O_QKM + 2 * QK_M_W
_O_OM = _O_VM + MLSTM_W
_O_G = _O_OM + MLSTM_W


def _const_spec(shape):
    nd = len(shape)
    return pl.BlockSpec(shape, lambda *_: (0,) * nd, pipeline_mode=pl.Buffered(1))


def _in_proj_kernel(x_ref, g_ref, w_ref, bg_ref, qa_ref, kvb_ref, kvf_ref, qkm_ref, vm_ref, om_ref, gt_ref):
    xf = x_ref[...]
    h = xf * lax.rsqrt(jnp.mean(xf * xf, axis=-1, keepdims=True) + EPS) * g_ref[...]
    hb = h.astype(BF16)

    def proj(lo, width):
        return jnp.dot(hb, w_ref[:, lo:lo + width], preferred_element_type=F32)

    qa_ref[...] = (proj(_O_QA, ATT_W) * (HEAD_DIM_ATT ** -0.5)).astype(BF16)
    kv = proj(_O_KV, 2 * KV_W)
    kvf_ref[...] = kv
    kvb_ref[...] = kv.astype(BF16)
    qkm = proj(_O_QKM, 2 * QK_M_W)
    qkm_ref[:, :QK_M_W] = qkm[:, :QK_M_W].astype(BF16)
    qkm_ref[:, QK_M_W:] = (qkm[:, QK_M_W:] * (DK_MLSTM ** -0.5)).astype(BF16)
    vm_ref[...] = proj(_O_VM, MLSTM_W).astype(BF16)
    om_ref[...] = jax.nn.sigmoid(proj(_O_OM, MLSTM_W))
    g = proj(_O_G, LANES) + bg_ref[...]
    lf = jnp.minimum(g, 0.0) - jnp.log1p(jnp.exp(-jnp.abs(g)))
    col = lax.broadcasted_iota(jnp.int32, g.shape, 1)
    gt_ref[...] = jnp.where(col < N_HEADS_MLSTM, g, lf)


def _in_proj(x2d, g_norm, w_pad, bg_pad, tm):
    n = x2d.shape[0]
    row = lambda w: pl.BlockSpec((tm, w), lambda i: (i, 0))
    return pl.pallas_call(
        _in_proj_kernel,
        grid=(n // tm,),
        in_specs=[row(D_MODEL), _const_spec((1, D_MODEL)), _const_spec(w_pad.shape), _const_spec((1, LANES))],
        out_specs=[row(ATT_W), row(2 * KV_W), row(2 * KV_W), row(2 * QK_M_W), row(MLSTM_W), row(MLSTM_W), row(LANES)],
        out_shape=[jax.ShapeDtypeStruct((n, ATT_W), BF16), jax.ShapeDtypeStruct((n, 2 * KV_W), BF16),
                   jax.ShapeDtypeStruct((n, 2 * KV_W), F32), jax.ShapeDtypeStruct((n, 2 * QK_M_W), BF16),
                   jax.ShapeDtypeStruct((n, MLSTM_W), BF16), jax.ShapeDtypeStruct((n, MLSTM_W), F32),
                   jax.ShapeDtypeStruct((n, LANES), F32)],
        compiler_params=pltpu.CompilerParams(dimension_semantics=("arbitrary",), vmem_limit_bytes=VMEM_LIMIT),
        name="in_proj",
    )(x2d, g_norm, w_pad, bg_pad)


def _t5_bucket_np(dist):
    max_exact = NUM_BUCKETS // 2
    d = np.maximum(dist, 0)
    ratio = np.maximum(d, 1).astype(np.float32) / np.float32(max_exact)
    large = max_exact + (np.log(ratio) / np.float32(math.log(MAX_DISTANCE / max_exact))
                         * np.float32(NUM_BUCKETS - max_exact)).astype(np.int32)
    large = np.minimum(large, NUM_BUCKETS - 1)
    return np.where(d < max_exact, d, large).astype(np.int32)


def _bias_table_kernel(rb_ref, bucket_ref, out_ref):
    h = pl.program_id(0)
    bucket = bucket_ref[...]
    acc = jnp.full(bucket.shape, -jnp.inf, F32)
    for j in range(NUM_BUCKETS):
        acc = jnp.where(bucket == j, rb_ref[j, h], acc)
    out_ref[0] = acc


def _bias_table(rel_bias, bucket):
    r, c = bucket.shape
    return pl.pallas_call(
        _bias_table_kernel,
        grid=(N_HEADS_ATT,),
        in_specs=[pl.BlockSpec(memory_space=pltpu.SMEM), pl.BlockSpec((r, c), lambda h: (0, 0))],
        out_specs=pl.BlockSpec((1, r, c), lambda h: (h, 0, 0)),
        out_shape=jax.ShapeDtypeStruct((N_HEADS_ATT, r, c), F32),
        compiler_params=pltpu.CompilerParams(dimension_semantics=("arbitrary",)),
        name="bias_table",
    )(rel_bias, jnp.asarray(bucket))


def _prompt_buckets():
    qi = np.arange(WINDOW)[:, None]
    kj = np.arange(2 * WINDOW)[None, :]
    dist = qi + WINDOW - kj
    in_win = (dist >= 0) & (dist <= WINDOW)
    b = _t5_bucket_np(dist)
    general = np.where(in_win, b, -1)
    first = np.where(in_win & (kj >= WINDOW), b, -1)
    return np.concatenate([general, first], axis=0).astype(np.int32)


def _sample_buckets(t, w, padded):
    dist = (w + np.arange(t))[:, None] - np.arange(padded)[None, :]
    valid = (dist >= 0) & (dist <= WINDOW) & (np.arange(padded)[None, :] < w + t)
    return np.where(valid, _t5_bucket_np(dist), -1).astype(np.int32)


def _attn_prompt_kernel(q_ref, kvp_ref, kvc_ref, bias_ref, sink_ref, g_ref, out_ref):
    first = pl.program_id(1) == 0
    off = pl.multiple_of(jnp.where(first, WINDOW, 0), WINDOW)
    kv = jnp.concatenate([kvp_ref[...], kvc_ref[...]], axis=0)
    outs = []
    ssq = jnp.zeros((WINDOW, 1), F32)
    for h in range(N_HEADS_ATT):
        g = h // GQA_GROUP
        q = q_ref[:, h * HEAD_DIM_ATT:(h + 1) * HEAD_DIM_ATT]
        k = kv[:, g * HEAD_DIM_ATT:(g + 1) * HEAD_DIM_ATT]
        v = kv[:, KV_W + g * HEAD_DIM_ATT:KV_W + (g + 1) * HEAD_DIM_ATT]
        s = lax.dot_general(q, k, (((1,), (1,)), ((), ())), preferred_element_type=F32)
        s = s + bias_ref[h, pl.ds(off, WINDOW), :]
        sink = sink_ref[h]
        m = jnp.maximum(jnp.max(s, axis=1, keepdims=True), sink)
        p = jnp.exp(s - m)
        denom = jnp.sum(p, axis=1, keepdims=True) + jnp.exp(sink - m)
        p = p * (1.0 / denom)
        o = jnp.dot(p.astype(BF16), v, preferred_element_type=F32)
        outs.append(o)
        ssq = ssq + jnp.sum(o * o, axis=1, keepdims=True)
    inv = lax.rsqrt(ssq * (1.0 / ATT_W) + EPS)
    for h in range(N_HEADS_ATT):
        sl = slice(h * HEAD_DIM_ATT, (h + 1) * HEAD_DIM_ATT)
        out_ref[:, sl] = (outs[h] * inv * g_ref[:, sl]).astype(BF16)


def _attn_prompt(qa, kvb, bias, sinks, g_att, batch, seq):
    nb = seq // WINDOW
    return pl.pallas_call(
        _attn_prompt_kernel,
        grid=(batch, nb),
        in_specs=[pl.BlockSpec((WINDOW, ATT_W), lambda b, j: (b * nb + j, 0)),
                  pl.BlockSpec((WINDOW, 2 * KV_W), lambda b, j: (b * nb + jnp.maximum(j - 1, 0), 0)),
                  pl.BlockSpec((WINDOW, 2 * KV_W), lambda b, j: (b * nb + j, 0)),
                  _const_spec(bias.shape),
                  pl.BlockSpec(memory_space=pltpu.SMEM),
                  _const_spec((1, ATT_W))],
        out_specs=pl.BlockSpec((WINDOW, ATT_W), lambda b, j: (b * nb + j, 0)),
        out_shape=jax.ShapeDtypeStruct((batch * seq, ATT_W), BF16),
        compiler_params=pltpu.CompilerParams(dimension_semantics=("arbitrary", "arbitrary"),
                                             vmem_limit_bytes=VMEM_LIMIT),
        name="attn_prompt",
    )(qa, kvb, kvb, bias, sinks, g_att)


def _attn_sample_kernel(q_ref, kvf_ref, ck_ref, cv_ref, bias_ref, sink_ref, g_ref, out_ref, kw_ref, vw_ref,
                        *, bb, t, w, kpad):
    seqs = range(bb)
    groups = range(N_KV_HEADS)
    tok = lambda b: slice(b * t, (b + 1) * t)
    gd = lambda g: slice(g * HEAD_DIM_ATT, (g + 1) * HEAD_DIM_ATT)
    zpad = jnp.zeros((kpad - w - t, KV_W), F32)
    kk, vv = [], []
    for b in seqs:
        ck = ck_ref[b]
        cv = cv_ref[b]
        k_new = kvf_ref[tok(b), :KV_W]
        v_new = kvf_ref[tok(b), KV_W:]
        kw_ref[b, :w - t, :] = ck[t:, :]
        kw_ref[b, w - t:, :] = k_new
        vw_ref[b, :w - t, :] = cv[t:, :]
        vw_ref[b, w - t:, :] = v_new
        kk.append(jnp.concatenate([ck, k_new, zpad], axis=0).astype(BF16))
        vv.append(jnp.concatenate([cv, v_new, zpad], axis=0).astype(BF16))
    scores = []
    for b in seqs:
        qb = q_ref[tok(b), :].astype(F32)
        for g in groups:
            qg = jnp.concatenate(
                [qb[:, (g * GQA_GROUP + i) * HEAD_DIM_ATT:(g * GQA_GROUP + i + 1) * HEAD_DIM_ATT]
                 for i in range(GQA_GROUP)], axis=0).astype(BF16)
            scores.append(lax.dot_general(qg, kk[b][:, gd(g)], (((1,), (1,)), ((), ())),
                                          preferred_element_type=F32))
    probs = []
    for b in seqs:
        for g in groups:
            s = scores[b * N_KV_HEADS + g] + bias_ref[g]
            sink = sink_ref[g]
            m = jnp.maximum(jnp.max(s, axis=1, keepdims=True), sink)
            p = jnp.exp(s - m)
            denom = jnp.sum(p, axis=1, keepdims=True) + jnp.exp(sink - m)
            probs.append((p * (1.0 / denom)).astype(BF16))
    outs = [jnp.dot(probs[b * N_KV_HEADS + g], vv[b][:, gd(g)], preferred_element_type=F32)
            for b in seqs for g in groups]
    for b in seqs:
        heads = [outs[b * N_KV_HEADS + g][i * t:(i + 1) * t, :] for g in groups for i in range(GQA_GROUP)]
        ssq = jnp.zeros((t, 1), F32)
        for oi in heads:
            ssq = ssq + jnp.sum(oi * oi, axis=1, keepdims=True)
        inv = lax.rsqrt(ssq * (1.0 / ATT_W) + EPS)
        for h in range(N_HEADS_ATT):
            sl = slice(h * HEAD_DIM_ATT, (h + 1) * HEAD_DIM_ATT)
            out_ref[tok(b), sl] = (heads[h] * inv * g_ref[:, sl]).astype(BF16)


def _attn_sample(qa, kvf, ck, cv, bias, sink_rows, g_att, nseq, t, bb):
    w = ck.shape[1]
    kpad = bias.shape[-1]
    rows = GQA_GROUP * t
    kern = functools.partial(_attn_sample_kernel, bb=bb, t=t, w=w, kpad=kpad)
    tok = lambda wd: pl.BlockSpec((bb * t, wd), lambda i: (i, 0))
    cache = pl.BlockSpec((bb, w, KV_W), lambda i: (i, 0, 0))
    return pl.pallas_call(
        kern,
        grid=(nseq // bb,),
        in_specs=[tok(ATT_W), tok(2 * KV_W), cache, cache,
                  _const_spec((N_KV_HEADS, rows, kpad)), _const_spec((N_KV_HEADS, rows, 1)), _const_spec((1, ATT_W))],
        out_specs=[tok(ATT_W), cache, cache],
        out_shape=[jax.ShapeDtypeStruct((nseq * t, ATT_W), BF16),
                   jax.ShapeDtypeStruct((nseq, w, KV_W), F32), jax.ShapeDtypeStruct((nseq, w, KV_W), F32)],
        compiler_params=pltpu.CompilerParams(dimension_semantics=("arbitrary",), vmem_limit_bytes=VMEM_LIMIT),
        name="attn_sample",
    )(qa, kvf, ck, cv, bias, sink_rows, g_att)


def _mlstm_kernel(qk_ref, v_ref, om_ref, gt_ref, gm_ref, c0_ref, n0_ref, m0_ref,
                  mix_ref, c_out, n_out, m_out, c_s, n_s, m_s, *, chunk, bb):
    c = pl.program_id(1)

    @pl.when(c == 0)
    def _():
        c_s[...] = c0_ref[...]
        n_s[...] = n0_ref[...]
        m_s[...] = m0_ref[...]

    L = chunk
    t_idx = lax.broadcasted_iota(jnp.int32, (L, L), 0)
    s_idx = lax.broadcasted_iota(jnp.int32, (L, L), 1)
    causal = s_idx <= t_idx
    eye = s_idx == t_idx
    for b in range(bb):
        for h in range(N_HEADS_MLSTM):
            q = qk_ref[b, :, h * DK_MLSTM:(h + 1) * DK_MLSTM]
            k = qk_ref[b, :, QK_M_W + h * DK_MLSTM:QK_M_W + (h + 1) * DK_MLSTM]
            v = v_ref[b, :, h * DV_MLSTM:(h + 1) * DV_MLSTM]
            ig_col = gt_ref[b, :, h:h + 1]
            lf_col = gt_ref[b, :, N_HEADS_MLSTM + h:N_HEADS_MLSTM + h + 1]
            c_prev = c_s[b, h]
            n_prev = n_s[b, h]
            m_prev = m_s[b, h]
            b_row = jnp.sum(jnp.where(t_idx <= s_idx, lf_col, 0.0), axis=0, keepdims=True)
            b_col = jnp.sum(jnp.where(eye, b_row, 0.0), axis=1, keepdims=True)
            ig_row = jnp.sum(jnp.where(eye, ig_col, 0.0), axis=0, keepdims=True)
            log_d = jnp.where(causal, b_col - b_row + ig_row, -jnp.inf)
            m_t = jnp.maximum(b_col + m_prev, jnp.max(log_d, axis=1, keepdims=True))
            d = jnp.exp(log_d - m_t)
            inter = jnp.exp(b_col + m_prev - m_t)
            s = lax.dot_general(q, k, (((1,), (1,)), ((), ())), preferred_element_type=F32) * d
            num = (inter * jnp.dot(q, c_prev.astype(BF16), preferred_element_type=F32)
                   + jnp.dot(s.astype(BF16), v, preferred_element_type=F32))
            qn = jnp.sum(q.astype(F32) * n_prev, axis=1, keepdims=True)
            den = inter * qn + jnp.sum(s, axis=1, keepdims=True)
            hh = num / jnp.maximum(jnp.abs(den), jnp.exp(-m_t))
            hm = om_ref[b, :, h * DV_MLSTM:(h + 1) * DV_MLSTM] * hh
            y = hm * lax.rsqrt(jnp.mean(hm * hm, axis=1, keepdims=True) + EPS)
            y = y * gm_ref[:, h * DV_MLSTM:(h + 1) * DV_MLSTM]
            mix_ref[b, :, h * DV_MLSTM:(h + 1) * DV_MLSTM] = y.astype(BF16)
            m_new = m_t[L - 1:L, :]
            b_end = b_col[L - 1:L, :]
            w_col = jnp.exp(b_end - b_col + ig_col - m_new)
            decay = jnp.exp(b_end + m_prev - m_new)
            kw = k.astype(F32) * w_col
            u = lax.dot_general(kw.astype(BF16), v, (((0,), (0,)), ((), ())), preferred_element_type=F32)
            c_s[b, h] = decay * c_prev + u
            n_s[b, h] = decay * n_prev + jnp.sum(kw, axis=0, keepdims=True)
            m_s[b, h] = m_new

    @pl.when(c == pl.num_programs(1) - 1)
    def _():
        c_out[...] = c_s[...]
        n_out[...] = n_s[...]
        m_out[...] = m_s[...]


def _mlstm(qkm, vm, om, gt, g_m, c0, n0, m0, chunk, bb):
    nseq, t = qkm.shape[:2]
    nc = t // chunk
    kern = functools.partial(_mlstm_kernel, chunk=chunk, bb=bb)
    tok = lambda wd: pl.BlockSpec((bb, chunk, wd), lambda i, c: (i, c, 0))
    st = lambda a, d: pl.BlockSpec((bb, N_HEADS_MLSTM, a, d), lambda i, c: (i, 0, 0, 0))
    return pl.pallas_call(
        kern,
        grid=(nseq // bb, nc),
        in_specs=[tok(2 * QK_M_W), tok(MLSTM_W), tok(MLSTM_W), tok(LANES), _const_spec((1, MLSTM_W)),
                  st(DK_MLSTM, DV_MLSTM), st(1, DK_MLSTM), st(1, 1)],
        out_specs=[tok(MLSTM_W), st(DK_MLSTM, DV_MLSTM), st(1, DK_MLSTM), st(1, 1)],
        out_shape=[jax.ShapeDtypeStruct((nseq, t, MLSTM_W), BF16),
                   jax.ShapeDtypeStruct((nseq, N_HEADS_MLSTM, DK_MLSTM, DV_MLSTM), F32),
                   jax.ShapeDtypeStruct((nseq, N_HEADS_MLSTM, 1, DK_MLSTM), F32),
                   jax.ShapeDtypeStruct((nseq, N_HEADS_MLSTM, 1, 1), F32)],
        scratch_shapes=[pltpu.VMEM((bb, N_HEADS_MLSTM, DK_MLSTM, DV_MLSTM), F32),
                        pltpu.VMEM((bb, N_HEADS_MLSTM, 1, DK_MLSTM), F32),
                        pltpu.VMEM((bb, N_HEADS_MLSTM, 1, 1), F32)],
        compiler_params=pltpu.CompilerParams(dimension_semantics=("arbitrary", "arbitrary"),
                                             vmem_limit_bytes=VMEM_LIMIT),
        name="mlstm",
    )(qkm, vm, om, gt, g_m, c0, n0, m0)


_GATE_LANE0 = N_HEADS_MLSTM


def _prefix_max_rows(x):
    n = x.shape[0]
    row = lax.broadcasted_iota(jnp.int32, x.shape, 0)
    k = 1
    while k < n:
        x = jnp.maximum(x, jnp.where(row >= k, pltpu.roll(x, k, axis=0), -jnp.inf))
        k *= 2
    return x


def _mlstm_seq_kernel(qk_ref, v_ref, om_ref, gt_ref, gm_ref, mix_ref, c_out, n_out, m_out, cn_s, m_s,
                      *, chunk, nsub, bb):
    c = pl.program_id(1)

    @pl.when(c == 0)
    def _():
        cn_s[...] = jnp.zeros_like(cn_s)
        m_s[...] = jnp.zeros_like(m_s)

    L = chunk
    t_idx = lax.broadcasted_iota(jnp.int32, (L, L), 0)
    s_idx = lax.broadcasted_iota(jnp.int32, (L, L), 1)
    causal = s_idx <= t_idx
    tril = causal.astype(F32)
    ones_col = (lax.broadcasted_iota(jnp.int32, (L, DV_MLSTM), 1) == 0).astype(BF16)
    for bi in range(bb):
        m_prev = m_s[bi]
        cn = [cn_s[bi, h] for h in range(N_HEADS_MLSTM)]
        for j in range(nsub):
            rows = slice(j * L, (j + 1) * L)
            gt = gt_ref[bi, rows, :]
            b = jnp.dot(tril, gt, precision=lax.Precision.HIGHEST, preferred_element_type=F32)
            a = pltpu.roll(gt, N_HEADS_MLSTM, axis=1) - b
            big_m = jnp.maximum(m_prev, _prefix_max_rows(a))
            inter = jnp.exp(m_prev - big_m)
            em = jnp.exp(-(b + big_m))
            m_end = big_m[L - 1:L, :]
            decay = jnp.exp(m_prev - m_end)
            m_prev = b[L - 1:L, :] + m_end
            a_t = a.T
            w_t = jnp.exp(a_t - m_end.T)
            for h in range(N_HEADS_MLSTM):
                ln = _GATE_LANE0 + h
                hv = slice(h * DV_MLSTM, (h + 1) * DV_MLSTM)
                q = qk_ref[bi, rows, h * DK_MLSTM:(h + 1) * DK_MLSTM]
                k = qk_ref[bi, rows, QK_M_W + h * DK_MLSTM:QK_M_W + (h + 1) * DK_MLSTM]
                v_ext = jnp.concatenate([v_ref[bi, rows, hv], ones_col], axis=1)
                d = jnp.exp(jnp.where(causal, a_t[ln:ln + 1, :] - big_m[:, ln:ln + 1], -jnp.inf))
                s = (lax.dot_general(q, k, (((1,), (1,)), ((), ())), preferred_element_type=F32) * d).astype(BF16)
                nd = (inter[:, ln:ln + 1] * jnp.dot(q, cn[h].astype(BF16), preferred_element_type=F32)
                      + jnp.dot(s, v_ext, preferred_element_type=F32))
                r = 1.0 / jnp.maximum(jnp.abs(nd[:, DV_MLSTM:DV_MLSTM + 1]), em[:, ln:ln + 1])
                t = om_ref[bi, rows, hv] * nd[:, :DV_MLSTM]
                msq = jnp.mean(t * t, axis=1, keepdims=True)
                scale = r * lax.rsqrt(r * r * msq + EPS)
                mix_ref[bi, rows, hv] = (t * scale * gm_ref[:, hv]).astype(BF16)
                kw_t = (k.astype(F32).T * w_t[ln:ln + 1, :]).astype(BF16)
                cn[h] = decay[:, ln:ln + 1] * cn[h] + jnp.dot(kw_t, v_ext, preferred_element_type=F32)
        m_s[bi] = m_prev
        for h in range(N_HEADS_MLSTM):
            cn_s[bi, h] = cn[h]

    @pl.when(c == pl.num_programs(1) - 1)
    def _():
        c_out[...] = cn_s[:, :, :, :DV_MLSTM]
        n_out[...] = cn_s[:, :, :, DV_MLSTM:DV_MLSTM + 1]
        m_out[...] = m_s[...]


def _mlstm_seq(qkm, vm, om, gt, g_m, chunk, nsub, bb):
    nseq, t = qkm.shape[:2]
    step = chunk * nsub
    kern = functools.partial(_mlstm_seq_kernel, chunk=chunk, nsub=nsub, bb=bb)
    tok = lambda wd: pl.BlockSpec((bb, step, wd), lambda i, c: (i, c, 0))
    st = lambda a, d: pl.BlockSpec((bb, N_HEADS_MLSTM, a, d), lambda i, c: (i, 0, 0, 0))
    return pl.pallas_call(
        kern,
        grid=(nseq // bb, t // step),
        in_specs=[tok(2 * QK_M_W), tok(MLSTM_W), tok(MLSTM_W), tok(LANES), _const_spec((1, MLSTM_W))],
        out_specs=[tok(MLSTM_W), st(DK_MLSTM, DV_MLSTM), st(DK_MLSTM, 1),
                   pl.BlockSpec((bb, 1, LANES), lambda i, c: (i, 0, 0))],
        out_shape=[jax.ShapeDtypeStruct((nseq, t, MLSTM_W), BF16),
                   jax.ShapeDtypeStruct((nseq, N_HEADS_MLSTM, DK_MLSTM, DV_MLSTM), F32),
                   jax.ShapeDtypeStruct((nseq, N_HEADS_MLSTM, DK_MLSTM, 1), F32),
                   jax.ShapeDtypeStruct((nseq, 1, LANES), F32)],
        scratch_shapes=[pltpu.VMEM((bb, N_HEADS_MLSTM, DK_MLSTM, 2 * DV_MLSTM), F32),
                        pltpu.VMEM((bb, 1, LANES), F32)],
        compiler_params=pltpu.CompilerParams(dimension_semantics=("arbitrary", "arbitrary"),
                                             vmem_limit_bytes=VMEM_LIMIT),
        name="mlstm_seq",
    )(qkm, vm, om, gt, g_m)


_T_QA = 0
_T_VA = _T_QA + ATT_W
_T_QM = _T_VA + KV_W
_T_VM = _T_QM + QK_M_W
_T_OM = _T_VM + MLSTM_W
_T_G = _T_OM + MLSTM_W
_T_ROWS = _T_G + 16
_N_KV = 0
_N_KM = _N_KV + 2 * KV_W
_N_G = _N_KM + QK_M_W
_N_COLS = _N_G + LANES


def _log_sigmoid(g):
    return jnp.minimum(g, 0.0) - jnp.log1p(jnp.exp(-jnp.abs(g)))


def _in_proj_t_kernel(x_ref, g_ref, wn_ref, wt_ref, bgr_ref, bgc_ref,
                      k_ref, kvf_ref, km_ref, gt_ref, qT_ref, vT_ref, qmT_ref, vmT_ref, omT_ref, gT_ref):
    xf = x_ref[...]
    hb = (xf * lax.rsqrt(jnp.mean(xf * xf, axis=-1, keepdims=True) + EPS) * g_ref[...]).astype(BF16)

    z_n = jnp.dot(hb, wn_ref[...], preferred_element_type=F32)
    z_t = lax.dot_general(wt_ref[...], hb, (((1,), (1,)), ((), ())), preferred_element_type=F32)
    nn = lambda lo, width: z_n[:, lo:lo + width]
    nt = lambda lo, width: z_t[lo:lo + width, :]

    kv = nn(_N_KV, 2 * KV_W)
    kvf_ref[...] = kv
    k_ref[...] = kv[:, :KV_W].astype(BF16)
    km_ref[...] = (nn(_N_KM, QK_M_W) * (DK_MLSTM ** -0.5)).astype(BF16)
    g = nn(_N_G, LANES) + bgr_ref[...]
    col = lax.broadcasted_iota(jnp.int32, g.shape, 1)
    gt_ref[...] = jnp.where(col < N_HEADS_MLSTM, g, _log_sigmoid(g))
    qT_ref[...] = (nt(_T_QA, ATT_W) * (HEAD_DIM_ATT ** -0.5)).astype(BF16)
    vT_ref[...] = nt(_T_VA, KV_W).astype(BF16)
    qmT_ref[...] = nt(_T_QM, QK_M_W).astype(BF16)
    vmT_ref[...] = nt(_T_VM, MLSTM_W).astype(BF16)
    omT_ref[...] = jax.nn.sigmoid(nt(_T_OM, MLSTM_W))
    g_t = nt(_T_G, 16)[:N_GATES, :] + bgc_ref[...]
    row = lax.broadcasted_iota(jnp.int32, g_t.shape, 0)
    gT_ref[...] = jnp.where(row < N_HEADS_MLSTM, g_t, _log_sigmoid(g_t))


def _in_proj_t(x2d, g_norm, w_n, w_t, bg_row, bg_col, tm):
    n = x2d.shape[0]
    row = lambda w: pl.BlockSpec((tm, w), lambda i: (i, 0))
    colb = lambda r: pl.BlockSpec((r, tm), lambda i: (0, i))
    sds = jax.ShapeDtypeStruct
    return pl.pallas_call(
        _in_proj_t_kernel,
        grid=(n // tm,),
        in_specs=[row(D_MODEL), _const_spec((1, D_MODEL)), _const_spec(w_n.shape), _const_spec(w_t.shape),
                  _const_spec((1, LANES)), _const_spec((N_GATES, 1))],
        out_specs=[row(KV_W), row(2 * KV_W), row(QK_M_W), row(LANES),
                   colb(ATT_W), colb(KV_W), colb(QK_M_W), colb(MLSTM_W), colb(MLSTM_W), colb(N_GATES)],
        out_shape=[sds((n, KV_W), BF16), sds((n, 2 * KV_W), F32), sds((n, QK_M_W), BF16), sds((n, LANES), F32),
                   sds((ATT_W, n), BF16), sds((KV_W, n), BF16), sds((QK_M_W, n), BF16), sds((MLSTM_W, n), BF16),
                   sds((MLSTM_W, n), F32), sds((N_GATES, n), F32)],
        compiler_params=pltpu.CompilerParams(dimension_semantics=("arbitrary",), vmem_limit_bytes=VMEM_LIMIT),
        name="in_proj_t",
    )(x2d, g_norm, w_n, w_t, bg_row, bg_col)


def _prompt_buckets_t():
    b = _prompt_buckets()
    return np.concatenate([b[:WINDOW].T, b[WINDOW:].T], axis=0).astype(np.int32)


def _attn_prompt_t_kernel(qT_ref, kp_ref, kc_ref, vTp_ref, vTc_ref, bias_ref, sink_ref, gb_ref, out_ref, *, qb):
    first = pl.program_id(1) == 0
    k_all = jnp.concatenate([kp_ref[...], kc_ref[...]], axis=0)
    vT_all = jnp.concatenate([vTp_ref[...], vTc_ref[...]], axis=1)
    for i in range(qb):
        cols = slice(i * WINDOW, (i + 1) * WINDOW)
        off = pl.multiple_of(jnp.where(first, 2 * WINDOW, 0), 2 * WINDOW) if i == 0 else 0
        k2 = k_all[i * WINDOW:(i + 2) * WINDOW, :]
        vT2 = vT_all[:, i * WINDOW:(i + 2) * WINDOW]
        hd = lambda h: slice((h // GQA_GROUP) * HEAD_DIM_ATT, (h // GQA_GROUP + 1) * HEAD_DIM_ATT)
        scores = [jnp.dot(k2[:, hd(h)], qT_ref[h * HEAD_DIM_ATT:(h + 1) * HEAD_DIM_ATT, cols],
                          preferred_element_type=F32) for h in range(N_HEADS_ATT)]
        probs, rden = [], []
        for h in range(N_HEADS_ATT):
            s = scores[h] + bias_ref[h, pl.ds(off, 2 * WINDOW), :]
            sink = sink_ref[h]
            m = jnp.maximum(jnp.max(s, axis=0, keepdims=True), sink)
            p = jnp.exp(s - m)
            rden.append(1.0 / (jnp.sum(p, axis=0, keepdims=True) + jnp.exp(sink - m)))
            probs.append(p.astype(BF16))
        outs = []
        ssq = jnp.zeros((1, WINDOW), F32)
        for h in range(N_HEADS_ATT):
            o = jnp.dot(vT2[hd(h), :], probs[h], preferred_element_type=F32) * rden[h]
            outs.append(o)
            ssq = ssq + jnp.sum(o * o, axis=0, keepdims=True)
        inv = lax.rsqrt(ssq * (1.0 / ATT_W) + EPS)
        for h in range(N_HEADS_ATT):
            rows = slice(h * HEAD_DIM_ATT, (h + 1) * HEAD_DIM_ATT)
            out_ref[rows, cols] = (outs[h] * inv * gb_ref[rows, :]).astype(BF16)


def _attn_prompt_t(qT, k, vT, bias, sinks, g_b, batch, seq, qb):
    nb = seq // WINDOW
    ns = nb // qb
    prev = lambda b, j: jnp.maximum(b * nb + j * qb - 1, 0)
    return pl.pallas_call(
        functools.partial(_attn_prompt_t_kernel, qb=qb),
        grid=(batch, ns),
        in_specs=[pl.BlockSpec((ATT_W, qb * WINDOW), lambda b, j: (0, b * ns + j)),
                  pl.BlockSpec((WINDOW, KV_W), lambda b, j: (prev(b, j), 0)),
                  pl.BlockSpec((qb * WINDOW, KV_W), lambda b, j: (b * ns + j, 0)),
                  pl.BlockSpec((KV_W, WINDOW), lambda b, j: (0, prev(b, j))),
                  pl.BlockSpec((KV_W, qb * WINDOW), lambda b, j: (0, b * ns + j)),
                  _const_spec(bias.shape),
                  pl.BlockSpec(memory_space=pltpu.SMEM),
                  _const_spec(g_b.shape)],
        out_specs=pl.BlockSpec((ATT_W, qb * WINDOW), lambda b, j: (0, b * ns + j)),
        out_shape=jax.ShapeDtypeStruct((ATT_W, batch * seq), BF16),
        compiler_params=pltpu.CompilerParams(dimension_semantics=("arbitrary", "arbitrary"),
                                             vmem_limit_bytes=VMEM_LIMIT),
        name="attn_prompt_t",
    )(qT, k, k, vT, vT, bias, sinks, g_b)


def _prefix_max_lanes(x):
    n = x.shape[1]
    col = lax.broadcasted_iota(jnp.int32, x.shape, 1)
    k = 1
    while k < n:
        x = jnp.maximum(x, jnp.where(col >= k, pltpu.roll(x, k, axis=1), -jnp.inf))
        k *= 2
    return x


def _mlstm_seq_t_kernel(qT_ref, k_ref, vT_ref, omT_ref, gt_ref, gT_ref, gb_ref, mix_ref, c_out, n_out, m_out,
                        cn_s, mr_s, mc_s, *, chunk, nsub):
    c = pl.program_id(1)

    @pl.when(c == 0)
    def _():
        cn_s[...] = jnp.zeros_like(cn_s)
        mr_s[...] = jnp.zeros_like(mr_s)
        mc_s[...] = jnp.zeros_like(mc_s)

    L = chunk
    i0 = lax.broadcasted_iota(jnp.int32, (L, L), 0)
    i1 = lax.broadcasted_iota(jnp.int32, (L, L), 1)
    causal_t = i0 <= i1
    triu = causal_t.astype(F32)
    tril = (i1 <= i0).astype(F32)
    ones_row = (lax.broadcasted_iota(jnp.int32, (DV_MLSTM, L), 0) == 0).astype(BF16)
    cn = [cn_s[h] for h in range(N_HEADS_MLSTM)]
    m_r = mr_s[...]
    m_c = mc_s[...]
    hi = lax.Precision.HIGHEST
    heads = range(N_HEADS_MLSTM)
    subs = range(nsub)
    col_of = lambda j: slice(j * L, (j + 1) * L)
    hv = lambda h: slice(h * DV_MLSTM, (h + 1) * DV_MLSTM)
    hk = lambda h: slice(h * DK_MLSTM, (h + 1) * DK_MLSTM)
    a_c, a_r, b_c, b_r, pm_r, amax_c = [], [], [], [], [], []
    for j in subs:
        gt = gt_ref[col_of(j), :]
        g_t = gT_ref[:, col_of(j)]
        b_c.append(jnp.dot(tril, gt, precision=hi, preferred_element_type=F32))
        a_c.append(pltpu.roll(gt, N_HEADS_MLSTM, axis=1) - b_c[j])
        b_r.append(jnp.dot(g_t, triu, precision=hi, preferred_element_type=F32))
        a_r.append(pltpu.roll(g_t, N_HEADS_MLSTM, axis=0) - b_r[j])
    kq = [[jnp.dot(k_ref[col_of(j), hk(h)], qT_ref[hk(h), col_of(j)], preferred_element_type=F32)
           for h in heads] for j in subs]
    vT_ext = [[jnp.concatenate([vT_ref[hv(h), col_of(j)], ones_row], axis=0) for h in heads] for j in subs]
    for j in subs:
        pm_r.append(_prefix_max_lanes(a_r[j]))
        amax_c.append(jnp.max(a_c[j], axis=0, keepdims=True))
    big_m, inter, em, w_c, decay_c = [], [], [], [], []
    for j in subs:
        big_m.append(jnp.maximum(m_r, pm_r[j]))
        inter.append(jnp.exp(m_r - big_m[j]))
        em.append(jnp.exp(-(b_r[j] + big_m[j])))
        m_end_c = jnp.maximum(m_c, amax_c[j])
        w_c.append(jnp.exp(a_c[j] - m_end_c))
        decay_c.append(jnp.exp(m_c - m_end_c))
        m_c = b_c[j][L - 1:L, :] + m_end_c
        m_r = jnp.broadcast_to(b_r[j][:, L - 1:L] + big_m[j][:, L - 1:L], (N_GATES, L))
    sT = [[None] * N_HEADS_MLSTM for _ in subs]
    kw = [[None] * N_HEADS_MLSTM for _ in subs]
    for j in subs:
        for h in heads:
            r = _GATE_LANE0 + h
            d = jnp.exp(jnp.where(causal_t, a_c[j][:, r:r + 1] - big_m[j][r:r + 1, :], -jnp.inf))
            sT[j][h] = (kq[j][h] * d).astype(BF16)
            kw[j][h] = (k_ref[col_of(j), hk(h)].astype(F32) * w_c[j][:, r:r + 1]).astype(BF16)
    intra = [[jnp.dot(vT_ext[j][h], sT[j][h], preferred_element_type=F32) for h in heads] for j in subs]
    upd = [[jnp.dot(vT_ext[j][h], kw[j][h], preferred_element_type=F32) for h in heads] for j in subs]
    for j in subs:
        qc = [jnp.dot(cn[h].astype(BF16), qT_ref[hk(h), col_of(j)], preferred_element_type=F32) for h in heads]
        for h in heads:
            r = _GATE_LANE0 + h
            nd = inter[j][r:r + 1, :] * qc[h] + intra[j][h]
            rr = 1.0 / jnp.maximum(jnp.abs(nd[DV_MLSTM:DV_MLSTM + 1, :]), em[j][r:r + 1, :])
            t = omT_ref[hv(h), col_of(j)] * nd[:DV_MLSTM, :]
            msq = jnp.mean(t * t, axis=0, keepdims=True)
            scale = rr * lax.rsqrt(rr * rr * msq + EPS)
            mix_ref[hv(h), col_of(j)] = (t * scale * gb_ref[hv(h), :]).astype(BF16)
            cn[h] = decay_c[j][:, r:r + 1] * cn[h] + upd[j][h]
    for h in range(N_HEADS_MLSTM):
        cn_s[h] = cn[h]
    mr_s[...] = m_r
    mc_s[...] = m_c

    @pl.when(c == pl.num_programs(1) - 1)
    def _():
        for h in range(N_HEADS_MLSTM):
            c_out[0, h] = cn_s[h, :DV_MLSTM, :].T
            n_out[0, h] = cn_s[h, DV_MLSTM:DV_MLSTM + 1, :]
        m_out[0] = mc_s[...]


def _mlstm_seq_t(qmT, km, vmT, omT, gt, gT, g_b, nseq, t, chunk, nsub):
    step = chunk * nsub
    cps = t // step
    kern = functools.partial(_mlstm_seq_t_kernel, chunk=chunk, nsub=nsub)
    colb = lambda r: pl.BlockSpec((r, step), lambda i, c: (0, i * cps + c))
    rowb = lambda w: pl.BlockSpec((step, w), lambda i, c: (i * cps + c, 0))
    st = lambda a, d: pl.BlockSpec((1, N_HEADS_MLSTM, a, d), lambda i, c: (i, 0, 0, 0))
    sds = jax.ShapeDtypeStruct
    return pl.pallas_call(
        kern,
        grid=(nseq, cps),
        in_specs=[colb(QK_M_W), rowb(QK_M_W), colb(MLSTM_W), colb(MLSTM_W), rowb(LANES), colb(N_GATES),
                  _const_spec(g_b.shape)],
        out_specs=[colb(MLSTM_W), st(DK_MLSTM, DV_MLSTM), st(1, DK_MLSTM),
                   pl.BlockSpec((1, 1, LANES), lambda i, c: (i, 0, 0))],
        out_shape=[sds((MLSTM_W, nseq * t), BF16), sds((nseq, N_HEADS_MLSTM, DK_MLSTM, DV_MLSTM), F32),
                   sds((nseq, N_HEADS_MLSTM, 1, DK_MLSTM), F32), sds((nseq, 1, LANES), F32)],
        scratch_shapes=[pltpu.VMEM((N_HEADS_MLSTM, 2 * DV_MLSTM, DK_MLSTM), F32),
                        pltpu.VMEM((N_GATES, chunk), F32), pltpu.VMEM((1, LANES), F32)],
        compiler_params=pltpu.CompilerParams(dimension_semantics=("arbitrary", "arbitrary"),
                                             vmem_limit_bytes=VMEM_LIMIT),
        name="mlstm_seq_t",
    )(qmT, km, vmT, omT, gt, gT, g_b)


def _out_mlp_kernel(x_ref, a_ref, hm_ref, woa_ref, wom_ref, gmlp_ref, wup_ref, wdn_ref, gfin_ref, y_ref, acc_ref,
                    *, fchunk, mix_t):
    mix_dims = (((0,), (0,)), ((), ())) if mix_t else (((1,), (0,)), ((), ()))
    x1 = (x_ref[...]
          + lax.dot_general(a_ref[...], woa_ref[...], mix_dims, preferred_element_type=F32)
          + lax.dot_general(hm_ref[...], wom_ref[...], mix_dims, preferred_element_type=F32))
    h = (x1 * lax.rsqrt(jnp.mean(x1 * x1, axis=-1, keepdims=True) + EPS) * gmlp_ref[...]).astype(BF16)
    acc_ref[...] = x1

    def ff_chunk(f, carry):
        lo = pl.multiple_of(f * fchunk, fchunk)
        u = jnp.maximum(jnp.dot(h, wup_ref[:, pl.ds(lo, fchunk)], preferred_element_type=F32), 0.0)
        acc_ref[...] += jnp.dot((u * u).astype(BF16), wdn_ref[pl.ds(lo, fchunk), :], preferred_element_type=F32)
        return carry

    lax.fori_loop(0, D_FF // fchunk, ff_chunk, 0)
    y = acc_ref[...]
    y_ref[...] = y * lax.rsqrt(jnp.mean(y * y, axis=-1, keepdims=True) + EPS) * gfin_ref[...]


def _out_mlp(x2d, mix_a, mix_m, wo_a, wo_m, g_mlp, w_up, w_dn, g_fin, tm, fchunk, mix_t):
    n = x2d.shape[0]
    row = lambda w: pl.BlockSpec((tm, w), lambda i: (i, 0))
    mix = (lambda w: pl.BlockSpec((w, tm), lambda i: (0, i))) if mix_t else row
    return pl.pallas_call(
        functools.partial(_out_mlp_kernel, fchunk=fchunk, mix_t=mix_t),
        grid=(n // tm,),
        in_specs=[row(D_MODEL), mix(ATT_W), mix(MLSTM_W), _const_spec(wo_a.shape), _const_spec(wo_m.shape),
                  _const_spec((1, D_MODEL)), _const_spec(w_up.shape), _const_spec(w_dn.shape),
                  _const_spec((1, D_MODEL))],
        out_specs=row(D_MODEL),
        out_shape=jax.ShapeDtypeStruct((n, D_MODEL), F32),
        scratch_shapes=[pltpu.VMEM((tm, D_MODEL), F32)],
        compiler_params=pltpu.CompilerParams(dimension_semantics=("arbitrary",), vmem_limit_bytes=VMEM_LIMIT),
        name="out_mlp",
    )(x2d, mix_a, mix_m, wo_a, wo_m, g_mlp, w_up, w_dn, g_fin)


def kernel(x_prompt, x_sample, cache_k_win, cache_v_win, state_C, state_n, state_m, rel_bias, norm_mix, w_in,
           b_gates, attn_sinks, norm_attn_out, norm_mlstm_out, w_out, norm_mlp, w_up, w_down, norm_final):
    depth = w_in.shape[0]
    assert depth == 1, "single-layer trunk"
    bp, sp = x_prompt.shape[:2]
    bs, ts = x_sample.shape[:2]
    wc = cache_k_win.shape[2]
    l = 0

    w_pad = jnp.pad(w_in[l], ((0, 0), (0, MAIN_COLS + LANES - w_in.shape[2]))).astype(BF16)
    bg_pad = jnp.pad(b_gates[l], (0, LANES - N_GATES)).reshape(1, LANES)
    wo_a = w_out[l, :ATT_W].astype(BF16)
    wo_m = w_out[l, ATT_W:].astype(BF16)
    wup = w_up[l].astype(BF16)
    wdn = w_down[l].astype(BF16)
    g_mix = norm_mix[l].reshape(1, D_MODEL)
    g_att = norm_attn_out[l].reshape(1, ATT_W)
    g_m = norm_mlstm_out[l].reshape(1, MLSTM_W)
    g_mlp = norm_mlp[l].reshape(1, D_MODEL)
    g_fin = norm_final.reshape(1, D_MODEL)
    sinks = attn_sinks[l]

    kpad = ((wc + ts + 15) // 16) * 16
    bias_s = _bias_table(rel_bias, _sample_buckets(ts, wc, kpad))
    bias_s = bias_s.reshape(N_KV_HEADS, GQA_GROUP * ts, kpad)
    sink_rows = jnp.repeat(sinks, ts).reshape(N_KV_HEADS, GQA_GROUP * ts, 1)

    xp = x_prompt.reshape(bp * sp, D_MODEL)
    wl = w_in[l]
    cols = lambda lo, n: wl[:, lo:lo + n]
    w_n = jnp.concatenate([cols(_O_KV, 2 * KV_W), cols(_O_QKM + QK_M_W, QK_M_W),
                           jnp.pad(cols(_O_G, N_GATES), ((0, 0), (0, LANES - N_GATES)))], axis=1).astype(BF16)
    w_t = jnp.concatenate([cols(_O_QA, ATT_W), cols(_O_KV + KV_W, KV_W), cols(_O_QKM, QK_M_W), cols(_O_VM, MLSTM_W),
                           cols(_O_OM, MLSTM_W), jnp.pad(cols(_O_G, N_GATES), ((0, 0), (0, 16 - N_GATES)))],
                          axis=1).T.astype(BF16)
    chunk = 128
    gb_att = jnp.broadcast_to(norm_attn_out[l].reshape(ATT_W, 1), (ATT_W, WINDOW))
    gb_m = jnp.broadcast_to(norm_mlstm_out[l].reshape(MLSTM_W, 1), (MLSTM_W, chunk))
    k_a, kvf, km, gt, qT, vT, qmT, vmT, omT, gT = _in_proj_t(xp, g_mix, w_n, w_t, bg_pad,
                                                             b_gates[l].reshape(N_GATES, 1), tm=1024)
    bias_t = _bias_table(rel_bias, _prompt_buckets_t())
    mix_a = _attn_prompt_t(qT, k_a, vT, bias_t, sinks, gb_att, bp, sp, qb=8)
    mix_m, c_p, n_p, m_p = _mlstm_seq_t(qmT, km, vmT, omT, gt, gT, gb_m, bp, sp, chunk=chunk, nsub=4)
    m_p = m_p[:, 0, _GATE_LANE0:_GATE_LANE0 + N_HEADS_MLSTM]
    y_p = _out_mlp(xp, mix_a, mix_m, wo_a, wo_m, g_mlp, wup, wdn, g_fin, tm=512, fchunk=4096, mix_t=True)
    win_p = min(WINDOW, sp)
    kv_win = kvf.reshape(bp, sp, 2 * KV_W)[:, sp - win_p:, :]
    k_win_p = kv_win[:, :, :KV_W].reshape(1, bp, win_p, N_KV_HEADS, HEAD_DIM_ATT)
    v_win_p = kv_win[:, :, KV_W:].reshape(1, bp, win_p, N_KV_HEADS, HEAD_DIM_ATT)

    xs = x_sample.reshape(bs * ts, D_MODEL)
    qa, kvb, kvf, qkm, vm, om, gt = _in_proj(xs, g_mix, w_pad, bg_pad, tm=256)
    ck = cache_k_win[l].reshape(bs, wc, KV_W)
    cv = cache_v_win[l].reshape(bs, wc, KV_W)
    mix_a, kw_s, vw_s = _attn_sample(qa, kvf, ck, cv, bias_s, sink_rows, g_att, bs, ts, bb=8)
    r3 = lambda a: a.reshape(bs, ts, a.shape[-1])
    mix_m, c_s, n_s, m_s = _mlstm(r3(qkm), r3(vm), r3(om), r3(gt), g_m, state_C[l],
                                  state_n[l].reshape(bs, N_HEADS_MLSTM, 1, DK_MLSTM),
                                  state_m[l].reshape(bs, N_HEADS_MLSTM, 1, 1), chunk=math.gcd(ts, 64), bb=8)
    y_s = _out_mlp(xs, mix_a, mix_m.reshape(bs * ts, MLSTM_W), wo_a, wo_m, g_mlp, wup, wdn, g_fin, tm=256, fchunk=512,
                   mix_t=False)

    return (y_p.reshape(bp, sp, D_MODEL), y_s.reshape(bs, ts, D_MODEL),
            k_win_p, v_win_p,
            c_p[None], n_p.reshape(1, bp, N_HEADS_MLSTM, DK_MLSTM), m_p.reshape(1, bp, N_HEADS_MLSTM),
            kw_s.reshape(1, bs, wc, N_KV_HEADS, HEAD_DIM_ATT), vw_s.reshape(1, bs, wc, N_KV_HEADS, HEAD_DIM_ATT),
            c_s[None], n_s.reshape(1, bs, N_HEADS_MLSTM, DK_MLSTM), m_s.reshape(1, bs, N_HEADS_MLSTM))
```

```python
import functools
import math

import numpy as np
import jax
import jax.numpy as jnp
from jax import lax
from jax.experimental import pallas as pl
from jax.experimental.pallas import tpu as pltpu

D_MODEL = 1024
N_HEADS_ATT = 8
N_KV_HEADS = 2
GQA_GROUP = N_HEADS_ATT // N_KV_HEADS
HEAD_DIM_ATT = 64
ATT_W = N_HEADS_ATT * HEAD_DIM_ATT
KV_W = N_KV_HEADS * HEAD_DIM_ATT
WINDOW = 128
NUM_BUCKETS = 32
MAX_DISTANCE = 128
N_HEADS_MLSTM = 4
DV_MLSTM = 128
DK_MLSTM = 64
MLSTM_W = N_HEADS_MLSTM * DV_MLSTM
QK_M_W = N_HEADS_MLSTM * DK_MLSTM
D_FF = 4 * D_MODEL
EPS = 1e-6
N_GATES = 2 * N_HEADS_MLSTM
LANES = 128
MAIN_COLS = ATT_W + 2 * KV_W + 2 * QK_M_W + 2 * MLSTM_W
VMEM_LIMIT = 56 * 1024 * 1024

F32 = jnp.float32
BF16 = jnp.bfloat16

_O_QA = 0
_O_KV = _O_QA + ATT_W
_O_QKM = _O_KV + 2 * KV_W
_O_VM = _O_QKM + 2 * QK_M_W
_O_OM = _O_VM + MLSTM_W
_O_G = _O_OM + MLSTM_W


def _const_spec(shape):
    nd = len(shape)
    return pl.BlockSpec(shape, lambda *_: (0,) * nd, pipeline_mode=pl.Buffered(1))


def _in_proj_kernel(x_ref, g_ref, w_ref, bg_ref, qa_ref, kvb_ref, kvf_ref, qkm_ref, vm_ref, om_ref, gt_ref):
    xf = x_ref[...]
    h = xf * lax.rsqrt(jnp.mean(xf * xf, axis=-1, keepdims=True) + EPS) * g_ref[...]
    hb = h.astype(BF16)

    def proj(lo, width):
        return jnp.dot(hb, w_ref[:, lo:lo + width], preferred_element_type=F32)

    qa_ref[...] = (proj(_O_QA, ATT_W) * (HEAD_DIM_ATT ** -0.5)).astype(BF16)
    kv = proj(_O_KV, 2 * KV_W)
    kvf_ref[...] = kv
    kvb_ref[...] = kv.astype(BF16)
    qkm = proj(_O_QKM, 2 * QK_M_W)
    qkm_ref[:, :QK_M_W] = qkm[:, :QK_M_W].astype(BF16)
    qkm_ref[:, QK_M_W:] = (qkm[:, QK_M_W:] * (DK_MLSTM ** -0.5)).astype(BF16)
    vm_ref[...] = proj(_O_VM, MLSTM_W).astype(BF16)
    om_ref[...] = jax.nn.sigmoid(proj(_O_OM, MLSTM_W))
    g = proj(_O_G, LANES) + bg_ref[...]
    lf = jnp.minimum(g, 0.0) - jnp.log1p(jnp.exp(-jnp.abs(g)))
    col = lax.broadcasted_iota(jnp.int32, g.shape, 1)
    gt_ref[...] = jnp.where(col < N_HEADS_MLSTM, g, lf)


def _in_proj(x2d, g_norm, w_pad, bg_pad, tm):
    n = x2d.shape[0]
    row = lambda w: pl.BlockSpec((tm, w), lambda i: (i, 0))
    return pl.pallas_call(
        _in_proj_kernel,
        grid=(n // tm,),
        in_specs=[row(D_MODEL), _const_spec((1, D_MODEL)), _const_spec(w_pad.shape), _const_spec((1, LANES))],
        out_specs=[row(ATT_W), row(2 * KV_W), row(2 * KV_W), row(2 * QK_M_W), row(MLSTM_W), row(MLSTM_W), row(LANES)],
        out_shape=[jax.ShapeDtypeStruct((n, ATT_W), BF16), jax.ShapeDtypeStruct((n, 2 * KV_W), BF16),
                   jax.ShapeDtypeStruct((n, 2 * KV_W), F32), jax.ShapeDtypeStruct((n, 2 * QK_M_W), BF16),
                   jax.ShapeDtypeStruct((n, MLSTM_W), BF16), jax.ShapeDtypeStruct((n, MLSTM_W), F32),
                   jax.ShapeDtypeStruct((n, LANES), F32)],
        compiler_params=pltpu.CompilerParams(dimension_semantics=("arbitrary",), vmem_limit_bytes=VMEM_LIMIT),
        name="in_proj",
    )(x2d, g_norm, w_pad, bg_pad)


def _t5_bucket_np(dist):
    max_exact = NUM_BUCKETS // 2
    d = np.maximum(dist, 0)
    ratio = np.maximum(d, 1).astype(np.float32) / np.float32(max_exact)
    large = max_exact + (np.log(ratio) / np.float32(math.log(MAX_DISTANCE / max_exact))
                         * np.float32(NUM_BUCKETS - max_exact)).astype(np.int32)
    large = np.minimum(large, NUM_BUCKETS - 1)
    return np.where(d < max_exact, d, large).astype(np.int32)


def _bias_table_kernel(rb_ref, bucket_ref, out_ref):
    h = pl.program_id(0)
    bucket = bucket_ref[...]
    acc = jnp.full(bucket.shape, -jnp.inf, F32)
    for j in range(NUM_BUCKETS):
        acc = jnp.where(bucket == j, rb_ref[j, h], acc)
    out_ref[0] = acc


def _bias_table(rel_bias, bucket):
    r, c = bucket.shape
    return pl.pallas_call(
        _bias_table_kernel,
        grid=(N_HEADS_ATT,),
        in_specs=[pl.BlockSpec(memory_space=pltpu.SMEM), pl.BlockSpec((r, c), lambda h: (0, 0))],
        out_specs=pl.BlockSpec((1, r, c), lambda h: (h, 0, 0)),
        out_shape=jax.ShapeDtypeStruct((N_HEADS_ATT, r, c), F32),
        compiler_params=pltpu.CompilerParams(dimension_semantics=("arbitrary",)),
        name="bias_table",
    )(rel_bias, jnp.asarray(bucket))


def _prompt_buckets():
    qi = np.arange(WINDOW)[:, None]
    kj = np.arange(2 * WINDOW)[None, :]
    dist = qi + WINDOW - kj
    in_win = (dist >= 0) & (dist <= WINDOW)
    b = _t5_bucket_np(dist)
    general = np.where(in_win, b, -1)
    first = np.where(in_win & (kj >= WINDOW), b, -1)
    return np.concatenate([general, first], axis=0).astype(np.int32)


def _sample_buckets(t, w, padded):
    dist = (w + np.arange(t))[:, None] - np.arange(padded)[None, :]
    valid = (dist >= 0) & (dist <= WINDOW) & (np.arange(padded)[None, :] < w + t)
    return np.where(valid, _t5_bucket_np(dist), -1).astype(np.int32)


def _attn_prompt_kernel(q_ref, kvp_ref, kvc_ref, bias_ref, sink_ref, g_ref, out_ref):
    first = pl.program_id(1) == 0
    off = pl.multiple_of(jnp.where(first, WINDOW, 0), WINDOW)
    kv = jnp.concatenate([kvp_ref[...], kvc_ref[...]], axis=0)
    outs = []
    ssq = jnp.zeros((WINDOW, 1), F32)
    for h in range(N_HEADS_ATT):
        g = h // GQA_GROUP
        q = q_ref[:, h * HEAD_DIM_ATT:(h + 1) * HEAD_DIM_ATT]
        k = kv[:, g * HEAD_DIM_ATT:(g + 1) * HEAD_DIM_ATT]
        v = kv[:, KV_W + g * HEAD_DIM_ATT:KV_W + (g + 1) * HEAD_DIM_ATT]
        s = lax.dot_general(q, k, (((1,), (1,)), ((), ())), preferred_element_type=F32)
        s = s + bias_ref[h, pl.ds(off, WINDOW), :]
        sink = sink_ref[h]
        m = jnp.maximum(jnp.max(s, axis=1, keepdims=True), sink)
        p = jnp.exp(s - m)
        denom = jnp.sum(p, axis=1, keepdims=True) + jnp.exp(sink - m)
        p = p * (1.0 / denom)
        o = jnp.dot(p.astype(BF16), v, preferred_element_type=F32)
        outs.append(o)
        ssq = ssq + jnp.sum(o * o, axis=1, keepdims=True)
    inv = lax.rsqrt(ssq * (1.0 / ATT_W) + EPS)
    for h in range(N_HEADS_ATT):
        sl = slice(h * HEAD_DIM_ATT, (h + 1) * HEAD_DIM_ATT)
        out_ref[:, sl] = (outs[h] * inv * g_ref[:, sl]).astype(BF16)


def _attn_prompt(qa, kvb, bias, sinks, g_att, batch, seq):
    nb = seq // WINDOW
    return pl.pallas_call(
        _attn_prompt_kernel,
        grid=(batch, nb),
        in_specs=[pl.BlockSpec((WINDOW, ATT_W), lambda b, j: (b * nb + j, 0)),
                  pl.BlockSpec((WINDOW, 2 * KV_W), lambda b, j: (b * nb + jnp.maximum(j - 1, 0), 0)),
                  pl.BlockSpec((WINDOW, 2 * KV_W), lambda b, j: (b * nb + j, 0)),
                  _const_spec(bias.shape),
                  pl.BlockSpec(memory_space=pltpu.SMEM),
                  _const_spec((1, ATT_W))],
        out_specs=pl.BlockSpec((WINDOW, ATT_W), lambda b, j: (b * nb + j, 0)),
        out_shape=jax.ShapeDtypeStruct((batch * seq, ATT_W), BF16),
        compiler_params=pltpu.CompilerParams(dimension_semantics=("arbitrary", "arbitrary"),
                                             vmem_limit_bytes=VMEM_LIMIT),
        name="attn_prompt",
    )(qa, kvb, kvb, bias, sinks, g_att)


def _attn_sample_kernel(q_ref, kvf_ref, ck_ref, cv_ref, bias_ref, sink_ref, g_ref, out_ref, kw_ref, vw_ref,
                        *, bb, t, w, kpad):
    seqs = range(bb)
    groups = range(N_KV_HEADS)
    tok = lambda b: slice(b * t, (b + 1) * t)
    gd = lambda g: slice(g * HEAD_DIM_ATT, (g + 1) * HEAD_DIM_ATT)
    zpad = jnp.zeros((kpad - w - t, KV_W), F32)
    kk, vv = [], []
    for b in seqs:
        ck = ck_ref[b]
        cv = cv_ref[b]
        k_new = kvf_ref[tok(b), :KV_W]
        v_new = kvf_ref[tok(b), KV_W:]
        kw_ref[b, :w - t, :] = ck[t:, :]
        kw_ref[b, w - t:, :] = k_new
        vw_ref[b, :w - t, :] = cv[t:, :]
        vw_ref[b, w - t:, :] = v_new
        kk.append(jnp.concatenate([ck, k_new, zpad], axis=0).astype(BF16))
        vv.append(jnp.concatenate([cv, v_new, zpad], axis=0).astype(BF16))
    scores = []
    for b in seqs:
        qb = q_ref[tok(b), :].astype(F32)
        for g in groups:
            qg = jnp.concatenate(
                [qb[:, (g * GQA_GROUP + i) * HEAD_DIM_ATT:(g * GQA_GROUP + i + 1) * HEAD_DIM_ATT]
                 for i in range(GQA_GROUP)], axis=0).astype(BF16)
            scores.append(lax.dot_general(qg, kk[b][:, gd(g)], (((1,), (1,)), ((), ())),
                                          preferred_element_type=F32))
    probs = []
    for b in seqs:
        for g in groups:
            s = scores[b * N_KV_HEADS + g] + bias_ref[g]
            sink = sink_ref[g]
            m = jnp.maximum(jnp.max(s, axis=1, keepdims=True), sink)
            p = jnp.exp(s - m)
            denom = jnp.sum(p, axis=1, keepdims=True) + jnp.exp(sink - m)
            probs.append((p * (1.0 / denom)).astype(BF16))
    outs = [jnp.dot(probs[b * N_KV_HEADS + g], vv[b][:, gd(g)], preferred_element_type=F32)
            for b in seqs for g in groups]
    for b in seqs:
        heads = [outs[b * N_KV_HEADS + g][i * t:(i + 1) * t, :] for g in groups for i in range(GQA_GROUP)]
        ssq = jnp.zeros((t, 1), F32)
        for oi in heads:
            ssq = ssq + jnp.sum(oi * oi, axis=1, keepdims=True)
        inv = lax.rsqrt(ssq * (1.0 / ATT_W) + EPS)
        for h in range(N_HEADS_ATT):
            sl = slice(h * HEAD_DIM_ATT, (h + 1) * HEAD_DIM_ATT)
            out_ref[tok(b), sl] = (heads[h] * inv * g_ref[:, sl]).astype(BF16)


def _attn_sample(qa, kvf, ck, cv, bias, sink_rows, g_att, nseq, t, bb):
    w = ck.shape[1]
    kpad = bias.shape[-1]
    rows = GQA_GROUP * t
    kern = functools.partial(_attn_sample_kernel, bb=bb, t=t, w=w, kpad=kpad)
    tok = lambda wd: pl.BlockSpec((bb * t, wd), lambda i: (i, 0))
    cache = pl.BlockSpec((bb, w, KV_W), lambda i: (i, 0, 0))
    return pl.pallas_call(
        kern,
        grid=(nseq // bb,),
        in_specs=[tok(ATT_W), tok(2 * KV_W), cache, cache,
                  _const_spec((N_KV_HEADS, rows, kpad)), _const_spec((N_KV_HEADS, rows, 1)), _const_spec((1, ATT_W))],
        out_specs=[tok(ATT_W), cache, cache],
        out_shape=[jax.ShapeDtypeStruct((nseq * t, ATT_W), BF16),
                   jax.ShapeDtypeStruct((nseq, w, KV_W), F32), jax.ShapeDtypeStruct((nseq, w, KV_W), F32)],
        compiler_params=pltpu.CompilerParams(dimension_semantics=("arbitrary",), vmem_limit_bytes=VMEM_LIMIT),
        name="attn_sample",
    )(qa, kvf, ck, cv, bias, sink_rows, g_att)


def _mlstm_kernel(qk_ref, v_ref, om_ref, gt_ref, gm_ref, c0_ref, n0_ref, m0_ref,
                  mix_ref, c_out, n_out, m_out, c_s, n_s, m_s, *, chunk, bb):
    c = pl.program_id(1)

    @pl.when(c == 0)
    def _():
        c_s[...] = c0_ref[...]
        n_s[...] = n0_ref[...]
        m_s[...] = m0_ref[...]

    L = chunk
    t_idx = lax.broadcasted_iota(jnp.int32, (L, L), 0)
    s_idx = lax.broadcasted_iota(jnp.int32, (L, L), 1)
    causal = s_idx <= t_idx
    eye = s_idx == t_idx
    for b in range(bb):
        for h in range(N_HEADS_MLSTM):
            q = qk_ref[b, :, h * DK_MLSTM:(h + 1) * DK_MLSTM]
            k = qk_ref[b, :, QK_M_W + h * DK_MLSTM:QK_M_W + (h + 1) * DK_MLSTM]
            v = v_ref[b, :, h * DV_MLSTM:(h + 1) * DV_MLSTM]
            ig_col = gt_ref[b, :, h:h + 1]
            lf_col = gt_ref[b, :, N_HEADS_MLSTM + h:N_HEADS_MLSTM + h + 1]
            c_prev = c_s[b, h]
            n_prev = n_s[b, h]
            m_prev = m_s[b, h]
            b_row = jnp.sum(jnp.where(t_idx <= s_idx, lf_col, 0.0), axis=0, keepdims=True)
            b_col = jnp.sum(jnp.where(eye, b_row, 0.0), axis=1, keepdims=True)
            ig_row = jnp.sum(jnp.where(eye, ig_col, 0.0), axis=0, keepdims=True)
            log_d = jnp.where(causal, b_col - b_row + ig_row, -jnp.inf)
            m_t = jnp.maximum(b_col + m_prev, jnp.max(log_d, axis=1, keepdims=True))
            d = jnp.exp(log_d - m_t)
            inter = jnp.exp(b_col + m_prev - m_t)
            s = lax.dot_general(q, k, (((1,), (1,)), ((), ())), preferred_element_type=F32) * d
            num = (inter * jnp.dot(q, c_prev.astype(BF16), preferred_element_type=F32)
                   + jnp.dot(s.astype(BF16), v, preferred_element_type=F32))
            qn = jnp.sum(q.astype(F32) * n_prev, axis=1, keepdims=True)
            den = inter * qn + jnp.sum(s, axis=1, keepdims=True)
            hh = num / jnp.maximum(jnp.abs(den), jnp.exp(-m_t))
            hm = om_ref[b, :, h * DV_MLSTM:(h + 1) * DV_MLSTM] * hh
            y = hm * lax.rsqrt(jnp.mean(hm * hm, axis=1, keepdims=True) + EPS)
            y = y * gm_ref[:, h * DV_MLSTM:(h + 1) * DV_MLSTM]
            mix_ref[b, :, h * DV_MLSTM:(h + 1) * DV_MLSTM] = y.astype(BF16)
            m_new = m_t[L - 1:L, :]
            b_end = b_col[L - 1:L, :]
            w_col = jnp.exp(b_end - b_col + ig_col - m_new)
            decay = jnp.exp(b_end + m_prev - m_new)
            kw = k.astype(F32) * w_col
            u = lax.dot_general(kw.astype(BF16), v, (((0,), (0,)), ((), ())), preferred_element_type=F32)
            c_s[b, h] = decay * c_prev + u
            n_s[b, h] = decay * n_prev + jnp.sum(kw, axis=0, keepdims=True)
            m_s[b, h] = m_new

    @pl.when(c == pl.num_programs(1) - 1)
    def _():
        c_out[...] = c_s[...]
        n_out[...] = n_s[...]
        m_out[...] = m_s[...]


def _mlstm(qkm, vm, om, gt, g_m, c0, n0, m0, chunk, bb):
    nseq, t = qkm.shape[:2]
    nc = t // chunk
    kern = functools.partial(_mlstm_kernel, chunk=chunk, bb=bb)
    tok = lambda wd: pl.BlockSpec((bb, chunk, wd), lambda i, c: (i, c, 0))
    st = lambda a, d: pl.BlockSpec((bb, N_HEADS_MLSTM, a, d), lambda i, c: (i, 0, 0, 0))
    return pl.pallas_call(
        kern,
        grid=(nseq // bb, nc),
        in_specs=[tok(2 * QK_M_W), tok(MLSTM_W), tok(MLSTM_W), tok(LANES), _const_spec((1, MLSTM_W)),
                  st(DK_MLSTM, DV_MLSTM), st(1, DK_MLSTM), st(1, 1)],
        out_specs=[tok(MLSTM_W), st(DK_MLSTM, DV_MLSTM), st(1, DK_MLSTM), st(1, 1)],
        out_shape=[jax.ShapeDtypeStruct((nseq, t, MLSTM_W), BF16),
                   jax.ShapeDtypeStruct((nseq, N_HEADS_MLSTM, DK_MLSTM, DV_MLSTM), F32),
                   jax.ShapeDtypeStruct((nseq, N_HEADS_MLSTM, 1, DK_MLSTM), F32),
                   jax.ShapeDtypeStruct((nseq, N_HEADS_MLSTM, 1, 1), F32)],
        scratch_shapes=[pltpu.VMEM((bb, N_HEADS_MLSTM, DK_MLSTM, DV_MLSTM), F32),
                        pltpu.VMEM((bb, N_HEADS_MLSTM, 1, DK_MLSTM), F32),
                        pltpu.VMEM((bb, N_HEADS_MLSTM, 1, 1), F32)],
        compiler_params=pltpu.CompilerParams(dimension_semantics=("arbitrary", "arbitrary"),
                                             vmem_limit_bytes=VMEM_LIMIT),
        name="mlstm",
    )(qkm, vm, om, gt, g_m, c0, n0, m0)


_GATE_LANE0 = N_HEADS_MLSTM


def _prefix_max_rows(x):
    n = x.shape[0]
    row = lax.broadcasted_iota(jnp.int32, x.shape, 0)
    k = 1
    while k < n:
        x = jnp.maximum(x, jnp.where(row >= k, pltpu.roll(x, k, axis=0), -jnp.inf))
        k *= 2
    return x


def _mlstm_seq_kernel(qk_ref, v_ref, om_ref, gt_ref, gm_ref, mix_ref, c_out, n_out, m_out, cn_s, m_s,
                      *, chunk, nsub, bb):
    c = pl.program_id(1)

    @pl.when(c == 0)
    def _():
        cn_s[...] = jnp.zeros_like(cn_s)
        m_s[...] = jnp.zeros_like(m_s)

    L = chunk
    t_idx = lax.broadcasted_iota(jnp.int32, (L, L), 0)
    s_idx = lax.broadcasted_iota(jnp.int32, (L, L), 1)
    causal = s_idx <= t_idx
    tril = causal.astype(F32)
    ones_col = (lax.broadcasted_iota(jnp.int32, (L, DV_MLSTM), 1) == 0).astype(BF16)
    for bi in range(bb):
        m_prev = m_s[bi]
        cn = [cn_s[bi, h] for h in range(N_HEADS_MLSTM)]
        for j in range(nsub):
            rows = slice(j * L, (j + 1) * L)
            gt = gt_ref[bi, rows, :]
            b = jnp.dot(tril, gt, precision=lax.Precision.HIGHEST, preferred_element_type=F32)
            a = pltpu.roll(gt, N_HEADS_MLSTM, axis=1) - b
            big_m = jnp.maximum(m_prev, _prefix_max_rows(a))
            inter = jnp.exp(m_prev - big_m)
            em = jnp.exp(-(b + big_m))
            m_end = big_m[L - 1:L, :]
            decay = jnp.exp(m_prev - m_end)
            m_prev = b[L - 1:L, :] + m_end
            a_t = a.T
            w_t = jnp.exp(a_t - m_end.T)
            for h in range(N_HEADS_MLSTM):
                ln = _GATE_LANE0 + h
                hv = slice(h * DV_MLSTM, (h + 1) * DV_MLSTM)
                q = qk_ref[bi, rows, h * DK_MLSTM:(h + 1) * DK_MLSTM]
                k = qk_ref[bi, rows, QK_M_W + h * DK_MLSTM:QK_M_W + (h + 1) * DK_MLSTM]
                v_ext = jnp.concatenate([v_ref[bi, rows, hv], ones_col], axis=1)
                d = jnp.exp(jnp.where(causal, a_t[ln:ln + 1, :] - big_m[:, ln:ln + 1], -jnp.inf))
                s = (lax.dot_general(q, k, (((1,), (1,)), ((), ())), preferred_element_type=F32) * d).astype(BF16)
                nd = (inter[:, ln:ln + 1] * jnp.dot(q, cn[h].astype(BF16), preferred_element_type=F32)
                      + jnp.dot(s, v_ext, preferred_element_type=F32))
                r = 1.0 / jnp.maximum(jnp.abs(nd[:, DV_MLSTM:DV_MLSTM + 1]), em[:, ln:ln + 1])
                t = om_ref[bi, rows, hv] * nd[:, :DV_MLSTM]
                msq = jnp.mean(t * t, axis=1, keepdims=True)
                scale = r * lax.rsqrt(r * r * msq + EPS)
                mix_ref[bi, rows, hv] = (t * scale * gm_ref[:, hv]).astype(BF16)
                kw_t = (k.astype(F32).T * w_t[ln:ln + 1, :]).astype(BF16)
                cn[h] = decay[:, ln:ln + 1] * cn[h] + jnp.dot(kw_t, v_ext, preferred_element_type=F32)
        m_s[bi] = m_prev
        for h in range(N_HEADS_MLSTM):
            cn_s[bi, h] = cn[h]

    @pl.when(c == pl.num_programs(1) - 1)
    def _():
        c_out[...] = cn_s[:, :, :, :DV_MLSTM]
        n_out[...] = cn_s[:, :, :, DV_MLSTM:DV_MLSTM + 1]
        m_out[...] = m_s[...]


def _mlstm_seq(qkm, vm, om, gt, g_m, chunk, nsub, bb):
    nseq, t = qkm.shape[:2]
    step = chunk * nsub
    kern = functools.partial(_mlstm_seq_kernel, chunk=chunk, nsub=nsub, bb=bb)
    tok = lambda wd: pl.BlockSpec((bb, step, wd), lambda i, c: (i, c, 0))
    st = lambda a, d: pl.BlockSpec((bb, N_HEADS_MLSTM, a, d), lambda i, c: (i, 0, 0, 0))
    return pl.pallas_call(
        kern,
        grid=(nseq // bb, t // step),
        in_specs=[tok(2 * QK_M_W), tok(MLSTM_W), tok(MLSTM_W), tok(LANES), _const_spec((1, MLSTM_W))],
        out_specs=[tok(MLSTM_W), st(DK_MLSTM, DV_MLSTM), st(DK_MLSTM, 1),
                   pl.BlockSpec((bb, 1, LANES), lambda i, c: (i, 0, 0))],
        out_shape=[jax.ShapeDtypeStruct((nseq, t, MLSTM_W), BF16),
                   jax.ShapeDtypeStruct((nseq, N_HEADS_MLSTM, DK_MLSTM, DV_MLSTM), F32),
                   jax.ShapeDtypeStruct((nseq, N_HEADS_MLSTM, DK_MLSTM, 1), F32),
                   jax.ShapeDtypeStruct((nseq, 1, LANES), F32)],
        scratch_shapes=[pltpu.VMEM((bb, N_HEADS_MLSTM, DK_MLSTM, 2 * DV_MLSTM), F32),
                        pltpu.VMEM((bb, 1, LANES), F32)],
        compiler_params=pltpu.CompilerParams(dimension_semantics=("arbitrary", "arbitrary"),
                                             vmem_limit_bytes=VMEM_LIMIT),
        name="mlstm_seq",
    )(qkm, vm, om, gt, g_m)


SEQ_PER_BLOCK = 16


def _seg_scan_rows(x, seg, forward):
    n = x.shape[0]
    pos = lax.broadcasted_iota(jnp.int32, x.shape, 0) % seg
    k = 1
    while k < seg:
        if forward:
            x = jnp.maximum(x, jnp.where(pos >= k, pltpu.roll(x, k, axis=0), -jnp.inf))
        else:
            x = jnp.maximum(x, jnp.where(pos < seg - k, pltpu.roll(x, n - k, axis=0), -jnp.inf))
        k *= 2
    return x


def _mlstm_blk_kernel(qk_ref, v_ref, om_ref, gt_ref, gm_ref, c0_ref, n0_ref, nrow_ref, m0_ref,
                      mix_ref, c_out, n_out, m_out, *, t):
    S = SEQ_PER_BLOCK
    L = S * t
    hi = lax.Precision.HIGHEST
    ti = lax.broadcasted_iota(jnp.int32, (L, L), 0)
    si = lax.broadcasted_iota(jnp.int32, (L, L), 1)
    same = (ti // t) == (si // t)
    causal = same & (si <= ti)
    tril_seg = causal.astype(F32)
    ones_seg = same.astype(F32)
    ones_col = (lax.broadcasted_iota(jnp.int32, (L, DV_MLSTM), 1) == 0).astype(BF16)
    bd_mask = (lax.broadcasted_iota(jnp.int32, (L, S * DK_MLSTM), 0) // t
               == lax.broadcasted_iota(jnp.int32, (L, S * DK_MLSTM), 1) // DK_MLSTM)
    bd_mask_t = (lax.broadcasted_iota(jnp.int32, (S * DK_MLSTM, L), 0) // DK_MLSTM
                 == lax.broadcasted_iota(jnp.int32, (S * DK_MLSTM, L), 1) // t)
    heads = range(N_HEADS_MLSTM)
    hk = lambda h: slice(h * DK_MLSTM, (h + 1) * DK_MLSTM)
    hkk = lambda h: slice(QK_M_W + h * DK_MLSTM, QK_M_W + (h + 1) * DK_MLSTM)
    hv = lambda h: slice(h * DV_MLSTM, (h + 1) * DV_MLSTM)

    gt = gt_ref[...]
    m_prev = m0_ref[...]
    b = jnp.dot(tril_seg, gt, precision=hi, preferred_element_type=F32)
    b_end = jnp.dot(ones_seg, gt, precision=hi, preferred_element_type=F32)
    a = pltpu.roll(gt, N_HEADS_MLSTM, axis=1) - b
    big_m = jnp.maximum(m_prev, _seg_scan_rows(a, t, True))
    inter = jnp.exp(m_prev - big_m)
    em = jnp.exp(-(b + big_m))
    m_end = _seg_scan_rows(big_m, t, False)
    w = jnp.exp(a - m_end)
    decay = jnp.exp(m_prev - m_end)
    m_out[...] = b_end + m_end
    a_t = a.T
    q = [qk_ref[:, hk(h)] for h in heads]
    k = [qk_ref[:, hkk(h)] for h in heads]
    v_ext = [jnp.concatenate([v_ref[:, hv(h)], ones_col], axis=1) for h in heads]
    qk = [lax.dot_general(q[h], k[h], (((1,), (1,)), ((), ())), preferred_element_type=F32) for h in heads]
    c_all = [c0_ref[:, h].reshape(S * DK_MLSTM, DV_MLSTM) for h in heads]
    q_bd = []
    for h in heads:
        q2 = jnp.concatenate([q[h], q[h]], axis=1)
        q_bd.append(jnp.where(bd_mask, jnp.concatenate([q2] * (S // 2), axis=1), jnp.zeros((), BF16)))
    qc = [jnp.dot(q_bd[h], c_all[h].astype(BF16), preferred_element_type=F32) for h in heads]
    s = []
    for h in heads:
        ln = _GATE_LANE0 + h
        d = jnp.exp(jnp.where(causal, a_t[ln:ln + 1, :] - big_m[:, ln:ln + 1], -jnp.inf))
        s.append((qk[h] * d).astype(BF16))
    sv = [jnp.dot(s[h], v_ext[h], preferred_element_type=F32) for h in heads]
    kw_bd = []
    for h in heads:
        ln = _GATE_LANE0 + h
        kw_t = (k[h].astype(F32) * w[:, ln:ln + 1]).T
        kw_bd.append(jnp.where(bd_mask_t, jnp.concatenate([kw_t] * S, axis=0), 0.0).astype(BF16))
    upd = [jnp.dot(kw_bd[h], v_ext[h], preferred_element_type=F32) for h in heads]
    for h in heads:
        ln = _GATE_LANE0 + h
        inter_h = inter[:, ln:ln + 1]
        qn = jnp.sum(q[h].astype(F32) * nrow_ref[:, hk(h)], axis=1, keepdims=True)
        den = inter_h * qn + sv[h][:, DV_MLSTM:DV_MLSTM + 1]
        r = 1.0 / jnp.maximum(jnp.abs(den), em[:, ln:ln + 1])
        tt = om_ref[:, hv(h)] * (inter_h * qc[h] + sv[h][:, :DV_MLSTM])
        msq = jnp.mean(tt * tt, axis=1, keepdims=True)
        scale = r * lax.rsqrt(r * r * msq + EPS)
        mix_ref[:, hv(h)] = (tt * scale * gm_ref[:, hv(h)]).astype(BF16)
        dec = decay[:, ln:ln + 1].reshape(S, t, 1)[:, 0:1, :]
        c_out[:, h] = dec * c0_ref[:, h] + upd[h][:, :DV_MLSTM].reshape(S, DK_MLSTM, DV_MLSTM)
        n_out[:, h] = dec * n0_ref[:, h] + upd[h][:, DV_MLSTM:DV_MLSTM + 1].reshape(S, DK_MLSTM, 1)


def _mlstm_blk(qkm, vm, om, gt, g_m, c0, n0_col, n0_rows, m0_rows, nseq, t):
    S = SEQ_PER_BLOCK
    L = S * t
    tok = lambda wd: pl.BlockSpec((L, wd), lambda i: (i, 0))
    st = lambda d: pl.BlockSpec((S, N_HEADS_MLSTM, DK_MLSTM, d), lambda i: (i, 0, 0, 0))
    sds = jax.ShapeDtypeStruct
    return pl.pallas_call(
        functools.partial(_mlstm_blk_kernel, t=t),
        grid=(nseq // S,),
        in_specs=[tok(2 * QK_M_W), tok(MLSTM_W), tok(MLSTM_W), tok(LANES), _const_spec((1, MLSTM_W)),
                  st(DV_MLSTM), st(1), tok(QK_M_W), tok(LANES)],
        out_specs=[tok(MLSTM_W), st(DV_MLSTM), st(1), tok(LANES)],
        out_shape=[sds((nseq * t, MLSTM_W), BF16), sds((nseq, N_HEADS_MLSTM, DK_MLSTM, DV_MLSTM), F32),
                   sds((nseq, N_HEADS_MLSTM, DK_MLSTM, 1), F32), sds((nseq * t, LANES), F32)],
        compiler_params=pltpu.CompilerParams(dimension_semantics=("arbitrary",), vmem_limit_bytes=VMEM_LIMIT),
        name="mlstm_blk",
    )(qkm, vm, om, gt, g_m, c0, n0_col, n0_rows, m0_rows)


_T_QA = 0
_T_VA = _T_QA + ATT_W
_T_QM = _T_VA + KV_W
_T_VM = _T_QM + QK_M_W
_T_OM = _T_VM + MLSTM_W
_T_G = _T_OM + MLSTM_W
_T_ROWS = _T_G + 16
_N_KV = 0
_N_KM = _N_KV + 2 * KV_W
_N_G = _N_KM + QK_M_W
_N_COLS = _N_G + LANES


def _log_sigmoid(g):
    return jnp.minimum(g, 0.0) - jnp.log1p(jnp.exp(-jnp.abs(g)))


def _in_proj_t_kernel(x_ref, g_ref, wn_ref, wt_ref, bgr_ref, bgc_ref,
                      k_ref, kvf_ref, km_ref, gt_ref, qT_ref, vT_ref, qmT_ref, vmT_ref, omT_ref, gT_ref):
    xf = x_ref[...]
    hb = (xf * lax.rsqrt(jnp.mean(xf * xf, axis=-1, keepdims=True) + EPS) * g_ref[...]).astype(BF16)

    z_n = jnp.dot(hb, wn_ref[...], preferred_element_type=F32)
    z_t = lax.dot_general(wt_ref[...], hb, (((1,), (1,)), ((), ())), preferred_element_type=F32)
    nn = lambda lo, width: z_n[:, lo:lo + width]
    nt = lambda lo, width: z_t[lo:lo + width, :]

    kv = nn(_N_KV, 2 * KV_W)
    kvf_ref[...] = kv
    k_ref[...] = kv[:, :KV_W].astype(BF16)
    km_ref[...] = (nn(_N_KM, QK_M_W) * (DK_MLSTM ** -0.5)).astype(BF16)
    g = nn(_N_G, LANES) + bgr_ref[...]
    col = lax.broadcasted_iota(jnp.int32, g.shape, 1)
    gt_ref[...] = jnp.where(col < N_HEADS_MLSTM, g, _log_sigmoid(g))
    qT_ref[...] = (nt(_T_QA, ATT_W) * (HEAD_DIM_ATT ** -0.5)).astype(BF16)
    vT_ref[...] = nt(_T_VA, KV_W).astype(BF16)
    qmT_ref[...] = nt(_T_QM, QK_M_W).astype(BF16)
    vmT_ref[...] = nt(_T_VM, MLSTM_W).astype(BF16)
    omT_ref[...] = jax.nn.sigmoid(nt(_T_OM, MLSTM_W))
    g_t = nt(_T_G, 16)[:N_GATES, :] + bgc_ref[...]
    row = lax.broadcasted_iota(jnp.int32, g_t.shape, 0)
    gT_ref[...] = jnp.where(row < N_HEADS_MLSTM, g_t, _log_sigmoid(g_t))


def _in_proj_t(x2d, g_norm, w_n, w_t, bg_row, bg_col, tm):
    n = x2d.shape[0]
    row = lambda w: pl.BlockSpec((tm, w), lambda i: (i, 0))
    colb = lambda r: pl.BlockSpec((r, tm), lambda i: (0, i))
    sds = jax.ShapeDtypeStruct
    return pl.pallas_call(
        _in_proj_t_kernel,
        grid=(n // tm,),
        in_specs=[row(D_MODEL), _const_spec((1, D_MODEL)), _const_spec(w_n.shape), _const_spec(w_t.shape),
                  _const_spec((1, LANES)), _const_spec((N_GATES, 1))],
        out_specs=[row(KV_W), row(2 * KV_W), row(QK_M_W), row(LANES),
                   colb(ATT_W), colb(KV_W), colb(QK_M_W), colb(MLSTM_W), colb(MLSTM_W), colb(N_GATES)],
        out_shape=[sds((n, KV_W), BF16), sds((n, 2 * KV_W), F32), sds((n, QK_M_W), BF16), sds((n, LANES), F32),
                   sds((ATT_W, n), BF16), sds((KV_W, n), BF16), sds((QK_M_W, n), BF16), sds((MLSTM_W, n), BF16),
                   sds((MLSTM_W, n), F32), sds((N_GATES, n), F32)],
        compiler_params=pltpu.CompilerParams(dimension_semantics=("arbitrary",), vmem_limit_bytes=VMEM_LIMIT),
        name="in_proj_t",
    )(x2d, g_norm, w_n, w_t, bg_row, bg_col)


def _prompt_buckets_t():
    b = _prompt_buckets()
    return np.concatenate([b[:WINDOW].T, b[WINDOW:].T], axis=0).astype(np.int32)


def _attn_prompt_t_kernel(qT_ref, kp_ref, kc_ref, vTp_ref, vTc_ref, bias_ref, sink_ref, gb_ref, out_ref, *, qb):
    first = pl.program_id(1) == 0
    k_all = jnp.concatenate([kp_ref[...], kc_ref[...]], axis=0)
    vT_all = jnp.concatenate([vTp_ref[...], vTc_ref[...]], axis=1)
    for i in range(qb):
        cols = slice(i * WINDOW, (i + 1) * WINDOW)
        off = pl.multiple_of(jnp.where(first, 2 * WINDOW, 0), 2 * WINDOW) if i == 0 else 0
        k2 = k_all[i * WINDOW:(i + 2) * WINDOW, :]
        vT2 = vT_all[:, i * WINDOW:(i + 2) * WINDOW]
        hd = lambda h: slice((h // GQA_GROUP) * HEAD_DIM_ATT, (h // GQA_GROUP + 1) * HEAD_DIM_ATT)
        scores = [jnp.dot(k2[:, hd(h)], qT_ref[h * HEAD_DIM_ATT:(h + 1) * HEAD_DIM_ATT, cols],
                          preferred_element_type=F32) for h in range(N_HEADS_ATT)]
        probs, rden = [], []
        for h in range(N_HEADS_ATT):
            s = scores[h] + bias_ref[h, pl.ds(off, 2 * WINDOW), :]
            sink = sink_ref[h]
            m = jnp.maximum(jnp.max(s, axis=0, keepdims=True), sink)
            p = jnp.exp(s - m)
            rden.append(1.0 / (jnp.sum(p, axis=0, keepdims=True) + jnp.exp(sink - m)))
            probs.append(p.astype(BF16))
        outs = []
        ssq = jnp.zeros((1, WINDOW), F32)
        for h in range(N_HEADS_ATT):
            o = jnp.dot(vT2[hd(h), :], probs[h], preferred_element_type=F32) * rden[h]
            outs.append(o)
            ssq = ssq + jnp.sum(o * o, axis=0, keepdims=True)
        inv = lax.rsqrt(ssq * (1.0 / ATT_W) + EPS)
        for h in range(N_HEADS_ATT):
            rows = slice(h * HEAD_DIM_ATT, (h + 1) * HEAD_DIM_ATT)
            out_ref[rows, cols] = (outs[h] * inv * gb_ref[rows, :]).astype(BF16)


def _attn_prompt_t(qT, k, vT, bias, sinks, g_b, batch, seq, qb):
    nb = seq // WINDOW
    ns = nb // qb
    prev = lambda b, j: jnp.maximum(b * nb + j * qb - 1, 0)
    return pl.pallas_call(
        functools.partial(_attn_prompt_t_kernel, qb=qb),
        grid=(batch, ns),
        in_specs=[pl.BlockSpec((ATT_W, qb * WINDOW), lambda b, j: (0, b * ns + j)),
                  pl.BlockSpec((WINDOW, KV_W), lambda b, j: (prev(b, j), 0)),
                  pl.BlockSpec((qb * WINDOW, KV_W), lambda b, j: (b * ns + j, 0)),
                  pl.BlockSpec((KV_W, WINDOW), lambda b, j: (0, prev(b, j))),
                  pl.BlockSpec((KV_W, qb * WINDOW), lambda b, j: (0, b * ns + j)),
                  _const_spec(bias.shape),
                  pl.BlockSpec(memory_space=pltpu.SMEM),
                  _const_spec(g_b.shape)],
        out_specs=pl.BlockSpec((ATT_W, qb * WINDOW), lambda b, j: (0, b * ns + j)),
        out_shape=jax.ShapeDtypeStruct((ATT_W, batch * seq), BF16),
        compiler_params=pltpu.CompilerParams(dimension_semantics=("arbitrary", "arbitrary"),
                                             vmem_limit_bytes=VMEM_LIMIT),
        name="attn_prompt_t",
    )(qT, k, k, vT, vT, bias, sinks, g_b)


def _prefix_max_lanes(x):
    n = x.shape[1]
    col = lax.broadcasted_iota(jnp.int32, x.shape, 1)
    k = 1
    while k < n:
        x = jnp.maximum(x, jnp.where(col >= k, pltpu.roll(x, k, axis=1), -jnp.inf))
        k *= 2
    return x


def _mlstm_seq_t_kernel(qT_ref, k_ref, vT_ref, omT_ref, gt_ref, gT_ref, gb_ref, mix_ref, c_out, n_out, m_out,
                        cn_s, mr_s, mc_s, *, chunk, nsub):
    c = pl.program_id(1)

    @pl.when(c == 0)
    def _():
        cn_s[...] = jnp.zeros_like(cn_s)
        mr_s[...] = jnp.zeros_like(mr_s)
        mc_s[...] = jnp.zeros_like(mc_s)

    L = chunk
    i0 = lax.broadcasted_iota(jnp.int32, (L, L), 0)
    i1 = lax.broadcasted_iota(jnp.int32, (L, L), 1)
    causal_t = i0 <= i1
    triu = causal_t.astype(F32)
    tril = (i1 <= i0).astype(F32)
    ones_row = (lax.broadcasted_iota(jnp.int32, (DV_MLSTM, L), 0) == 0).astype(BF16)
    cn = [cn_s[h] for h in range(N_HEADS_MLSTM)]
    m_r = mr_s[...]
    m_c = mc_s[...]
    hi = lax.Precision.HIGHEST
    heads = range(N_HEADS_MLSTM)
    subs = range(nsub)
    col_of = lambda j: slice(j * L, (j + 1) * L)
    hv = lambda h: slice(h * DV_MLSTM, (h + 1) * DV_MLSTM)
    hk = lambda h: slice(h * DK_MLSTM, (h + 1) * DK_MLSTM)
    a_c, a_r, b_c, b_r, pm_r, amax_c = [], [], [], [], [], []
    for j in subs:
        gt = gt_ref[col_of(j), :]
        g_t = gT_ref[:, col_of(j)]
        b_c.append(jnp.dot(tril, gt, precision=hi, preferred_element_type=F32))
        a_c.append(pltpu.roll(gt, N_HEADS_MLSTM, axis=1) - b_c[j])
        b_r.append(jnp.dot(g_t, triu, precision=hi, preferred_element_type=F32))
        a_r.append(pltpu.roll(g_t, N_HEADS_MLSTM, axis=0) - b_r[j])
    kq = [[jnp.dot(k_ref[col_of(j), hk(h)], qT_ref[hk(h), col_of(j)], preferred_element_type=F32)
           for h in heads] for j in subs]
    vT_ext = [[jnp.concatenate([vT_ref[hv(h), col_of(j)], ones_row], axis=0) for h in heads] for j in subs]
    for j in subs:
        pm_r.append(_prefix_max_lanes(a_r[j]))
        amax_c.append(jnp.max(a_c[j], axis=0, keepdims=True))
    big_m, inter, em, w_c, decay_c = [], [], [], [], []
    for j in subs:
        big_m.append(jnp.maximum(m_r, pm_r[j]))
        inter.append(jnp.exp(m_r - big_m[j]))
        em.append(jnp.exp(-(b_r[j] + big_m[j])))
        m_end_c = jnp.maximum(m_c, amax_c[j])
        w_c.append(jnp.exp(a_c[j] - m_end_c))
        decay_c.append(jnp.exp(m_c - m_end_c))
        m_c = b_c[j][L - 1:L, :] + m_end_c
        m_r = jnp.broadcast_to(b_r[j][:, L - 1:L] + big_m[j][:, L - 1:L], (N_GATES, L))
    sT = [[None] * N_HEADS_MLSTM for _ in subs]
    kw = [[None] * N_HEADS_MLSTM for _ in subs]
    for j in subs:
        for h in heads:
            r = _GATE_LANE0 + h
            d = jnp.exp(jnp.where(causal_t, a_c[j][:, r:r + 1] - big_m[j][r:r + 1, :], -jnp.inf))
            sT[j][h] = (kq[j][h] * d).astype(BF16)
            kw[j][h] = (k_ref[col_of(j), hk(h)].astype(F32) * w_c[j][:, r:r + 1]).astype(BF16)
    intra = [[jnp.dot(vT_ext[j][h], sT[j][h], preferred_element_type=F32) for h in heads] for j in subs]
    upd = [[jnp.dot(vT_ext[j][h], kw[j][h], preferred_element_type=F32) for h in heads] for j in subs]
    for j in subs:
        qc = [jnp.dot(cn[h].astype(BF16), qT_ref[hk(h), col_of(j)], preferred_element_type=F32) for h in heads]
        for h in heads:
            r = _GATE_LANE0 + h
            nd = inter[j][r:r + 1, :] * qc[h] + intra[j][h]
            rr = 1.0 / jnp.maximum(jnp.abs(nd[DV_MLSTM:DV_MLSTM + 1, :]), em[j][r:r + 1, :])
            t = omT_ref[hv(h), col_of(j)] * nd[:DV_MLSTM, :]
            msq = jnp.mean(t * t, axis=0, keepdims=True)
            scale = rr * lax.rsqrt(rr * rr * msq + EPS)
            mix_ref[hv(h), col_of(j)] = (t * scale * gb_ref[hv(h), :]).astype(BF16)
            cn[h] = decay_c[j][:, r:r + 1] * cn[h] + upd[j][h]
    for h in range(N_HEADS_MLSTM):
        cn_s[h] = cn[h]
    mr_s[...] = m_r
    mc_s[...] = m_c

    @pl.when(c == pl.num_programs(1) - 1)
    def _():
        for h in range(N_HEADS_MLSTM):
            c_out[0, h] = cn_s[h, :DV_MLSTM, :].T
            n_out[0, h] = cn_s[h, DV_MLSTM:DV_MLSTM + 1, :]
        m_out[0] = mc_s[...]


def _mlstm_seq_t(qmT, km, vmT, omT, gt, gT, g_b, nseq, t, chunk, nsub):
    step = chunk * nsub
    cps = t // step
    kern = functools.partial(_mlstm_seq_t_kernel, chunk=chunk, nsub=nsub)
    colb = lambda r: pl.BlockSpec((r, step), lambda i, c: (0, i * cps + c))
    rowb = lambda w: pl.BlockSpec((step, w), lambda i, c: (i * cps + c, 0))
    st = lambda a, d: pl.BlockSpec((1, N_HEADS_MLSTM, a, d), lambda i, c: (i, 0, 0, 0))
    sds = jax.ShapeDtypeStruct
    return pl.pallas_call(
        kern,
        grid=(nseq, cps),
        in_specs=[colb(QK_M_W), rowb(QK_M_W), colb(MLSTM_W), colb(MLSTM_W), rowb(LANES), colb(N_GATES),
                  _const_spec(g_b.shape)],
        out_specs=[colb(MLSTM_W), st(DK_MLSTM, DV_MLSTM), st(1, DK_MLSTM),
                   pl.BlockSpec((1, 1, LANES), lambda i, c: (i, 0, 0))],
        out_shape=[sds((MLSTM_W, nseq * t), BF16), sds((nseq, N_HEADS_MLSTM, DK_MLSTM, DV_MLSTM), F32),
                   sds((nseq, N_HEADS_MLSTM, 1, DK_MLSTM), F32), sds((nseq, 1, LANES), F32)],
        scratch_shapes=[pltpu.VMEM((N_HEADS_MLSTM, 2 * DV_MLSTM, DK_MLSTM), F32),
                        pltpu.VMEM((N_GATES, chunk), F32), pltpu.VMEM((1, LANES), F32)],
        compiler_params=pltpu.CompilerParams(dimension_semantics=("arbitrary", "arbitrary"),
                                             vmem_limit_bytes=VMEM_LIMIT),
        name="mlstm_seq_t",
    )(qmT, km, vmT, omT, gt, gT, g_b)


def _out_mlp_kernel(x_ref, a_ref, hm_ref, woa_ref, wom_ref, gmlp_ref, wup_ref, wdn_ref, gfin_ref, y_ref, acc_ref,
                    *, fchunk, mix_t):
    mix_dims = (((0,), (0,)), ((), ())) if mix_t else (((1,), (0,)), ((), ()))
    x1 = (x_ref[...]
          + lax.dot_general(a_ref[...], woa_ref[...], mix_dims, preferred_element_type=F32)
          + lax.dot_general(hm_ref[...], wom_ref[...], mix_dims, preferred_element_type=F32))
    h = (x1 * lax.rsqrt(jnp.mean(x1 * x1, axis=-1, keepdims=True) + EPS) * gmlp_ref[...]).astype(BF16)
    acc_ref[...] = x1

    def ff_chunk(f, carry):
        lo = pl.multiple_of(f * fchunk, fchunk)
        u = jnp.maximum(jnp.dot(h, wup_ref[:, pl.ds(lo, fchunk)], preferred_element_type=F32), 0.0)
        acc_ref[...] += jnp.dot((u * u).astype(BF16), wdn_ref[pl.ds(lo, fchunk), :], preferred_element_type=F32)
        return carry

    lax.fori_loop(0, D_FF // fchunk, ff_chunk, 0)
    y = acc_ref[...]
    y_ref[...] = y * lax.rsqrt(jnp.mean(y * y, axis=-1, keepdims=True) + EPS) * gfin_ref[...]


def _out_mlp(x2d, mix_a, mix_m, wo_a, wo_m, g_mlp, w_up, w_dn, g_fin, tm, fchunk, mix_t):
    n = x2d.shape[0]
    row = lambda w: pl.BlockSpec((tm, w), lambda i: (i, 0))
    mix = (lambda w: pl.BlockSpec((w, tm), lambda i: (0, i))) if mix_t else row
    return pl.pallas_call(
        functools.partial(_out_mlp_kernel, fchunk=fchunk, mix_t=mix_t),
        grid=(n // tm,),
        in_specs=[row(D_MODEL), mix(ATT_W), mix(MLSTM_W), _const_spec(wo_a.shape), _const_spec(wo_m.shape),
                  _const_spec((1, D_MODEL)), _const_spec(w_up.shape), _const_spec(w_dn.shape),
                  _const_spec((1, D_MODEL))],
        out_specs=row(D_MODEL),
        out_shape=jax.ShapeDtypeStruct((n, D_MODEL), F32),
        scratch_shapes=[pltpu.VMEM((tm, D_MODEL), F32)],
        compiler_params=pltpu.CompilerParams(dimension_semantics=("arbitrary",), vmem_limit_bytes=VMEM_LIMIT),
        name="out_mlp",
    )(x2d, mix_a, mix_m, wo_a, wo_m, g_mlp, w_up, w_dn, g_fin)


def kernel(x_prompt, x_sample, cache_k_win, cache_v_win, state_C, state_n, state_m, rel_bias, norm_mix, w_in,
           b_gates, attn_sinks, norm_attn_out, norm_mlstm_out, w_out, norm_mlp, w_up, w_down, norm_final):
    depth = w_in.shape[0]
    assert depth == 1, "single-layer trunk"
    bp, sp = x_prompt.shape[:2]
    bs, ts = x_sample.shape[:2]
    wc = cache_k_win.shape[2]
    l = 0

    w_pad = jnp.pad(w_in[l], ((0, 0), (0, MAIN_COLS + LANES - w_in.shape[2]))).astype(BF16)
    bg_pad = jnp.pad(b_gates[l], (0, LANES - N_GATES)).reshape(1, LANES)
    wo_a = w_out[l, :ATT_W].astype(BF16)
    wo_m = w_out[l, ATT_W:].astype(BF16)
    wup = w_up[l].astype(BF16)
    wdn = w_down[l].astype(BF16)
    g_mix = norm_mix[l].reshape(1, D_MODEL)
    g_att = norm_attn_out[l].reshape(1, ATT_W)
    g_m = norm_mlstm_out[l].reshape(1, MLSTM_W)
    g_mlp = norm_mlp[l].reshape(1, D_MODEL)
    g_fin = norm_final.reshape(1, D_MODEL)
    sinks = attn_sinks[l]

    kpad = ((wc + ts + 15) // 16) * 16
    bias_s = _bias_table(rel_bias, _sample_buckets(ts, wc, kpad))
    bias_s = bias_s.reshape(N_KV_HEADS, GQA_GROUP * ts, kpad)
    sink_rows = jnp.repeat(sinks, ts).reshape(N_KV_HEADS, GQA_GROUP * ts, 1)

    xp = x_prompt.reshape(bp * sp, D_MODEL)
    wl = w_in[l]
    cols = lambda lo, n: wl[:, lo:lo + n]
    w_n = jnp.concatenate([cols(_O_KV, 2 * KV_W), cols(_O_QKM + QK_M_W, QK_M_W),
                           jnp.pad(cols(_O_G, N_GATES), ((0, 0), (0, LANES - N_GATES)))], axis=1).astype(BF16)
    w_t = jnp.concatenate([cols(_O_QA, ATT_W), cols(_O_KV + KV_W, KV_W), cols(_O_QKM, QK_M_W), cols(_O_VM, MLSTM_W),
                           cols(_O_OM, MLSTM_W), jnp.pad(cols(_O_G, N_GATES), ((0, 0), (0, 16 - N_GATES)))],
                          axis=1).T.astype(BF16)
    chunk = 128
    gb_att = jnp.broadcast_to(norm_attn_out[l].reshape(ATT_W, 1), (ATT_W, WINDOW))
    gb_m = jnp.broadcast_to(norm_mlstm_out[l].reshape(MLSTM_W, 1), (MLSTM_W, chunk))
    k_a, kvf, km, gt, qT, vT, qmT, vmT, omT, gT = _in_proj_t(xp, g_mix, w_n, w_t, bg_pad,
                                                             b_gates[l].reshape(N_GATES, 1), tm=1024)
    bias_t = _bias_table(rel_bias, _prompt_buckets_t())
    mix_a = _attn_prompt_t(qT, k_a, vT, bias_t, sinks, gb_att, bp, sp, qb=8)
    mix_m, c_p, n_p, m_p = _mlstm_seq_t(qmT, km, vmT, omT, gt, gT, gb_m, bp, sp, chunk=chunk, nsub=4)
    m_p = m_p[:, 0, _GATE_LANE0:_GATE_LANE0 + N_HEADS_MLSTM]
    y_p = _out_mlp(xp, mix_a, mix_m, wo_a, wo_m, g_mlp, wup, wdn, g_fin, tm=512, fchunk=D_FF, mix_t=True)
    win_p = min(WINDOW, sp)
    kv_win = kvf.reshape(bp, sp, 2 * KV_W)[:, sp - win_p:, :]
    k_win_p = kv_win[:, :, :KV_W].reshape(1, bp, win_p, N_KV_HEADS, HEAD_DIM_ATT)
    v_win_p = kv_win[:, :, KV_W:].reshape(1, bp, win_p, N_KV_HEADS, HEAD_DIM_ATT)

    xs = x_sample.reshape(bs * ts, D_MODEL)
    qa, kvb, kvf, qkm, vm, om, gt = _in_proj(xs, g_mix, w_pad, bg_pad, tm=256)
    ck = cache_k_win[l].reshape(bs, wc, KV_W)
    cv = cache_v_win[l].reshape(bs, wc, KV_W)
    mix_a, kw_s, vw_s = _attn_sample(qa, kvf, ck, cv, bias_s, sink_rows, g_att, bs, ts, bb=8)
    assert math.gcd(ts, 64) == ts and bs % SEQ_PER_BLOCK == 0, "sample group: one mLSTM chunk per sequence"
    n0 = state_n[l]
    n0_rows = jnp.repeat(n0.reshape(bs, QK_M_W), ts, axis=0)
    m0_rows = jnp.pad(jnp.repeat(state_m[l], ts, axis=0),
                      ((0, 0), (_GATE_LANE0, LANES - _GATE_LANE0 - N_HEADS_MLSTM)))
    mix_m, c_s, n_s, m_rows = _mlstm_blk(qkm, vm, om, gt, g_m, state_C[l],
                                         n0.reshape(bs, N_HEADS_MLSTM, DK_MLSTM, 1), n0_rows, m0_rows, bs, ts)
    m_s = m_rows[::ts, _GATE_LANE0:_GATE_LANE0 + N_HEADS_MLSTM]
    y_s = _out_mlp(xs, mix_a, mix_m, wo_a, wo_m, g_mlp, wup, wdn, g_fin, tm=512, fchunk=D_FF,
                   mix_t=False)

    return (y_p.reshape(bp, sp, D_MODEL), y_s.reshape(bs, ts, D_MODEL),
            k_win_p, v_win_p,
            c_p[None], n_p.reshape(1, bp, N_HEADS_MLSTM, DK_MLSTM), m_p.reshape(1, bp, N_HEADS_MLSTM),
            kw_s.reshape(1, bs, wc, N_KV_HEADS, HEAD_DIM_ATT), vw_s.reshape(1, bs, wc, N_KV_HEADS, HEAD_DIM_ATT),
            c_s[None], n_s.reshape(1, bs, N_HEADS_MLSTM, DK_MLSTM), m_s.reshape(1, bs, N_HEADS_MLSTM))
```

```python
import functools
import math

import numpy as np
import jax
import jax.numpy as jnp
from jax import lax
from jax.experimental import pallas as pl
from jax.experimental.pallas import tpu as pltpu

D_MODEL = 1024
N_HEADS_ATT = 8
N_KV_HEADS = 2
GQA_GROUP = N_HEADS_ATT // N_KV_HEADS
HEAD_DIM_ATT = 64
ATT_W = N_HEADS_ATT * HEAD_DIM_ATT
KV_W = N_KV_HEADS * HEAD_DIM_ATT
WINDOW = 128
NUM_BUCKETS = 32
MAX_DISTANCE = 128
N_HEADS_MLSTM = 4
DV_MLSTM = 128
DK_MLSTM = 64
MLSTM_W = N_HEADS_MLSTM * DV_MLSTM
QK_M_W = N_HEADS_MLSTM * DK_MLSTM
D_FF = 4 * D_MODEL
EPS = 1e-6
N_GATES = 2 * N_HEADS_MLSTM
LANES = 128
MAIN_COLS = ATT_W + 2 * KV_W + 2 * QK_M_W + 2 * MLSTM_W
VMEM_LIMIT = 56 * 1024 * 1024

F32 = jnp.float32
BF16 = jnp.bfloat16

_O_QA = 0
_O_KV = _O_QA + ATT_W
_O_QKM = _O_KV + 2 * KV_W
_O_VM = _O_QKM + 2 * QK_M_W
_O_OM = _O_VM + MLSTM_W
_O_G = _O_OM + MLSTM_W


def _const_spec(shape):
    nd = len(shape)
    return pl.BlockSpec(shape, lambda *_: (0,) * nd, pipeline_mode=pl.Buffered(1))


def _in_proj_kernel(x_ref, g_ref, w_ref, bg_ref, qa_ref, kvb_ref, kvf_ref, qkm_ref, vm_ref, om_ref, gt_ref):
    xf = x_ref[...]
    h = xf * lax.rsqrt(jnp.mean(xf * xf, axis=-1, keepdims=True) + EPS) * g_ref[...]
    hb = h.astype(BF16)

    def proj(lo, width):
        return jnp.dot(hb, w_ref[:, lo:lo + width], preferred_element_type=F32)

    qa_ref[...] = (proj(_O_QA, ATT_W) * (HEAD_DIM_ATT ** -0.5)).astype(BF16)
    kv = proj(_O_KV, 2 * KV_W)
    kvf_ref[...] = kv
    kvb_ref[...] = kv.astype(BF16)
    qkm = proj(_O_QKM, 2 * QK_M_W)
    qkm_ref[:, :QK_M_W] = qkm[:, :QK_M_W].astype(BF16)
    qkm_ref[:, QK_M_W:] = (qkm[:, QK_M_W:] * (DK_MLSTM ** -0.5)).astype(BF16)
    vm_ref[...] = proj(_O_VM, MLSTM_W).astype(BF16)
    om_ref[...] = jax.nn.sigmoid(proj(_O_OM, MLSTM_W))
    g = proj(_O_G, LANES) + bg_ref[...]
    lf = jnp.minimum(g, 0.0) - jnp.log1p(jnp.exp(-jnp.abs(g)))
    col = lax.broadcasted_iota(jnp.int32, g.shape, 1)
    gt_ref[...] = jnp.where(col < N_HEADS_MLSTM, g, lf)


def _in_proj(x2d, g_norm, w_pad, bg_pad, tm):
    n = x2d.shape[0]
    row = lambda w: pl.BlockSpec((tm, w), lambda i: (i, 0))
    return pl.pallas_call(
        _in_proj_kernel,
        grid=(n // tm,),
        in_specs=[row(D_MODEL), _const_spec((1, D_MODEL)), _const_spec(w_pad.shape), _const_spec((1, LANES))],
        out_specs=[row(ATT_W), row(2 * KV_W), row(2 * KV_W), row(2 * QK_M_W), row(MLSTM_W), row(MLSTM_W), row(LANES)],
        out_shape=[jax.ShapeDtypeStruct((n, ATT_W), BF16), jax.ShapeDtypeStruct((n, 2 * KV_W), BF16),
                   jax.ShapeDtypeStruct((n, 2 * KV_W), F32), jax.ShapeDtypeStruct((n, 2 * QK_M_W), BF16),
                   jax.ShapeDtypeStruct((n, MLSTM_W), BF16), jax.ShapeDtypeStruct((n, MLSTM_W), F32),
                   jax.ShapeDtypeStruct((n, LANES), F32)],
        compiler_params=pltpu.CompilerParams(dimension_semantics=("arbitrary",), vmem_limit_bytes=VMEM_LIMIT),
        name="in_proj",
    )(x2d, g_norm, w_pad, bg_pad)


def _t5_bucket_np(dist):
    max_exact = NUM_BUCKETS // 2
    d = np.maximum(dist, 0)
    ratio = np.maximum(d, 1).astype(np.float32) / np.float32(max_exact)
    large = max_exact + (np.log(ratio) / np.float32(math.log(MAX_DISTANCE / max_exact))
                         * np.float32(NUM_BUCKETS - max_exact)).astype(np.int32)
    large = np.minimum(large, NUM_BUCKETS - 1)
    return np.where(d < max_exact, d, large).astype(np.int32)


def _bias_table_kernel(rb_ref, bucket_ref, out_ref):
    h = pl.program_id(0)
    bucket = bucket_ref[...]
    acc = jnp.full(bucket.shape, -jnp.inf, F32)
    for j in range(NUM_BUCKETS):
        acc = jnp.where(bucket == j, rb_ref[j, h], acc)
    out_ref[0] = acc


def _bias_table(rel_bias, bucket):
    r, c = bucket.shape
    return pl.pallas_call(
        _bias_table_kernel,
        grid=(N_HEADS_ATT,),
        in_specs=[pl.BlockSpec(memory_space=pltpu.SMEM), pl.BlockSpec((r, c), lambda h: (0, 0))],
        out_specs=pl.BlockSpec((1, r, c), lambda h: (h, 0, 0)),
        out_shape=jax.ShapeDtypeStruct((N_HEADS_ATT, r, c), F32),
        compiler_params=pltpu.CompilerParams(dimension_semantics=("arbitrary",)),
        name="bias_table",
    )(rel_bias, jnp.asarray(bucket))


def _prompt_buckets():
    qi = np.arange(WINDOW)[:, None]
    kj = np.arange(2 * WINDOW)[None, :]
    dist = qi + WINDOW - kj
    in_win = (dist >= 0) & (dist <= WINDOW)
    b = _t5_bucket_np(dist)
    general = np.where(in_win, b, -1)
    first = np.where(in_win & (kj >= WINDOW), b, -1)
    return np.concatenate([general, first], axis=0).astype(np.int32)


def _sample_buckets(t, w, padded):
    dist = (w + np.arange(t))[:, None] - np.arange(padded)[None, :]
    valid = (dist >= 0) & (dist <= WINDOW) & (np.arange(padded)[None, :] < w + t)
    return np.where(valid, _t5_bucket_np(dist), -1).astype(np.int32)


def _attn_prompt_kernel(q_ref, kvp_ref, kvc_ref, bias_ref, sink_ref, g_ref, out_ref):
    first = pl.program_id(1) == 0
    off = pl.multiple_of(jnp.where(first, WINDOW, 0), WINDOW)
    kv = jnp.concatenate([kvp_ref[...], kvc_ref[...]], axis=0)
    outs = []
    ssq = jnp.zeros((WINDOW, 1), F32)
    for h in range(N_HEADS_ATT):
        g = h // GQA_GROUP
        q = q_ref[:, h * HEAD_DIM_ATT:(h + 1) * HEAD_DIM_ATT]
        k = kv[:, g * HEAD_DIM_ATT:(g + 1) * HEAD_DIM_ATT]
        v = kv[:, KV_W + g * HEAD_DIM_ATT:KV_W + (g + 1) * HEAD_DIM_ATT]
        s = lax.dot_general(q, k, (((1,), (1,)), ((), ())), preferred_element_type=F32)
        s = s + bias_ref[h, pl.ds(off, WINDOW), :]
        sink = sink_ref[h]
        m = jnp.maximum(jnp.max(s, axis=1, keepdims=True), sink)
        p = jnp.exp(s - m)
        denom = jnp.sum(p, axis=1, keepdims=True) + jnp.exp(sink - m)
        p = p * (1.0 / denom)
        o = jnp.dot(p.astype(BF16), v, preferred_element_type=F32)
        outs.append(o)
        ssq = ssq + jnp.sum(o * o, axis=1, keepdims=True)
    inv = lax.rsqrt(ssq * (1.0 / ATT_W) + EPS)
    for h in range(N_HEADS_ATT):
        sl = slice(h * HEAD_DIM_ATT, (h + 1) * HEAD_DIM_ATT)
        out_ref[:, sl] = (outs[h] * inv * g_ref[:, sl]).astype(BF16)


def _attn_prompt(qa, kvb, bias, sinks, g_att, batch, seq):
    nb = seq // WINDOW
    return pl.pallas_call(
        _attn_prompt_kernel,
        grid=(batch, nb),
        in_specs=[pl.BlockSpec((WINDOW, ATT_W), lambda b, j: (b * nb + j, 0)),
                  pl.BlockSpec((WINDOW, 2 * KV_W), lambda b, j: (b * nb + jnp.maximum(j - 1, 0), 0)),
                  pl.BlockSpec((WINDOW, 2 * KV_W), lambda b, j: (b * nb + j, 0)),
                  _const_spec(bias.shape),
                  pl.BlockSpec(memory_space=pltpu.SMEM),
                  _const_spec((1, ATT_W))],
        out_specs=pl.BlockSpec((WINDOW, ATT_W), lambda b, j: (b * nb + j, 0)),
        out_shape=jax.ShapeDtypeStruct((batch * seq, ATT_W), BF16),
        compiler_params=pltpu.CompilerParams(dimension_semantics=("arbitrary", "arbitrary"),
                                             vmem_limit_bytes=VMEM_LIMIT),
        name="attn_prompt",
    )(qa, kvb, kvb, bias, sinks, g_att)


def _attn_sample_kernel(q_ref, kvf_ref, ck_ref, cv_ref, bias_ref, sink_ref, g_ref, out_ref, kw_ref, vw_ref,
                        *, bb, t, w, kpad):
    seqs = range(bb)
    groups = range(N_KV_HEADS)
    tok = lambda b: slice(b * t, (b + 1) * t)
    gd = lambda g: slice(g * HEAD_DIM_ATT, (g + 1) * HEAD_DIM_ATT)
    zpad = jnp.zeros((kpad - w - t, KV_W), F32)
    kk, vv = [], []
    for b in seqs:
        ck = ck_ref[b]
        cv = cv_ref[b]
        k_new = kvf_ref[tok(b), :KV_W]
        v_new = kvf_ref[tok(b), KV_W:]
        kw_ref[b, :w - t, :] = ck[t:, :]
        kw_ref[b, w - t:, :] = k_new
        vw_ref[b, :w - t, :] = cv[t:, :]
        vw_ref[b, w - t:, :] = v_new
        kk.append(jnp.concatenate([ck, k_new, zpad], axis=0).astype(BF16))
        vv.append(jnp.concatenate([cv, v_new, zpad], axis=0).astype(BF16))
    scores = []
    for b in seqs:
        qb = q_ref[tok(b), :].astype(F32)
        for g in groups:
            qg = jnp.concatenate(
                [qb[:, (g * GQA_GROUP + i) * HEAD_DIM_ATT:(g * GQA_GROUP + i + 1) * HEAD_DIM_ATT]
                 for i in range(GQA_GROUP)], axis=0).astype(BF16)
            scores.append(lax.dot_general(qg, kk[b][:, gd(g)], (((1,), (1,)), ((), ())),
                                          preferred_element_type=F32))
    probs = []
    for b in seqs:
        for g in groups:
            s = scores[b * N_KV_HEADS + g] + bias_ref[g]
            sink = sink_ref[g]
            m = jnp.maximum(jnp.max(s, axis=1, keepdims=True), sink)
            p = jnp.exp(s - m)
            denom = jnp.sum(p, axis=1, keepdims=True) + jnp.exp(sink - m)
            probs.append((p * (1.0 / denom)).astype(BF16))
    outs = [jnp.dot(probs[b * N_KV_HEADS + g], vv[b][:, gd(g)], preferred_element_type=F32)
            for b in seqs for g in groups]
    for b in seqs:
        heads = [outs[b * N_KV_HEADS + g][i * t:(i + 1) * t, :] for g in groups for i in range(GQA_GROUP)]
        ssq = jnp.zeros((t, 1), F32)
        for oi in heads:
            ssq = ssq + jnp.sum(oi * oi, axis=1, keepdims=True)
        inv = lax.rsqrt(ssq * (1.0 / ATT_W) + EPS)
        for h in range(N_HEADS_ATT):
            sl = slice(h * HEAD_DIM_ATT, (h + 1) * HEAD_DIM_ATT)
            out_ref[tok(b), sl] = (heads[h] * inv * g_ref[:, sl]).astype(BF16)


def _attn_sample(qa, kvf, ck, cv, bias, sink_rows, g_att, nseq, t, bb):
    w = ck.shape[1]
    kpad = bias.shape[-1]
    rows = GQA_GROUP * t
    kern = functools.partial(_attn_sample_kernel, bb=bb, t=t, w=w, kpad=kpad)
    tok = lambda wd: pl.BlockSpec((bb * t, wd), lambda i: (i, 0))
    cache = pl.BlockSpec((bb, w, KV_W), lambda i: (i, 0, 0))
    return pl.pallas_call(
        kern,
        grid=(nseq // bb,),
        in_specs=[tok(ATT_W), tok(2 * KV_W), cache, cache,
                  _const_spec((N_KV_HEADS, rows, kpad)), _const_spec((N_KV_HEADS, rows, 1)), _const_spec((1, ATT_W))],
        out_specs=[tok(ATT_W), cache, cache],
        out_shape=[jax.ShapeDtypeStruct((nseq * t, ATT_W), BF16),
                   jax.ShapeDtypeStruct((nseq, w, KV_W), F32), jax.ShapeDtypeStruct((nseq, w, KV_W), F32)],
        compiler_params=pltpu.CompilerParams(dimension_semantics=("arbitrary",), vmem_limit_bytes=VMEM_LIMIT),
        name="attn_sample",
    )(qa, kvf, ck, cv, bias, sink_rows, g_att)


def _mlstm_kernel(qk_ref, v_ref, om_ref, gt_ref, gm_ref, c0_ref, n0_ref, m0_ref,
                  mix_ref, c_out, n_out, m_out, c_s, n_s, m_s, *, chunk, bb):
    c = pl.program_id(1)

    @pl.when(c == 0)
    def _():
        c_s[...] = c0_ref[...]
        n_s[...] = n0_ref[...]
        m_s[...] = m0_ref[...]

    L = chunk
    t_idx = lax.broadcasted_iota(jnp.int32, (L, L), 0)
    s_idx = lax.broadcasted_iota(jnp.int32, (L, L), 1)
    causal = s_idx <= t_idx
    eye = s_idx == t_idx
    for b in range(bb):
        for h in range(N_HEADS_MLSTM):
            q = qk_ref[b, :, h * DK_MLSTM:(h + 1) * DK_MLSTM]
            k = qk_ref[b, :, QK_M_W + h * DK_MLSTM:QK_M_W + (h + 1) * DK_MLSTM]
            v = v_ref[b, :, h * DV_MLSTM:(h + 1) * DV_MLSTM]
            ig_col = gt_ref[b, :, h:h + 1]
            lf_col = gt_ref[b, :, N_HEADS_MLSTM + h:N_HEADS_MLSTM + h + 1]
            c_prev = c_s[b, h]
            n_prev = n_s[b, h]
            m_prev = m_s[b, h]
            b_row = jnp.sum(jnp.where(t_idx <= s_idx, lf_col, 0.0), axis=0, keepdims=True)
            b_col = jnp.sum(jnp.where(eye, b_row, 0.0), axis=1, keepdims=True)
            ig_row = jnp.sum(jnp.where(eye, ig_col, 0.0), axis=0, keepdims=True)
            log_d = jnp.where(causal, b_col - b_row + ig_row, -jnp.inf)
            m_t = jnp.maximum(b_col + m_prev, jnp.max(log_d, axis=1, keepdims=True))
            d = jnp.exp(log_d - m_t)
            inter = jnp.exp(b_col + m_prev - m_t)
            s = lax.dot_general(q, k, (((1,), (1,)), ((), ())), preferred_element_type=F32) * d
            num = (inter * jnp.dot(q, c_prev.astype(BF16), preferred_element_type=F32)
                   + jnp.dot(s.astype(BF16), v, preferred_element_type=F32))
            qn = jnp.sum(q.astype(F32) * n_prev, axis=1, keepdims=True)
            den = inter * qn + jnp.sum(s, axis=1, keepdims=True)
            hh = num / jnp.maximum(jnp.abs(den), jnp.exp(-m_t))
            hm = om_ref[b, :, h * DV_MLSTM:(h + 1) * DV_MLSTM] * hh
            y = hm * lax.rsqrt(jnp.mean(hm * hm, axis=1, keepdims=True) + EPS)
            y = y * gm_ref[:, h * DV_MLSTM:(h + 1) * DV_MLSTM]
            mix_ref[b, :, h * DV_MLSTM:(h + 1) * DV_MLSTM] = y.astype(BF16)
            m_new = m_t[L - 1:L, :]
            b_end = b_col[L - 1:L, :]
            w_col = jnp.exp(b_end - b_col + ig_col - m_new)
            decay = jnp.exp(b_end + m_prev - m_new)
            kw = k.astype(F32) * w_col
            u = lax.dot_general(kw.astype(BF16), v, (((0,), (0,)), ((), ())), preferred_element_type=F32)
            c_s[b, h] = decay * c_prev + u
            n_s[b, h] = decay * n_prev + jnp.sum(kw, axis=0, keepdims=True)
            m_s[b, h] = m_new

    @pl.when(c == pl.num_programs(1) - 1)
    def _():
        c_out[...] = c_s[...]
        n_out[...] = n_s[...]
        m_out[...] = m_s[...]


def _mlstm(qkm, vm, om, gt, g_m, c0, n0, m0, chunk, bb):
    nseq, t = qkm.shape[:2]
    nc = t // chunk
    kern = functools.partial(_mlstm_kernel, chunk=chunk, bb=bb)
    tok = lambda wd: pl.BlockSpec((bb, chunk, wd), lambda i, c: (i, c, 0))
    st = lambda a, d: pl.BlockSpec((bb, N_HEADS_MLSTM, a, d), lambda i, c: (i, 0, 0, 0))
    return pl.pallas_call(
        kern,
        grid=(nseq // bb, nc),
        in_specs=[tok(2 * QK_M_W), tok(MLSTM_W), tok(MLSTM_W), tok(LANES), _const_spec((1, MLSTM_W)),
                  st(DK_MLSTM, DV_MLSTM), st(1, DK_MLSTM), st(1, 1)],
        out_specs=[tok(MLSTM_W), st(DK_MLSTM, DV_MLSTM), st(1, DK_MLSTM), st(1, 1)],
        out_shape=[jax.ShapeDtypeStruct((nseq, t, MLSTM_W), BF16),
                   jax.ShapeDtypeStruct((nseq, N_HEADS_MLSTM, DK_MLSTM, DV_MLSTM), F32),
                   jax.ShapeDtypeStruct((nseq, N_HEADS_MLSTM, 1, DK_MLSTM), F32),
                   jax.ShapeDtypeStruct((nseq, N_HEADS_MLSTM, 1, 1), F32)],
        scratch_shapes=[pltpu.VMEM((bb, N_HEADS_MLSTM, DK_MLSTM, DV_MLSTM), F32),
                        pltpu.VMEM((bb, N_HEADS_MLSTM, 1, DK_MLSTM), F32),
                        pltpu.VMEM((bb, N_HEADS_MLSTM, 1, 1), F32)],
        compiler_params=pltpu.CompilerParams(dimension_semantics=("arbitrary", "arbitrary"),
                                             vmem_limit_bytes=VMEM_LIMIT),
        name="mlstm",
    )(qkm, vm, om, gt, g_m, c0, n0, m0)


_GATE_LANE0 = N_HEADS_MLSTM


def _prefix_max_rows(x):
    n = x.shape[0]
    row = lax.broadcasted_iota(jnp.int32, x.shape, 0)
    k = 1
    while k < n:
        x = jnp.maximum(x, jnp.where(row >= k, pltpu.roll(x, k, axis=0), -jnp.inf))
        k *= 2
    return x


def _mlstm_seq_kernel(qk_ref, v_ref, om_ref, gt_ref, gm_ref, mix_ref, c_out, n_out, m_out, cn_s, m_s,
                      *, chunk, nsub, bb):
    c = pl.program_id(1)

    @pl.when(c == 0)
    def _():
        cn_s[...] = jnp.zeros_like(cn_s)
        m_s[...] = jnp.zeros_like(m_s)

    L = chunk
    t_idx = lax.broadcasted_iota(jnp.int32, (L, L), 0)
    s_idx = lax.broadcasted_iota(jnp.int32, (L, L), 1)
    causal = s_idx <= t_idx
    tril = causal.astype(F32)
    ones_col = (lax.broadcasted_iota(jnp.int32, (L, DV_MLSTM), 1) == 0).astype(BF16)
    for bi in range(bb):
        m_prev = m_s[bi]
        cn = [cn_s[bi, h] for h in range(N_HEADS_MLSTM)]
        for j in range(nsub):
            rows = slice(j * L, (j + 1) * L)
            gt = gt_ref[bi, rows, :]
            b = jnp.dot(tril, gt, precision=lax.Precision.HIGHEST, preferred_element_type=F32)
            a = pltpu.roll(gt, N_HEADS_MLSTM, axis=1) - b
            big_m = jnp.maximum(m_prev, _prefix_max_rows(a))
            inter = jnp.exp(m_prev - big_m)
            em = jnp.exp(-(b + big_m))
            m_end = big_m[L - 1:L, :]
            decay = jnp.exp(m_prev - m_end)
            m_prev = b[L - 1:L, :] + m_end
            a_t = a.T
            w_t = jnp.exp(a_t - m_end.T)
            for h in range(N_HEADS_MLSTM):
                ln = _GATE_LANE0 + h
                hv = slice(h * DV_MLSTM, (h + 1) * DV_MLSTM)
                q = qk_ref[bi, rows, h * DK_MLSTM:(h + 1) * DK_MLSTM]
                k = qk_ref[bi, rows, QK_M_W + h * DK_MLSTM:QK_M_W + (h + 1) * DK_MLSTM]
                v_ext = jnp.concatenate([v_ref[bi, rows, hv], ones_col], axis=1)
                d = jnp.exp(jnp.where(causal, a_t[ln:ln + 1, :] - big_m[:, ln:ln + 1], -jnp.inf))
                s = (lax.dot_general(q, k, (((1,), (1,)), ((), ())), preferred_element_type=F32) * d).astype(BF16)
                nd = (inter[:, ln:ln + 1] * jnp.dot(q, cn[h].astype(BF16), preferred_element_type=F32)
                      + jnp.dot(s, v_ext, preferred_element_type=F32))
                r = 1.0 / jnp.maximum(jnp.abs(nd[:, DV_MLSTM:DV_MLSTM + 1]), em[:, ln:ln + 1])
                t = om_ref[bi, rows, hv] * nd[:, :DV_MLSTM]
                msq = jnp.mean(t * t, axis=1, keepdims=True)
                scale = r * lax.rsqrt(r * r * msq + EPS)
                mix_ref[bi, rows, hv] = (t * scale * gm_ref[:, hv]).astype(BF16)
                kw_t = (k.astype(F32).T * w_t[ln:ln + 1, :]).astype(BF16)
                cn[h] = decay[:, ln:ln + 1] * cn[h] + jnp.dot(kw_t, v_ext, preferred_element_type=F32)
        m_s[bi] = m_prev
        for h in range(N_HEADS_MLSTM):
            cn_s[bi, h] = cn[h]

    @pl.when(c == pl.num_programs(1) - 1)
    def _():
        c_out[...] = cn_s[:, :, :, :DV_MLSTM]
        n_out[...] = cn_s[:, :, :, DV_MLSTM:DV_MLSTM + 1]
        m_out[...] = m_s[...]


def _mlstm_seq(qkm, vm, om, gt, g_m, chunk, nsub, bb):
    nseq, t = qkm.shape[:2]
    step = chunk * nsub
    kern = functools.partial(_mlstm_seq_kernel, chunk=chunk, nsub=nsub, bb=bb)
    tok = lambda wd: pl.BlockSpec((bb, step, wd), lambda i, c: (i, c, 0))
    st = lambda a, d: pl.BlockSpec((bb, N_HEADS_MLSTM, a, d), lambda i, c: (i, 0, 0, 0))
    return pl.pallas_call(
        kern,
        grid=(nseq // bb, t // step),
        in_specs=[tok(2 * QK_M_W), tok(MLSTM_W), tok(MLSTM_W), tok(LANES), _const_spec((1, MLSTM_W))],
        out_specs=[tok(MLSTM_W), st(DK_MLSTM, DV_MLSTM), st(DK_MLSTM, 1),
                   pl.BlockSpec((bb, 1, LANES), lambda i, c: (i, 0, 0))],
        out_shape=[jax.ShapeDtypeStruct((nseq, t, MLSTM_W), BF16),
                   jax.ShapeDtypeStruct((nseq, N_HEADS_MLSTM, DK_MLSTM, DV_MLSTM), F32),
                   jax.ShapeDtypeStruct((nseq, N_HEADS_MLSTM, DK_MLSTM, 1), F32),
                   jax.ShapeDtypeStruct((nseq, 1, LANES), F32)],
        scratch_shapes=[pltpu.VMEM((bb, N_HEADS_MLSTM, DK_MLSTM, 2 * DV_MLSTM), F32),
                        pltpu.VMEM((bb, 1, LANES), F32)],
        compiler_params=pltpu.CompilerParams(dimension_semantics=("arbitrary", "arbitrary"),
                                             vmem_limit_bytes=VMEM_LIMIT),
        name="mlstm_seq",
    )(qkm, vm, om, gt, g_m)


SEQ_PER_BLOCK = 16


def _seg_scan_rows(x, seg, forward):
    n = x.shape[0]
    pos = lax.broadcasted_iota(jnp.int32, x.shape, 0) % seg
    k = 1
    while k < seg:
        if forward:
            x = jnp.maximum(x, jnp.where(pos >= k, pltpu.roll(x, k, axis=0), -jnp.inf))
        else:
            x = jnp.maximum(x, jnp.where(pos < seg - k, pltpu.roll(x, n - k, axis=0), -jnp.inf))
        k *= 2
    return x


def _mlstm_blk_kernel(qk_ref, v_ref, om_ref, gt_ref, gm_ref, c0_ref, n0_ref, nrow_ref, m0_ref,
                      mix_ref, c_out, n_out, m_out, *, t):
    S = SEQ_PER_BLOCK
    L = S * t
    hi = lax.Precision.HIGHEST
    ti = lax.broadcasted_iota(jnp.int32, (L, L), 0)
    si = lax.broadcasted_iota(jnp.int32, (L, L), 1)
    same = (ti // t) == (si // t)
    causal = same & (si <= ti)
    tril_seg = causal.astype(F32)
    ones_seg = same.astype(F32)
    ones_col = (lax.broadcasted_iota(jnp.int32, (L, DV_MLSTM), 1) == 0).astype(BF16)
    bd_mask = (lax.broadcasted_iota(jnp.int32, (L, S * DK_MLSTM), 0) // t
               == lax.broadcasted_iota(jnp.int32, (L, S * DK_MLSTM), 1) // DK_MLSTM)
    bd_mask_t = (lax.broadcasted_iota(jnp.int32, (S * DK_MLSTM, L), 0) // DK_MLSTM
                 == lax.broadcasted_iota(jnp.int32, (S * DK_MLSTM, L), 1) // t)
    heads = range(N_HEADS_MLSTM)
    hk = lambda h: slice(h * DK_MLSTM, (h + 1) * DK_MLSTM)
    hkk = lambda h: slice(QK_M_W + h * DK_MLSTM, QK_M_W + (h + 1) * DK_MLSTM)
    hv = lambda h: slice(h * DV_MLSTM, (h + 1) * DV_MLSTM)

    gt = gt_ref[...]
    m_prev = m0_ref[...]
    b = jnp.dot(tril_seg, gt, precision=hi, preferred_element_type=F32)
    b_end = jnp.dot(ones_seg, gt, precision=hi, preferred_element_type=F32)
    a = pltpu.roll(gt, N_HEADS_MLSTM, axis=1) - b
    big_m = jnp.maximum(m_prev, _seg_scan_rows(a, t, True))
    inter = jnp.exp(m_prev - big_m)
    em = jnp.exp(-(b + big_m))
    m_end = _seg_scan_rows(big_m, t, False)
    w = jnp.exp(a - m_end)
    decay = jnp.exp(m_prev - m_end)
    m_out[...] = b_end + m_end
    a_t = a.T
    q = [qk_ref[:, hk(h)] for h in heads]
    k = [qk_ref[:, hkk(h)] for h in heads]
    v_ext = [jnp.concatenate([v_ref[:, hv(h)], ones_col], axis=1) for h in heads]
    qk = [lax.dot_general(q[h], k[h], (((1,), (1,)), ((), ())), preferred_element_type=F32) for h in heads]
    c_all = [c0_ref[:, h].reshape(S * DK_MLSTM, DV_MLSTM) for h in heads]
    q_bd = []
    for h in heads:
        q2 = jnp.concatenate([q[h], q[h]], axis=1)
        q_bd.append(jnp.where(bd_mask, jnp.concatenate([q2] * (S // 2), axis=1), jnp.zeros((), BF16)))
    qc = [jnp.dot(q_bd[h], c_all[h].astype(BF16), preferred_element_type=F32) for h in heads]
    s = []
    for h in heads:
        ln = _GATE_LANE0 + h
        d = jnp.exp(jnp.where(causal, a_t[ln:ln + 1, :] - big_m[:, ln:ln + 1], -jnp.inf))
        s.append((qk[h] * d).astype(BF16))
    sv = [jnp.dot(s[h], v_ext[h], preferred_element_type=F32) for h in heads]
    kw, kw_bd = [], []
    for h in heads:
        ln = _GATE_LANE0 + h
        kw_f = k[h].astype(F32) * w[:, ln:ln + 1]
        kw.append(kw_f.astype(BF16))
        kw_bd.append(jnp.where(bd_mask_t, jnp.concatenate([kw_f.T] * S, axis=0), 0.0).astype(BF16))
    upd = [jnp.dot(kw_bd[h], v_ref[:, hv(h)], preferred_element_type=F32) for h in heads]
    seq_of_tok = lax.broadcasted_iota(jnp.int32, (S, L), 1) // t == lax.broadcasted_iota(jnp.int32, (S, L), 0)
    first_tok = lax.broadcasted_iota(jnp.int32, (S, L), 1) == lax.broadcasted_iota(jnp.int32, (S, L), 0) * t
    n_upd = [jnp.dot(seq_of_tok.astype(BF16), kw[h], preferred_element_type=F32) for h in heads]
    decay_seq = jnp.dot(first_tok.astype(F32), decay, precision=hi, preferred_element_type=F32)
    for h in heads:
        ln = _GATE_LANE0 + h
        inter_h = inter[:, ln:ln + 1]
        qn = jnp.sum(q[h].astype(F32) * nrow_ref[:, hk(h)], axis=1, keepdims=True)
        den = inter_h * qn + sv[h][:, DV_MLSTM:DV_MLSTM + 1]
        r = 1.0 / jnp.maximum(jnp.abs(den), em[:, ln:ln + 1])
        tt = om_ref[:, hv(h)] * (inter_h * qc[h] + sv[h][:, :DV_MLSTM])
        msq = jnp.mean(tt * tt, axis=1, keepdims=True)
        scale = r * lax.rsqrt(r * r * msq + EPS)
        mix_ref[:, hv(h)] = (tt * scale * gm_ref[:, hv(h)]).astype(BF16)
        dec = decay[:, ln:ln + 1].reshape(S, t, 1)[:, 0:1, :]
        c_out[:, h] = dec * c0_ref[:, h] + upd[h].reshape(S, DK_MLSTM, DV_MLSTM)
        n_out[:, hk(h)] = decay_seq[:, ln:ln + 1] * n0_ref[:, hk(h)] + n_upd[h]


def _mlstm_blk(qkm, vm, om, gt, g_m, c0, n0_seq, n0_rows, m0_rows, nseq, t):
    S = SEQ_PER_BLOCK
    L = S * t
    tok = lambda wd: pl.BlockSpec((L, wd), lambda i: (i, 0))
    st_c = pl.BlockSpec((S, N_HEADS_MLSTM, DK_MLSTM, DV_MLSTM), lambda i: (i, 0, 0, 0))
    st_n = pl.BlockSpec((S, QK_M_W), lambda i: (i, 0))
    sds = jax.ShapeDtypeStruct
    return pl.pallas_call(
        functools.partial(_mlstm_blk_kernel, t=t),
        grid=(nseq // S,),
        in_specs=[tok(2 * QK_M_W), tok(MLSTM_W), tok(MLSTM_W), tok(LANES), _const_spec((1, MLSTM_W)),
                  st_c, st_n, tok(QK_M_W), tok(LANES)],
        out_specs=[tok(MLSTM_W), st_c, st_n, tok(LANES)],
        out_shape=[sds((nseq * t, MLSTM_W), BF16), sds((nseq, N_HEADS_MLSTM, DK_MLSTM, DV_MLSTM), F32),
                   sds((nseq, QK_M_W), F32), sds((nseq * t, LANES), F32)],
        compiler_params=pltpu.CompilerParams(dimension_semantics=("arbitrary",), vmem_limit_bytes=VMEM_LIMIT),
        name="mlstm_blk",
    )(qkm, vm, om, gt, g_m, c0, n0_seq, n0_rows, m0_rows)


_T_QA = 0
_T_VA = _T_QA + ATT_W
_T_QM = _T_VA + KV_W
_T_VM = _T_QM + QK_M_W
_T_OM = _T_VM + MLSTM_W
_T_G = _T_OM + MLSTM_W
_T_ROWS = _T_G + 16
_N_KV = 0
_N_KM = _N_KV + 2 * KV_W
_N_G = _N_KM + QK_M_W
_N_COLS = _N_G + LANES


def _log_sigmoid(g):
    return jnp.minimum(g, 0.0) - jnp.log1p(jnp.exp(-jnp.abs(g)))


def _in_proj_t_kernel(x_ref, g_ref, wn_ref, wt_ref, bgr_ref, bgc_ref,
                      k_ref, kvf_ref, km_ref, gt_ref, qT_ref, vT_ref, qmT_ref, vmT_ref, omT_ref, gT_ref):
    xf = x_ref[...]
    hb = (xf * lax.rsqrt(jnp.mean(xf * xf, axis=-1, keepdims=True) + EPS) * g_ref[...]).astype(BF16)

    z_n = jnp.dot(hb, wn_ref[...], preferred_element_type=F32)
    z_t = lax.dot_general(wt_ref[...], hb, (((1,), (1,)), ((), ())), preferred_element_type=F32)
    nn = lambda lo, width: z_n[:, lo:lo + width]
    nt = lambda lo, width: z_t[lo:lo + width, :]

    kv = nn(_N_KV, 2 * KV_W)
    kvf_ref[...] = kv
    k_ref[...] = kv[:, :KV_W].astype(BF16)
    km_ref[...] = (nn(_N_KM, QK_M_W) * (DK_MLSTM ** -0.5)).astype(BF16)
    g = nn(_N_G, LANES) + bgr_ref[...]
    col = lax.broadcasted_iota(jnp.int32, g.shape, 1)
    gt_ref[...] = jnp.where(col < N_HEADS_MLSTM, g, _log_sigmoid(g))
    qT_ref[...] = (nt(_T_QA, ATT_W) * (HEAD_DIM_ATT ** -0.5)).astype(BF16)
    vT_ref[...] = nt(_T_VA, KV_W).astype(BF16)
    qmT_ref[...] = nt(_T_QM, QK_M_W).astype(BF16)
    vmT_ref[...] = nt(_T_VM, MLSTM_W).astype(BF16)
    omT_ref[...] = jax.nn.sigmoid(nt(_T_OM, MLSTM_W))
    g_t = nt(_T_G, 16)[:N_GATES, :] + bgc_ref[...]
    row = lax.broadcasted_iota(jnp.int32, g_t.shape, 0)
    gT_ref[...] = jnp.where(row < N_HEADS_MLSTM, g_t, _log_sigmoid(g_t))


def _in_proj_t(x2d, g_norm, w_n, w_t, bg_row, bg_col, tm):
    n = x2d.shape[0]
    row = lambda w: pl.BlockSpec((tm, w), lambda i: (i, 0))
    colb = lambda r: pl.BlockSpec((r, tm), lambda i: (0, i))
    sds = jax.ShapeDtypeStruct
    return pl.pallas_call(
        _in_proj_t_kernel,
        grid=(n // tm,),
        in_specs=[row(D_MODEL), _const_spec((1, D_MODEL)), _const_spec(w_n.shape), _const_spec(w_t.shape),
                  _const_spec((1, LANES)), _const_spec((N_GATES, 1))],
        out_specs=[row(KV_W), row(2 * KV_W), row(QK_M_W), row(LANES),
                   colb(ATT_W), colb(KV_W), colb(QK_M_W), colb(MLSTM_W), colb(MLSTM_W), colb(N_GATES)],
        out_shape=[sds((n, KV_W), BF16), sds((n, 2 * KV_W), F32), sds((n, QK_M_W), BF16), sds((n, LANES), F32),
                   sds((ATT_W, n), BF16), sds((KV_W, n), BF16), sds((QK_M_W, n), BF16), sds((MLSTM_W, n), BF16),
                   sds((MLSTM_W, n), F32), sds((N_GATES, n), F32)],
        compiler_params=pltpu.CompilerParams(dimension_semantics=("arbitrary",), vmem_limit_bytes=VMEM_LIMIT),
        name="in_proj_t",
    )(x2d, g_norm, w_n, w_t, bg_row, bg_col)


def _in_proj_s_kernel(x_ref, g_ref, wn_ref, wt_ref, bgr_ref, qa_ref, kvf_ref, qkm_ref, vm_ref, om_ref, gt_ref):
    xf = x_ref[...]
    hb = (xf * lax.rsqrt(jnp.mean(xf * xf, axis=-1, keepdims=True) + EPS) * g_ref[...]).astype(BF16)
    z_n = jnp.dot(hb, wn_ref[...], preferred_element_type=F32)
    z_t = lax.dot_general(hb, wt_ref[...], (((1,), (1,)), ((), ())), preferred_element_type=F32)
    qa_ref[...] = (z_t[:, _T_QA:_T_QA + ATT_W] * (HEAD_DIM_ATT ** -0.5)).astype(BF16)
    kvf_ref[:, :KV_W] = z_n[:, _N_KV:_N_KV + KV_W]
    kvf_ref[:, KV_W:] = z_t[:, _T_VA:_T_VA + KV_W]
    qkm_ref[:, :QK_M_W] = z_t[:, _T_QM:_T_QM + QK_M_W].astype(BF16)
    qkm_ref[:, QK_M_W:] = (z_n[:, _N_KM:_N_KM + QK_M_W] * (DK_MLSTM ** -0.5)).astype(BF16)
    vm_ref[...] = z_t[:, _T_VM:_T_VM + MLSTM_W].astype(BF16)
    om_ref[...] = jax.nn.sigmoid(z_t[:, _T_OM:_T_OM + MLSTM_W])
    g = z_n[:, _N_G:_N_G + LANES] + bgr_ref[...]
    col = lax.broadcasted_iota(jnp.int32, g.shape, 1)
    gt_ref[...] = jnp.where(col < N_HEADS_MLSTM, g, _log_sigmoid(g))


def _in_proj_s(x2d, g_norm, w_n, w_t, bg_row, tm):
    n = x2d.shape[0]
    row = lambda w: pl.BlockSpec((tm, w), lambda i: (i, 0))
    sds = jax.ShapeDtypeStruct
    return pl.pallas_call(
        _in_proj_s_kernel,
        grid=(n // tm,),
        in_specs=[row(D_MODEL), _const_spec((1, D_MODEL)), _const_spec(w_n.shape), _const_spec(w_t.shape),
                  _const_spec((1, LANES))],
        out_specs=[row(ATT_W), row(2 * KV_W), row(2 * QK_M_W), row(MLSTM_W), row(MLSTM_W), row(LANES)],
        out_shape=[sds((n, ATT_W), BF16), sds((n, 2 * KV_W), F32), sds((n, 2 * QK_M_W), BF16),
                   sds((n, MLSTM_W), BF16), sds((n, MLSTM_W), F32), sds((n, LANES), F32)],
        compiler_params=pltpu.CompilerParams(dimension_semantics=("arbitrary",), vmem_limit_bytes=VMEM_LIMIT),
        name="in_proj_s",
    )(x2d, g_norm, w_n, w_t, bg_row)


def _prompt_buckets_t():
    b = _prompt_buckets()
    return np.concatenate([b[:WINDOW].T, b[WINDOW:].T], axis=0).astype(np.int32)


def _attn_prompt_t_kernel(qT_ref, kp_ref, kc_ref, vTp_ref, vTc_ref, bias_ref, sink_ref, gb_ref, out_ref, *, qb):
    first = pl.program_id(1) == 0
    k_all = jnp.concatenate([kp_ref[...], kc_ref[...]], axis=0)
    vT_all = jnp.concatenate([vTp_ref[...], vTc_ref[...]], axis=1)
    for i in range(qb):
        cols = slice(i * WINDOW, (i + 1) * WINDOW)
        off = pl.multiple_of(jnp.where(first, 2 * WINDOW, 0), 2 * WINDOW) if i == 0 else 0
        k2 = k_all[i * WINDOW:(i + 2) * WINDOW, :]
        vT2 = vT_all[:, i * WINDOW:(i + 2) * WINDOW]
        hd = lambda h: slice((h // GQA_GROUP) * HEAD_DIM_ATT, (h // GQA_GROUP + 1) * HEAD_DIM_ATT)
        scores = [jnp.dot(k2[:, hd(h)], qT_ref[h * HEAD_DIM_ATT:(h + 1) * HEAD_DIM_ATT, cols],
                          preferred_element_type=F32) for h in range(N_HEADS_ATT)]
        probs, rden = [], []
        for h in range(N_HEADS_ATT):
            s = scores[h] + bias_ref[h, pl.ds(off, 2 * WINDOW), :]
            sink = sink_ref[h]
            m = jnp.maximum(jnp.max(s, axis=0, keepdims=True), sink)
            p = jnp.exp(s - m)
            rden.append(1.0 / (jnp.sum(p, axis=0, keepdims=True) + jnp.exp(sink - m)))
            probs.append(p.astype(BF16))
        outs = []
        ssq = jnp.zeros((1, WINDOW), F32)
        for h in range(N_HEADS_ATT):
            o = jnp.dot(vT2[hd(h), :], probs[h], preferred_element_type=F32) * rden[h]
            outs.append(o)
            ssq = ssq + jnp.sum(o * o, axis=0, keepdims=True)
        inv = lax.rsqrt(ssq * (1.0 / ATT_W) + EPS)
        for h in range(N_HEADS_ATT):
            rows = slice(h * HEAD_DIM_ATT, (h + 1) * HEAD_DIM_ATT)
            out_ref[rows, cols] = (outs[h] * inv * gb_ref[rows, :]).astype(BF16)


def _attn_prompt_t(qT, k, vT, bias, sinks, g_b, batch, seq, qb):
    nb = seq // WINDOW
    ns = nb // qb
    prev = lambda b, j: jnp.maximum(b * nb + j * qb - 1, 0)
    return pl.pallas_call(
        functools.partial(_attn_prompt_t_kernel, qb=qb),
        grid=(batch, ns),
        in_specs=[pl.BlockSpec((ATT_W, qb * WINDOW), lambda b, j: (0, b * ns + j)),
                  pl.BlockSpec((WINDOW, KV_W), lambda b, j: (prev(b, j), 0)),
                  pl.BlockSpec((qb * WINDOW, KV_W), lambda b, j: (b * ns + j, 0)),
                  pl.BlockSpec((KV_W, WINDOW), lambda b, j: (0, prev(b, j))),
                  pl.BlockSpec((KV_W, qb * WINDOW), lambda b, j: (0, b * ns + j)),
                  _const_spec(bias.shape),
                  pl.BlockSpec(memory_space=pltpu.SMEM),
                  _const_spec(g_b.shape)],
        out_specs=pl.BlockSpec((ATT_W, qb * WINDOW), lambda b, j: (0, b * ns + j)),
        out_shape=jax.ShapeDtypeStruct((ATT_W, batch * seq), BF16),
        compiler_params=pltpu.CompilerParams(dimension_semantics=("arbitrary", "arbitrary"),
                                             vmem_limit_bytes=VMEM_LIMIT),
        name="attn_prompt_t",
    )(qT, k, k, vT, vT, bias, sinks, g_b)


def _prefix_max_lanes(x):
    n = x.shape[1]
    col = lax.broadcasted_iota(jnp.int32, x.shape, 1)
    k = 1
    while k < n:
        x = jnp.maximum(x, jnp.where(col >= k, pltpu.roll(x, k, axis=1), -jnp.inf))
        k *= 2
    return x


def _mlstm_seq_t_kernel(qT_ref, k_ref, vT_ref, omT_ref, gt_ref, gT_ref, gb_ref, mix_ref, c_out, n_out, m_out,
                        cn_s, mr_s, mc_s, *, chunk, nsub):
    c = pl.program_id(1)

    @pl.when(c == 0)
    def _():
        cn_s[...] = jnp.zeros_like(cn_s)
        mr_s[...] = jnp.zeros_like(mr_s)
        mc_s[...] = jnp.zeros_like(mc_s)

    L = chunk
    i0 = lax.broadcasted_iota(jnp.int32, (L, L), 0)
    i1 = lax.broadcasted_iota(jnp.int32, (L, L), 1)
    causal_t = i0 <= i1
    triu = causal_t.astype(F32)
    tril = (i1 <= i0).astype(F32)
    ones_row = (lax.broadcasted_iota(jnp.int32, (DV_MLSTM, L), 0) == 0).astype(BF16)
    cn = [cn_s[h] for h in range(N_HEADS_MLSTM)]
    m_r = mr_s[...]
    m_c = mc_s[...]
    hi = lax.Precision.HIGHEST
    heads = range(N_HEADS_MLSTM)
    subs = range(nsub)
    col_of = lambda j: slice(j * L, (j + 1) * L)
    hv = lambda h: slice(h * DV_MLSTM, (h + 1) * DV_MLSTM)
    hk = lambda h: slice(h * DK_MLSTM, (h + 1) * DK_MLSTM)
    a_c, a_r, b_c, b_r, pm_r, amax_c = [], [], [], [], [], []
    for j in subs:
        gt = gt_ref[col_of(j), :]
        g_t = gT_ref[:, col_of(j)]
        b_c.append(jnp.dot(tril, gt, precision=hi, preferred_element_type=F32))
        a_c.append(pltpu.roll(gt, N_HEADS_MLSTM, axis=1) - b_c[j])
        b_r.append(jnp.dot(g_t, triu, precision=hi, preferred_element_type=F32))
        a_r.append(pltpu.roll(g_t, N_HEADS_MLSTM, axis=0) - b_r[j])
    kq = [[jnp.dot(k_ref[col_of(j), hk(h)], qT_ref[hk(h), col_of(j)], preferred_element_type=F32)
           for h in heads] for j in subs]
    vT_ext = [[jnp.concatenate([vT_ref[hv(h), col_of(j)], ones_row], axis=0) for h in heads] for j in subs]
    for j in subs:
        pm_r.append(_prefix_max_lanes(a_r[j]))
        amax_c.append(jnp.max(a_c[j], axis=0, keepdims=True))
    big_m, inter, em, w_c, decay_c = [], [], [], [], []
    for j in subs:
        big_m.append(jnp.maximum(m_r, pm_r[j]))
        inter.append(jnp.exp(m_r - big_m[j]))
        em.append(jnp.exp(-(b_r[j] + big_m[j])))
        m_end_c = jnp.maximum(m_c, amax_c[j])
        w_c.append(jnp.exp(a_c[j] - m_end_c))
        decay_c.append(jnp.exp(m_c - m_end_c))
        m_c = b_c[j][L - 1:L, :] + m_end_c
        m_r = jnp.broadcast_to(b_r[j][:, L - 1:L] + big_m[j][:, L - 1:L], (N_GATES, L))
    sT = [[None] * N_HEADS_MLSTM for _ in subs]
    kw = [[None] * N_HEADS_MLSTM for _ in subs]
    for j in subs:
        for h in heads:
            r = _GATE_LANE0 + h
            d = jnp.exp(jnp.where(causal_t, a_c[j][:, r:r + 1] - big_m[j][r:r + 1, :], -jnp.inf))
            sT[j][h] = (kq[j][h] * d).astype(BF16)
            kw[j][h] = (k_ref[col_of(j), hk(h)].astype(F32) * w_c[j][:, r:r + 1]).astype(BF16)
    intra = [[jnp.dot(vT_ext[j][h], sT[j][h], preferred_element_type=F32) for h in heads] for j in subs]
    upd = [[jnp.dot(vT_ext[j][h], kw[j][h], preferred_element_type=F32) for h in heads] for j in subs]
    for j in subs:
        qc = [jnp.dot(cn[h].astype(BF16), qT_ref[hk(h), col_of(j)], preferred_element_type=F32) for h in heads]
        for h in heads:
            r = _GATE_LANE0 + h
            nd = inter[j][r:r + 1, :] * qc[h] + intra[j][h]
            rr = 1.0 / jnp.maximum(jnp.abs(nd[DV_MLSTM:DV_MLSTM + 1, :]), em[j][r:r + 1, :])
            t = omT_ref[hv(h), col_of(j)] * nd[:DV_MLSTM, :]
            msq = jnp.mean(t * t, axis=0, keepdims=True)
            scale = rr * lax.rsqrt(rr * rr * msq + EPS)
            mix_ref[hv(h), col_of(j)] = (t * scale * gb_ref[hv(h), :]).astype(BF16)
            cn[h] = decay_c[j][:, r:r + 1] * cn[h] + upd[j][h]
    for h in range(N_HEADS_MLSTM):
        cn_s[h] = cn[h]
    mr_s[...] = m_r
    mc_s[...] = m_c

    @pl.when(c == pl.num_programs(1) - 1)
    def _():
        for h in range(N_HEADS_MLSTM):
            c_out[0, h] = cn_s[h, :DV_MLSTM, :].T
            n_out[0, h] = cn_s[h, DV_MLSTM:DV_MLSTM + 1, :]
        m_out[0] = mc_s[...]


def _mlstm_seq_t(qmT, km, vmT, omT, gt, gT, g_b, nseq, t, chunk, nsub):
    step = chunk * nsub
    cps = t // step
    kern = functools.partial(_mlstm_seq_t_kernel, chunk=chunk, nsub=nsub)
    colb = lambda r: pl.BlockSpec((r, step), lambda i, c: (0, i * cps + c))
    rowb = lambda w: pl.BlockSpec((step, w), lambda i, c: (i * cps + c, 0))
    st = lambda a, d: pl.BlockSpec((1, N_HEADS_MLSTM, a, d), lambda i, c: (i, 0, 0, 0))
    sds = jax.ShapeDtypeStruct
    return pl.pallas_call(
        kern,
        grid=(nseq, cps),
        in_specs=[colb(QK_M_W), rowb(QK_M_W), colb(MLSTM_W), colb(MLSTM_W), rowb(LANES), colb(N_GATES),
                  _const_spec(g_b.shape)],
        out_specs=[colb(MLSTM_W), st(DK_MLSTM, DV_MLSTM), st(1, DK_MLSTM),
                   pl.BlockSpec((1, 1, LANES), lambda i, c: (i, 0, 0))],
        out_shape=[sds((MLSTM_W, nseq * t), BF16), sds((nseq, N_HEADS_MLSTM, DK_MLSTM, DV_MLSTM), F32),
                   sds((nseq, N_HEADS_MLSTM, 1, DK_MLSTM), F32), sds((nseq, 1, LANES), F32)],
        scratch_shapes=[pltpu.VMEM((N_HEADS_MLSTM, 2 * DV_MLSTM, DK_MLSTM), F32),
                        pltpu.VMEM((N_GATES, chunk), F32), pltpu.VMEM((1, LANES), F32)],
        compiler_params=pltpu.CompilerParams(dimension_semantics=("arbitrary", "arbitrary"),
                                             vmem_limit_bytes=VMEM_LIMIT),
        name="mlstm_seq_t",
    )(qmT, km, vmT, omT, gt, gT, g_b)


def _out_mlp_kernel(x_ref, a_ref, hm_ref, woa_ref, wom_ref, gmlp_ref, wup_ref, wdn_ref, gfin_ref, y_ref, acc_ref,
                    *, fchunk, mix_t):
    mix_dims = (((0,), (0,)), ((), ())) if mix_t else (((1,), (0,)), ((), ()))
    x1 = (x_ref[...]
          + lax.dot_general(a_ref[...], woa_ref[...], mix_dims, preferred_element_type=F32)
          + lax.dot_general(hm_ref[...], wom_ref[...], mix_dims, preferred_element_type=F32))
    h = (x1 * lax.rsqrt(jnp.mean(x1 * x1, axis=-1, keepdims=True) + EPS) * gmlp_ref[...]).astype(BF16)
    acc_ref[...] = x1

    def ff_chunk(f, carry):
        lo = pl.multiple_of(f * fchunk, fchunk)
        u = jnp.maximum(jnp.dot(h, wup_ref[:, pl.ds(lo, fchunk)], preferred_element_type=F32), 0.0)
        acc_ref[...] += jnp.dot((u * u).astype(BF16), wdn_ref[pl.ds(lo, fchunk), :], preferred_element_type=F32)
        return carry

    lax.fori_loop(0, D_FF // fchunk, ff_chunk, 0)
    y = acc_ref[...]
    y_ref[...] = y * lax.rsqrt(jnp.mean(y * y, axis=-1, keepdims=True) + EPS) * gfin_ref[...]


def _out_mlp(x2d, mix_a, mix_m, wo_a, wo_m, g_mlp, w_up, w_dn, g_fin, tm, fchunk, mix_t):
    n = x2d.shape[0]
    row = lambda w: pl.BlockSpec((tm, w), lambda i: (i, 0))
    mix = (lambda w: pl.BlockSpec((w, tm), lambda i: (0, i))) if mix_t else row
    return pl.pallas_call(
        functools.partial(_out_mlp_kernel, fchunk=fchunk, mix_t=mix_t),
        grid=(n // tm,),
        in_specs=[row(D_MODEL), mix(ATT_W), mix(MLSTM_W), _const_spec(wo_a.shape), _const_spec(wo_m.shape),
                  _const_spec((1, D_MODEL)), _const_spec(w_up.shape), _const_spec(w_dn.shape),
                  _const_spec((1, D_MODEL))],
        out_specs=row(D_MODEL),
        out_shape=jax.ShapeDtypeStruct((n, D_MODEL), F32),
        scratch_shapes=[pltpu.VMEM((tm, D_MODEL), F32)],
        compiler_params=pltpu.CompilerParams(dimension_semantics=("arbitrary",), vmem_limit_bytes=VMEM_LIMIT),
        name="out_mlp",
    )(x2d, mix_a, mix_m, wo_a, wo_m, g_mlp, w_up, w_dn, g_fin)


def kernel(x_prompt, x_sample, cache_k_win, cache_v_win, state_C, state_n, state_m, rel_bias, norm_mix, w_in,
           b_gates, attn_sinks, norm_attn_out, norm_mlstm_out, w_out, norm_mlp, w_up, w_down, norm_final):
    depth = w_in.shape[0]
    assert depth == 1, "single-layer trunk"
    bp, sp = x_prompt.shape[:2]
    bs, ts = x_sample.shape[:2]
    wc = cache_k_win.shape[2]
    l = 0

    bg_pad = jnp.pad(b_gates[l], (0, LANES - N_GATES)).reshape(1, LANES)
    wo_a = w_out[l, :ATT_W].astype(BF16)
    wo_m = w_out[l, ATT_W:].astype(BF16)
    wup = w_up[l].astype(BF16)
    wdn = w_down[l].astype(BF16)
    g_mix = norm_mix[l].reshape(1, D_MODEL)
    g_att = norm_attn_out[l].reshape(1, ATT_W)
    g_m = norm_mlstm_out[l].reshape(1, MLSTM_W)
    g_mlp = norm_mlp[l].reshape(1, D_MODEL)
    g_fin = norm_final.reshape(1, D_MODEL)
    sinks = attn_sinks[l]

    kpad = ((wc + ts + 15) // 16) * 16
    bias_s = _bias_table(rel_bias, _sample_buckets(ts, wc, kpad))
    bias_s = bias_s.reshape(N_KV_HEADS, GQA_GROUP * ts, kpad)
    sink_rows = jnp.repeat(sinks, ts).reshape(N_KV_HEADS, GQA_GROUP * ts, 1)

    xp = x_prompt.reshape(bp * sp, D_MODEL)
    wl = w_in[l]
    cols = lambda lo, n: wl[:, lo:lo + n]
    w_n = jnp.concatenate([cols(_O_KV, 2 * KV_W), cols(_O_QKM + QK_M_W, QK_M_W),
                           jnp.pad(cols(_O_G, N_GATES), ((0, 0), (0, LANES - N_GATES)))], axis=1).astype(BF16)
    w_t = jnp.concatenate([cols(_O_QA, ATT_W), cols(_O_KV + KV_W, KV_W), cols(_O_QKM, QK_M_W), cols(_O_VM, MLSTM_W),
                           cols(_O_OM, MLSTM_W), jnp.pad(cols(_O_G, N_GATES), ((0, 0), (0, 16 - N_GATES)))],
                          axis=1).T.astype(BF16)
    chunk = 128
    gb_att = jnp.broadcast_to(norm_attn_out[l].reshape(ATT_W, 1), (ATT_W, WINDOW))
    gb_m = jnp.broadcast_to(norm_mlstm_out[l].reshape(MLSTM_W, 1), (MLSTM_W, chunk))
    k_a, kvf, km, gt, qT, vT, qmT, vmT, omT, gT = _in_proj_t(xp, g_mix, w_n, w_t, bg_pad,
                                                             b_gates[l].reshape(N_GATES, 1), tm=1024)
    bias_t = _bias_table(rel_bias, _prompt_buckets_t())
    mix_a = _attn_prompt_t(qT, k_a, vT, bias_t, sinks, gb_att, bp, sp, qb=8)
    mix_m, c_p, n_p, m_p = _mlstm_seq_t(qmT, km, vmT, omT, gt, gT, gb_m, bp, sp, chunk=chunk, nsub=4)
    m_p = m_p[:, 0, _GATE_LANE0:_GATE_LANE0 + N_HEADS_MLSTM]
    y_p = _out_mlp(xp, mix_a, mix_m, wo_a, wo_m, g_mlp, wup, wdn, g_fin, tm=512, fchunk=D_FF, mix_t=True)
    win_p = min(WINDOW, sp)
    kv_win = kvf.reshape(bp, sp, 2 * KV_W)[:, sp - win_p:, :]
    k_win_p = kv_win[:, :, :KV_W].reshape(1, bp, win_p, N_KV_HEADS, HEAD_DIM_ATT)
    v_win_p = kv_win[:, :, KV_W:].reshape(1, bp, win_p, N_KV_HEADS, HEAD_DIM_ATT)

    xs = x_sample.reshape(bs * ts, D_MODEL)
    qa, kvf, qkm, vm, om, gt = _in_proj_s(xs, g_mix, w_n, w_t, bg_pad, tm=512)
    ck = cache_k_win[l].reshape(bs, wc, KV_W)
    cv = cache_v_win[l].reshape(bs, wc, KV_W)
    mix_a, kw_s, vw_s = _attn_sample(qa, kvf, ck, cv, bias_s, sink_rows, g_att, bs, ts, bb=8)
    assert math.gcd(ts, 64) == ts and bs % SEQ_PER_BLOCK == 0, "sample group: one mLSTM chunk per sequence"
    n0 = state_n[l].reshape(bs, QK_M_W)
    n0_rows = jnp.repeat(n0, ts, axis=0)
    m0_rows = jnp.pad(jnp.repeat(state_m[l], ts, axis=0),
                      ((0, 0), (_GATE_LANE0, LANES - _GATE_LANE0 - N_HEADS_MLSTM)))
    mix_m, c_s, n_s, m_rows = _mlstm_blk(qkm, vm, om, gt, g_m, state_C[l], n0, n0_rows, m0_rows, bs, ts)
    m_s = m_rows[::ts, _GATE_LANE0:_GATE_LANE0 + N_HEADS_MLSTM]
    y_s = _out_mlp(xs, mix_a, mix_m, wo_a, wo_m, g_mlp, wup, wdn, g_fin, tm=512, fchunk=D_FF,
                   mix_t=False)

    return (y_p.reshape(bp, sp, D_MODEL), y_s.reshape(bs, ts, D_MODEL),
            k_win_p, v_win_p,
            c_p[None], n_p.reshape(1, bp, N_HEADS_MLSTM, DK_MLSTM), m_p.reshape(1, bp, N_HEADS_MLSTM),
            kw_s.reshape(1, bs, wc, N_KV_HEADS, HEAD_DIM_ATT), vw_s.reshape(1, bs, wc, N_KV_HEADS, HEAD_DIM_ATT),
            c_s[None], n_s.reshape(1, bs, N_HEADS_MLSTM, DK_MLSTM), m_s.reshape(1, bs, N_HEADS_MLSTM))
```

```python
import functools
import math

import numpy as np
import jax
import jax.numpy as jnp
from jax import lax
from jax.experimental import pallas as pl
from jax.experimental.pallas import tpu as pltpu

D_MODEL = 1024
N_HEADS_ATT = 8
N_KV_HEADS = 2
GQA_GROUP = N_HEADS_ATT // N_KV_HEADS
HEAD_DIM_ATT = 64
ATT_W = N_HEADS_ATT * HEAD_DIM_ATT
KV_W = N_KV_HEADS * HEAD_DIM_ATT
WINDOW = 128
NUM_BUCKETS = 32
MAX_DISTANCE = 128
N_HEADS_MLSTM = 4
DV_MLSTM = 128
DK_MLSTM = 64
MLSTM_W = N_HEADS_MLSTM * DV_MLSTM
QK_M_W = N_HEADS_MLSTM * DK_MLSTM
D_FF = 4 * D_MODEL
EPS = 1e-6
N_GATES = 2 * N_HEADS_MLSTM
LANES = 128
BF16_ROWS = 16
VMEM_LIMIT = 56 * 1024 * 1024

F32 = jnp.float32
BF16 = jnp.bfloat16

_O_QA = 0
_O_KV = _O_QA + ATT_W
_O_QKM = _O_KV + 2 * KV_W
_O_VM = _O_QKM + 2 * QK_M_W
_O_OM = _O_VM + MLSTM_W
_O_G = _O_OM + MLSTM_W
_T_QA = 0
_T_VA = _T_QA + ATT_W
_T_QM = _T_VA + KV_W
_T_VM = _T_QM + QK_M_W
_T_OM = _T_VM + MLSTM_W
_T_G = _T_OM + MLSTM_W
_T_ROWS = _T_G + BF16_ROWS
_N_KV = 0
_N_KM = _N_KV + 2 * KV_W
_N_G = _N_KM + QK_M_W
_N_COLS = _N_G + LANES
_GATE0 = N_HEADS_MLSTM


def _const_spec(shape):
    nd = len(shape)
    return pl.BlockSpec(shape, lambda *_: (0,) * nd, pipeline_mode=pl.Buffered(1))


def _log_sigmoid(g):
    return jnp.minimum(g, 0.0) - jnp.log1p(jnp.exp(-jnp.abs(g)))


def _rms_rows(x, gain):
    return x * lax.rsqrt(jnp.mean(x * x, axis=-1, keepdims=True) + EPS) * gain


def _in_proj_t_kernel(x_ref, g_ref, wn_ref, wt_ref, bgc_ref,
                      k_ref, kvf_ref, km_ref, qT_ref, vT_ref, qmT_ref, vmT_ref, omT_ref, gT_ref):
    hb = _rms_rows(x_ref[...], g_ref[...]).astype(BF16)
    z_n = jnp.dot(hb, wn_ref[:, :_N_G], preferred_element_type=F32)
    z_t = lax.dot_general(wt_ref[...], hb, (((1,), (1,)), ((), ())), preferred_element_type=F32)
    kvf_ref[...] = z_n[:, _N_KV:_N_KV + 2 * KV_W]
    k_ref[...] = z_n[:, _N_KV:_N_KV + KV_W].astype(BF16)
    km_ref[...] = (z_n[:, _N_KM:_N_KM + QK_M_W] * (DK_MLSTM ** -0.5)).astype(BF16)
    qT_ref[...] = (z_t[_T_QA:_T_QA + ATT_W, :] * (HEAD_DIM_ATT ** -0.5)).astype(BF16)
    vT_ref[...] = z_t[_T_VA:_T_VA + KV_W, :].astype(BF16)
    qmT_ref[...] = z_t[_T_QM:_T_QM + QK_M_W, :].astype(BF16)
    vmT_ref[...] = z_t[_T_VM:_T_VM + MLSTM_W, :].astype(BF16)
    omT_ref[...] = jax.nn.sigmoid(z_t[_T_OM:_T_OM + MLSTM_W, :])
    g_t = z_t[_T_G:_T_G + N_GATES, :] + bgc_ref[...]
    row = lax.broadcasted_iota(jnp.int32, g_t.shape, 0)
    gT_ref[...] = jnp.where(row < N_HEADS_MLSTM, g_t, _log_sigmoid(g_t))


def _in_proj_t(x2d, g_norm, w_n, w_t, bg_col, tm):
    n = x2d.shape[0]
    row = lambda w: pl.BlockSpec((tm, w), lambda i: (i, 0))
    colb = lambda r: pl.BlockSpec((r, tm), lambda i: (0, i))
    sds = jax.ShapeDtypeStruct
    return pl.pallas_call(
        _in_proj_t_kernel,
        grid=(n // tm,),
        in_specs=[row(D_MODEL), _const_spec((1, D_MODEL)), _const_spec(w_n.shape), _const_spec(w_t.shape),
                  _const_spec((N_GATES, 1))],
        out_specs=[row(KV_W), row(2 * KV_W), row(QK_M_W),
                   colb(ATT_W), colb(KV_W), colb(QK_M_W), colb(MLSTM_W), colb(MLSTM_W), colb(N_GATES)],
        out_shape=[sds((n, KV_W), BF16), sds((n, 2 * KV_W), F32), sds((n, QK_M_W), BF16),
                   sds((ATT_W, n), BF16), sds((KV_W, n), BF16), sds((QK_M_W, n), BF16), sds((MLSTM_W, n), BF16),
                   sds((MLSTM_W, n), F32), sds((N_GATES, n), F32)],
        compiler_params=pltpu.CompilerParams(dimension_semantics=("arbitrary",), vmem_limit_bytes=VMEM_LIMIT),
        name="in_proj_t",
    )(x2d, g_norm, w_n, w_t, bg_col)


def _in_proj_s_kernel(x_ref, g_ref, wn_ref, wt_ref, bgr_ref, qa_ref, kvf_ref, qkm_ref, vm_ref, om_ref, gt_ref):
    hb = _rms_rows(x_ref[...], g_ref[...]).astype(BF16)
    z_n = jnp.dot(hb, wn_ref[...], preferred_element_type=F32)
    z_t = lax.dot_general(hb, wt_ref[...], (((1,), (1,)), ((), ())), preferred_element_type=F32)
    qa_ref[...] = (z_t[:, _T_QA:_T_QA + ATT_W] * (HEAD_DIM_ATT ** -0.5)).astype(BF16)
    kvf_ref[:, :KV_W] = z_n[:, _N_KV:_N_KV + KV_W]
    kvf_ref[:, KV_W:] = z_t[:, _T_VA:_T_VA + KV_W]
    qkm_ref[:, :QK_M_W] = z_t[:, _T_QM:_T_QM + QK_M_W].astype(BF16)
    qkm_ref[:, QK_M_W:] = (z_n[:, _N_KM:_N_KM + QK_M_W] * (DK_MLSTM ** -0.5)).astype(BF16)
    vm_ref[...] = z_t[:, _T_VM:_T_VM + MLSTM_W].astype(BF16)
    om_ref[...] = jax.nn.sigmoid(z_t[:, _T_OM:_T_OM + MLSTM_W])
    g = z_n[:, _N_G:_N_G + LANES] + bgr_ref[...]
    col = lax.broadcasted_iota(jnp.int32, g.shape, 1)
    gt_ref[...] = jnp.where(col < N_HEADS_MLSTM, g, _log_sigmoid(g))


def _in_proj_s(x2d, g_norm, w_n, w_t, bg_row, tm):
    n = x2d.shape[0]
    row = lambda w: pl.BlockSpec((tm, w), lambda i: (i, 0))
    sds = jax.ShapeDtypeStruct
    return pl.pallas_call(
        _in_proj_s_kernel,
        grid=(n // tm,),
        in_specs=[row(D_MODEL), _const_spec((1, D_MODEL)), _const_spec(w_n.shape), _const_spec(w_t.shape),
                  _const_spec((1, LANES))],
        out_specs=[row(ATT_W), row(2 * KV_W), row(2 * QK_M_W), row(MLSTM_W), row(MLSTM_W), row(LANES)],
        out_shape=[sds((n, ATT_W), BF16), sds((n, 2 * KV_W), F32), sds((n, 2 * QK_M_W), BF16),
                   sds((n, MLSTM_W), BF16), sds((n, MLSTM_W), F32), sds((n, LANES), F32)],
        compiler_params=pltpu.CompilerParams(dimension_semantics=("arbitrary",), vmem_limit_bytes=VMEM_LIMIT),
        name="in_proj_s",
    )(x2d, g_norm, w_n, w_t, bg_row)


def _t5_bucket_np(dist):
    max_exact = NUM_BUCKETS // 2
    d = np.maximum(dist, 0)
    ratio = np.maximum(d, 1).astype(np.float32) / np.float32(max_exact)
    large = max_exact + (np.log(ratio) / np.float32(math.log(MAX_DISTANCE / max_exact))
                         * np.float32(NUM_BUCKETS - max_exact)).astype(np.int32)
    large = np.minimum(large, NUM_BUCKETS - 1)
    return np.where(d < max_exact, d, large).astype(np.int32)


def _bias_table_kernel(rb_ref, bucket_ref, out_ref):
    h = pl.program_id(0)
    bucket = bucket_ref[...]
    acc = jnp.full(bucket.shape, -jnp.inf, F32)
    for j in range(NUM_BUCKETS):
        acc = jnp.where(bucket == j, rb_ref[j, h], acc)
    out_ref[0] = acc


def _bias_table(rel_bias, bucket):
    r, c = bucket.shape
    return pl.pallas_call(
        _bias_table_kernel,
        grid=(N_HEADS_ATT,),
        in_specs=[pl.BlockSpec(memory_space=pltpu.SMEM), pl.BlockSpec((r, c), lambda h: (0, 0))],
        out_specs=pl.BlockSpec((1, r, c), lambda h: (h, 0, 0)),
        out_shape=jax.ShapeDtypeStruct((N_HEADS_ATT, r, c), F32),
        compiler_params=pltpu.CompilerParams(dimension_semantics=("arbitrary",)),
        name="bias_table",
    )(rel_bias, jnp.asarray(bucket))


def _prompt_buckets_t():
    qi = np.arange(WINDOW)[None, :]
    kj = np.arange(2 * WINDOW)[:, None]
    dist = qi + WINDOW - kj
    in_win = (dist >= 0) & (dist <= WINDOW)
    b = _t5_bucket_np(dist)
    general = np.where(in_win, b, -1)
    first = np.where(in_win & (kj >= WINDOW), b, -1)
    return np.concatenate([general, first], axis=0).astype(np.int32)


def _sample_buckets(t, w, padded):
    dist = (w + np.arange(t))[:, None] - np.arange(padded)[None, :]
    valid = (dist >= 0) & (dist <= WINDOW) & (np.arange(padded)[None, :] < w + t)
    return np.where(valid, _t5_bucket_np(dist), -1).astype(np.int32)


def _attn_prompt_t_kernel(qT_ref, kp_ref, kc_ref, vTp_ref, vTc_ref, bias_ref, sink_ref, gb_ref, out_ref, *, qb):
    first = pl.program_id(1) == 0
    k_all = jnp.concatenate([kp_ref[...], kc_ref[...]], axis=0)
    vT_all = jnp.concatenate([vTp_ref[...], vTc_ref[...]], axis=1)
    hd = lambda h: slice((h // GQA_GROUP) * HEAD_DIM_ATT, (h // GQA_GROUP + 1) * HEAD_DIM_ATT)
    for i in range(qb):
        cols = slice(i * WINDOW, (i + 1) * WINDOW)
        off = pl.multiple_of(jnp.where(first, 2 * WINDOW, 0), 2 * WINDOW) if i == 0 else 0
        k2 = k_all[i * WINDOW:(i + 2) * WINDOW, :]
        vT2 = vT_all[:, i * WINDOW:(i + 2) * WINDOW]
        scores = [jnp.dot(k2[:, hd(h)], qT_ref[h * HEAD_DIM_ATT:(h + 1) * HEAD_DIM_ATT, cols],
                          preferred_element_type=F32) for h in range(N_HEADS_ATT)]
        probs, rden = [], []
        for h in range(N_HEADS_ATT):
            s = scores[h] + bias_ref[h, pl.ds(off, 2 * WINDOW), :]
            sink = sink_ref[h]
            m = jnp.maximum(jnp.max(s, axis=0, keepdims=True), sink)
            p = jnp.exp(s - m)
            rden.append(1.0 / (jnp.sum(p, axis=0, keepdims=True) + jnp.exp(sink - m)))
            probs.append(p.astype(BF16))
        outs = []
        ssq = jnp.zeros((1, WINDOW), F32)
        for h in range(N_HEADS_ATT):
            o = jnp.dot(vT2[hd(h), :], probs[h], preferred_element_type=F32) * rden[h]
            outs.append(o)
            ssq = ssq + jnp.sum(o * o, axis=0, keepdims=True)
        inv = lax.rsqrt(ssq * (1.0 / ATT_W) + EPS)
        for h in range(N_HEADS_ATT):
            rows = slice(h * HEAD_DIM_ATT, (h + 1) * HEAD_DIM_ATT)
            out_ref[rows, cols] = (outs[h] * inv * gb_ref[rows, :]).astype(BF16)


def _attn_prompt_t(qT, k, vT, bias, sinks, g_b, batch, seq, qb):
    nb = seq // WINDOW
    ns = nb // qb
    prev = lambda b, j: jnp.maximum(b * nb + j * qb - 1, 0)
    return pl.pallas_call(
        functools.partial(_attn_prompt_t_kernel, qb=qb),
        grid=(batch, ns),
        in_specs=[pl.BlockSpec((ATT_W, qb * WINDOW), lambda b, j: (0, b * ns + j)),
                  pl.BlockSpec((WINDOW, KV_W), lambda b, j: (prev(b, j), 0)),
                  pl.BlockSpec((qb * WINDOW, KV_W), lambda b, j: (b * ns + j, 0)),
                  pl.BlockSpec((KV_W, WINDOW), lambda b, j: (0, prev(b, j))),
                  pl.BlockSpec((KV_W, qb * WINDOW), lambda b, j: (0, b * ns + j)),
                  _const_spec(bias.shape),
                  pl.BlockSpec(memory_space=pltpu.SMEM),
                  _const_spec(g_b.shape)],
        out_specs=pl.BlockSpec((ATT_W, qb * WINDOW), lambda b, j: (0, b * ns + j)),
        out_shape=jax.ShapeDtypeStruct((ATT_W, batch * seq), BF16),
        compiler_params=pltpu.CompilerParams(dimension_semantics=("arbitrary", "arbitrary"),
                                             vmem_limit_bytes=VMEM_LIMIT),
        name="attn_prompt_t",
    )(qT, k, k, vT, vT, bias, sinks, g_b)


def _attn_sample_kernel(q_ref, kvf_ref, ck_ref, cv_ref, bias_ref, sink_ref, g_ref, out_ref, kw_ref, vw_ref,
                        *, bb, t, w, kpad):
    seqs = range(bb)
    groups = range(N_KV_HEADS)
    tok = lambda b: slice(b * t, (b + 1) * t)
    gd = lambda g: slice(g * HEAD_DIM_ATT, (g + 1) * HEAD_DIM_ATT)
    zpad = jnp.zeros((kpad - w - t, KV_W), F32)
    kk, vv = [], []
    for b in seqs:
        ck = ck_ref[b]
        cv = cv_ref[b]
        k_new = kvf_ref[tok(b), :KV_W]
        v_new = kvf_ref[tok(b), KV_W:]
        kw_ref[b, :w - t, :] = ck[t:, :]
        kw_ref[b, w - t:, :] = k_new
        vw_ref[b, :w - t, :] = cv[t:, :]
        vw_ref[b, w - t:, :] = v_new
        kk.append(jnp.concatenate([ck, k_new, zpad], axis=0).astype(BF16))
        vv.append(jnp.concatenate([cv, v_new, zpad], axis=0).astype(BF16))
    scores = []
    for b in seqs:
        qb = q_ref[tok(b), :].astype(F32)
        for g in groups:
            qg = jnp.concatenate(
                [qb[:, (g * GQA_GROUP + i) * HEAD_DIM_ATT:(g * GQA_GROUP + i + 1) * HEAD_DIM_ATT]
                 for i in range(GQA_GROUP)], axis=0).astype(BF16)
            scores.append(lax.dot_general(qg, kk[b][:, gd(g)], (((1,), (1,)), ((), ())),
                                          preferred_element_type=F32))
    probs = []
    for b in seqs:
        for g in groups:
            s = scores[b * N_KV_HEADS + g] + bias_ref[g]
            sink = sink_ref[g]
            m = jnp.maximum(jnp.max(s, axis=1, keepdims=True), sink)
            p = jnp.exp(s - m)
            denom = jnp.sum(p, axis=1, keepdims=True) + jnp.exp(sink - m)
            probs.append((p * (1.0 / denom)).astype(BF16))
    outs = [jnp.dot(probs[b * N_KV_HEADS + g], vv[b][:, gd(g)], preferred_element_type=F32)
            for b in seqs for g in groups]
    for b in seqs:
        heads = [outs[b * N_KV_HEADS + g][i * t:(i + 1) * t, :] for g in groups for i in range(GQA_GROUP)]
        ssq = jnp.zeros((t, 1), F32)
        for oi in heads:
            ssq = ssq + jnp.sum(oi * oi, axis=1, keepdims=True)
        inv = lax.rsqrt(ssq * (1.0 / ATT_W) + EPS)
        for h in range(N_HEADS_ATT):
            sl = slice(h * HEAD_DIM_ATT, (h + 1) * HEAD_DIM_ATT)
            out_ref[tok(b), sl] = (heads[h] * inv * g_ref[:, sl]).astype(BF16)


def _attn_sample(qa, kvf, ck, cv, bias, sink_rows, g_att, nseq, t, bb):
    w = ck.shape[1]
    kpad = bias.shape[-1]
    rows = GQA_GROUP * t
    kern = functools.partial(_attn_sample_kernel, bb=bb, t=t, w=w, kpad=kpad)
    tok = lambda wd: pl.BlockSpec((bb * t, wd), lambda i: (i, 0))
    cache = pl.BlockSpec((bb, w, KV_W), lambda i: (i, 0, 0))
    return pl.pallas_call(
        kern,
        grid=(nseq // bb,),
        in_specs=[tok(ATT_W), tok(2 * KV_W), cache, cache,
                  _const_spec((N_KV_HEADS, rows, kpad)), _const_spec((N_KV_HEADS, rows, 1)), _const_spec((1, ATT_W))],
        out_specs=[tok(ATT_W), cache, cache],
        out_shape=[jax.ShapeDtypeStruct((nseq * t, ATT_W), BF16),
                   jax.ShapeDtypeStruct((nseq, w, KV_W), F32), jax.ShapeDtypeStruct((nseq, w, KV_W), F32)],
        compiler_params=pltpu.CompilerParams(dimension_semantics=("arbitrary",), vmem_limit_bytes=VMEM_LIMIT),
        name="attn_sample",
    )(qa, kvf, ck, cv, bias, sink_rows, g_att)


SEQ_PER_BLOCK = 16


def _seg_scan_rows(x, seg, forward):
    n = x.shape[0]
    pos = lax.broadcasted_iota(jnp.int32, x.shape, 0) % seg
    k = 1
    while k < seg:
        if forward:
            x = jnp.maximum(x, jnp.where(pos >= k, pltpu.roll(x, k, axis=0), -jnp.inf))
        else:
            x = jnp.maximum(x, jnp.where(pos < seg - k, pltpu.roll(x, n - k, axis=0), -jnp.inf))
        k *= 2
    return x


def _mlstm_blk_kernel(qk_ref, v_ref, om_ref, gt_ref, gm_ref, c0_ref, n0_ref, nrow_ref, m0_ref,
                      mix_ref, c_out, n_out, m_out, *, t):
    S = SEQ_PER_BLOCK
    L = S * t
    hi = lax.Precision.HIGHEST
    ti = lax.broadcasted_iota(jnp.int32, (L, L), 0)
    si = lax.broadcasted_iota(jnp.int32, (L, L), 1)
    same = (ti // t) == (si // t)
    causal = same & (si <= ti)
    tril_seg = causal.astype(F32)
    ones_seg = same.astype(F32)
    ones_col = (lax.broadcasted_iota(jnp.int32, (L, DV_MLSTM), 1) == 0).astype(BF16)
    bd_mask = (lax.broadcasted_iota(jnp.int32, (L, S * DK_MLSTM), 0) // t
               == lax.broadcasted_iota(jnp.int32, (L, S * DK_MLSTM), 1) // DK_MLSTM)
    bd_mask_t = (lax.broadcasted_iota(jnp.int32, (S * DK_MLSTM, L), 0) // DK_MLSTM
                 == lax.broadcasted_iota(jnp.int32, (S * DK_MLSTM, L), 1) // t)
    heads = range(N_HEADS_MLSTM)
    hk = lambda h: slice(h * DK_MLSTM, (h + 1) * DK_MLSTM)
    hkk = lambda h: slice(QK_M_W + h * DK_MLSTM, QK_M_W + (h + 1) * DK_MLSTM)
    hv = lambda h: slice(h * DV_MLSTM, (h + 1) * DV_MLSTM)

    gt = gt_ref[...]
    m_prev = m0_ref[...]
    b = jnp.dot(tril_seg, gt, precision=hi, preferred_element_type=F32)
    b_end = jnp.dot(ones_seg, gt, precision=hi, preferred_element_type=F32)
    a = pltpu.roll(gt, N_HEADS_MLSTM, axis=1) - b
    big_m = jnp.maximum(m_prev, _seg_scan_rows(a, t, True))
    inter = jnp.exp(m_prev - big_m)
    em = jnp.exp(-(b + big_m))
    m_end = _seg_scan_rows(big_m, t, False)
    w = jnp.exp(a - m_end)
    decay = jnp.exp(m_prev - m_end)
    m_out[...] = b_end + m_end
    a_t = a.T
    q = [qk_ref[:, hk(h)] for h in heads]
    k = [qk_ref[:, hkk(h)] for h in heads]
    v_ext = [jnp.concatenate([v_ref[:, hv(h)], ones_col], axis=1) for h in heads]
    qk = [lax.dot_general(q[h], k[h], (((1,), (1,)), ((), ())), preferred_element_type=F32) for h in heads]
    c_all = [c0_ref[:, h].reshape(S * DK_MLSTM, DV_MLSTM) for h in heads]
    q_bd = []
    for h in heads:
        q2 = jnp.concatenate([q[h], q[h]], axis=1)
        q_bd.append(jnp.where(bd_mask, jnp.concatenate([q2] * (S // 2), axis=1), jnp.zeros((), BF16)))
    qc = [jnp.dot(q_bd[h], c_all[h].astype(BF16), preferred_element_type=F32) for h in heads]
    s = []
    for h in heads:
        ln = _GATE0 + h
        d = jnp.exp(jnp.where(causal, a_t[ln:ln + 1, :] - big_m[:, ln:ln + 1], -jnp.inf))
        s.append((qk[h] * d).astype(BF16))
    sv = [jnp.dot(s[h], v_ext[h], preferred_element_type=F32) for h in heads]
    kw, kw_bd = [], []
    for h in heads:
        ln = _GATE0 + h
        kw_f = k[h].astype(F32) * w[:, ln:ln + 1]
        kw.append(kw_f.astype(BF16))
        kw_bd.append(jnp.where(bd_mask_t, jnp.concatenate([kw_f.T] * S, axis=0), 0.0).astype(BF16))
    upd = [jnp.dot(kw_bd[h], v_ref[:, hv(h)], preferred_element_type=F32) for h in heads]
    seq_of_tok = lax.broadcasted_iota(jnp.int32, (S, L), 1) // t == lax.broadcasted_iota(jnp.int32, (S, L), 0)
    first_tok = lax.broadcasted_iota(jnp.int32, (S, L), 1) == lax.broadcasted_iota(jnp.int32, (S, L), 0) * t
    n_upd = [jnp.dot(seq_of_tok.astype(BF16), kw[h], preferred_element_type=F32) for h in heads]
    decay_seq = jnp.dot(first_tok.astype(F32), decay, precision=hi, preferred_element_type=F32)
    for h in heads:
        ln = _GATE0 + h
        inter_h = inter[:, ln:ln + 1]
        qn = jnp.sum(q[h].astype(F32) * nrow_ref[:, hk(h)], axis=1, keepdims=True)
        den = inter_h * qn + sv[h][:, DV_MLSTM:DV_MLSTM + 1]
        r = 1.0 / jnp.maximum(jnp.abs(den), em[:, ln:ln + 1])
        tt = om_ref[:, hv(h)] * (inter_h * qc[h] + sv[h][:, :DV_MLSTM])
        msq = jnp.mean(tt * tt, axis=1, keepdims=True)
        scale = r * lax.rsqrt(r * r * msq + EPS)
        mix_ref[:, hv(h)] = (tt * scale * gm_ref[:, hv(h)]).astype(BF16)
        dec = decay[:, ln:ln + 1].reshape(S, t, 1)[:, 0:1, :]
        c_out[:, h] = dec * c0_ref[:, h] + upd[h].reshape(S, DK_MLSTM, DV_MLSTM)
        n_out[:, hk(h)] = decay_seq[:, ln:ln + 1] * n0_ref[:, hk(h)] + n_upd[h]


def _mlstm_blk(qkm, vm, om, gt, g_m, c0, n0_seq, n0_rows, m0_rows, nseq, t):
    S = SEQ_PER_BLOCK
    L = S * t
    tok = lambda wd: pl.BlockSpec((L, wd), lambda i: (i, 0))
    st_c = pl.BlockSpec((S, N_HEADS_MLSTM, DK_MLSTM, DV_MLSTM), lambda i: (i, 0, 0, 0))
    st_n = pl.BlockSpec((S, QK_M_W), lambda i: (i, 0))
    sds = jax.ShapeDtypeStruct
    return pl.pallas_call(
        functools.partial(_mlstm_blk_kernel, t=t),
        grid=(nseq // S,),
        in_specs=[tok(2 * QK_M_W), tok(MLSTM_W), tok(MLSTM_W), tok(LANES), _const_spec((1, MLSTM_W)),
                  st_c, st_n, tok(QK_M_W), tok(LANES)],
        out_specs=[tok(MLSTM_W), st_c, st_n, tok(LANES)],
        out_shape=[sds((nseq * t, MLSTM_W), BF16), sds((nseq, N_HEADS_MLSTM, DK_MLSTM, DV_MLSTM), F32),
                   sds((nseq, QK_M_W), F32), sds((nseq * t, LANES), F32)],
        compiler_params=pltpu.CompilerParams(dimension_semantics=("arbitrary",), vmem_limit_bytes=VMEM_LIMIT),
        name="mlstm_blk",
    )(qkm, vm, om, gt, g_m, c0, n0_seq, n0_rows, m0_rows)


def _out_mlp_kernel(x_ref, a_ref, hm_ref, woa_ref, wom_ref, gmlp_ref, wup_ref, wdn_ref, gfin_ref, y_ref):
    x1 = (x_ref[...]
          + jnp.dot(a_ref[...], woa_ref[...], preferred_element_type=F32)
          + jnp.dot(hm_ref[...], wom_ref[...], preferred_element_type=F32))
    h = _rms_rows(x1, gmlp_ref[...]).astype(BF16)
    u = jnp.maximum(jnp.dot(h, wup_ref[...], preferred_element_type=F32), 0.0)
    y = x1 + jnp.dot((u * u).astype(BF16), wdn_ref[...], preferred_element_type=F32)
    y_ref[...] = _rms_rows(y, gfin_ref[...])


def _out_mlp(x2d, mix_a, mix_m, wo_a, wo_m, g_mlp, w_up, w_dn, g_fin, tm):
    n = x2d.shape[0]
    row = lambda w: pl.BlockSpec((tm, w), lambda i: (i, 0))
    return pl.pallas_call(
        _out_mlp_kernel,
        grid=(n // tm,),
        in_specs=[row(D_MODEL), row(ATT_W), row(MLSTM_W), _const_spec(wo_a.shape), _const_spec(wo_m.shape),
                  _const_spec((1, D_MODEL)), _const_spec(w_up.shape), _const_spec(w_dn.shape),
                  _const_spec((1, D_MODEL))],
        out_specs=row(D_MODEL),
        out_shape=jax.ShapeDtypeStruct((n, D_MODEL), F32),
        compiler_params=pltpu.CompilerParams(dimension_semantics=("arbitrary",), vmem_limit_bytes=VMEM_LIMIT),
        name="out_mlp",
    )(x2d, mix_a, mix_m, wo_a, wo_m, g_mlp, w_up, w_dn, g_fin)


_CN_ROWS = DV_MLSTM + BF16_ROWS


def _prefix_max_lanes(x):
    n = x.shape[1]
    col = lax.broadcasted_iota(jnp.int32, x.shape, 1)
    k = 1
    while k < n:
        x = jnp.maximum(x, jnp.where(col >= k, pltpu.roll(x, k, axis=1), -jnp.inf))
        k *= 2
    return x


def _mlstm_stages(qT_ref, k_ref, vT_ref, omT_ref, gT_ref, gb_ref, write_mix, state, chunk, nsub):
    L = chunk
    i0 = lax.broadcasted_iota(jnp.int32, (L, L), 0)
    i1 = lax.broadcasted_iota(jnp.int32, (L, L), 1)
    causal_t = i0 <= i1
    triu = causal_t.astype(F32)
    ones_rows = (lax.broadcasted_iota(jnp.int32, (BF16_ROWS, L), 0) == 0).astype(BF16)
    zpad = jnp.zeros((LANES - 2 * N_GATES, L), F32)
    cn, m_r = state['cn'], state['m_r']
    hi = lax.Precision.HIGHEST
    heads = range(N_HEADS_MLSTM)
    subs = range(nsub)
    col_of = lambda j: slice(j * L, (j + 1) * L)
    hv = lambda h: slice(h * DV_MLSTM, (h + 1) * DV_MLSTM)
    hk = lambda h: slice(h * DK_MLSTM, (h + 1) * DK_MLSTM)
    a_r, b_r, pm_r = [], [], []
    for j in subs:
        g_t = gT_ref[:, col_of(j)]
        b_r.append(jnp.dot(g_t, triu, precision=hi, preferred_element_type=F32))
        a_r.append(pltpu.roll(g_t, N_HEADS_MLSTM, axis=0) - b_r[j])
    yield
    kq = [[jnp.dot(k_ref[col_of(j), hk(h)], qT_ref[hk(h), col_of(j)], preferred_element_type=F32)
           for h in heads] for j in subs]
    vT_ext = [[jnp.concatenate([vT_ref[hv(h), col_of(j)], ones_rows], axis=0) for h in heads] for j in subs]
    for j in subs:
        pm_r.append(_prefix_max_lanes(a_r[j]))
    yield
    big_m, inter, em, decay, cols_t = [], [], [], [], []
    for j in subs:
        big_m.append(jnp.maximum(m_r, pm_r[j]))
        inter.append(jnp.exp(m_r - big_m[j]))
        em.append(jnp.exp(-(b_r[j] + big_m[j])))
        m_end = big_m[j][:, L - 1:L]
        w_r = jnp.exp(a_r[j] - m_end)
        decay.append(jnp.exp(m_r[:, 0:1] - m_end))
        m_r = jnp.broadcast_to(b_r[j][:, L - 1:L] + m_end, (N_GATES, L))
        cols_t.append(jnp.concatenate([a_r[j], w_r, zpad], axis=0).T)
    sT = [[None] * N_HEADS_MLSTM for _ in subs]
    kw = [[None] * N_HEADS_MLSTM for _ in subs]
    intra = [[None] * N_HEADS_MLSTM for _ in subs]
    upd = [[None] * N_HEADS_MLSTM for _ in subs]
    for j in subs:
        for h in heads:
            r = _GATE0 + h
            d = jnp.exp(jnp.where(causal_t, cols_t[j][:, r:r + 1] - big_m[j][r:r + 1, :], -jnp.inf))
            sT[j][h] = (kq[j][h] * d).astype(BF16)
            w_col = cols_t[j][:, N_GATES + r:N_GATES + r + 1]
            kw[j][h] = (k_ref[col_of(j), hk(h)].astype(F32) * w_col).astype(BF16)
        for h in heads:
            intra[j][h] = jnp.dot(vT_ext[j][h], sT[j][h], preferred_element_type=F32)
            upd[j][h] = jnp.dot(vT_ext[j][h], kw[j][h], preferred_element_type=F32)
        yield
    for j in subs:
        qc = [jnp.dot(cn[h].astype(BF16), qT_ref[hk(h), col_of(j)], preferred_element_type=F32) for h in heads]
        for h in heads:
            r = _GATE0 + h
            nd = inter[j][r:r + 1, :] * qc[h] + intra[j][h]
            rr = 1.0 / jnp.maximum(jnp.abs(nd[DV_MLSTM:DV_MLSTM + 1, :]), em[j][r:r + 1, :])
            t = omT_ref[hv(h), col_of(j)] * nd[:DV_MLSTM, :]
            msq = jnp.mean(t * t, axis=0, keepdims=True)
            scale = rr * lax.rsqrt(rr * rr * msq + EPS)
            write_mix(hv(h), col_of(j), (t * scale * gb_ref[hv(h), :]).astype(BF16))
            cn[h] = decay[j][r:r + 1, :] * cn[h] + upd[j][h]
        yield
    state['m_r'] = m_r


def _mlstm_mlp_kernel(x_ref, a_ref, qT_ref, k_ref, vT_ref, omT_ref, gT_ref, gb_ref,
                      woa_ref, wom_ref, gmlp_ref, wup_ref, wdn_ref, gfin_ref,
                      y_ref, c_out, n_out, m_out, mix_s, cn_s, mr_s,
                      *, chunk, nsub, fchunk, tiles_per_seq, n_tiles):
    s = pl.program_id(0)
    slot = s % 2

    @pl.when(s == 0)
    def _():
        mix_s[...] = jnp.zeros_like(mix_s)

    @pl.when(s % tiles_per_seq == 0)
    def _():
        cn_s[...] = jnp.zeros_like(cn_s)
        mr_s[...] = jnp.zeros_like(mr_s)

    state = {'cn': [cn_s[h] for h in range(N_HEADS_MLSTM)], 'm_r': mr_s[...]}

    def write_mix(rows, cols, val):
        mix_s[slot, rows, cols] = val

    stages = _mlstm_stages(qT_ref, k_ref, vT_ref, omT_ref, gT_ref, gb_ref, write_mix, state, chunk, nsub)
    next(stages)
    tdims = (((0,), (0,)), ((), ()))
    x1 = (x_ref[...]
          + lax.dot_general(a_ref[...], woa_ref[...], tdims, preferred_element_type=F32)
          + lax.dot_general(mix_s[1 - slot], wom_ref[...], tdims, preferred_element_type=F32))
    h = _rms_rows(x1, gmlp_ref[...]).astype(BF16)
    acc = x1
    n_ff = D_FF // fchunk
    for f in range(n_ff):
        next(stages, None)
        if f % 4 == 3:
            next(stages, None)
        u = jnp.maximum(jnp.dot(h, wup_ref[:, f * fchunk:(f + 1) * fchunk], preferred_element_type=F32), 0.0)
        acc = acc + jnp.dot((u * u).astype(BF16), wdn_ref[f * fchunk:(f + 1) * fchunk, :],
                            preferred_element_type=F32)
    for _ in stages:
        pass
    y_ref[...] = _rms_rows(acc, gfin_ref[...])
    for hh in range(N_HEADS_MLSTM):
        cn_s[hh] = state['cn'][hh]
    mr_s[...] = state['m_r']

    @pl.when((s % tiles_per_seq == tiles_per_seq - 1) & (s < n_tiles))
    def _():
        for hh in range(N_HEADS_MLSTM):
            c_out[0, hh] = cn_s[hh, :DV_MLSTM, :].T
            n_out[0, hh] = cn_s[hh, DV_MLSTM:DV_MLSTM + 1, :]
        m_out[0] = mr_s[...]


def _mlstm_mlp(x2d, mix_aT, qmT, km, vmT, omT, gT, g_b, wo_a, wo_m, g_mlp, w_up, w_dn, g_fin,
               nseq, t, chunk, nsub, fchunk):
    tm = chunk * nsub
    n_tiles = (nseq * t) // tm
    tiles_per_seq = t // tm
    cur = lambda s: jnp.minimum(s, n_tiles - 1)
    prv = lambda s: jnp.maximum(s - 1, 0)
    colb = lambda r: pl.BlockSpec((r, tm), lambda s: (0, cur(s)))
    st = lambda a, d: pl.BlockSpec((1, N_HEADS_MLSTM, a, d), lambda s: (cur(s) // tiles_per_seq, 0, 0, 0))
    sds = jax.ShapeDtypeStruct
    kern = functools.partial(_mlstm_mlp_kernel, chunk=chunk, nsub=nsub, fchunk=fchunk,
                             tiles_per_seq=tiles_per_seq, n_tiles=n_tiles)
    return pl.pallas_call(
        kern,
        grid=(n_tiles + 1,),
        in_specs=[pl.BlockSpec((tm, D_MODEL), lambda s: (prv(s), 0)),
                  pl.BlockSpec((ATT_W, tm), lambda s: (0, prv(s))),
                  colb(QK_M_W), pl.BlockSpec((tm, QK_M_W), lambda s: (cur(s), 0)), colb(MLSTM_W), colb(MLSTM_W),
                  colb(N_GATES),
                  _const_spec(g_b.shape), _const_spec(wo_a.shape), _const_spec(wo_m.shape),
                  _const_spec((1, D_MODEL)), _const_spec(w_up.shape), _const_spec(w_dn.shape),
                  _const_spec((1, D_MODEL))],
        out_specs=[pl.BlockSpec((tm, D_MODEL), lambda s: (prv(s), 0)),
                   st(DK_MLSTM, DV_MLSTM), st(1, DK_MLSTM),
                   pl.BlockSpec((1, N_GATES, chunk), lambda s: (cur(s) // tiles_per_seq, 0, 0))],
        out_shape=[sds((nseq * t, D_MODEL), F32), sds((nseq, N_HEADS_MLSTM, DK_MLSTM, DV_MLSTM), F32),
                   sds((nseq, N_HEADS_MLSTM, 1, DK_MLSTM), F32), sds((nseq, N_GATES, chunk), F32)],
        scratch_shapes=[pltpu.VMEM((2, MLSTM_W, tm), BF16),
                        pltpu.VMEM((N_HEADS_MLSTM, _CN_ROWS, DK_MLSTM), F32),
                        pltpu.VMEM((N_GATES, chunk), F32)],
        compiler_params=pltpu.CompilerParams(dimension_semantics=("arbitrary",), vmem_limit_bytes=VMEM_LIMIT),
        name="mlstm_mlp",
    )(x2d, mix_aT, qmT, km, vmT, omT, gT, g_b, wo_a, wo_m, g_mlp, w_up, w_dn, g_fin)


def kernel(x_prompt, x_sample, cache_k_win, cache_v_win, state_C, state_n, state_m, rel_bias, norm_mix, w_in,
           b_gates, attn_sinks, norm_attn_out, norm_mlstm_out, w_out, norm_mlp, w_up, w_down, norm_final):
    depth = w_in.shape[0]
    assert depth == 1, "single-layer trunk"
    bp, sp = x_prompt.shape[:2]
    bs, ts = x_sample.shape[:2]
    wc = cache_k_win.shape[2]
    l = 0
    chunk = 128

    wl = w_in[l]
    cols = lambda lo, n: wl[:, lo:lo + n]
    w_n = jnp.concatenate([cols(_O_KV, 2 * KV_W), cols(_O_QKM + QK_M_W, QK_M_W),
                           jnp.pad(cols(_O_G, N_GATES), ((0, 0), (0, LANES - N_GATES)))], axis=1).astype(BF16)
    w_t = jnp.concatenate([cols(_O_QA, ATT_W), cols(_O_KV + KV_W, KV_W), cols(_O_QKM, QK_M_W), cols(_O_VM, MLSTM_W),
                           cols(_O_OM, MLSTM_W), jnp.pad(cols(_O_G, N_GATES), ((0, 0), (0, BF16_ROWS - N_GATES)))],
                          axis=1).T.astype(BF16)
    bg_row = jnp.pad(b_gates[l], (0, LANES - N_GATES)).reshape(1, LANES)
    bg_col = b_gates[l].reshape(N_GATES, 1)
    wo_a = w_out[l, :ATT_W].astype(BF16)
    wo_m = w_out[l, ATT_W:].astype(BF16)
    wup = w_up[l].astype(BF16)
    wdn = w_down[l].astype(BF16)
    g_mix = norm_mix[l].reshape(1, D_MODEL)
    g_att = norm_attn_out[l].reshape(1, ATT_W)
    g_m = norm_mlstm_out[l].reshape(1, MLSTM_W)
    g_mlp = norm_mlp[l].reshape(1, D_MODEL)
    g_fin = norm_final.reshape(1, D_MODEL)
    gb_att = jnp.broadcast_to(norm_attn_out[l].reshape(ATT_W, 1), (ATT_W, WINDOW))
    gb_m = jnp.broadcast_to(norm_mlstm_out[l].reshape(MLSTM_W, 1), (MLSTM_W, chunk))
    sinks = attn_sinks[l]

    xp = x_prompt.reshape(bp * sp, D_MODEL)
    k_a, kvf, km, qT, vT, qmT, vmT, omT, gT = _in_proj_t(xp, g_mix, w_n, w_t, bg_col, tm=1024)
    bias_t = _bias_table(rel_bias, _prompt_buckets_t())
    mix_a = _attn_prompt_t(qT, k_a, vT, bias_t, sinks, gb_att, bp, sp, qb=8)
    y_p, c_p, n_p, m_p = _mlstm_mlp(xp, mix_a, qmT, km, vmT, omT, gT, gb_m, wo_a, wo_m, g_mlp, wup, wdn, g_fin,
                                    bp, sp, chunk=chunk, nsub=4, fchunk=D_FF)
    m_p = m_p[:, _GATE0:_GATE0 + N_HEADS_MLSTM, 0]
    win_p = min(WINDOW, sp)
    kv_win = kvf.reshape(bp, sp, 2 * KV_W)[:, sp - win_p:, :]
    k_win_p = kv_win[:, :, :KV_W].reshape(1, bp, win_p, N_KV_HEADS, HEAD_DIM_ATT)
    v_win_p = kv_win[:, :, KV_W:].reshape(1, bp, win_p, N_KV_HEADS, HEAD_DIM_ATT)

    assert math.gcd(ts, 64) == ts and bs % SEQ_PER_BLOCK == 0, "sample group: one mLSTM chunk per sequence"
    kpad = ((wc + ts + BF16_ROWS - 1) // BF16_ROWS) * BF16_ROWS
    bias_s = _bias_table(rel_bias, _sample_buckets(ts, wc, kpad))
    bias_s = bias_s.reshape(N_KV_HEADS, GQA_GROUP * ts, kpad)
    sink_rows = jnp.repeat(sinks, ts).reshape(N_KV_HEADS, GQA_GROUP * ts, 1)
    xs = x_sample.reshape(bs * ts, D_MODEL)
    qa, kvf_s, qkm, vm, om, gt = _in_proj_s(xs, g_mix, w_n, w_t, bg_row, tm=512)
    ck = cache_k_win[l].reshape(bs, wc, KV_W)
    cv = cache_v_win[l].reshape(bs, wc, KV_W)
    mix_a_s, kw_s, vw_s = _attn_sample(qa, kvf_s, ck, cv, bias_s, sink_rows, g_att, bs, ts, bb=8)
    n0 = state_n[l].reshape(bs, QK_M_W)
    n0_rows = jnp.repeat(n0, ts, axis=0)
    m0_rows = jnp.pad(jnp.repeat(state_m[l], ts, axis=0),
                      ((0, 0), (_GATE0, LANES - _GATE0 - N_HEADS_MLSTM)))
    mix_m_s, c_s, n_s, m_rows = _mlstm_blk(qkm, vm, om, gt, g_m, state_C[l], n0, n0_rows, m0_rows, bs, ts)
    m_s = m_rows[::ts, _GATE0:_GATE0 + N_HEADS_MLSTM]
    y_s = _out_mlp(xs, mix_a_s, mix_m_s, wo_a, wo_m, g_mlp, wup, wdn, g_fin, tm=512)

    return (y_p.reshape(bp, sp, D_MODEL), y_s.reshape(bs, ts, D_MODEL),
            k_win_p, v_win_p,
            c_p[None], n_p.reshape(1, bp, N_HEADS_MLSTM, DK_MLSTM), m_p.reshape(1, bp, N_HEADS_MLSTM),
            kw_s.reshape(1, bs, wc, N_KV_HEADS, HEAD_DIM_ATT), vw_s.reshape(1, bs, wc, N_KV_HEADS, HEAD_DIM_ATT),
            c_s[None], n_s.reshape(1, bs, N_HEADS_MLSTM, DK_MLSTM), m_s.reshape(1, bs, N_HEADS_MLSTM))
```

```python
import functools
import math

import numpy as np
import jax
import jax.numpy as jnp
from jax import lax
from jax.experimental import pallas as pl
from jax.experimental.pallas import tpu as pltpu

D_MODEL = 1024
N_HEADS_ATT = 8
N_KV_HEADS = 2
GQA_GROUP = N_HEADS_ATT // N_KV_HEADS
HEAD_DIM_ATT = 64
ATT_W = N_HEADS_ATT * HEAD_DIM_ATT
KV_W = N_KV_HEADS * HEAD_DIM_ATT
WINDOW = 128
NUM_BUCKETS = 32
MAX_DISTANCE = 128
N_HEADS_MLSTM = 4
DV_MLSTM = 128
DK_MLSTM = 64
MLSTM_W = N_HEADS_MLSTM * DV_MLSTM
QK_M_W = N_HEADS_MLSTM * DK_MLSTM
D_FF = 4 * D_MODEL
EPS = 1e-6
N_GATES = 2 * N_HEADS_MLSTM
LANES = 128
BF16_ROWS = 16
VMEM_LIMIT = 56 * 1024 * 1024

F32 = jnp.float32
BF16 = jnp.bfloat16

_O_QA = 0
_O_KV = _O_QA + ATT_W
_O_QKM = _O_KV + 2 * KV_W
_O_VM = _O_QKM + 2 * QK_M_W
_O_OM = _O_VM + MLSTM_W
_O_G = _O_OM + MLSTM_W
_T_QA = 0
_T_VA = _T_QA + ATT_W
_T_QM = _T_VA + KV_W
_T_VM = _T_QM + QK_M_W
_T_OM = _T_VM + MLSTM_W
_T_G = _T_OM + MLSTM_W
_T_ROWS = _T_G + BF16_ROWS
_N_KV = 0
_N_KM = _N_KV + 2 * KV_W
_N_G = _N_KM + QK_M_W
_N_COLS = _N_G + LANES
_GATE0 = N_HEADS_MLSTM


def _const_spec(shape):
    nd = len(shape)
    return pl.BlockSpec(shape, lambda *_: (0,) * nd, pipeline_mode=pl.Buffered(1))


def _log_sigmoid(g):
    return jnp.minimum(g, 0.0) - jnp.log1p(jnp.exp(-jnp.abs(g)))


def _rms_rows(x, gain):
    return x * lax.rsqrt(jnp.mean(x * x, axis=-1, keepdims=True) + EPS) * gain


def _in_proj_t_kernel(x_ref, g_ref, wn_ref, wt_ref, bgc_ref,
                      k_ref, kvf_ref, km_ref, qT_ref, vT_ref, qmT_ref, vmT_ref, omT_ref, gT_ref):
    hb = _rms_rows(x_ref[...], g_ref[...]).astype(BF16)
    z_n = jnp.dot(hb, wn_ref[:, :_N_G], preferred_element_type=F32)
    z_t = lax.dot_general(wt_ref[...], hb, (((1,), (1,)), ((), ())), preferred_element_type=F32)
    kvf_ref[...] = z_n[:, _N_KV:_N_KV + 2 * KV_W]
    k_ref[...] = z_n[:, _N_KV:_N_KV + KV_W].astype(BF16)
    km_ref[...] = (z_n[:, _N_KM:_N_KM + QK_M_W] * (DK_MLSTM ** -0.5)).astype(BF16)
    qT_ref[...] = (z_t[_T_QA:_T_QA + ATT_W, :] * (HEAD_DIM_ATT ** -0.5)).astype(BF16)
    vT_ref[...] = z_t[_T_VA:_T_VA + KV_W, :].astype(BF16)
    qmT_ref[...] = z_t[_T_QM:_T_QM + QK_M_W, :].astype(BF16)
    vmT_ref[...] = z_t[_T_VM:_T_VM + MLSTM_W, :].astype(BF16)
    omT_ref[...] = jax.nn.sigmoid(z_t[_T_OM:_T_OM + MLSTM_W, :])
    g_t = z_t[_T_G:_T_G + N_GATES, :] + bgc_ref[...]
    row = lax.broadcasted_iota(jnp.int32, g_t.shape, 0)
    gT_ref[...] = jnp.where(row < N_HEADS_MLSTM, g_t, _log_sigmoid(g_t))


def _in_proj_t(x2d, g_norm, w_n, w_t, bg_col, tm):
    n = x2d.shape[0]
    row = lambda w: pl.BlockSpec((tm, w), lambda i: (i, 0))
    colb = lambda r: pl.BlockSpec((r, tm), lambda i: (0, i))
    sds = jax.ShapeDtypeStruct
    return pl.pallas_call(
        _in_proj_t_kernel,
        grid=(n // tm,),
        in_specs=[row(D_MODEL), _const_spec((1, D_MODEL)), _const_spec(w_n.shape), _const_spec(w_t.shape),
                  _const_spec((N_GATES, 1))],
        out_specs=[row(KV_W), row(2 * KV_W), row(QK_M_W),
                   colb(ATT_W), colb(KV_W), colb(QK_M_W), colb(MLSTM_W), colb(MLSTM_W), colb(N_GATES)],
        out_shape=[sds((n, KV_W), BF16), sds((n, 2 * KV_W), F32), sds((n, QK_M_W), BF16),
                   sds((ATT_W, n), BF16), sds((KV_W, n), BF16), sds((QK_M_W, n), BF16), sds((MLSTM_W, n), BF16),
                   sds((MLSTM_W, n), F32), sds((N_GATES, n), F32)],
        compiler_params=pltpu.CompilerParams(dimension_semantics=("arbitrary",), vmem_limit_bytes=VMEM_LIMIT),
        name="in_proj_t",
    )(x2d, g_norm, w_n, w_t, bg_col)


def _in_proj_s_kernel(x_ref, g_ref, wn_ref, wt_ref, bgr_ref, qa_ref, kvf_ref, qkm_ref, vm_ref, om_ref, gt_ref):
    hb = _rms_rows(x_ref[...], g_ref[...]).astype(BF16)
    z_n = jnp.dot(hb, wn_ref[...], preferred_element_type=F32)
    z_t = lax.dot_general(hb, wt_ref[...], (((1,), (1,)), ((), ())), preferred_element_type=F32)
    qa_ref[...] = (z_t[:, _T_QA:_T_QA + ATT_W] * (HEAD_DIM_ATT ** -0.5)).astype(BF16)
    kvf_ref[:, :KV_W] = z_n[:, _N_KV:_N_KV + KV_W]
    kvf_ref[:, KV_W:] = z_t[:, _T_VA:_T_VA + KV_W]
    qkm_ref[:, :QK_M_W] = z_t[:, _T_QM:_T_QM + QK_M_W].astype(BF16)
    qkm_ref[:, QK_M_W:] = (z_n[:, _N_KM:_N_KM + QK_M_W] * (DK_MLSTM ** -0.5)).astype(BF16)
    vm_ref[...] = z_t[:, _T_VM:_T_VM + MLSTM_W].astype(BF16)
    om_ref[...] = jax.nn.sigmoid(z_t[:, _T_OM:_T_OM + MLSTM_W])
    g = z_n[:, _N_G:_N_G + LANES] + bgr_ref[...]
    col = lax.broadcasted_iota(jnp.int32, g.shape, 1)
    gt_ref[...] = jnp.where(col < N_HEADS_MLSTM, g, _log_sigmoid(g))


def _in_proj_s(x2d, g_norm, w_n, w_t, bg_row, tm):
    n = x2d.shape[0]
    row = lambda w: pl.BlockSpec((tm, w), lambda i: (i, 0))
    sds = jax.ShapeDtypeStruct
    return pl.pallas_call(
        _in_proj_s_kernel,
        grid=(n // tm,),
        in_specs=[row(D_MODEL), _const_spec((1, D_MODEL)), _const_spec(w_n.shape), _const_spec(w_t.shape),
                  _const_spec((1, LANES))],
        out_specs=[row(ATT_W), row(2 * KV_W), row(2 * QK_M_W), row(MLSTM_W), row(MLSTM_W), row(LANES)],
        out_shape=[sds((n, ATT_W), BF16), sds((n, 2 * KV_W), F32), sds((n, 2 * QK_M_W), BF16),
                   sds((n, MLSTM_W), BF16), sds((n, MLSTM_W), F32), sds((n, LANES), F32)],
        compiler_params=pltpu.CompilerParams(dimension_semantics=("arbitrary",), vmem_limit_bytes=VMEM_LIMIT),
        name="in_proj_s",
    )(x2d, g_norm, w_n, w_t, bg_row)


def _t5_bucket_np(dist):
    max_exact = NUM_BUCKETS // 2
    d = np.maximum(dist, 0)
    ratio = np.maximum(d, 1).astype(np.float32) / np.float32(max_exact)
    large = max_exact + (np.log(ratio) / np.float32(math.log(MAX_DISTANCE / max_exact))
                         * np.float32(NUM_BUCKETS - max_exact)).astype(np.int32)
    large = np.minimum(large, NUM_BUCKETS - 1)
    return np.where(d < max_exact, d, large).astype(np.int32)


def _bias_table_kernel(rb_ref, bucket_ref, out_ref):
    h = pl.program_id(0)
    bucket = bucket_ref[...]
    acc = jnp.full(bucket.shape, -jnp.inf, F32)
    for j in range(NUM_BUCKETS):
        acc = jnp.where(bucket == j, rb_ref[j, h], acc)
    out_ref[0] = acc


def _bias_table(rel_bias, bucket):
    r, c = bucket.shape
    return pl.pallas_call(
        _bias_table_kernel,
        grid=(N_HEADS_ATT,),
        in_specs=[pl.BlockSpec(memory_space=pltpu.SMEM), pl.BlockSpec((r, c), lambda h: (0, 0))],
        out_specs=pl.BlockSpec((1, r, c), lambda h: (h, 0, 0)),
        out_shape=jax.ShapeDtypeStruct((N_HEADS_ATT, r, c), F32),
        compiler_params=pltpu.CompilerParams(dimension_semantics=("arbitrary",)),
        name="bias_table",
    )(rel_bias, jnp.asarray(bucket))


def _prompt_buckets_t():
    qi = np.arange(WINDOW)[None, :]
    kj = np.arange(2 * WINDOW)[:, None]
    dist = qi + WINDOW - kj
    in_win = (dist >= 0) & (dist <= WINDOW)
    b = _t5_bucket_np(dist)
    general = np.where(in_win, b, -1)
    first = np.where(in_win & (kj >= WINDOW), b, -1)
    return np.concatenate([general, first], axis=0).astype(np.int32)


def _sample_buckets(t, w, padded):
    dist = (w + np.arange(t))[:, None] - np.arange(padded)[None, :]
    valid = (dist >= 0) & (dist <= WINDOW) & (np.arange(padded)[None, :] < w + t)
    return np.where(valid, _t5_bucket_np(dist), -1).astype(np.int32)


def _attn_prompt_t_kernel(qT_ref, kp_ref, kc_ref, vTp_ref, vTc_ref, bias_ref, sink_ref, gb_ref, out_ref, *, qb):
    first = pl.program_id(1) == 0
    k_all = jnp.concatenate([kp_ref[...], kc_ref[...]], axis=0)
    vT_all = jnp.concatenate([vTp_ref[...], vTc_ref[...]], axis=1)

    def write_out(rows, cols, val):
        out_ref[rows, cols] = val

    for _ in _attn_blocks(lambda rows, cols: qT_ref[rows, cols], k_all, vT_all, first, bias_ref, sink_ref, gb_ref,
                          write_out, qb):
        pass


def _attn_prompt_t(qT, k, vT, bias, sinks, g_b, batch, seq, qb):
    nb = seq // WINDOW
    ns = nb // qb
    prev = lambda b, j: jnp.maximum(b * nb + j * qb - 1, 0)
    return pl.pallas_call(
        functools.partial(_attn_prompt_t_kernel, qb=qb),
        grid=(batch, ns),
        in_specs=[pl.BlockSpec((ATT_W, qb * WINDOW), lambda b, j: (0, b * ns + j)),
                  pl.BlockSpec((WINDOW, KV_W), lambda b, j: (prev(b, j), 0)),
                  pl.BlockSpec((qb * WINDOW, KV_W), lambda b, j: (b * ns + j, 0)),
                  pl.BlockSpec((KV_W, WINDOW), lambda b, j: (0, prev(b, j))),
                  pl.BlockSpec((KV_W, qb * WINDOW), lambda b, j: (0, b * ns + j)),
                  _const_spec(bias.shape),
                  pl.BlockSpec(memory_space=pltpu.SMEM),
                  _const_spec(g_b.shape)],
        out_specs=pl.BlockSpec((ATT_W, qb * WINDOW), lambda b, j: (0, b * ns + j)),
        out_shape=jax.ShapeDtypeStruct((ATT_W, batch * seq), BF16),
        compiler_params=pltpu.CompilerParams(dimension_semantics=("arbitrary", "arbitrary"),
                                             vmem_limit_bytes=VMEM_LIMIT),
        name="attn_prompt_t",
    )(qT, k, k, vT, vT, bias, sinks, g_b)


def _attn_block(i, qT, k_all, vT_all, first, bias_ref, sink_ref, gb_ref, write_out):
    gd = lambda g: slice(g * HEAD_DIM_ATT, (g + 1) * HEAD_DIM_ATT)
    cols = slice(i * WINDOW, (i + 1) * WINDOW)
    off = pl.multiple_of(jnp.where(first, 2 * WINDOW, 0), 2 * WINDOW) if i == 0 else 0
    k2 = k_all[i * WINDOW:(i + 2) * WINDOW, :]
    vT2 = vT_all[:, i * WINDOW:(i + 2) * WINDOW]
    scores = []
    for g in range(N_KV_HEADS):
        q_grp = jnp.concatenate([qT(slice((g * GQA_GROUP + j) * HEAD_DIM_ATT, (g * GQA_GROUP + j + 1) * HEAD_DIM_ATT),
                                    cols) for j in range(GQA_GROUP)], axis=1)
        scores.append(jnp.dot(k2[:, gd(g)], q_grp, preferred_element_type=F32))
    yield
    probs, rden = [], []
    for h in range(N_HEADS_ATT):
        g, j = divmod(h, GQA_GROUP)
        s = scores[g][:, j * WINDOW:(j + 1) * WINDOW] + bias_ref[h, pl.ds(off, 2 * WINDOW), :]
        sink = sink_ref[h]
        m = jnp.maximum(jnp.max(s, axis=0, keepdims=True), sink)
        p = jnp.exp(s - m)
        rden.append(1.0 / (jnp.sum(p, axis=0, keepdims=True) + jnp.exp(sink - m)))
        probs.append(p.astype(BF16))
    yield
    outs, ssq = [], jnp.zeros((1, WINDOW), F32)
    for g in range(N_KV_HEADS):
        p_grp = jnp.concatenate(probs[g * GQA_GROUP:(g + 1) * GQA_GROUP], axis=1)
        o_grp = jnp.dot(vT2[gd(g), :], p_grp, preferred_element_type=F32)
        for j in range(GQA_GROUP):
            o = o_grp[:, j * WINDOW:(j + 1) * WINDOW] * rden[g * GQA_GROUP + j]
            outs.append(o)
            ssq = ssq + jnp.sum(o * o, axis=0, keepdims=True)
    inv = lax.rsqrt(ssq * (1.0 / ATT_W) + EPS)
    for h in range(N_HEADS_ATT):
        rows = slice(h * HEAD_DIM_ATT, (h + 1) * HEAD_DIM_ATT)
        write_out(rows, cols, (outs[h] * inv * gb_ref[rows, :]).astype(BF16))
    yield


def _attn_blocks(qT, k_all, vT_all, first, bias_ref, sink_ref, gb_ref, write_out, qb):
    gens = [_attn_block(i, qT, k_all, vT_all, first, bias_ref, sink_ref, gb_ref, write_out) for i in range(qb)]
    next(gens[0])
    next(gens[0])
    for i in range(qb):
        if i + 1 < qb:
            next(gens[i + 1])
        yield
        next(gens[i])
        if i + 1 < qb:
            next(gens[i + 1])


PROJ_ROW_CHUNK = 256


def _proj_attn_kernel(x_ref, g_ref, wn_ref, wt_ref, bgc_ref, bias_ref, sink_ref, gb_ref,
                      kvf_ref, km_ref, qmT_ref, vmT_ref, omT_ref, gT_ref, mix_ref,
                      q_s, k_s, vT_s, hk_s, hv_s, *, tm, tiles_per_seq):
    s = pl.program_id(0)
    slot = s % 2
    prev = 1 - slot
    qb = tm // WINDOW

    @pl.when(s == 0)
    def _():
        q_s[...] = jnp.zeros_like(q_s)
        k_s[...] = jnp.zeros_like(k_s)
        vT_s[...] = jnp.zeros_like(vT_s)
        hk_s[...] = jnp.zeros_like(hk_s)
        hv_s[...] = jnp.zeros_like(hv_s)

    first = (s + tiles_per_seq - 1) % tiles_per_seq == 0
    k_all = jnp.concatenate([hk_s[...], k_s[prev]], axis=0)
    vT_all = jnp.concatenate([hv_s[...], vT_s[prev]], axis=1)

    def write_out(rows, cols, val):
        mix_ref[rows, cols] = val

    blocks = _attn_blocks(lambda rows, cols: q_s[prev, rows, cols], k_all, vT_all, first, bias_ref, sink_ref, gb_ref,
                          write_out, qb)
    hb = _rms_rows(x_ref[...], g_ref[...]).astype(BF16)

    def proj_t(lo, hi):
        return lax.dot_general(wt_ref[lo:hi, :], hb, (((1,), (1,)), ((), ())), preferred_element_type=F32)

    def put_q(lo, z):
        q_s[slot, lo:lo + z.shape[0], :] = (z * (HEAD_DIM_ATT ** -0.5)).astype(BF16)

    def put_v(lo, z):
        vT_s[slot, lo:lo + z.shape[0], :] = z.astype(BF16)

    def put_qm(lo, z):
        qmT_ref[lo:lo + z.shape[0], :] = z.astype(BF16)

    def put_vm(lo, z):
        vmT_ref[lo:lo + z.shape[0], :] = z.astype(BF16)

    def put_om(lo, z):
        omT_ref[lo:lo + z.shape[0], :] = jax.nn.sigmoid(z)

    def put_g(lo, z):
        g_t = z[:N_GATES, :] + bgc_ref[...]
        row = lax.broadcasted_iota(jnp.int32, g_t.shape, 0)
        gT_ref[...] = jnp.where(row < N_HEADS_MLSTM, g_t, _log_sigmoid(g_t))

    segments = [(_T_QA, _T_VA, put_q), (_T_VA, _T_QM, put_v), (_T_QM, _T_VM, put_qm), (_T_VM, _T_OM, put_vm),
                (_T_OM, _T_G, put_om), (_T_G, _T_ROWS, put_g)]

    def proj_rows(lo, hi):
        z = proj_t(lo, hi)
        for a, b, put in segments:
            if a < hi and lo < b:
                put(max(lo, a) - a, z[max(lo, a) - lo:min(hi, b) - lo, :])

    z_n = jnp.dot(hb, wn_ref[:, :_N_G], preferred_element_type=F32)
    kvf_ref[...] = z_n[:, _N_KV:_N_KV + 2 * KV_W]
    km_ref[...] = (z_n[:, _N_KM:_N_KM + QK_M_W] * (DK_MLSTM ** -0.5)).astype(BF16)
    k_s[slot] = z_n[:, _N_KV:_N_KV + KV_W].astype(BF16)
    bounds = [min(c * PROJ_ROW_CHUNK, _T_ROWS) for c in range(-(-_T_ROWS // PROJ_ROW_CHUNK) + 1)]
    chunks = list(zip(bounds[:-1], bounds[1:]))
    for _ in blocks:
        if chunks:
            proj_rows(*chunks.pop(0))
    for lo, hi in chunks:
        proj_rows(lo, hi)
    hk_s[...] = k_s[prev, tm - WINDOW:, :]
    hv_s[...] = vT_s[prev, :, tm - WINDOW:]


def _proj_attn(x2d, g_norm, w_n, w_t, bg_col, bias, sinks, g_b, seq, tm):
    n = x2d.shape[0]
    n_tiles = n // tm
    tiles_per_seq = seq // tm
    cur = lambda s: jnp.minimum(s, n_tiles - 1)
    prv = lambda s: jnp.maximum(s - 1, 0)
    row = lambda w: pl.BlockSpec((tm, w), lambda s: (cur(s), 0))
    colb = lambda r: pl.BlockSpec((r, tm), lambda s: (0, cur(s)))
    sds = jax.ShapeDtypeStruct
    return pl.pallas_call(
        functools.partial(_proj_attn_kernel, tm=tm, tiles_per_seq=tiles_per_seq),
        grid=(n_tiles + 1,),
        in_specs=[row(D_MODEL), _const_spec((1, D_MODEL)), _const_spec(w_n.shape), _const_spec(w_t.shape),
                  _const_spec((N_GATES, 1)), _const_spec(bias.shape), pl.BlockSpec(memory_space=pltpu.SMEM),
                  _const_spec(g_b.shape)],
        out_specs=[row(2 * KV_W), row(QK_M_W), colb(QK_M_W), colb(MLSTM_W), colb(MLSTM_W), colb(N_GATES),
                   pl.BlockSpec((ATT_W, tm), lambda s: (0, prv(s)))],
        out_shape=[sds((n, 2 * KV_W), F32), sds((n, QK_M_W), BF16), sds((QK_M_W, n), BF16), sds((MLSTM_W, n), BF16),
                   sds((MLSTM_W, n), F32), sds((N_GATES, n), F32), sds((ATT_W, n), BF16)],
        scratch_shapes=[pltpu.VMEM((2, ATT_W, tm), BF16), pltpu.VMEM((2, tm, KV_W), BF16),
                        pltpu.VMEM((2, KV_W, tm), BF16), pltpu.VMEM((WINDOW, KV_W), BF16),
                        pltpu.VMEM((KV_W, WINDOW), BF16)],
        compiler_params=pltpu.CompilerParams(dimension_semantics=("arbitrary",), vmem_limit_bytes=VMEM_LIMIT),
        name="proj_attn",
    )(x2d, g_norm, w_n, w_t, bg_col, bias, sinks, g_b)


def _attn_sample_kernel(q_ref, kvf_ref, ck_ref, cv_ref, bias_ref, sink_ref, g_ref, out_ref, kw_ref, vw_ref,
                        *, bb, t, w, kpad):
    seqs = range(bb)
    groups = range(N_KV_HEADS)
    tok = lambda b: slice(b * t, (b + 1) * t)
    gd = lambda g: slice(g * HEAD_DIM_ATT, (g + 1) * HEAD_DIM_ATT)
    zpad = jnp.zeros((kpad - w - t, KV_W), F32)
    kk, vv = [], []
    for b in seqs:
        ck = ck_ref[b]
        cv = cv_ref[b]
        k_new = kvf_ref[tok(b), :KV_W]
        v_new = kvf_ref[tok(b), KV_W:]
        kw_ref[b, :w - t, :] = ck[t:, :]
        kw_ref[b, w - t:, :] = k_new
        vw_ref[b, :w - t, :] = cv[t:, :]
        vw_ref[b, w - t:, :] = v_new
        kk.append(jnp.concatenate([ck, k_new, zpad], axis=0).astype(BF16))
        vv.append(jnp.concatenate([cv, v_new, zpad], axis=0).astype(BF16))
    scores = []
    for b in seqs:
        qb = q_ref[tok(b), :].astype(F32)
        for g in groups:
            qg = jnp.concatenate(
                [qb[:, (g * GQA_GROUP + i) * HEAD_DIM_ATT:(g * GQA_GROUP + i + 1) * HEAD_DIM_ATT]
                 for i in range(GQA_GROUP)], axis=0).astype(BF16)
            scores.append(lax.dot_general(qg, kk[b][:, gd(g)], (((1,), (1,)), ((), ())),
                                          preferred_element_type=F32))
    probs = []
    for b in seqs:
        for g in groups:
            s = scores[b * N_KV_HEADS + g] + bias_ref[g]
            sink = sink_ref[g]
            m = jnp.maximum(jnp.max(s, axis=1, keepdims=True), sink)
            p = jnp.exp(s - m)
            denom = jnp.sum(p, axis=1, keepdims=True) + jnp.exp(sink - m)
            probs.append((p * (1.0 / denom)).astype(BF16))
    outs = [jnp.dot(probs[b * N_KV_HEADS + g], vv[b][:, gd(g)], preferred_element_type=F32)
            for b in seqs for g in groups]
    for b in seqs:
        heads = [outs[b * N_KV_HEADS + g][i * t:(i + 1) * t, :] for g in groups for i in range(GQA_GROUP)]
        ssq = jnp.zeros((t, 1), F32)
        for oi in heads:
            ssq = ssq + jnp.sum(oi * oi, axis=1, keepdims=True)
        inv = lax.rsqrt(ssq * (1.0 / ATT_W) + EPS)
        for h in range(N_HEADS_ATT):
            sl = slice(h * HEAD_DIM_ATT, (h + 1) * HEAD_DIM_ATT)
            out_ref[tok(b), sl] = (heads[h] * inv * g_ref[:, sl]).astype(BF16)


def _attn_sample(qa, kvf, ck, cv, bias, sink_rows, g_att, nseq, t, bb):
    w = ck.shape[1]
    kpad = bias.shape[-1]
    rows = GQA_GROUP * t
    kern = functools.partial(_attn_sample_kernel, bb=bb, t=t, w=w, kpad=kpad)
    tok = lambda wd: pl.BlockSpec((bb * t, wd), lambda i: (i, 0))
    cache = pl.BlockSpec((bb, w, KV_W), lambda i: (i, 0, 0))
    return pl.pallas_call(
        kern,
        grid=(nseq // bb,),
        in_specs=[tok(ATT_W), tok(2 * KV_W), cache, cache,
                  _const_spec((N_KV_HEADS, rows, kpad)), _const_spec((N_KV_HEADS, rows, 1)), _const_spec((1, ATT_W))],
        out_specs=[tok(ATT_W), cache, cache],
        out_shape=[jax.ShapeDtypeStruct((nseq * t, ATT_W), BF16),
                   jax.ShapeDtypeStruct((nseq, w, KV_W), F32), jax.ShapeDtypeStruct((nseq, w, KV_W), F32)],
        compiler_params=pltpu.CompilerParams(dimension_semantics=("arbitrary",), vmem_limit_bytes=VMEM_LIMIT),
        name="attn_sample",
    )(qa, kvf, ck, cv, bias, sink_rows, g_att)


SEQ_PER_BLOCK = 16


def _seg_scan_rows(x, seg, forward):
    n = x.shape[0]
    pos = lax.broadcasted_iota(jnp.int32, x.shape, 0) % seg
    k = 1
    while k < seg:
        if forward:
            x = jnp.maximum(x, jnp.where(pos >= k, pltpu.roll(x, k, axis=0), -jnp.inf))
        else:
            x = jnp.maximum(x, jnp.where(pos < seg - k, pltpu.roll(x, n - k, axis=0), -jnp.inf))
        k *= 2
    return x


def _mlstm_blk_kernel(qk_ref, v_ref, om_ref, gt_ref, gm_ref, c0_ref, n0_ref, nrow_ref, m0_ref,
                      mix_ref, c_out, n_out, m_out, *, t):
    S = SEQ_PER_BLOCK
    L = S * t
    hi = lax.Precision.HIGHEST
    ti = lax.broadcasted_iota(jnp.int32, (L, L), 0)
    si = lax.broadcasted_iota(jnp.int32, (L, L), 1)
    same = (ti // t) == (si // t)
    causal = same & (si <= ti)
    tril_seg = causal.astype(F32)
    ones_seg = same.astype(F32)
    ones_col = (lax.broadcasted_iota(jnp.int32, (L, DV_MLSTM), 1) == 0).astype(BF16)
    bd_mask = (lax.broadcasted_iota(jnp.int32, (L, S * DK_MLSTM), 0) // t
               == lax.broadcasted_iota(jnp.int32, (L, S * DK_MLSTM), 1) // DK_MLSTM)
    bd_mask_t = (lax.broadcasted_iota(jnp.int32, (S * DK_MLSTM, L), 0) // DK_MLSTM
                 == lax.broadcasted_iota(jnp.int32, (S * DK_MLSTM, L), 1) // t)
    heads = range(N_HEADS_MLSTM)
    hk = lambda h: slice(h * DK_MLSTM, (h + 1) * DK_MLSTM)
    hkk = lambda h: slice(QK_M_W + h * DK_MLSTM, QK_M_W + (h + 1) * DK_MLSTM)
    hv = lambda h: slice(h * DV_MLSTM, (h + 1) * DV_MLSTM)

    gt = gt_ref[...]
    m_prev = m0_ref[...]
    b = jnp.dot(tril_seg, gt, precision=hi, preferred_element_type=F32)
    b_end = jnp.dot(ones_seg, gt, precision=hi, preferred_element_type=F32)
    a = pltpu.roll(gt, N_HEADS_MLSTM, axis=1) - b
    big_m = jnp.maximum(m_prev, _seg_scan_rows(a, t, True))
    inter = jnp.exp(m_prev - big_m)
    em = jnp.exp(-(b + big_m))
    m_end = _seg_scan_rows(big_m, t, False)
    w = jnp.exp(a - m_end)
    decay = jnp.exp(m_prev - m_end)
    m_out[...] = b_end + m_end
    a_t = a.T
    q = [qk_ref[:, hk(h)] for h in heads]
    k = [qk_ref[:, hkk(h)] for h in heads]
    v_ext = [jnp.concatenate([v_ref[:, hv(h)], ones_col], axis=1) for h in heads]
    qk = [lax.dot_general(q[h], k[h], (((1,), (1,)), ((), ())), preferred_element_type=F32) for h in heads]
    c_all = [c0_ref[:, h].reshape(S * DK_MLSTM, DV_MLSTM) for h in heads]
    q_bd = []
    for h in heads:
        q2 = jnp.concatenate([q[h], q[h]], axis=1)
        q_bd.append(jnp.where(bd_mask, jnp.concatenate([q2] * (S // 2), axis=1), jnp.zeros((), BF16)))
    qc = [jnp.dot(q_bd[h], c_all[h].astype(BF16), preferred_element_type=F32) for h in heads]
    s = []
    for h in heads:
        ln = _GATE0 + h
        d = jnp.exp(jnp.where(causal, a_t[ln:ln + 1, :] - big_m[:, ln:ln + 1], -jnp.inf))
        s.append((qk[h] * d).astype(BF16))
    sv = [jnp.dot(s[h], v_ext[h], preferred_element_type=F32) for h in heads]
    kw, kw_bd = [], []
    for h in heads:
        ln = _GATE0 + h
        kw_f = k[h].astype(F32) * w[:, ln:ln + 1]
        kw.append(kw_f.astype(BF16))
        kw_bd.append(jnp.where(bd_mask_t, jnp.concatenate([kw_f.T] * S, axis=0), 0.0).astype(BF16))
    upd = [jnp.dot(kw_bd[h], v_ref[:, hv(h)], preferred_element_type=F32) for h in heads]
    seq_of_tok = lax.broadcasted_iota(jnp.int32, (S, L), 1) // t == lax.broadcasted_iota(jnp.int32, (S, L), 0)
    first_tok = lax.broadcasted_iota(jnp.int32, (S, L), 1) == lax.broadcasted_iota(jnp.int32, (S, L), 0) * t
    n_upd = [jnp.dot(seq_of_tok.astype(BF16), kw[h], preferred_element_type=F32) for h in heads]
    decay_seq = jnp.dot(first_tok.astype(F32), decay, precision=hi, preferred_element_type=F32)
    for h in heads:
        ln = _GATE0 + h
        inter_h = inter[:, ln:ln + 1]
        qn = jnp.sum(q[h].astype(F32) * nrow_ref[:, hk(h)], axis=1, keepdims=True)
        den = inter_h * qn + sv[h][:, DV_MLSTM:DV_MLSTM + 1]
        r = 1.0 / jnp.maximum(jnp.abs(den), em[:, ln:ln + 1])
        tt = om_ref[:, hv(h)] * (inter_h * qc[h] + sv[h][:, :DV_MLSTM])
        msq = jnp.mean(tt * tt, axis=1, keepdims=True)
        scale = r * lax.rsqrt(r * r * msq + EPS)
        mix_ref[:, hv(h)] = (tt * scale * gm_ref[:, hv(h)]).astype(BF16)
        dec = decay[:, ln:ln + 1].reshape(S, t, 1)[:, 0:1, :]
        c_out[:, h] = dec * c0_ref[:, h] + upd[h].reshape(S, DK_MLSTM, DV_MLSTM)
        n_out[:, hk(h)] = decay_seq[:, ln:ln + 1] * n0_ref[:, hk(h)] + n_upd[h]


def _mlstm_blk(qkm, vm, om, gt, g_m, c0, n0_seq, n0_rows, m0_rows, nseq, t):
    S = SEQ_PER_BLOCK
    L = S * t
    tok = lambda wd: pl.BlockSpec((L, wd), lambda i: (i, 0))
    st_c = pl.BlockSpec((S, N_HEADS_MLSTM, DK_MLSTM, DV_MLSTM), lambda i: (i, 0, 0, 0))
    st_n = pl.BlockSpec((S, QK_M_W), lambda i: (i, 0))
    sds = jax.ShapeDtypeStruct
    return pl.pallas_call(
        functools.partial(_mlstm_blk_kernel, t=t),
        grid=(nseq // S,),
        in_specs=[tok(2 * QK_M_W), tok(MLSTM_W), tok(MLSTM_W), tok(LANES), _const_spec((1, MLSTM_W)),
                  st_c, st_n, tok(QK_M_W), tok(LANES)],
        out_specs=[tok(MLSTM_W), st_c, st_n, tok(LANES)],
        out_shape=[sds((nseq * t, MLSTM_W), BF16), sds((nseq, N_HEADS_MLSTM, DK_MLSTM, DV_MLSTM), F32),
                   sds((nseq, QK_M_W), F32), sds((nseq * t, LANES), F32)],
        compiler_params=pltpu.CompilerParams(dimension_semantics=("arbitrary",), vmem_limit_bytes=VMEM_LIMIT),
        name="mlstm_blk",
    )(qkm, vm, om, gt, g_m, c0, n0_seq, n0_rows, m0_rows)


def _out_mlp_kernel(x_ref, a_ref, hm_ref, woa_ref, wom_ref, gmlp_ref, wup_ref, wdn_ref, gfin_ref, y_ref):
    x1 = (x_ref[...]
          + jnp.dot(a_ref[...], woa_ref[...], preferred_element_type=F32)
          + jnp.dot(hm_ref[...], wom_ref[...], preferred_element_type=F32))
    h = _rms_rows(x1, gmlp_ref[...]).astype(BF16)
    u = jnp.maximum(jnp.dot(h, wup_ref[...], preferred_element_type=F32), 0.0)
    y = x1 + jnp.dot((u * u).astype(BF16), wdn_ref[...], preferred_element_type=F32)
    y_ref[...] = _rms_rows(y, gfin_ref[...])


def _out_mlp(x2d, mix_a, mix_m, wo_a, wo_m, g_mlp, w_up, w_dn, g_fin, tm):
    n = x2d.shape[0]
    row = lambda w: pl.BlockSpec((tm, w), lambda i: (i, 0))
    return pl.pallas_call(
        _out_mlp_kernel,
        grid=(n // tm,),
        in_specs=[row(D_MODEL), row(ATT_W), row(MLSTM_W), _const_spec(wo_a.shape), _const_spec(wo_m.shape),
                  _const_spec((1, D_MODEL)), _const_spec(w_up.shape), _const_spec(w_dn.shape),
                  _const_spec((1, D_MODEL))],
        out_specs=row(D_MODEL),
        out_shape=jax.ShapeDtypeStruct((n, D_MODEL), F32),
        compiler_params=pltpu.CompilerParams(dimension_semantics=("arbitrary",), vmem_limit_bytes=VMEM_LIMIT),
        name="out_mlp",
    )(x2d, mix_a, mix_m, wo_a, wo_m, g_mlp, w_up, w_dn, g_fin)


_CN_ROWS = DV_MLSTM + BF16_ROWS


def _prefix_max_lanes(x):
    n = x.shape[1]
    col = lax.broadcasted_iota(jnp.int32, x.shape, 1)
    k = 1
    while k < n:
        x = jnp.maximum(x, jnp.where(col >= k, pltpu.roll(x, k, axis=1), -jnp.inf))
        k *= 2
    return x


def _mlstm_stages(qT_ref, k_ref, vT_ref, omT_ref, gT_ref, gb_ref, write_mix, state, chunk, nsub):
    L = chunk
    i0 = lax.broadcasted_iota(jnp.int32, (L, L), 0)
    i1 = lax.broadcasted_iota(jnp.int32, (L, L), 1)
    causal_t = i0 <= i1
    triu = causal_t.astype(F32)
    ones_rows = (lax.broadcasted_iota(jnp.int32, (BF16_ROWS, L), 0) == 0).astype(BF16)
    zpad = jnp.zeros((LANES - 2 * N_GATES, L), F32)
    cn, m_r = state['cn'], state['m_r']
    hi = lax.Precision.HIGHEST
    heads = range(N_HEADS_MLSTM)
    subs = range(nsub)
    col_of = lambda j: slice(j * L, (j + 1) * L)
    hv = lambda h: slice(h * DV_MLSTM, (h + 1) * DV_MLSTM)
    hk = lambda h: slice(h * DK_MLSTM, (h + 1) * DK_MLSTM)
    a_r, b_r, pm_r = [], [], []
    for j in subs:
        g_t = gT_ref[:, col_of(j)]
        b_r.append(jnp.dot(g_t, triu, precision=hi, preferred_element_type=F32))
        a_r.append(pltpu.roll(g_t, N_HEADS_MLSTM, axis=0) - b_r[j])
    yield
    kq = [[jnp.dot(k_ref[col_of(j), hk(h)], qT_ref[hk(h), col_of(j)], preferred_element_type=F32)
           for h in heads] for j in subs]
    vT_ext = [[jnp.concatenate([vT_ref[hv(h), col_of(j)], ones_rows], axis=0) for h in heads] for j in subs]
    for j in subs:
        pm_r.append(_prefix_max_lanes(a_r[j]))
    yield
    big_m, inter, em, decay, cols_t = [], [], [], [], []
    for j in subs:
        big_m.append(jnp.maximum(m_r, pm_r[j]))
        inter.append(jnp.exp(m_r - big_m[j]))
        em.append(jnp.exp(-(b_r[j] + big_m[j])))
        m_end = big_m[j][:, L - 1:L]
        w_r = jnp.exp(a_r[j] - m_end)
        decay.append(jnp.exp(m_r[:, 0:1] - m_end))
        m_r = jnp.broadcast_to(b_r[j][:, L - 1:L] + m_end, (N_GATES, L))
        cols_t.append(jnp.concatenate([a_r[j], w_r, zpad], axis=0).T)
    sT = [[None] * N_HEADS_MLSTM for _ in subs]
    kw = [[None] * N_HEADS_MLSTM for _ in subs]
    intra = [[None] * N_HEADS_MLSTM for _ in subs]
    upd = [[None] * N_HEADS_MLSTM for _ in subs]
    for j in subs:
        for h in heads:
            r = _GATE0 + h
            d = jnp.exp(jnp.where(causal_t, cols_t[j][:, r:r + 1] - big_m[j][r:r + 1, :], -jnp.inf))
            sT[j][h] = (kq[j][h] * d).astype(BF16)
            w_col = cols_t[j][:, N_GATES + r:N_GATES + r + 1]
            kw[j][h] = (k_ref[col_of(j), hk(h)].astype(F32) * w_col).astype(BF16)
        for h in heads:
            intra[j][h] = jnp.dot(vT_ext[j][h], sT[j][h], preferred_element_type=F32)
            upd[j][h] = jnp.dot(vT_ext[j][h], kw[j][h], preferred_element_type=F32)
        yield
    for j in subs:
        qc = [jnp.dot(cn[h].astype(BF16), qT_ref[hk(h), col_of(j)], preferred_element_type=F32) for h in heads]
        for h in heads:
            r = _GATE0 + h
            nd = inter[j][r:r + 1, :] * qc[h] + intra[j][h]
            rr = 1.0 / jnp.maximum(jnp.abs(nd[DV_MLSTM:DV_MLSTM + 1, :]), em[j][r:r + 1, :])
            t = omT_ref[hv(h), col_of(j)] * nd[:DV_MLSTM, :]
            msq = jnp.mean(t * t, axis=0, keepdims=True)
            scale = rr * lax.rsqrt(rr * rr * msq + EPS)
            write_mix(hv(h), col_of(j), (t * scale * gb_ref[hv(h), :]).astype(BF16))
            cn[h] = decay[j][r:r + 1, :] * cn[h] + upd[j][h]
        yield
    state['m_r'] = m_r


def _mlstm_mlp_kernel(x_ref, a_ref, qT_ref, k_ref, vT_ref, omT_ref, gT_ref, gb_ref,
                      woa_ref, wom_ref, gmlp_ref, wup_ref, wdn_ref, gfin_ref,
                      y_ref, c_out, n_out, m_out, mix_s, cn_s, mr_s,
                      *, chunk, nsub, fchunk, tiles_per_seq, n_tiles):
    s = pl.program_id(0)
    slot = s % 2

    @pl.when(s == 0)
    def _():
        mix_s[...] = jnp.zeros_like(mix_s)

    @pl.when(s % tiles_per_seq == 0)
    def _():
        cn_s[...] = jnp.zeros_like(cn_s)
        mr_s[...] = jnp.zeros_like(mr_s)

    state = {'cn': [cn_s[h] for h in range(N_HEADS_MLSTM)], 'm_r': mr_s[...]}

    def write_mix(rows, cols, val):
        mix_s[slot, rows, cols] = val

    stages = _mlstm_stages(qT_ref, k_ref, vT_ref, omT_ref, gT_ref, gb_ref, write_mix, state, chunk, nsub)
    next(stages)
    tdims = (((0,), (0,)), ((), ()))
    x1 = (x_ref[...]
          + lax.dot_general(a_ref[...], woa_ref[...], tdims, preferred_element_type=F32)
          + lax.dot_general(mix_s[1 - slot], wom_ref[...], tdims, preferred_element_type=F32))
    h = _rms_rows(x1, gmlp_ref[...]).astype(BF16)
    acc = x1
    n_ff = D_FF // fchunk
    for f in range(n_ff):
        next(stages, None)
        if f % 4 == 3:
            next(stages, None)
        u = jnp.maximum(jnp.dot(h, wup_ref[:, f * fchunk:(f + 1) * fchunk], preferred_element_type=F32), 0.0)
        acc = acc + jnp.dot((u * u).astype(BF16), wdn_ref[f * fchunk:(f + 1) * fchunk, :],
                            preferred_element_type=F32)
    for _ in stages:
        pass
    y_ref[...] = _rms_rows(acc, gfin_ref[...])
    for hh in range(N_HEADS_MLSTM):
        cn_s[hh] = state['cn'][hh]
    mr_s[...] = state['m_r']

    @pl.when((s % tiles_per_seq == tiles_per_seq - 1) & (s < n_tiles))
    def _():
        for hh in range(N_HEADS_MLSTM):
            c_out[0, hh] = cn_s[hh, :DV_MLSTM, :].T
            n_out[0, hh] = cn_s[hh, DV_MLSTM:DV_MLSTM + 1, :]
        m_out[0] = mr_s[...]


def _mlstm_mlp(x2d, mix_aT, qmT, km, vmT, omT, gT, g_b, wo_a, wo_m, g_mlp, w_up, w_dn, g_fin,
               nseq, t, chunk, nsub, fchunk):
    tm = chunk * nsub
    n_tiles = (nseq * t) // tm
    tiles_per_seq = t // tm
    cur = lambda s: jnp.minimum(s, n_tiles - 1)
    prv = lambda s: jnp.maximum(s - 1, 0)
    colb = lambda r: pl.BlockSpec((r, tm), lambda s: (0, cur(s)))
    st = lambda a, d: pl.BlockSpec((1, N_HEADS_MLSTM, a, d), lambda s: (cur(s) // tiles_per_seq, 0, 0, 0))
    sds = jax.ShapeDtypeStruct
    kern = functools.partial(_mlstm_mlp_kernel, chunk=chunk, nsub=nsub, fchunk=fchunk,
                             tiles_per_seq=tiles_per_seq, n_tiles=n_tiles)
    return pl.pallas_call(
        kern,
        grid=(n_tiles + 1,),
        in_specs=[pl.BlockSpec((tm, D_MODEL), lambda s: (prv(s), 0)),
                  pl.BlockSpec((ATT_W, tm), lambda s: (0, prv(s))),
                  colb(QK_M_W), pl.BlockSpec((tm, QK_M_W), lambda s: (cur(s), 0)), colb(MLSTM_W), colb(MLSTM_W),
                  colb(N_GATES),
                  _const_spec(g_b.shape), _const_spec(wo_a.shape), _const_spec(wo_m.shape),
                  _const_spec((1, D_MODEL)), _const_spec(w_up.shape), _const_spec(w_dn.shape),
                  _const_spec((1, D_MODEL))],
        out_specs=[pl.BlockSpec((tm, D_MODEL), lambda s: (prv(s), 0)),
                   st(DK_MLSTM, DV_MLSTM), st(1, DK_MLSTM),
                   pl.BlockSpec((1, N_GATES, chunk), lambda s: (cur(s) // tiles_per_seq, 0, 0))],
        out_shape=[sds((nseq * t, D_MODEL), F32), sds((nseq, N_HEADS_MLSTM, DK_MLSTM, DV_MLSTM), F32),
                   sds((nseq, N_HEADS_MLSTM, 1, DK_MLSTM), F32), sds((nseq, N_GATES, chunk), F32)],
        scratch_shapes=[pltpu.VMEM((2, MLSTM_W, tm), BF16),
                        pltpu.VMEM((N_HEADS_MLSTM, _CN_ROWS, DK_MLSTM), F32),
                        pltpu.VMEM((N_GATES, chunk), F32)],
        compiler_params=pltpu.CompilerParams(dimension_semantics=("arbitrary",), vmem_limit_bytes=VMEM_LIMIT),
        name="mlstm_mlp",
    )(x2d, mix_aT, qmT, km, vmT, omT, gT, g_b, wo_a, wo_m, g_mlp, w_up, w_dn, g_fin)


def kernel(x_prompt, x_sample, cache_k_win, cache_v_win, state_C, state_n, state_m, rel_bias, norm_mix, w_in,
           b_gates, attn_sinks, norm_attn_out, norm_mlstm_out, w_out, norm_mlp, w_up, w_down, norm_final):
    depth = w_in.shape[0]
    assert depth == 1, "single-layer trunk"
    bp, sp = x_prompt.shape[:2]
    bs, ts = x_sample.shape[:2]
    wc = cache_k_win.shape[2]
    l = 0
    chunk = 128

    wl = w_in[l]
    cols = lambda lo, n: wl[:, lo:lo + n]
    w_n = jnp.concatenate([cols(_O_KV, 2 * KV_W), cols(_O_QKM + QK_M_W, QK_M_W),
                           jnp.pad(cols(_O_G, N_GATES), ((0, 0), (0, LANES - N_GATES)))], axis=1).astype(BF16)
    w_t = jnp.concatenate([cols(_O_QA, ATT_W), cols(_O_KV + KV_W, KV_W), cols(_O_QKM, QK_M_W), cols(_O_VM, MLSTM_W),
                           cols(_O_OM, MLSTM_W), jnp.pad(cols(_O_G, N_GATES), ((0, 0), (0, BF16_ROWS - N_GATES)))],
                          axis=1).T.astype(BF16)
    bg_row = jnp.pad(b_gates[l], (0, LANES - N_GATES)).reshape(1, LANES)
    bg_col = b_gates[l].reshape(N_GATES, 1)
    wo_a = w_out[l, :ATT_W].astype(BF16)
    wo_m = w_out[l, ATT_W:].astype(BF16)
    wup = w_up[l].astype(BF16)
    wdn = w_down[l].astype(BF16)
    g_mix = norm_mix[l].reshape(1, D_MODEL)
    g_att = norm_attn_out[l].reshape(1, ATT_W)
    g_m = norm_mlstm_out[l].reshape(1, MLSTM_W)
    g_mlp = norm_mlp[l].reshape(1, D_MODEL)
    g_fin = norm_final.reshape(1, D_MODEL)
    gb_att = jnp.broadcast_to(norm_attn_out[l].reshape(ATT_W, 1), (ATT_W, WINDOW))
    gb_m = jnp.broadcast_to(norm_mlstm_out[l].reshape(MLSTM_W, 1), (MLSTM_W, chunk))
    sinks = attn_sinks[l]

    xp = x_prompt.reshape(bp * sp, D_MODEL)
    bias_t = _bias_table(rel_bias, _prompt_buckets_t())
    kvf, km, qmT, vmT, omT, gT, mix_a = _proj_attn(xp, g_mix, w_n, w_t, bg_col, bias_t, sinks, gb_att, sp, tm=1024)
    y_p, c_p, n_p, m_p = _mlstm_mlp(xp, mix_a, qmT, km, vmT, omT, gT, gb_m, wo_a, wo_m, g_mlp, wup, wdn, g_fin,
                                    bp, sp, chunk=chunk, nsub=4, fchunk=D_FF)
    m_p = m_p[:, _GATE0:_GATE0 + N_HEADS_MLSTM, 0]
    win_p = min(WINDOW, sp)
    kv_win = kvf.reshape(bp, sp, 2 * KV_W)[:, sp - win_p:, :]
    k_win_p = kv_win[:, :, :KV_W].reshape(1, bp, win_p, N_KV_HEADS, HEAD_DIM_ATT)
    v_win_p = kv_win[:, :, KV_W:].reshape(1, bp, win_p, N_KV_HEADS, HEAD_DIM_ATT)

    assert math.gcd(ts, 64) == ts and bs % SEQ_PER_BLOCK == 0, "sample group: one mLSTM chunk per sequence"
    kpad = ((wc + ts + BF16_ROWS - 1) // BF16_ROWS) * BF16_ROWS
    bias_s = _bias_table(rel_bias, _sample_buckets(ts, wc, kpad))
    bias_s = bias_s.reshape(N_KV_HEADS, GQA_GROUP * ts, kpad)
    sink_rows = jnp.repeat(sinks, ts).reshape(N_KV_HEADS, GQA_GROUP * ts, 1)
    xs = x_sample.reshape(bs * ts, D_MODEL)
    qa, kvf_s, qkm, vm, om, gt = _in_proj_s(xs, g_mix, w_n, w_t, bg_row, tm=512)
    ck = cache_k_win[l].reshape(bs, wc, KV_W)
    cv = cache_v_win[l].reshape(bs, wc, KV_W)
    mix_a_s, kw_s, vw_s = _attn_sample(qa, kvf_s, ck, cv, bias_s, sink_rows, g_att, bs, ts, bb=8)
    n0 = state_n[l].reshape(bs, QK_M_W)
    n0_rows = jnp.repeat(n0, ts, axis=0)
    m0_rows = jnp.pad(jnp.repeat(state_m[l], ts, axis=0),
                      ((0, 0), (_GATE0, LANES - _GATE0 - N_HEADS_MLSTM)))
    mix_m_s, c_s, n_s, m_rows = _mlstm_blk(qkm, vm, om, gt, g_m, state_C[l], n0, n0_rows, m0_rows, bs, ts)
    m_s = m_rows[::ts, _GATE0:_GATE0 + N_HEADS_MLSTM]
    y_s = _out_mlp(xs, mix_a_s, mix_m_s, wo_a, wo_m, g_mlp, wup, wdn, g_fin, tm=512)

    return (y_p.reshape(bp, sp, D_MODEL), y_s.reshape(bs, ts, D_MODEL),
            k_win_p, v_win_p,
            c_p[None], n_p.reshape(1, bp, N_HEADS_MLSTM, DK_MLSTM), m_p.reshape(1, bp, N_HEADS_MLSTM),
            kw_s.reshape(1, bs, wc, N_KV_HEADS, HEAD_DIM_ATT), vw_s.reshape(1, bs, wc, N_KV_HEADS, HEAD_DIM_ATT),
            c_s[None], n_s.reshape(1, bs, N_HEADS_MLSTM, DK_MLSTM), m_s.reshape(1, bs, N_HEADS_MLSTM))
```

```python
import functools
import math

import numpy as np
import jax
import jax.numpy as jnp
from jax import lax
from jax.experimental import pallas as pl
from jax.experimental.pallas import tpu as pltpu

D_MODEL = 1024
N_HEADS_ATT = 8
N_KV_HEADS = 2
GQA_GROUP = N_HEADS_ATT // N_KV_HEADS
HEAD_DIM_ATT = 64
ATT_W = N_HEADS_ATT * HEAD_DIM_ATT
KV_W = N_KV_HEADS * HEAD_DIM_ATT
WINDOW = 128
NUM_BUCKETS = 32
MAX_DISTANCE = 128
N_HEADS_MLSTM = 4
DV_MLSTM = 128
DK_MLSTM = 64
MLSTM_W = N_HEADS_MLSTM * DV_MLSTM
QK_M_W = N_HEADS_MLSTM * DK_MLSTM
D_FF = 4 * D_MODEL
EPS = 1e-6
N_GATES = 2 * N_HEADS_MLSTM
LANES = 128
BF16_ROWS = 16
VMEM_LIMIT = 56 * 1024 * 1024

F32 = jnp.float32
BF16 = jnp.bfloat16

_O_QA = 0
_O_KV = _O_QA + ATT_W
_O_QKM = _O_KV + 2 * KV_W
_O_VM = _O_QKM + 2 * QK_M_W
_O_OM = _O_VM + MLSTM_W
_O_G = _O_OM + MLSTM_W
_T_QA = 0
_T_VA = _T_QA + ATT_W
_T_QM = _T_VA + KV_W
_T_VM = _T_QM + QK_M_W
_T_OM = _T_VM + MLSTM_W
_T_G = _T_OM + MLSTM_W
_T_ROWS = _T_G + BF16_ROWS
_N_KV = 0
_N_KM = _N_KV + 2 * KV_W
_N_G = _N_KM + QK_M_W
_N_COLS = _N_G + LANES
_GATE0 = N_HEADS_MLSTM


def _const_spec(shape):
    nd = len(shape)
    return pl.BlockSpec(shape, lambda *_: (0,) * nd, pipeline_mode=pl.Buffered(1))


def _log_sigmoid(g):
    return jnp.minimum(g, 0.0) - jnp.log1p(jnp.exp(-jnp.abs(g)))


def _rms_rows(x, gain):
    return x * lax.rsqrt(jnp.mean(x * x, axis=-1, keepdims=True) + EPS) * gain


def _in_proj_t_kernel(x_ref, g_ref, wn_ref, wt_ref, bgc_ref,
                      k_ref, kvf_ref, km_ref, qT_ref, vT_ref, qmT_ref, vmT_ref, omT_ref, gT_ref):
    hb = _rms_rows(x_ref[...], g_ref[...]).astype(BF16)
    z_n = jnp.dot(hb, wn_ref[:, :_N_G], preferred_element_type=F32)
    z_t = lax.dot_general(wt_ref[...], hb, (((1,), (1,)), ((), ())), preferred_element_type=F32)
    kvf_ref[...] = z_n[:, _N_KV:_N_KV + 2 * KV_W]
    k_ref[...] = z_n[:, _N_KV:_N_KV + KV_W].astype(BF16)
    km_ref[...] = (z_n[:, _N_KM:_N_KM + QK_M_W] * (DK_MLSTM ** -0.5)).astype(BF16)
    qT_ref[...] = (z_t[_T_QA:_T_QA + ATT_W, :] * (HEAD_DIM_ATT ** -0.5)).astype(BF16)
    vT_ref[...] = z_t[_T_VA:_T_VA + KV_W, :].astype(BF16)
    qmT_ref[...] = z_t[_T_QM:_T_QM + QK_M_W, :].astype(BF16)
    vmT_ref[...] = z_t[_T_VM:_T_VM + MLSTM_W, :].astype(BF16)
    omT_ref[...] = jax.nn.sigmoid(z_t[_T_OM:_T_OM + MLSTM_W, :])
    g_t = z_t[_T_G:_T_G + N_GATES, :] + bgc_ref[...]
    row = lax.broadcasted_iota(jnp.int32, g_t.shape, 0)
    gT_ref[...] = jnp.where(row < N_HEADS_MLSTM, g_t, _log_sigmoid(g_t))


def _in_proj_t(x2d, g_norm, w_n, w_t, bg_col, tm):
    n = x2d.shape[0]
    row = lambda w: pl.BlockSpec((tm, w), lambda i: (i, 0))
    colb = lambda r: pl.BlockSpec((r, tm), lambda i: (0, i))
    sds = jax.ShapeDtypeStruct
    return pl.pallas_call(
        _in_proj_t_kernel,
        grid=(n // tm,),
        in_specs=[row(D_MODEL), _const_spec((1, D_MODEL)), _const_spec(w_n.shape), _const_spec(w_t.shape),
                  _const_spec((N_GATES, 1))],
        out_specs=[row(KV_W), row(2 * KV_W), row(QK_M_W),
                   colb(ATT_W), colb(KV_W), colb(QK_M_W), colb(MLSTM_W), colb(MLSTM_W), colb(N_GATES)],
        out_shape=[sds((n, KV_W), BF16), sds((n, 2 * KV_W), F32), sds((n, QK_M_W), BF16),
                   sds((ATT_W, n), BF16), sds((KV_W, n), BF16), sds((QK_M_W, n), BF16), sds((MLSTM_W, n), BF16),
                   sds((MLSTM_W, n), F32), sds((N_GATES, n), F32)],
        compiler_params=pltpu.CompilerParams(dimension_semantics=("arbitrary",), vmem_limit_bytes=VMEM_LIMIT),
        name="in_proj_t",
    )(x2d, g_norm, w_n, w_t, bg_col)


def _in_proj_s_kernel(x_ref, g_ref, wn_ref, wt_ref, bgr_ref, qa_ref, kvf_ref, qkm_ref, vm_ref, om_ref, gt_ref):
    hb = _rms_rows(x_ref[...], g_ref[...]).astype(BF16)
    z_n = jnp.dot(hb, wn_ref[...], preferred_element_type=F32)
    z_t = lax.dot_general(hb, wt_ref[...], (((1,), (1,)), ((), ())), preferred_element_type=F32)
    qa_ref[...] = (z_t[:, _T_QA:_T_QA + ATT_W] * (HEAD_DIM_ATT ** -0.5)).astype(BF16)
    kvf_ref[:, :KV_W] = z_n[:, _N_KV:_N_KV + KV_W]
    kvf_ref[:, KV_W:] = z_t[:, _T_VA:_T_VA + KV_W]
    qkm_ref[:, :QK_M_W] = z_t[:, _T_QM:_T_QM + QK_M_W].astype(BF16)
    qkm_ref[:, QK_M_W:] = (z_n[:, _N_KM:_N_KM + QK_M_W] * (DK_MLSTM ** -0.5)).astype(BF16)
    vm_ref[...] = z_t[:, _T_VM:_T_VM + MLSTM_W].astype(BF16)
    om_ref[...] = jax.nn.sigmoid(z_t[:, _T_OM:_T_OM + MLSTM_W])
    g = z_n[:, _N_G:_N_G + LANES] + bgr_ref[...]
    col = lax.broadcasted_iota(jnp.int32, g.shape, 1)
    gt_ref[...] = jnp.where(col < N_HEADS_MLSTM, g, _log_sigmoid(g))


def _in_proj_s(x2d, g_norm, w_n, w_t, bg_row, tm):
    n = x2d.shape[0]
    row = lambda w: pl.BlockSpec((tm, w), lambda i: (i, 0))
    sds = jax.ShapeDtypeStruct
    return pl.pallas_call(
        _in_proj_s_kernel,
        grid=(n // tm,),
        in_specs=[row(D_MODEL), _const_spec((1, D_MODEL)), _const_spec(w_n.shape), _const_spec(w_t.shape),
                  _const_spec((1, LANES))],
        out_specs=[row(ATT_W), row(2 * KV_W), row(2 * QK_M_W), row(MLSTM_W), row(MLSTM_W), row(LANES)],
        out_shape=[sds((n, ATT_W), BF16), sds((n, 2 * KV_W), F32), sds((n, 2 * QK_M_W), BF16),
                   sds((n, MLSTM_W), BF16), sds((n, MLSTM_W), F32), sds((n, LANES), F32)],
        compiler_params=pltpu.CompilerParams(dimension_semantics=("arbitrary",), vmem_limit_bytes=VMEM_LIMIT),
        name="in_proj_s",
    )(x2d, g_norm, w_n, w_t, bg_row)


def _t5_bucket_np(dist):
    max_exact = NUM_BUCKETS // 2
    d = np.maximum(dist, 0)
    ratio = np.maximum(d, 1).astype(np.float32) / np.float32(max_exact)
    large = max_exact + (np.log(ratio) / np.float32(math.log(MAX_DISTANCE / max_exact))
                         * np.float32(NUM_BUCKETS - max_exact)).astype(np.int32)
    large = np.minimum(large, NUM_BUCKETS - 1)
    return np.where(d < max_exact, d, large).astype(np.int32)


def _bias_table_kernel(rb_ref, bucket_ref, out_ref):
    h = pl.program_id(0)
    bucket = bucket_ref[...]
    acc = jnp.full(bucket.shape, -jnp.inf, F32)
    for j in range(NUM_BUCKETS):
        acc = jnp.where(bucket == j, rb_ref[j, h], acc)
    out_ref[0] = acc


def _bias_table(rel_bias, bucket):
    r, c = bucket.shape
    return pl.pallas_call(
        _bias_table_kernel,
        grid=(N_HEADS_ATT,),
        in_specs=[pl.BlockSpec(memory_space=pltpu.SMEM), pl.BlockSpec((r, c), lambda h: (0, 0))],
        out_specs=pl.BlockSpec((1, r, c), lambda h: (h, 0, 0)),
        out_shape=jax.ShapeDtypeStruct((N_HEADS_ATT, r, c), F32),
        compiler_params=pltpu.CompilerParams(dimension_semantics=("arbitrary",)),
        name="bias_table",
    )(rel_bias, jnp.asarray(bucket))


def _prompt_buckets_t():
    qi = np.arange(WINDOW)[None, :]
    kj = np.arange(2 * WINDOW)[:, None]
    dist = qi + WINDOW - kj
    in_win = (dist >= 0) & (dist <= WINDOW)
    b = _t5_bucket_np(dist)
    general = np.where(in_win, b, -1)
    first = np.where(in_win & (kj >= WINDOW), b, -1)
    return np.concatenate([general, first], axis=0).astype(np.int32)


def _sample_buckets(t, w, padded):
    dist = (w + np.arange(t))[:, None] - np.arange(padded)[None, :]
    valid = (dist >= 0) & (dist <= WINDOW) & (np.arange(padded)[None, :] < w + t)
    return np.where(valid, _t5_bucket_np(dist), -1).astype(np.int32)


def _attn_prompt_t_kernel(qT_ref, kp_ref, kc_ref, vTp_ref, vTc_ref, bias_ref, sink_ref, gb_ref, out_ref, *, qb):
    first = pl.program_id(1) == 0
    k_all = jnp.concatenate([kp_ref[...], kc_ref[...]], axis=0)
    vT_all = jnp.concatenate([vTp_ref[...], vTc_ref[...]], axis=1)

    def write_out(rows, cols, val):
        out_ref[rows, cols] = val

    for _ in _attn_blocks(lambda rows, cols: qT_ref[rows, cols], k_all, vT_all, first, bias_ref, sink_ref, gb_ref,
                          write_out, qb):
        pass


def _attn_prompt_t(qT, k, vT, bias, sinks, g_b, batch, seq, qb):
    nb = seq // WINDOW
    ns = nb // qb
    prev = lambda b, j: jnp.maximum(b * nb + j * qb - 1, 0)
    return pl.pallas_call(
        functools.partial(_attn_prompt_t_kernel, qb=qb),
        grid=(batch, ns),
        in_specs=[pl.BlockSpec((ATT_W, qb * WINDOW), lambda b, j: (0, b * ns + j)),
                  pl.BlockSpec((WINDOW, KV_W), lambda b, j: (prev(b, j), 0)),
                  pl.BlockSpec((qb * WINDOW, KV_W), lambda b, j: (b * ns + j, 0)),
                  pl.BlockSpec((KV_W, WINDOW), lambda b, j: (0, prev(b, j))),
                  pl.BlockSpec((KV_W, qb * WINDOW), lambda b, j: (0, b * ns + j)),
                  _const_spec(bias.shape),
                  pl.BlockSpec(memory_space=pltpu.SMEM),
                  _const_spec(g_b.shape)],
        out_specs=pl.BlockSpec((ATT_W, qb * WINDOW), lambda b, j: (0, b * ns + j)),
        out_shape=jax.ShapeDtypeStruct((ATT_W, batch * seq), BF16),
        compiler_params=pltpu.CompilerParams(dimension_semantics=("arbitrary", "arbitrary"),
                                             vmem_limit_bytes=VMEM_LIMIT),
        name="attn_prompt_t",
    )(qT, k, k, vT, vT, bias, sinks, g_b)


def _attn_block(i, qT, k_all, vT_all, first, bias_ref, sink_ref, gb_ref, write_out):
    gd = lambda g: slice(g * HEAD_DIM_ATT, (g + 1) * HEAD_DIM_ATT)
    cols = slice(i * WINDOW, (i + 1) * WINDOW)
    off = pl.multiple_of(jnp.where(first, 2 * WINDOW, 0), 2 * WINDOW) if i == 0 else 0
    k2 = k_all[i * WINDOW:(i + 2) * WINDOW, :]
    vT2 = vT_all[:, i * WINDOW:(i + 2) * WINDOW]
    scores = []
    for g in range(N_KV_HEADS):
        q_grp = jnp.concatenate([qT(slice((g * GQA_GROUP + j) * HEAD_DIM_ATT, (g * GQA_GROUP + j + 1) * HEAD_DIM_ATT),
                                    cols) for j in range(GQA_GROUP)], axis=1)
        scores.append(jnp.dot(k2[:, gd(g)], q_grp, preferred_element_type=F32))
    yield
    probs, rden = [], []
    for h in range(N_HEADS_ATT):
        g, j = divmod(h, GQA_GROUP)
        s = scores[g][:, j * WINDOW:(j + 1) * WINDOW] + bias_ref[h, pl.ds(off, 2 * WINDOW), :]
        sink = sink_ref[h]
        m = jnp.maximum(jnp.max(s, axis=0, keepdims=True), sink)
        p = jnp.exp(s - m)
        rden.append(1.0 / (jnp.sum(p, axis=0, keepdims=True) + jnp.exp(sink - m)))
        probs.append(p.astype(BF16))
    yield
    outs, ssq = [], jnp.zeros((1, WINDOW), F32)
    for g in range(N_KV_HEADS):
        p_grp = jnp.concatenate(probs[g * GQA_GROUP:(g + 1) * GQA_GROUP], axis=1)
        o_grp = jnp.dot(vT2[gd(g), :], p_grp, preferred_element_type=F32)
        for j in range(GQA_GROUP):
            o = o_grp[:, j * WINDOW:(j + 1) * WINDOW] * rden[g * GQA_GROUP + j]
            outs.append(o)
            ssq = ssq + jnp.sum(o * o, axis=0, keepdims=True)
    inv = lax.rsqrt(ssq * (1.0 / ATT_W) + EPS)
    for h in range(N_HEADS_ATT):
        rows = slice(h * HEAD_DIM_ATT, (h + 1) * HEAD_DIM_ATT)
        write_out(rows, cols, (outs[h] * inv * gb_ref[rows, :]).astype(BF16))
    yield


def _attn_blocks(qT, k_all, vT_all, first, bias_ref, sink_ref, gb_ref, write_out, qb):
    gens = [_attn_block(i, qT, k_all, vT_all, first, bias_ref, sink_ref, gb_ref, write_out) for i in range(qb)]
    next(gens[0])
    next(gens[0])
    for i in range(qb):
        if i + 1 < qb:
            next(gens[i + 1])
        yield
        next(gens[i])
        if i + 1 < qb:
            next(gens[i + 1])


PROJ_ROW_CHUNK = 256


def _proj_attn_kernel(x_ref, g_ref, wn_ref, wt_ref, bgc_ref, bias_ref, sink_ref, gb_ref,
                      kvf_ref, km_ref, qmT_ref, vmT_ref, omT_ref, gT_ref, mix_ref,
                      q_s, k_s, vT_s, hk_s, hv_s, *, tm, tiles_per_seq):
    s = pl.program_id(0)
    slot = s % 2
    prev = 1 - slot
    qb = tm // WINDOW

    @pl.when(s == 0)
    def _():
        q_s[...] = jnp.zeros_like(q_s)
        k_s[...] = jnp.zeros_like(k_s)
        vT_s[...] = jnp.zeros_like(vT_s)
        hk_s[...] = jnp.zeros_like(hk_s)
        hv_s[...] = jnp.zeros_like(hv_s)

    first = (s + tiles_per_seq - 1) % tiles_per_seq == 0
    k_all = jnp.concatenate([hk_s[...], k_s[prev]], axis=0)
    vT_all = jnp.concatenate([hv_s[...], vT_s[prev]], axis=1)

    def write_out(rows, cols, val):
        mix_ref[rows, cols] = val

    blocks = _attn_blocks(lambda rows, cols: q_s[prev, rows, cols], k_all, vT_all, first, bias_ref, sink_ref, gb_ref,
                          write_out, qb)
    hb = _rms_rows(x_ref[...], g_ref[...]).astype(BF16)

    def proj_t(lo, hi):
        return lax.dot_general(wt_ref[lo:hi, :], hb, (((1,), (1,)), ((), ())), preferred_element_type=F32)

    def put_q(lo, z):
        q_s[slot, lo:lo + z.shape[0], :] = (z * (HEAD_DIM_ATT ** -0.5)).astype(BF16)

    def put_v(lo, z):
        vT_s[slot, lo:lo + z.shape[0], :] = z.astype(BF16)

    def put_qm(lo, z):
        qmT_ref[lo:lo + z.shape[0], :] = z.astype(BF16)

    def put_vm(lo, z):
        vmT_ref[lo:lo + z.shape[0], :] = z.astype(BF16)

    def put_om(lo, z):
        omT_ref[lo:lo + z.shape[0], :] = jax.nn.sigmoid(z)

    def put_g(lo, z):
        g_t = z[:N_GATES, :] + bgc_ref[...]
        row = lax.broadcasted_iota(jnp.int32, g_t.shape, 0)
        gT_ref[...] = jnp.where(row < N_HEADS_MLSTM, g_t, _log_sigmoid(g_t))

    segments = [(_T_QA, _T_VA, put_q), (_T_VA, _T_QM, put_v), (_T_QM, _T_VM, put_qm), (_T_VM, _T_OM, put_vm),
                (_T_OM, _T_G, put_om), (_T_G, _T_ROWS, put_g)]

    def proj_rows(lo, hi):
        z = proj_t(lo, hi)
        for a, b, put in segments:
            if a < hi and lo < b:
                put(max(lo, a) - a, z[max(lo, a) - lo:min(hi, b) - lo, :])

    z_n = jnp.dot(hb, wn_ref[:, :_N_G], preferred_element_type=F32)
    kvf_ref[...] = z_n[:, _N_KV:_N_KV + 2 * KV_W]
    km_ref[...] = (z_n[:, _N_KM:_N_KM + QK_M_W] * (DK_MLSTM ** -0.5)).astype(BF16)
    k_s[slot] = z_n[:, _N_KV:_N_KV + KV_W].astype(BF16)
    bounds = [min(c * PROJ_ROW_CHUNK, _T_ROWS) for c in range(-(-_T_ROWS // PROJ_ROW_CHUNK) + 1)]
    chunks = list(zip(bounds[:-1], bounds[1:]))
    for _ in blocks:
        if chunks:
            proj_rows(*chunks.pop(0))
    for lo, hi in chunks:
        proj_rows(lo, hi)
    hk_s[...] = k_s[prev, tm - WINDOW:, :]
    hv_s[...] = vT_s[prev, :, tm - WINDOW:]


def _proj_attn(x2d, g_norm, w_n, w_t, bg_col, bias, sinks, g_b, seq, tm):
    n = x2d.shape[0]
    n_tiles = n // tm
    tiles_per_seq = seq // tm
    cur = lambda s: jnp.minimum(s, n_tiles - 1)
    prv = lambda s: jnp.maximum(s - 1, 0)
    row = lambda w: pl.BlockSpec((tm, w), lambda s: (cur(s), 0))
    colb = lambda r: pl.BlockSpec((r, tm), lambda s: (0, cur(s)))
    sds = jax.ShapeDtypeStruct
    return pl.pallas_call(
        functools.partial(_proj_attn_kernel, tm=tm, tiles_per_seq=tiles_per_seq),
        grid=(n_tiles + 1,),
        in_specs=[row(D_MODEL), _const_spec((1, D_MODEL)), _const_spec(w_n.shape), _const_spec(w_t.shape),
                  _const_spec((N_GATES, 1)), _const_spec(bias.shape), pl.BlockSpec(memory_space=pltpu.SMEM),
                  _const_spec(g_b.shape)],
        out_specs=[row(2 * KV_W), row(QK_M_W), colb(QK_M_W), colb(MLSTM_W), colb(MLSTM_W), colb(N_GATES),
                   pl.BlockSpec((ATT_W, tm), lambda s: (0, prv(s)))],
        out_shape=[sds((n, 2 * KV_W), F32), sds((n, QK_M_W), BF16), sds((QK_M_W, n), BF16), sds((MLSTM_W, n), BF16),
                   sds((MLSTM_W, n), F32), sds((N_GATES, n), F32), sds((ATT_W, n), BF16)],
        scratch_shapes=[pltpu.VMEM((2, ATT_W, tm), BF16), pltpu.VMEM((2, tm, KV_W), BF16),
                        pltpu.VMEM((2, KV_W, tm), BF16), pltpu.VMEM((WINDOW, KV_W), BF16),
                        pltpu.VMEM((KV_W, WINDOW), BF16)],
        compiler_params=pltpu.CompilerParams(dimension_semantics=("arbitrary",), vmem_limit_bytes=VMEM_LIMIT),
        name="proj_attn",
    )(x2d, g_norm, w_n, w_t, bg_col, bias, sinks, g_b)


def _attn_sample_kernel(q_ref, kvf_ref, ck_ref, cv_ref, bias_ref, sink_ref, g_ref, out_ref, kw_ref, vw_ref,
                        *, bb, t, w, kpad):
    seqs = range(bb)
    groups = range(N_KV_HEADS)
    tok = lambda b: slice(b * t, (b + 1) * t)
    gd = lambda g: slice(g * HEAD_DIM_ATT, (g + 1) * HEAD_DIM_ATT)
    zpad = jnp.zeros((kpad - w - t, KV_W), F32)
    kk, vv = [], []
    for b in seqs:
        ck = ck_ref[b]
        cv = cv_ref[b]
        k_new = kvf_ref[tok(b), :KV_W]
        v_new = kvf_ref[tok(b), KV_W:]
        kw_ref[b, :w - t, :] = ck[t:, :]
        kw_ref[b, w - t:, :] = k_new
        vw_ref[b, :w - t, :] = cv[t:, :]
        vw_ref[b, w - t:, :] = v_new
        kk.append(jnp.concatenate([ck, k_new, zpad], axis=0).astype(BF16))
        vv.append(jnp.concatenate([cv, v_new, zpad], axis=0).astype(BF16))
    scores = []
    for b in seqs:
        qb = q_ref[tok(b), :].astype(F32)
        for g in groups:
            qg = jnp.concatenate(
                [qb[:, (g * GQA_GROUP + i) * HEAD_DIM_ATT:(g * GQA_GROUP + i + 1) * HEAD_DIM_ATT]
                 for i in range(GQA_GROUP)], axis=0).astype(BF16)
            scores.append(lax.dot_general(qg, kk[b][:, gd(g)], (((1,), (1,)), ((), ())),
                                          preferred_element_type=F32))
    probs = []
    for b in seqs:
        for g in groups:
            s = scores[b * N_KV_HEADS + g] + bias_ref[g]
            sink = sink_ref[g]
            m = jnp.maximum(jnp.max(s, axis=1, keepdims=True), sink)
            p = jnp.exp(s - m)
            denom = jnp.sum(p, axis=1, keepdims=True) + jnp.exp(sink - m)
            probs.append((p * (1.0 / denom)).astype(BF16))
    outs = [jnp.dot(probs[b * N_KV_HEADS + g], vv[b][:, gd(g)], preferred_element_type=F32)
            for b in seqs for g in groups]
    for b in seqs:
        heads = [outs[b * N_KV_HEADS + g][i * t:(i + 1) * t, :] for g in groups for i in range(GQA_GROUP)]
        ssq = jnp.zeros((t, 1), F32)
        for oi in heads:
            ssq = ssq + jnp.sum(oi * oi, axis=1, keepdims=True)
        inv = lax.rsqrt(ssq * (1.0 / ATT_W) + EPS)
        for h in range(N_HEADS_ATT):
            sl = slice(h * HEAD_DIM_ATT, (h + 1) * HEAD_DIM_ATT)
            out_ref[tok(b), sl] = (heads[h] * inv * g_ref[:, sl]).astype(BF16)


def _attn_sample(qa, kvf, ck, cv, bias, sink_rows, g_att, nseq, t, bb):
    w = ck.shape[1]
    kpad = bias.shape[-1]
    rows = GQA_GROUP * t
    kern = functools.partial(_attn_sample_kernel, bb=bb, t=t, w=w, kpad=kpad)
    tok = lambda wd: pl.BlockSpec((bb * t, wd), lambda i: (i, 0))
    cache = pl.BlockSpec((bb, w, KV_W), lambda i: (i, 0, 0))
    return pl.pallas_call(
        kern,
        grid=(nseq // bb,),
        in_specs=[tok(ATT_W), tok(2 * KV_W), cache, cache,
                  _const_spec((N_KV_HEADS, rows, kpad)), _const_spec((N_KV_HEADS, rows, 1)), _const_spec((1, ATT_W))],
        out_specs=[tok(ATT_W), cache, cache],
        out_shape=[jax.ShapeDtypeStruct((nseq * t, ATT_W), BF16),
                   jax.ShapeDtypeStruct((nseq, w, KV_W), F32), jax.ShapeDtypeStruct((nseq, w, KV_W), F32)],
        compiler_params=pltpu.CompilerParams(dimension_semantics=("arbitrary",), vmem_limit_bytes=VMEM_LIMIT),
        name="attn_sample",
    )(qa, kvf, ck, cv, bias, sink_rows, g_att)


def _attn_sample_b_kernel(q_ref, kvf_ref, ck_ref, cv_ref, bias_ref, sink_ref, g_ref, out_ref, kw_ref, vw_ref,
                          *, bb, t, w, kpad):
    rows = GQA_GROUP * t
    ck = ck_ref[...]
    cv = cv_ref[...]
    k_new = kvf_ref[:, :KV_W].reshape(bb, t, KV_W)
    v_new = kvf_ref[:, KV_W:].reshape(bb, t, KV_W)
    kw_ref[:, :w - t, :] = ck[:, t:, :]
    kw_ref[:, w - t:, :] = k_new
    vw_ref[:, :w - t, :] = cv[:, t:, :]
    vw_ref[:, w - t:, :] = v_new
    zpad = jnp.zeros((bb, kpad - w - t, KV_W), F32)
    kk = jnp.concatenate([ck, k_new, zpad], axis=1).astype(BF16)
    vv = jnp.concatenate([cv, v_new, zpad], axis=1).astype(BF16)
    q3 = q_ref[...].astype(F32).reshape(bb, t, ATT_W)
    outs = []
    ssq = jnp.zeros((bb, 1, t, 1), F32)
    for g in range(N_KV_HEADS):
        gd = slice(g * HEAD_DIM_ATT, (g + 1) * HEAD_DIM_ATT)
        qg = jnp.concatenate([q3[:, :, (g * GQA_GROUP + i) * HEAD_DIM_ATT:(g * GQA_GROUP + i + 1) * HEAD_DIM_ATT]
                              for i in range(GQA_GROUP)], axis=1).astype(BF16)
        s = jnp.einsum('bqd,bkd->bqk', qg, kk[:, :, gd], preferred_element_type=F32) + bias_ref[g]
        sink = sink_ref[g]
        m = jnp.maximum(jnp.max(s, axis=-1, keepdims=True), sink)
        p = jnp.exp(s - m)
        denom = jnp.sum(p, axis=-1, keepdims=True) + jnp.exp(sink - m)
        p = (p * (1.0 / denom)).astype(BF16)
        o = jnp.einsum('bqk,bkd->bqd', p, vv[:, :, gd], preferred_element_type=F32)
        o = o.reshape(bb, GQA_GROUP, t, HEAD_DIM_ATT)
        outs.append(o)
        ssq = ssq + jnp.sum(jnp.sum(o * o, axis=-1, keepdims=True), axis=1, keepdims=True)
    inv = lax.rsqrt(ssq * (1.0 / ATT_W) + EPS)
    for g in range(N_KV_HEADS):
        y = outs[g] * inv
        for i in range(GQA_GROUP):
            sl = slice((g * GQA_GROUP + i) * HEAD_DIM_ATT, (g * GQA_GROUP + i + 1) * HEAD_DIM_ATT)
            out_ref[:, sl] = (y[:, i].reshape(bb * t, HEAD_DIM_ATT) * g_ref[:, sl]).astype(BF16)


def _attn_sample_b(qa, kvf, ck, cv, bias, sink_rows, g_att, nseq, t, bb):
    w = ck.shape[1]
    kpad = bias.shape[-1]
    rows = GQA_GROUP * t
    kern = functools.partial(_attn_sample_b_kernel, bb=bb, t=t, w=w, kpad=kpad)
    tok = lambda wd: pl.BlockSpec((bb * t, wd), lambda i: (i, 0))
    cache = pl.BlockSpec((bb, w, KV_W), lambda i: (i, 0, 0))
    return pl.pallas_call(
        kern,
        grid=(nseq // bb,),
        in_specs=[tok(ATT_W), tok(2 * KV_W), cache, cache,
                  _const_spec((N_KV_HEADS, rows, kpad)), _const_spec((N_KV_HEADS, rows, 1)), _const_spec((1, ATT_W))],
        out_specs=[tok(ATT_W), cache, cache],
        out_shape=[jax.ShapeDtypeStruct((nseq * t, ATT_W), BF16),
                   jax.ShapeDtypeStruct((nseq, w, KV_W), F32), jax.ShapeDtypeStruct((nseq, w, KV_W), F32)],
        compiler_params=pltpu.CompilerParams(dimension_semantics=("arbitrary",), vmem_limit_bytes=VMEM_LIMIT),
        name="attn_sample_b",
    )(qa, kvf, ck, cv, bias, sink_rows, g_att)


SEQ_PER_BLOCK = 16


def _seg_scan_rows(x, seg, forward):
    n = x.shape[0]
    pos = lax.broadcasted_iota(jnp.int32, x.shape, 0) % seg
    k = 1
    while k < seg:
        if forward:
            x = jnp.maximum(x, jnp.where(pos >= k, pltpu.roll(x, k, axis=0), -jnp.inf))
        else:
            x = jnp.maximum(x, jnp.where(pos < seg - k, pltpu.roll(x, n - k, axis=0), -jnp.inf))
        k *= 2
    return x


def _mlstm_blk_kernel(qk_ref, v_ref, om_ref, gt_ref, gm_ref, c0_ref, n0_ref, nrow_ref, m0_ref,
                      mix_ref, c_out, n_out, m_out, *, t):
    S = SEQ_PER_BLOCK
    L = S * t
    hi = lax.Precision.HIGHEST
    ti = lax.broadcasted_iota(jnp.int32, (L, L), 0)
    si = lax.broadcasted_iota(jnp.int32, (L, L), 1)
    same = (ti // t) == (si // t)
    causal = same & (si <= ti)
    tril_seg = causal.astype(F32)
    ones_seg = same.astype(F32)
    ones_col = (lax.broadcasted_iota(jnp.int32, (L, DV_MLSTM), 1) == 0).astype(BF16)
    bd_mask = (lax.broadcasted_iota(jnp.int32, (L, S * DK_MLSTM), 0) // t
               == lax.broadcasted_iota(jnp.int32, (L, S * DK_MLSTM), 1) // DK_MLSTM)
    bd_mask_t = (lax.broadcasted_iota(jnp.int32, (S * DK_MLSTM, L), 0) // DK_MLSTM
                 == lax.broadcasted_iota(jnp.int32, (S * DK_MLSTM, L), 1) // t)
    heads = range(N_HEADS_MLSTM)
    hk = lambda h: slice(h * DK_MLSTM, (h + 1) * DK_MLSTM)
    hkk = lambda h: slice(QK_M_W + h * DK_MLSTM, QK_M_W + (h + 1) * DK_MLSTM)
    hv = lambda h: slice(h * DV_MLSTM, (h + 1) * DV_MLSTM)

    gt = gt_ref[...]
    m_prev = m0_ref[...]
    b = jnp.dot(tril_seg, gt, precision=hi, preferred_element_type=F32)
    b_end = jnp.dot(ones_seg, gt, precision=hi, preferred_element_type=F32)
    a = pltpu.roll(gt, N_HEADS_MLSTM, axis=1) - b
    big_m = jnp.maximum(m_prev, _seg_scan_rows(a, t, True))
    inter = jnp.exp(m_prev - big_m)
    em = jnp.exp(-(b + big_m))
    m_end = _seg_scan_rows(big_m, t, False)
    w = jnp.exp(a - m_end)
    decay = jnp.exp(m_prev - m_end)
    m_out[...] = b_end + m_end
    a_t = a.T
    q = [qk_ref[:, hk(h)] for h in heads]
    k = [qk_ref[:, hkk(h)] for h in heads]
    v_ext = [jnp.concatenate([v_ref[:, hv(h)], ones_col], axis=1) for h in heads]
    qk = [lax.dot_general(q[h], k[h], (((1,), (1,)), ((), ())), preferred_element_type=F32) for h in heads]
    c_all = [c0_ref[:, h].reshape(S * DK_MLSTM, DV_MLSTM) for h in heads]
    q_bd = []
    for h in heads:
        q2 = jnp.concatenate([q[h], q[h]], axis=1)
        q_bd.append(jnp.where(bd_mask, jnp.concatenate([q2] * (S // 2), axis=1), jnp.zeros((), BF16)))
    qc = [jnp.dot(q_bd[h], c_all[h].astype(BF16), preferred_element_type=F32) for h in heads]
    s = []
    for h in heads:
        ln = _GATE0 + h
        d = jnp.exp(jnp.where(causal, a_t[ln:ln + 1, :] - big_m[:, ln:ln + 1], -jnp.inf))
        s.append((qk[h] * d).astype(BF16))
    sv = [jnp.dot(s[h], v_ext[h], preferred_element_type=F32) for h in heads]
    kw, kw_bd = [], []
    for h in heads:
        ln = _GATE0 + h
        kw_f = k[h].astype(F32) * w[:, ln:ln + 1]
        kw.append(kw_f.astype(BF16))
        kw_bd.append(jnp.where(bd_mask_t, jnp.concatenate([kw_f.T] * S, axis=0), 0.0).astype(BF16))
    upd = [jnp.dot(kw_bd[h], v_ref[:, hv(h)], preferred_element_type=F32) for h in heads]
    seq_of_tok = lax.broadcasted_iota(jnp.int32, (S, L), 1) // t == lax.broadcasted_iota(jnp.int32, (S, L), 0)
    first_tok = lax.broadcasted_iota(jnp.int32, (S, L), 1) == lax.broadcasted_iota(jnp.int32, (S, L), 0) * t
    n_upd = [jnp.dot(seq_of_tok.astype(BF16), kw[h], preferred_element_type=F32) for h in heads]
    decay_seq = jnp.dot(first_tok.astype(F32), decay, precision=hi, preferred_element_type=F32)
    for h in heads:
        ln = _GATE0 + h
        inter_h = inter[:, ln:ln + 1]
        qn = jnp.sum(q[h].astype(F32) * nrow_ref[:, hk(h)], axis=1, keepdims=True)
        den = inter_h * qn + sv[h][:, DV_MLSTM:DV_MLSTM + 1]
        r = 1.0 / jnp.maximum(jnp.abs(den), em[:, ln:ln + 1])
        tt = om_ref[:, hv(h)] * (inter_h * qc[h] + sv[h][:, :DV_MLSTM])
        msq = jnp.mean(tt * tt, axis=1, keepdims=True)
        scale = r * lax.rsqrt(r * r * msq + EPS)
        mix_ref[:, hv(h)] = (tt * scale * gm_ref[:, hv(h)]).astype(BF16)
        dec = decay[:, ln:ln + 1].reshape(S, t, 1)[:, 0:1, :]
        c_out[:, h] = dec * c0_ref[:, h] + upd[h].reshape(S, DK_MLSTM, DV_MLSTM)
        n_out[:, hk(h)] = decay_seq[:, ln:ln + 1] * n0_ref[:, hk(h)] + n_upd[h]


def _mlstm_blk(qkm, vm, om, gt, g_m, c0, n0_seq, n0_rows, m0_rows, nseq, t):
    S = SEQ_PER_BLOCK
    L = S * t
    tok = lambda wd: pl.BlockSpec((L, wd), lambda i: (i, 0))
    st_c = pl.BlockSpec((S, N_HEADS_MLSTM, DK_MLSTM, DV_MLSTM), lambda i: (i, 0, 0, 0))
    st_n = pl.BlockSpec((S, QK_M_W), lambda i: (i, 0))
    sds = jax.ShapeDtypeStruct
    return pl.pallas_call(
        functools.partial(_mlstm_blk_kernel, t=t),
        grid=(nseq // S,),
        in_specs=[tok(2 * QK_M_W), tok(MLSTM_W), tok(MLSTM_W), tok(LANES), _const_spec((1, MLSTM_W)),
                  st_c, st_n, tok(QK_M_W), tok(LANES)],
        out_specs=[tok(MLSTM_W), st_c, st_n, tok(LANES)],
        out_shape=[sds((nseq * t, MLSTM_W), BF16), sds((nseq, N_HEADS_MLSTM, DK_MLSTM, DV_MLSTM), F32),
                   sds((nseq, QK_M_W), F32), sds((nseq * t, LANES), F32)],
        compiler_params=pltpu.CompilerParams(dimension_semantics=("arbitrary",), vmem_limit_bytes=VMEM_LIMIT),
        name="mlstm_blk",
    )(qkm, vm, om, gt, g_m, c0, n0_seq, n0_rows, m0_rows)


def _out_mlp_kernel(x_ref, a_ref, hm_ref, woa_ref, wom_ref, gmlp_ref, wup_ref, wdn_ref, gfin_ref, y_ref):
    x1 = (x_ref[...]
          + jnp.dot(a_ref[...], woa_ref[...], preferred_element_type=F32)
          + jnp.dot(hm_ref[...], wom_ref[...], preferred_element_type=F32))
    h = _rms_rows(x1, gmlp_ref[...]).astype(BF16)
    u = jnp.maximum(jnp.dot(h, wup_ref[...], preferred_element_type=F32), 0.0)
    y = x1 + jnp.dot((u * u).astype(BF16), wdn_ref[...], preferred_element_type=F32)
    y_ref[...] = _rms_rows(y, gfin_ref[...])


def _out_mlp(x2d, mix_a, mix_m, wo_a, wo_m, g_mlp, w_up, w_dn, g_fin, tm):
    n = x2d.shape[0]
    row = lambda w: pl.BlockSpec((tm, w), lambda i: (i, 0))
    return pl.pallas_call(
        _out_mlp_kernel,
        grid=(n // tm,),
        in_specs=[row(D_MODEL), row(ATT_W), row(MLSTM_W), _const_spec(wo_a.shape), _const_spec(wo_m.shape),
                  _const_spec((1, D_MODEL)), _const_spec(w_up.shape), _const_spec(w_dn.shape),
                  _const_spec((1, D_MODEL))],
        out_specs=row(D_MODEL),
        out_shape=jax.ShapeDtypeStruct((n, D_MODEL), F32),
        compiler_params=pltpu.CompilerParams(dimension_semantics=("arbitrary",), vmem_limit_bytes=VMEM_LIMIT),
        name="out_mlp",
    )(x2d, mix_a, mix_m, wo_a, wo_m, g_mlp, w_up, w_dn, g_fin)


_CN_ROWS = DV_MLSTM + BF16_ROWS


def _prefix_max_lanes(x):
    n = x.shape[1]
    col = lax.broadcasted_iota(jnp.int32, x.shape, 1)
    k = 1
    while k < n:
        x = jnp.maximum(x, jnp.where(col >= k, pltpu.roll(x, k, axis=1), -jnp.inf))
        k *= 2
    return x


def _mlstm_stages(qT_ref, k_ref, vT_ref, omT_ref, gT_ref, gb_ref, write_mix, state, chunk, nsub):
    L = chunk
    i0 = lax.broadcasted_iota(jnp.int32, (L, L), 0)
    i1 = lax.broadcasted_iota(jnp.int32, (L, L), 1)
    causal_t = i0 <= i1
    triu = causal_t.astype(F32)
    ones_rows = (lax.broadcasted_iota(jnp.int32, (BF16_ROWS, L), 0) == 0).astype(BF16)
    zpad = jnp.zeros((LANES - 2 * N_GATES, L), F32)
    cn, m_r = state['cn'], state['m_r']
    hi = lax.Precision.HIGHEST
    heads = range(N_HEADS_MLSTM)
    subs = range(nsub)
    col_of = lambda j: slice(j * L, (j + 1) * L)
    hv = lambda h: slice(h * DV_MLSTM, (h + 1) * DV_MLSTM)
    hk = lambda h: slice(h * DK_MLSTM, (h + 1) * DK_MLSTM)
    a_r, b_r, pm_r = [], [], []
    for j in subs:
        g_t = gT_ref[:, col_of(j)]
        b_r.append(jnp.dot(g_t, triu, precision=hi, preferred_element_type=F32))
        a_r.append(pltpu.roll(g_t, N_HEADS_MLSTM, axis=0) - b_r[j])
    yield
    kq = [[jnp.dot(k_ref[col_of(j), hk(h)], qT_ref[hk(h), col_of(j)], preferred_element_type=F32)
           for h in heads] for j in subs]
    vT_ext = [[jnp.concatenate([vT_ref[hv(h), col_of(j)], ones_rows], axis=0) for h in heads] for j in subs]
    for j in subs:
        pm_r.append(_prefix_max_lanes(a_r[j]))
    yield
    big_m, inter, em, decay, cols_t = [], [], [], [], []
    for j in subs:
        big_m.append(jnp.maximum(m_r, pm_r[j]))
        inter.append(jnp.exp(m_r - big_m[j]))
        em.append(jnp.exp(-(b_r[j] + big_m[j])))
        m_end = big_m[j][:, L - 1:L]
        w_r = jnp.exp(a_r[j] - m_end)
        decay.append(jnp.exp(m_r[:, 0:1] - m_end))
        m_r = jnp.broadcast_to(b_r[j][:, L - 1:L] + m_end, (N_GATES, L))
        cols_t.append(jnp.concatenate([a_r[j], w_r, zpad], axis=0).T)
    sT = [[None] * N_HEADS_MLSTM for _ in subs]
    kw = [[None] * N_HEADS_MLSTM for _ in subs]
    intra = [[None] * N_HEADS_MLSTM for _ in subs]
    upd = [[None] * N_HEADS_MLSTM for _ in subs]
    for j in subs:
        for h in heads:
            r = _GATE0 + h
            d = jnp.exp(jnp.where(causal_t, cols_t[j][:, r:r + 1] - big_m[j][r:r + 1, :], -jnp.inf))
            sT[j][h] = (kq[j][h] * d).astype(BF16)
            w_col = cols_t[j][:, N_GATES + r:N_GATES + r + 1]
            kw[j][h] = (k_ref[col_of(j), hk(h)].astype(F32) * w_col).astype(BF16)
        for h in heads:
            intra[j][h] = jnp.dot(vT_ext[j][h], sT[j][h], preferred_element_type=F32)
            upd[j][h] = jnp.dot(vT_ext[j][h], kw[j][h], preferred_element_type=F32)
        yield
    for j in subs:
        qc = [jnp.dot(cn[h].astype(BF16), qT_ref[hk(h), col_of(j)], preferred_element_type=F32) for h in heads]
        for h in heads:
            r = _GATE0 + h
            nd = inter[j][r:r + 1, :] * qc[h] + intra[j][h]
            rr = 1.0 / jnp.maximum(jnp.abs(nd[DV_MLSTM:DV_MLSTM + 1, :]), em[j][r:r + 1, :])
            t = omT_ref[hv(h), col_of(j)] * nd[:DV_MLSTM, :]
            msq = jnp.mean(t * t, axis=0, keepdims=True)
            scale = rr * lax.rsqrt(rr * rr * msq + EPS)
            write_mix(hv(h), col_of(j), (t * scale * gb_ref[hv(h), :]).astype(BF16))
            cn[h] = decay[j][r:r + 1, :] * cn[h] + upd[j][h]
        yield
    state['m_r'] = m_r


def _mlstm_mlp_kernel(x_ref, a_ref, qT_ref, k_ref, vT_ref, omT_ref, gT_ref, gb_ref,
                      woa_ref, wom_ref, gmlp_ref, wup_ref, wdn_ref, gfin_ref,
                      y_ref, c_out, n_out, m_out, mix_s, cn_s, mr_s,
                      *, chunk, nsub, fchunk, tiles_per_seq, n_tiles):
    s = pl.program_id(0)
    slot = s % 2

    @pl.when(s == 0)
    def _():
        mix_s[...] = jnp.zeros_like(mix_s)

    @pl.when(s % tiles_per_seq == 0)
    def _():
        cn_s[...] = jnp.zeros_like(cn_s)
        mr_s[...] = jnp.zeros_like(mr_s)

    state = {'cn': [cn_s[h] for h in range(N_HEADS_MLSTM)], 'm_r': mr_s[...]}

    def write_mix(rows, cols, val):
        mix_s[slot, rows, cols] = val

    stages = _mlstm_stages(qT_ref, k_ref, vT_ref, omT_ref, gT_ref, gb_ref, write_mix, state, chunk, nsub)
    next(stages)
    tdims = (((0,), (0,)), ((), ()))
    x1 = (x_ref[...]
          + lax.dot_general(a_ref[...], woa_ref[...], tdims, preferred_element_type=F32)
          + lax.dot_general(mix_s[1 - slot], wom_ref[...], tdims, preferred_element_type=F32))
    h = _rms_rows(x1, gmlp_ref[...]).astype(BF16)
    acc = x1
    n_ff = D_FF // fchunk
    for f in range(n_ff):
        next(stages, None)
        if f % 4 == 3:
            next(stages, None)
        u = jnp.maximum(jnp.dot(h, wup_ref[:, f * fchunk:(f + 1) * fchunk], preferred_element_type=F32), 0.0)
        acc = acc + jnp.dot((u * u).astype(BF16), wdn_ref[f * fchunk:(f + 1) * fchunk, :],
                            preferred_element_type=F32)
    for _ in stages:
        pass
    y_ref[...] = _rms_rows(acc, gfin_ref[...])
    for hh in range(N_HEADS_MLSTM):
        cn_s[hh] = state['cn'][hh]
    mr_s[...] = state['m_r']

    @pl.when((s % tiles_per_seq == tiles_per_seq - 1) & (s < n_tiles))
    def _():
        for hh in range(N_HEADS_MLSTM):
            c_out[0, hh] = cn_s[hh, :DV_MLSTM, :].T
            n_out[0, hh] = cn_s[hh, DV_MLSTM:DV_MLSTM + 1, :]
        m_out[0] = mr_s[...]


def _mlstm_mlp(x2d, mix_aT, qmT, km, vmT, omT, gT, g_b, wo_a, wo_m, g_mlp, w_up, w_dn, g_fin,
               nseq, t, chunk, nsub, fchunk):
    tm = chunk * nsub
    n_tiles = (nseq * t) // tm
    tiles_per_seq = t // tm
    cur = lambda s: jnp.minimum(s, n_tiles - 1)
    prv = lambda s: jnp.maximum(s - 1, 0)
    colb = lambda r: pl.BlockSpec((r, tm), lambda s: (0, cur(s)))
    st = lambda a, d: pl.BlockSpec((1, N_HEADS_MLSTM, a, d), lambda s: (cur(s) // tiles_per_seq, 0, 0, 0))
    sds = jax.ShapeDtypeStruct
    kern = functools.partial(_mlstm_mlp_kernel, chunk=chunk, nsub=nsub, fchunk=fchunk,
                             tiles_per_seq=tiles_per_seq, n_tiles=n_tiles)
    return pl.pallas_call(
        kern,
        grid=(n_tiles + 1,),
        in_specs=[pl.BlockSpec((tm, D_MODEL), lambda s: (prv(s), 0)),
                  pl.BlockSpec((ATT_W, tm), lambda s: (0, prv(s))),
                  colb(QK_M_W), pl.BlockSpec((tm, QK_M_W), lambda s: (cur(s), 0)), colb(MLSTM_W), colb(MLSTM_W),
                  colb(N_GATES),
                  _const_spec(g_b.shape), _const_spec(wo_a.shape), _const_spec(wo_m.shape),
                  _const_spec((1, D_MODEL)), _const_spec(w_up.shape), _const_spec(w_dn.shape),
                  _const_spec((1, D_MODEL))],
        out_specs=[pl.BlockSpec((tm, D_MODEL), lambda s: (prv(s), 0)),
                   st(DK_MLSTM, DV_MLSTM), st(1, DK_MLSTM),
                   pl.BlockSpec((1, N_GATES, chunk), lambda s: (cur(s) // tiles_per_seq, 0, 0))],
        out_shape=[sds((nseq * t, D_MODEL), F32), sds((nseq, N_HEADS_MLSTM, DK_MLSTM, DV_MLSTM), F32),
                   sds((nseq, N_HEADS_MLSTM, 1, DK_MLSTM), F32), sds((nseq, N_GATES, chunk), F32)],
        scratch_shapes=[pltpu.VMEM((2, MLSTM_W, tm), BF16),
                        pltpu.VMEM((N_HEADS_MLSTM, _CN_ROWS, DK_MLSTM), F32),
                        pltpu.VMEM((N_GATES, chunk), F32)],
        compiler_params=pltpu.CompilerParams(dimension_semantics=("arbitrary",), vmem_limit_bytes=VMEM_LIMIT),
        name="mlstm_mlp",
    )(x2d, mix_aT, qmT, km, vmT, omT, gT, g_b, wo_a, wo_m, g_mlp, w_up, w_dn, g_fin)


def kernel(x_prompt, x_sample, cache_k_win, cache_v_win, state_C, state_n, state_m, rel_bias, norm_mix, w_in,
           b_gates, attn_sinks, norm_attn_out, norm_mlstm_out, w_out, norm_mlp, w_up, w_down, norm_final):
    depth = w_in.shape[0]
    assert depth == 1, "single-layer trunk"
    bp, sp = x_prompt.shape[:2]
    bs, ts = x_sample.shape[:2]
    wc = cache_k_win.shape[2]
    l = 0
    chunk = 128

    wl = w_in[l]
    cols = lambda lo, n: wl[:, lo:lo + n]
    w_n = jnp.concatenate([cols(_O_KV, 2 * KV_W), cols(_O_QKM + QK_M_W, QK_M_W),
                           jnp.pad(cols(_O_G, N_GATES), ((0, 0), (0, LANES - N_GATES)))], axis=1).astype(BF16)
    w_t = jnp.concatenate([cols(_O_QA, ATT_W), cols(_O_KV + KV_W, KV_W), cols(_O_QKM, QK_M_W), cols(_O_VM, MLSTM_W),
                           cols(_O_OM, MLSTM_W), jnp.pad(cols(_O_G, N_GATES), ((0, 0), (0, BF16_ROWS - N_GATES)))],
                          axis=1).T.astype(BF16)
    bg_row = jnp.pad(b_gates[l], (0, LANES - N_GATES)).reshape(1, LANES)
    bg_col = b_gates[l].reshape(N_GATES, 1)
    wo_a = w_out[l, :ATT_W].astype(BF16)
    wo_m = w_out[l, ATT_W:].astype(BF16)
    wup = w_up[l].astype(BF16)
    wdn = w_down[l].astype(BF16)
    g_mix = norm_mix[l].reshape(1, D_MODEL)
    g_att = norm_attn_out[l].reshape(1, ATT_W)
    g_m = norm_mlstm_out[l].reshape(1, MLSTM_W)
    g_mlp = norm_mlp[l].reshape(1, D_MODEL)
    g_fin = norm_final.reshape(1, D_MODEL)
    gb_att = jnp.broadcast_to(norm_attn_out[l].reshape(ATT_W, 1), (ATT_W, WINDOW))
    gb_m = jnp.broadcast_to(norm_mlstm_out[l].reshape(MLSTM_W, 1), (MLSTM_W, chunk))
    sinks = attn_sinks[l]

    xp = x_prompt.reshape(bp * sp, D_MODEL)
    bias_t = _bias_table(rel_bias, _prompt_buckets_t())
    kvf, km, qmT, vmT, omT, gT, mix_a = _proj_attn(xp, g_mix, w_n, w_t, bg_col, bias_t, sinks, gb_att, sp, tm=1024)
    y_p, c_p, n_p, m_p = _mlstm_mlp(xp, mix_a, qmT, km, vmT, omT, gT, gb_m, wo_a, wo_m, g_mlp, wup, wdn, g_fin,
                                    bp, sp, chunk=chunk, nsub=4, fchunk=D_FF)
    m_p = m_p[:, _GATE0:_GATE0 + N_HEADS_MLSTM, 0]
    win_p = min(WINDOW, sp)
    kv_win = kvf.reshape(bp, sp, 2 * KV_W)[:, sp - win_p:, :]
    k_win_p = kv_win[:, :, :KV_W].reshape(1, bp, win_p, N_KV_HEADS, HEAD_DIM_ATT)
    v_win_p = kv_win[:, :, KV_W:].reshape(1, bp, win_p, N_KV_HEADS, HEAD_DIM_ATT)

    assert math.gcd(ts, 64) == ts and bs % SEQ_PER_BLOCK == 0, "sample group: one mLSTM chunk per sequence"
    kpad = ((wc + ts + BF16_ROWS - 1) // BF16_ROWS) * BF16_ROWS
    bias_s = _bias_table(rel_bias, _sample_buckets(ts, wc, kpad))
    bias_s = bias_s.reshape(N_KV_HEADS, GQA_GROUP * ts, kpad)
    sink_rows = jnp.repeat(sinks, ts).reshape(N_KV_HEADS, GQA_GROUP * ts, 1)
    xs = x_sample.reshape(bs * ts, D_MODEL)
    qa, kvf_s, qkm, vm, om, gt = _in_proj_s(xs, g_mix, w_n, w_t, bg_row, tm=512)
    ck = cache_k_win[l].reshape(bs, wc, KV_W)
    cv = cache_v_win[l].reshape(bs, wc, KV_W)
    mix_a_s, kw_s, vw_s = _attn_sample_b(qa, kvf_s, ck, cv, bias_s, sink_rows, g_att, bs, ts, bb=16)
    n0 = state_n[l].reshape(bs, QK_M_W)
    n0_rows = jnp.repeat(n0, ts, axis=0)
    m0_rows = jnp.pad(jnp.repeat(state_m[l], ts, axis=0),
                      ((0, 0), (_GATE0, LANES - _GATE0 - N_HEADS_MLSTM)))
    mix_m_s, c_s, n_s, m_rows = _mlstm_blk(qkm, vm, om, gt, g_m, state_C[l], n0, n0_rows, m0_rows, bs, ts)
    m_s = m_rows[::ts, _GATE0:_GATE0 + N_HEADS_MLSTM]
    y_s = _out_mlp(xs, mix_a_s, mix_m_s, wo_a, wo_m, g_mlp, wup, wdn, g_fin, tm=512)

    return (y_p.reshape(bp, sp, D_MODEL), y_s.reshape(bs, ts, D_MODEL),
            k_win_p, v_win_p,
            c_p[None], n_p.reshape(1, bp, N_HEADS_MLSTM, DK_MLSTM), m_p.reshape(1, bp, N_HEADS_MLSTM),
            kw_s.reshape(1, bs, wc, N_KV_HEADS, HEAD_DIM_ATT), vw_s.reshape(1, bs, wc, N_KV_HEADS, HEAD_DIM_ATT),
            c_s[None], n_s.reshape(1, bs, N_HEADS_MLSTM, DK_MLSTM), m_s.reshape(1, bs, N_HEADS_MLSTM))
```

```python
import functools
import math

import numpy as np
import jax
import jax.numpy as jnp
from jax import lax
from jax.experimental import pallas as pl
from jax.experimental.pallas import tpu as pltpu

D_MODEL = 1024
N_HEADS_ATT = 8
N_KV_HEADS = 2
GQA_GROUP = N_HEADS_ATT // N_KV_HEADS
HEAD_DIM_ATT = 64
ATT_W = N_HEADS_ATT * HEAD_DIM_ATT
KV_W = N_KV_HEADS * HEAD_DIM_ATT
WINDOW = 128
NUM_BUCKETS = 32
MAX_DISTANCE = 128
N_HEADS_MLSTM = 4
DV_MLSTM = 128
DK_MLSTM = 64
MLSTM_W = N_HEADS_MLSTM * DV_MLSTM
QK_M_W = N_HEADS_MLSTM * DK_MLSTM
D_FF = 4 * D_MODEL
EPS = 1e-6
N_GATES = 2 * N_HEADS_MLSTM
LANES = 128
BF16_ROWS = 16
VMEM_LIMIT = 56 * 1024 * 1024

F32 = jnp.float32
BF16 = jnp.bfloat16

_O_QA = 0
_O_KV = _O_QA + ATT_W
_O_QKM = _O_KV + 2 * KV_W
_O_VM = _O_QKM + 2 * QK_M_W
_O_OM = _O_VM + MLSTM_W
_O_G = _O_OM + MLSTM_W
_T_QA = 0
_T_VA = _T_QA + ATT_W
_T_QM = _T_VA + KV_W
_T_VM = _T_QM + QK_M_W
_T_OM = _T_VM + MLSTM_W
_T_G = _T_OM + MLSTM_W
_T_ROWS = _T_G + BF16_ROWS
_N_KV = 0
_N_KM = _N_KV + 2 * KV_W
_N_G = _N_KM + QK_M_W
_N_COLS = _N_G + LANES
_GATE0 = N_HEADS_MLSTM


def _const_spec(shape):
    nd = len(shape)
    return pl.BlockSpec(shape, lambda *_: (0,) * nd, pipeline_mode=pl.Buffered(1))


def _log_sigmoid(g):
    return jnp.minimum(g, 0.0) - jnp.log1p(jnp.exp(-jnp.abs(g)))


def _rms_rows(x, gain):
    return x * lax.rsqrt(jnp.mean(x * x, axis=-1, keepdims=True) + EPS) * gain


def _in_proj_t_kernel(x_ref, g_ref, wn_ref, wt_ref, bgc_ref,
                      k_ref, kvf_ref, km_ref, qT_ref, vT_ref, qmT_ref, vmT_ref, omT_ref, gT_ref):
    hb = _rms_rows(x_ref[...], g_ref[...]).astype(BF16)
    z_n = jnp.dot(hb, wn_ref[:, :_N_G], preferred_element_type=F32)
    z_t = lax.dot_general(wt_ref[...], hb, (((1,), (1,)), ((), ())), preferred_element_type=F32)
    kvf_ref[...] = z_n[:, _N_KV:_N_KV + 2 * KV_W]
    k_ref[...] = z_n[:, _N_KV:_N_KV + KV_W].astype(BF16)
    km_ref[...] = (z_n[:, _N_KM:_N_KM + QK_M_W] * (DK_MLSTM ** -0.5)).astype(BF16)
    qT_ref[...] = (z_t[_T_QA:_T_QA + ATT_W, :] * (HEAD_DIM_ATT ** -0.5)).astype(BF16)
    vT_ref[...] = z_t[_T_VA:_T_VA + KV_W, :].astype(BF16)
    qmT_ref[...] = z_t[_T_QM:_T_QM + QK_M_W, :].astype(BF16)
    vmT_ref[...] = z_t[_T_VM:_T_VM + MLSTM_W, :].astype(BF16)
    omT_ref[...] = jax.nn.sigmoid(z_t[_T_OM:_T_OM + MLSTM_W, :])
    g_t = z_t[_T_G:_T_G + N_GATES, :] + bgc_ref[...]
    row = lax.broadcasted_iota(jnp.int32, g_t.shape, 0)
    gT_ref[...] = jnp.where(row < N_HEADS_MLSTM, g_t, _log_sigmoid(g_t))


def _in_proj_t(x2d, g_norm, w_n, w_t, bg_col, tm):
    n = x2d.shape[0]
    row = lambda w: pl.BlockSpec((tm, w), lambda i: (i, 0))
    colb = lambda r: pl.BlockSpec((r, tm), lambda i: (0, i))
    sds = jax.ShapeDtypeStruct
    return pl.pallas_call(
        _in_proj_t_kernel,
        grid=(n // tm,),
        in_specs=[row(D_MODEL), _const_spec((1, D_MODEL)), _const_spec(w_n.shape), _const_spec(w_t.shape),
                  _const_spec((N_GATES, 1))],
        out_specs=[row(KV_W), row(2 * KV_W), row(QK_M_W),
                   colb(ATT_W), colb(KV_W), colb(QK_M_W), colb(MLSTM_W), colb(MLSTM_W), colb(N_GATES)],
        out_shape=[sds((n, KV_W), BF16), sds((n, 2 * KV_W), F32), sds((n, QK_M_W), BF16),
                   sds((ATT_W, n), BF16), sds((KV_W, n), BF16), sds((QK_M_W, n), BF16), sds((MLSTM_W, n), BF16),
                   sds((MLSTM_W, n), F32), sds((N_GATES, n), F32)],
        compiler_params=pltpu.CompilerParams(dimension_semantics=("arbitrary",), vmem_limit_bytes=VMEM_LIMIT),
        name="in_proj_t",
    )(x2d, g_norm, w_n, w_t, bg_col)


def _in_proj_s_kernel(x_ref, g_ref, wn_ref, wt_ref, bgr_ref, qa_ref, kvf_ref, qkm_ref, vm_ref, om_ref, gt_ref):
    hb = _rms_rows(x_ref[...], g_ref[...]).astype(BF16)
    z_n = jnp.dot(hb, wn_ref[...], preferred_element_type=F32)
    z_t = lax.dot_general(hb, wt_ref[...], (((1,), (1,)), ((), ())), preferred_element_type=F32)
    qa_ref[...] = (z_t[:, _T_QA:_T_QA + ATT_W] * (HEAD_DIM_ATT ** -0.5)).astype(BF16)
    kvf_ref[:, :KV_W] = z_n[:, _N_KV:_N_KV + KV_W]
    kvf_ref[:, KV_W:] = z_t[:, _T_VA:_T_VA + KV_W]
    qkm_ref[:, :QK_M_W] = z_t[:, _T_QM:_T_QM + QK_M_W].astype(BF16)
    qkm_ref[:, QK_M_W:] = (z_n[:, _N_KM:_N_KM + QK_M_W] * (DK_MLSTM ** -0.5)).astype(BF16)
    vm_ref[...] = z_t[:, _T_VM:_T_VM + MLSTM_W].astype(BF16)
    om_ref[...] = jax.nn.sigmoid(z_t[:, _T_OM:_T_OM + MLSTM_W])
    g = z_n[:, _N_G:_N_G + LANES] + bgr_ref[...]
    col = lax.broadcasted_iota(jnp.int32, g.shape, 1)
    gt_ref[...] = jnp.where(col < N_HEADS_MLSTM, g, _log_sigmoid(g))


def _in_proj_s(x2d, g_norm, w_n, w_t, bg_row, tm):
    n = x2d.shape[0]
    row = lambda w: pl.BlockSpec((tm, w), lambda i: (i, 0))
    sds = jax.ShapeDtypeStruct
    return pl.pallas_call(
        _in_proj_s_kernel,
        grid=(n // tm,),
        in_specs=[row(D_MODEL), _const_spec((1, D_MODEL)), _const_spec(w_n.shape), _const_spec(w_t.shape),
                  _const_spec((1, LANES))],
        out_specs=[row(ATT_W), row(2 * KV_W), row(2 * QK_M_W), row(MLSTM_W), row(MLSTM_W), row(LANES)],
        out_shape=[sds((n, ATT_W), BF16), sds((n, 2 * KV_W), F32), sds((n, 2 * QK_M_W), BF16),
                   sds((n, MLSTM_W), BF16), sds((n, MLSTM_W), F32), sds((n, LANES), F32)],
        compiler_params=pltpu.CompilerParams(dimension_semantics=("arbitrary",), vmem_limit_bytes=VMEM_LIMIT),
        name="in_proj_s",
    )(x2d, g_norm, w_n, w_t, bg_row)


def _t5_bucket_np(dist):
    max_exact = NUM_BUCKETS // 2
    d = np.maximum(dist, 0)
    ratio = np.maximum(d, 1).astype(np.float32) / np.float32(max_exact)
    large = max_exact + (np.log(ratio) / np.float32(math.log(MAX_DISTANCE / max_exact))
                         * np.float32(NUM_BUCKETS - max_exact)).astype(np.int32)
    large = np.minimum(large, NUM_BUCKETS - 1)
    return np.where(d < max_exact, d, large).astype(np.int32)


def _bias_table_kernel(rb_ref, bucket_ref, out_ref):
    h = pl.program_id(0)
    bucket = bucket_ref[...]
    acc = jnp.full(bucket.shape, -jnp.inf, F32)
    for j in range(NUM_BUCKETS):
        acc = jnp.where(bucket == j, rb_ref[j, h], acc)
    out_ref[0] = acc


def _bias_table(rel_bias, bucket):
    r, c = bucket.shape
    return pl.pallas_call(
        _bias_table_kernel,
        grid=(N_HEADS_ATT,),
        in_specs=[pl.BlockSpec(memory_space=pltpu.SMEM), pl.BlockSpec((r, c), lambda h: (0, 0))],
        out_specs=pl.BlockSpec((1, r, c), lambda h: (h, 0, 0)),
        out_shape=jax.ShapeDtypeStruct((N_HEADS_ATT, r, c), F32),
        compiler_params=pltpu.CompilerParams(dimension_semantics=("arbitrary",)),
        name="bias_table",
    )(rel_bias, jnp.asarray(bucket))


def _prompt_buckets_t():
    qi = np.arange(WINDOW)[None, :]
    ci = np.arange(WINDOW)[:, None]
    upper = ci > qi
    general = _t5_bucket_np(np.where(upper, qi + WINDOW - ci, qi - ci))
    first = np.where(upper, -1, general)
    return np.concatenate([general, first], axis=0).astype(np.int32)


def _sample_buckets(t, w, padded):
    dist = (w + np.arange(t))[:, None] - np.arange(padded)[None, :]
    valid = (dist >= 0) & (dist <= WINDOW) & (np.arange(padded)[None, :] < w + t)
    return np.where(valid, _t5_bucket_np(dist), -1).astype(np.int32)


def _attn_prompt_t_kernel(qT_ref, kp_ref, kc_ref, vTp_ref, vTc_ref, bias_ref, sink_ref, gb_ref, out_ref, *, qb):
    first = pl.program_id(1) == 0
    k_all = jnp.concatenate([kp_ref[...], kc_ref[...]], axis=0)
    vT_all = jnp.concatenate([vTp_ref[...], vTc_ref[...]], axis=1)

    def write_out(rows, cols, val):
        out_ref[rows, cols] = val

    for _ in _attn_blocks(lambda rows, cols: qT_ref[rows, cols], k_all, vT_all, first, bias_ref, sink_ref, gb_ref,
                          write_out, qb):
        pass


def _attn_prompt_t(qT, k, vT, bias, sinks, g_b, batch, seq, qb):
    nb = seq // WINDOW
    ns = nb // qb
    prev = lambda b, j: jnp.maximum(b * nb + j * qb - 1, 0)
    return pl.pallas_call(
        functools.partial(_attn_prompt_t_kernel, qb=qb),
        grid=(batch, ns),
        in_specs=[pl.BlockSpec((ATT_W, qb * WINDOW), lambda b, j: (0, b * ns + j)),
                  pl.BlockSpec((WINDOW, KV_W), lambda b, j: (prev(b, j), 0)),
                  pl.BlockSpec((qb * WINDOW, KV_W), lambda b, j: (b * ns + j, 0)),
                  pl.BlockSpec((KV_W, WINDOW), lambda b, j: (0, prev(b, j))),
                  pl.BlockSpec((KV_W, qb * WINDOW), lambda b, j: (0, b * ns + j)),
                  _const_spec(bias.shape),
                  pl.BlockSpec(memory_space=pltpu.SMEM),
                  _const_spec(g_b.shape)],
        out_specs=pl.BlockSpec((ATT_W, qb * WINDOW), lambda b, j: (0, b * ns + j)),
        out_shape=jax.ShapeDtypeStruct((ATT_W, batch * seq), BF16),
        compiler_params=pltpu.CompilerParams(dimension_semantics=("arbitrary", "arbitrary"),
                                             vmem_limit_bytes=VMEM_LIMIT),
        name="attn_prompt_t",
    )(qT, k, k, vT, vT, bias, sinks, g_b)


def _attn_block(i, qT, k_all, vT_all, first, bias_ref, sink_ref, gb_ref, write_out):
    W = WINDOW
    gd = lambda g: slice(g * HEAD_DIM_ATT, (g + 1) * HEAD_DIM_ATT)
    cols = slice(i * W, (i + 1) * W)
    is_first = first if i == 0 else False
    off = pl.multiple_of(jnp.where(first, W, 0), W) if i == 0 else 0
    k2 = k_all[i * W:(i + 2) * W, :]
    vT2 = vT_all[:, i * W:(i + 2) * W]
    ci = lax.broadcasted_iota(jnp.int32, (W, W), 0)
    qi = lax.broadcasted_iota(jnp.int32, (W, W), 1)
    upper = ci > qi
    eye = ci == qi
    scores = []
    for g in range(N_KV_HEADS):
        q_grp = jnp.concatenate([qT(slice((g * GQA_GROUP + j) * HEAD_DIM_ATT, (g * GQA_GROUP + j + 1) * HEAD_DIM_ATT),
                                    cols) for j in range(GQA_GROUP)], axis=1)
        scores.append(jnp.dot(k2[:, gd(g)], q_grp, preferred_element_type=F32))
    yield
    probs, rden, pfar = [], [], []
    for h in range(N_HEADS_ATT):
        g, j = divmod(h, GQA_GROUP)
        s_prev = scores[g][:W, j * W:(j + 1) * W]
        s_cur = scores[g][W:, j * W:(j + 1) * W]
        s = jnp.where(upper, s_prev, s_cur) + bias_ref[h, pl.ds(off, W), :]
        far_bias = jnp.where(is_first, -jnp.inf, sink_ref[1, h])
        s_far = jnp.sum(jnp.where(eye, s_prev, 0.0), axis=0, keepdims=True) + far_bias
        sink = sink_ref[0, h]
        m = jnp.maximum(jnp.maximum(jnp.max(s, axis=0, keepdims=True), s_far), sink)
        p = jnp.exp(s - m)
        p_far = jnp.exp(s_far - m)
        rden.append(1.0 / (jnp.sum(p, axis=0, keepdims=True) + p_far + jnp.exp(sink - m)))
        pfar.append(p_far)
        zero = jnp.zeros_like(p)
        probs.append(jnp.concatenate([jnp.where(upper, p, zero), jnp.where(upper, zero, p)], axis=0).astype(BF16))
    yield
    outs, ssq = [], jnp.zeros((1, W), F32)
    for g in range(N_KV_HEADS):
        p_grp = jnp.concatenate(probs[g * GQA_GROUP:(g + 1) * GQA_GROUP], axis=1)
        o_grp = jnp.dot(vT2[gd(g), :], p_grp, preferred_element_type=F32)
        v_far = vT2[gd(g), :W].astype(F32)
        for j in range(GQA_GROUP):
            h = g * GQA_GROUP + j
            o = (o_grp[:, j * W:(j + 1) * W] + v_far * pfar[h]) * rden[h]
            outs.append(o)
            ssq = ssq + jnp.sum(o * o, axis=0, keepdims=True)
    inv = lax.rsqrt(ssq * (1.0 / ATT_W) + EPS)
    for h in range(N_HEADS_ATT):
        rows = slice(h * HEAD_DIM_ATT, (h + 1) * HEAD_DIM_ATT)
        write_out(rows, cols, (outs[h] * inv * gb_ref[rows, :]).astype(BF16))
    yield


def _attn_blocks(qT, k_all, vT_all, first, bias_ref, sink_ref, gb_ref, write_out, qb):
    gens = [_attn_block(i, qT, k_all, vT_all, first, bias_ref, sink_ref, gb_ref, write_out) for i in range(qb)]
    next(gens[0])
    next(gens[0])
    for i in range(qb):
        if i + 1 < qb:
            next(gens[i + 1])
        yield
        next(gens[i])
        if i + 1 < qb:
            next(gens[i + 1])


PROJ_ROW_CHUNK = 256


def _proj_attn_kernel(x_ref, g_ref, wn_ref, wt_ref, bgc_ref, bias_ref, sink_ref, gb_ref,
                      kvf_ref, km_ref, qmT_ref, vmT_ref, omT_ref, gT_ref, mix_ref,
                      q_s, k_s, vT_s, hk_s, hv_s, *, tm, tiles_per_seq):
    s = pl.program_id(0)
    slot = s % 2
    prev = 1 - slot
    qb = tm // WINDOW

    @pl.when(s == 0)
    def _():
        q_s[...] = jnp.zeros_like(q_s)
        k_s[...] = jnp.zeros_like(k_s)
        vT_s[...] = jnp.zeros_like(vT_s)
        hk_s[...] = jnp.zeros_like(hk_s)
        hv_s[...] = jnp.zeros_like(hv_s)

    first = (s + tiles_per_seq - 1) % tiles_per_seq == 0
    k_all = jnp.concatenate([hk_s[...], k_s[prev]], axis=0)
    vT_all = jnp.concatenate([hv_s[...], vT_s[prev]], axis=1)

    def write_out(rows, cols, val):
        mix_ref[rows, cols] = val

    blocks = _attn_blocks(lambda rows, cols: q_s[prev, rows, cols], k_all, vT_all, first, bias_ref, sink_ref, gb_ref,
                          write_out, qb)
    hb = _rms_rows(x_ref[...], g_ref[...]).astype(BF16)

    def proj_t(lo, hi):
        return lax.dot_general(wt_ref[lo:hi, :], hb, (((1,), (1,)), ((), ())), preferred_element_type=F32)

    def put_q(lo, z):
        q_s[slot, lo:lo + z.shape[0], :] = (z * (HEAD_DIM_ATT ** -0.5)).astype(BF16)

    def put_v(lo, z):
        vT_s[slot, lo:lo + z.shape[0], :] = z.astype(BF16)

    def put_qm(lo, z):
        qmT_ref[lo:lo + z.shape[0], :] = z.astype(BF16)

    def put_vm(lo, z):
        vmT_ref[lo:lo + z.shape[0], :] = z.astype(BF16)

    def put_om(lo, z):
        omT_ref[lo:lo + z.shape[0], :] = jax.nn.sigmoid(z)

    def put_g(lo, z):
        g_t = z[:N_GATES, :] + bgc_ref[...]
        row = lax.broadcasted_iota(jnp.int32, g_t.shape, 0)
        gT_ref[...] = jnp.where(row < N_HEADS_MLSTM, g_t, _log_sigmoid(g_t))

    segments = [(_T_QA, _T_VA, put_q), (_T_VA, _T_QM, put_v), (_T_QM, _T_VM, put_qm), (_T_VM, _T_OM, put_vm),
                (_T_OM, _T_G, put_om), (_T_G, _T_ROWS, put_g)]

    def proj_rows(lo, hi):
        z = proj_t(lo, hi)
        for a, b, put in segments:
            if a < hi and lo < b:
                put(max(lo, a) - a, z[max(lo, a) - lo:min(hi, b) - lo, :])

    z_n = jnp.dot(hb, wn_ref[:, :_N_G], preferred_element_type=F32)
    kvf_ref[...] = z_n[:, _N_KV:_N_KV + 2 * KV_W]
    km_ref[...] = (z_n[:, _N_KM:_N_KM + QK_M_W] * (DK_MLSTM ** -0.5)).astype(BF16)
    k_s[slot] = z_n[:, _N_KV:_N_KV + KV_W].astype(BF16)
    bounds = [min(c * PROJ_ROW_CHUNK, _T_ROWS) for c in range(-(-_T_ROWS // PROJ_ROW_CHUNK) + 1)]
    chunks = list(zip(bounds[:-1], bounds[1:]))
    for _ in blocks:
        if chunks:
            proj_rows(*chunks.pop(0))
    for lo, hi in chunks:
        proj_rows(lo, hi)
    hk_s[...] = k_s[prev, tm - WINDOW:, :]
    hv_s[...] = vT_s[prev, :, tm - WINDOW:]


def _proj_attn(x2d, g_norm, w_n, w_t, bg_col, bias, sinks, g_b, seq, tm):
    n = x2d.shape[0]
    n_tiles = n // tm
    tiles_per_seq = seq // tm
    cur = lambda s: jnp.minimum(s, n_tiles - 1)
    prv = lambda s: jnp.maximum(s - 1, 0)
    row = lambda w: pl.BlockSpec((tm, w), lambda s: (cur(s), 0))
    colb = lambda r: pl.BlockSpec((r, tm), lambda s: (0, cur(s)))
    sds = jax.ShapeDtypeStruct
    return pl.pallas_call(
        functools.partial(_proj_attn_kernel, tm=tm, tiles_per_seq=tiles_per_seq),
        grid=(n_tiles + 1,),
        in_specs=[row(D_MODEL), _const_spec((1, D_MODEL)), _const_spec(w_n.shape), _const_spec(w_t.shape),
                  _const_spec((N_GATES, 1)), _const_spec(bias.shape), pl.BlockSpec(memory_space=pltpu.SMEM),
                  _const_spec(g_b.shape)],
        out_specs=[row(2 * KV_W), row(QK_M_W), colb(QK_M_W), colb(MLSTM_W), colb(MLSTM_W), colb(N_GATES),
                   pl.BlockSpec((ATT_W, tm), lambda s: (0, prv(s)))],
        out_shape=[sds((n, 2 * KV_W), F32), sds((n, QK_M_W), BF16), sds((QK_M_W, n), BF16), sds((MLSTM_W, n), BF16),
                   sds((MLSTM_W, n), F32), sds((N_GATES, n), F32), sds((ATT_W, n), BF16)],
        scratch_shapes=[pltpu.VMEM((2, ATT_W, tm), BF16), pltpu.VMEM((2, tm, KV_W), BF16),
                        pltpu.VMEM((2, KV_W, tm), BF16), pltpu.VMEM((WINDOW, KV_W), BF16),
                        pltpu.VMEM((KV_W, WINDOW), BF16)],
        compiler_params=pltpu.CompilerParams(dimension_semantics=("arbitrary",), vmem_limit_bytes=VMEM_LIMIT),
        name="proj_attn",
    )(x2d, g_norm, w_n, w_t, bg_col, bias, sinks, g_b)


def _attn_sample_kernel(q_ref, kvf_ref, ck_ref, cv_ref, bias_ref, sink_ref, g_ref, out_ref, kw_ref, vw_ref,
                        *, bb, t, w, kpad):
    seqs = range(bb)
    groups = range(N_KV_HEADS)
    tok = lambda b: slice(b * t, (b + 1) * t)
    gd = lambda g: slice(g * HEAD_DIM_ATT, (g + 1) * HEAD_DIM_ATT)
    zpad = jnp.zeros((kpad - w - t, KV_W), F32)
    kk, vv = [], []
    for b in seqs:
        ck = ck_ref[b]
        cv = cv_ref[b]
        k_new = kvf_ref[tok(b), :KV_W]
        v_new = kvf_ref[tok(b), KV_W:]
        kw_ref[b, :w - t, :] = ck[t:, :]
        kw_ref[b, w - t:, :] = k_new
        vw_ref[b, :w - t, :] = cv[t:, :]
        vw_ref[b, w - t:, :] = v_new
        kk.append(jnp.concatenate([ck, k_new, zpad], axis=0).astype(BF16))
        vv.append(jnp.concatenate([cv, v_new, zpad], axis=0).astype(BF16))
    scores = []
    for b in seqs:
        qb = q_ref[tok(b), :].astype(F32)
        for g in groups:
            qg = jnp.concatenate(
                [qb[:, (g * GQA_GROUP + i) * HEAD_DIM_ATT:(g * GQA_GROUP + i + 1) * HEAD_DIM_ATT]
                 for i in range(GQA_GROUP)], axis=0).astype(BF16)
            scores.append(lax.dot_general(qg, kk[b][:, gd(g)], (((1,), (1,)), ((), ())),
                                          preferred_element_type=F32))
    probs = []
    for b in seqs:
        for g in groups:
            s = scores[b * N_KV_HEADS + g] + bias_ref[g]
            sink = sink_ref[g]
            m = jnp.maximum(jnp.max(s, axis=1, keepdims=True), sink)
            p = jnp.exp(s - m)
            denom = jnp.sum(p, axis=1, keepdims=True) + jnp.exp(sink - m)
            probs.append((p * (1.0 / denom)).astype(BF16))
    outs = [jnp.dot(probs[b * N_KV_HEADS + g], vv[b][:, gd(g)], preferred_element_type=F32)
            for b in seqs for g in groups]
    for b in seqs:
        heads = [outs[b * N_KV_HEADS + g][i * t:(i + 1) * t, :] for g in groups for i in range(GQA_GROUP)]
        ssq = jnp.zeros((t, 1), F32)
        for oi in heads:
            ssq = ssq + jnp.sum(oi * oi, axis=1, keepdims=True)
        inv = lax.rsqrt(ssq * (1.0 / ATT_W) + EPS)
        for h in range(N_HEADS_ATT):
            sl = slice(h * HEAD_DIM_ATT, (h + 1) * HEAD_DIM_ATT)
            out_ref[tok(b), sl] = (heads[h] * inv * g_ref[:, sl]).astype(BF16)


def _attn_sample(qa, kvf, ck, cv, bias, sink_rows, g_att, nseq, t, bb):
    w = ck.shape[1]
    kpad = bias.shape[-1]
    rows = GQA_GROUP * t
    kern = functools.partial(_attn_sample_kernel, bb=bb, t=t, w=w, kpad=kpad)
    tok = lambda wd: pl.BlockSpec((bb * t, wd), lambda i: (i, 0))
    cache = pl.BlockSpec((bb, w, KV_W), lambda i: (i, 0, 0))
    return pl.pallas_call(
        kern,
        grid=(nseq // bb,),
        in_specs=[tok(ATT_W), tok(2 * KV_W), cache, cache,
                  _const_spec((N_KV_HEADS, rows, kpad)), _const_spec((N_KV_HEADS, rows, 1)), _const_spec((1, ATT_W))],
        out_specs=[tok(ATT_W), cache, cache],
        out_shape=[jax.ShapeDtypeStruct((nseq * t, ATT_W), BF16),
                   jax.ShapeDtypeStruct((nseq, w, KV_W), F32), jax.ShapeDtypeStruct((nseq, w, KV_W), F32)],
        compiler_params=pltpu.CompilerParams(dimension_semantics=("arbitrary",), vmem_limit_bytes=VMEM_LIMIT),
        name="attn_sample",
    )(qa, kvf, ck, cv, bias, sink_rows, g_att)


def _attn_sample_b_kernel(q_ref, kvf_ref, ck_ref, cv_ref, bias_ref, sink_ref, g_ref, out_ref, kw_ref, vw_ref,
                          *, bb, t, w, kpad):
    rows = GQA_GROUP * t
    ck = ck_ref[...]
    cv = cv_ref[...]
    k_new = kvf_ref[:, :KV_W].reshape(bb, t, KV_W)
    v_new = kvf_ref[:, KV_W:].reshape(bb, t, KV_W)
    kw_ref[:, :w - t, :] = ck[:, t:, :]
    kw_ref[:, w - t:, :] = k_new
    vw_ref[:, :w - t, :] = cv[:, t:, :]
    vw_ref[:, w - t:, :] = v_new
    zpad = jnp.zeros((bb, kpad - w - t, KV_W), F32)
    kk = jnp.concatenate([ck, k_new, zpad], axis=1).astype(BF16)
    vv = jnp.concatenate([cv, v_new, zpad], axis=1).astype(BF16)
    q3 = q_ref[...].astype(F32).reshape(bb, t, ATT_W)
    outs = []
    ssq = jnp.zeros((bb, 1, t, 1), F32)
    for g in range(N_KV_HEADS):
        gd = slice(g * HEAD_DIM_ATT, (g + 1) * HEAD_DIM_ATT)
        qg = jnp.concatenate([q3[:, :, (g * GQA_GROUP + i) * HEAD_DIM_ATT:(g * GQA_GROUP + i + 1) * HEAD_DIM_ATT]
                              for i in range(GQA_GROUP)], axis=1).astype(BF16)
        s = jnp.einsum('bqd,bkd->bqk', qg, kk[:, :, gd], preferred_element_type=F32) + bias_ref[g]
        sink = sink_ref[g]
        m = jnp.maximum(jnp.max(s, axis=-1, keepdims=True), sink)
        p = jnp.exp(s - m)
        denom = jnp.sum(p, axis=-1, keepdims=True) + jnp.exp(sink - m)
        p = (p * (1.0 / denom)).astype(BF16)
        o = jnp.einsum('bqk,bkd->bqd', p, vv[:, :, gd], preferred_element_type=F32)
        o = o.reshape(bb, GQA_GROUP, t, HEAD_DIM_ATT)
        outs.append(o)
        ssq = ssq + jnp.sum(jnp.sum(o * o, axis=-1, keepdims=True), axis=1, keepdims=True)
    inv = lax.rsqrt(ssq * (1.0 / ATT_W) + EPS)
    for g in range(N_KV_HEADS):
        y = outs[g] * inv
        for i in range(GQA_GROUP):
            sl = slice((g * GQA_GROUP + i) * HEAD_DIM_ATT, (g * GQA_GROUP + i + 1) * HEAD_DIM_ATT)
            out_ref[:, sl] = (y[:, i].reshape(bb * t, HEAD_DIM_ATT) * g_ref[:, sl]).astype(BF16)


def _attn_sample_b(qa, kvf, ck, cv, bias, sink_rows, g_att, nseq, t, bb):
    w = ck.shape[1]
    kpad = bias.shape[-1]
    rows = GQA_GROUP * t
    kern = functools.partial(_attn_sample_b_kernel, bb=bb, t=t, w=w, kpad=kpad)
    tok = lambda wd: pl.BlockSpec((bb * t, wd), lambda i: (i, 0))
    cache = pl.BlockSpec((bb, w, KV_W), lambda i: (i, 0, 0))
    return pl.pallas_call(
        kern,
        grid=(nseq // bb,),
        in_specs=[tok(ATT_W), tok(2 * KV_W), cache, cache,
                  _const_spec((N_KV_HEADS, rows, kpad)), _const_spec((N_KV_HEADS, rows, 1)), _const_spec((1, ATT_W))],
        out_specs=[tok(ATT_W), cache, cache],
        out_shape=[jax.ShapeDtypeStruct((nseq * t, ATT_W), BF16),
                   jax.ShapeDtypeStruct((nseq, w, KV_W), F32), jax.ShapeDtypeStruct((nseq, w, KV_W), F32)],
        compiler_params=pltpu.CompilerParams(dimension_semantics=("arbitrary",), vmem_limit_bytes=VMEM_LIMIT),
        name="attn_sample_b",
    )(qa, kvf, ck, cv, bias, sink_rows, g_att)


SEQ_PER_BLOCK = 16


def _seg_scan_rows(x, seg, forward):
    n = x.shape[0]
    pos = lax.broadcasted_iota(jnp.int32, x.shape, 0) % seg
    k = 1
    while k < seg:
        if forward:
            x = jnp.maximum(x, jnp.where(pos >= k, pltpu.roll(x, k, axis=0), -jnp.inf))
        else:
            x = jnp.maximum(x, jnp.where(pos < seg - k, pltpu.roll(x, n - k, axis=0), -jnp.inf))
        k *= 2
    return x


def _mlstm_blk_kernel(qk_ref, v_ref, om_ref, gt_ref, gm_ref, c0_ref, n0_ref, nrow_ref, m0_ref,
                      mix_ref, c_out, n_out, m_out, *, t):
    S = SEQ_PER_BLOCK
    L = S * t
    hi = lax.Precision.HIGHEST
    ti = lax.broadcasted_iota(jnp.int32, (L, L), 0)
    si = lax.broadcasted_iota(jnp.int32, (L, L), 1)
    same = (ti // t) == (si // t)
    causal = same & (si <= ti)
    tril_seg = causal.astype(F32)
    ones_seg = same.astype(F32)
    ones_col = (lax.broadcasted_iota(jnp.int32, (L, DV_MLSTM), 1) == 0).astype(BF16)
    bd_mask = (lax.broadcasted_iota(jnp.int32, (L, S * DK_MLSTM), 0) // t
               == lax.broadcasted_iota(jnp.int32, (L, S * DK_MLSTM), 1) // DK_MLSTM)
    bd_mask_t = (lax.broadcasted_iota(jnp.int32, (S * DK_MLSTM, L), 0) // DK_MLSTM
                 == lax.broadcasted_iota(jnp.int32, (S * DK_MLSTM, L), 1) // t)
    heads = range(N_HEADS_MLSTM)
    hk = lambda h: slice(h * DK_MLSTM, (h + 1) * DK_MLSTM)
    hkk = lambda h: slice(QK_M_W + h * DK_MLSTM, QK_M_W + (h + 1) * DK_MLSTM)
    hv = lambda h: slice(h * DV_MLSTM, (h + 1) * DV_MLSTM)

    gt = gt_ref[...]
    m_prev = m0_ref[...]
    b = jnp.dot(tril_seg, gt, precision=hi, preferred_element_type=F32)
    b_end = jnp.dot(ones_seg, gt, precision=hi, preferred_element_type=F32)
    a = pltpu.roll(gt, N_HEADS_MLSTM, axis=1) - b
    big_m = jnp.maximum(m_prev, _seg_scan_rows(a, t, True))
    inter = jnp.exp(m_prev - big_m)
    em = jnp.exp(-(b + big_m))
    m_end = _seg_scan_rows(big_m, t, False)
    w = jnp.exp(a - m_end)
    decay = jnp.exp(m_prev - m_end)
    m_out[...] = b_end + m_end
    a_t = a.T
    q = [qk_ref[:, hk(h)] for h in heads]
    k = [qk_ref[:, hkk(h)] for h in heads]
    v_ext = [jnp.concatenate([v_ref[:, hv(h)], ones_col], axis=1) for h in heads]
    qk = [lax.dot_general(q[h], k[h], (((1,), (1,)), ((), ())), preferred_element_type=F32) for h in heads]
    c_all = [c0_ref[:, h].reshape(S * DK_MLSTM, DV_MLSTM) for h in heads]
    q_bd = []
    for h in heads:
        q2 = jnp.concatenate([q[h], q[h]], axis=1)
        q_bd.append(jnp.where(bd_mask, jnp.concatenate([q2] * (S // 2), axis=1), jnp.zeros((), BF16)))
    qc = [jnp.dot(q_bd[h], c_all[h].astype(BF16), preferred_element_type=F32) for h in heads]
    s = []
    for h in heads:
        ln = _GATE0 + h
        d = jnp.exp(jnp.where(causal, a_t[ln:ln + 1, :] - big_m[:, ln:ln + 1], -jnp.inf))
        s.append((qk[h] * d).astype(BF16))
    sv = [jnp.dot(s[h], v_ext[h], preferred_element_type=F32) for h in heads]
    kw, kw_bd = [], []
    for h in heads:
        ln = _GATE0 + h
        kw_f = k[h].astype(F32) * w[:, ln:ln + 1]
        kw.append(kw_f.astype(BF16))
        kw_bd.append(jnp.where(bd_mask_t, jnp.concatenate([kw_f.T] * S, axis=0), 0.0).astype(BF16))
    upd = [jnp.dot(kw_bd[h], v_ref[:, hv(h)], preferred_element_type=F32) for h in heads]
    seq_of_tok = lax.broadcasted_iota(jnp.int32, (S, L), 1) // t == lax.broadcasted_iota(jnp.int32, (S, L), 0)
    first_tok = lax.broadcasted_iota(jnp.int32, (S, L), 1) == lax.broadcasted_iota(jnp.int32, (S, L), 0) * t
    n_upd = [jnp.dot(seq_of_tok.astype(BF16), kw[h], preferred_element_type=F32) for h in heads]
    decay_seq = jnp.dot(first_tok.astype(F32), decay, precision=hi, preferred_element_type=F32)
    for h in heads:
        ln = _GATE0 + h
        inter_h = inter[:, ln:ln + 1]
        qn = jnp.sum(q[h].astype(F32) * nrow_ref[:, hk(h)], axis=1, keepdims=True)
        den = inter_h * qn + sv[h][:, DV_MLSTM:DV_MLSTM + 1]
        r = 1.0 / jnp.maximum(jnp.abs(den), em[:, ln:ln + 1])
        tt = om_ref[:, hv(h)] * (inter_h * qc[h] + sv[h][:, :DV_MLSTM])
        msq = jnp.mean(tt * tt, axis=1, keepdims=True)
        scale = r * lax.rsqrt(r * r * msq + EPS)
        mix_ref[:, hv(h)] = (tt * scale * gm_ref[:, hv(h)]).astype(BF16)
        dec = decay[:, ln:ln + 1].reshape(S, t, 1)[:, 0:1, :]
        c_out[:, h] = dec * c0_ref[:, h] + upd[h].reshape(S, DK_MLSTM, DV_MLSTM)
        n_out[:, hk(h)] = decay_seq[:, ln:ln + 1] * n0_ref[:, hk(h)] + n_upd[h]


def _mlstm_blk(qkm, vm, om, gt, g_m, c0, n0_seq, n0_rows, m0_rows, nseq, t):
    S = SEQ_PER_BLOCK
    L = S * t
    tok = lambda wd: pl.BlockSpec((L, wd), lambda i: (i, 0))
    st_c = pl.BlockSpec((S, N_HEADS_MLSTM, DK_MLSTM, DV_MLSTM), lambda i: (i, 0, 0, 0))
    st_n = pl.BlockSpec((S, QK_M_W), lambda i: (i, 0))
    sds = jax.ShapeDtypeStruct
    return pl.pallas_call(
        functools.partial(_mlstm_blk_kernel, t=t),
        grid=(nseq // S,),
        in_specs=[tok(2 * QK_M_W), tok(MLSTM_W), tok(MLSTM_W), tok(LANES), _const_spec((1, MLSTM_W)),
                  st_c, st_n, tok(QK_M_W), tok(LANES)],
        out_specs=[tok(MLSTM_W), st_c, st_n, tok(LANES)],
        out_shape=[sds((nseq * t, MLSTM_W), BF16), sds((nseq, N_HEADS_MLSTM, DK_MLSTM, DV_MLSTM), F32),
                   sds((nseq, QK_M_W), F32), sds((nseq * t, LANES), F32)],
        compiler_params=pltpu.CompilerParams(dimension_semantics=("arbitrary",), vmem_limit_bytes=VMEM_LIMIT),
        name="mlstm_blk",
    )(qkm, vm, om, gt, g_m, c0, n0_seq, n0_rows, m0_rows)


def _out_mlp_kernel(x_ref, a_ref, hm_ref, woa_ref, wom_ref, gmlp_ref, wup_ref, wdn_ref, gfin_ref, y_ref):
    x1 = (x_ref[...]
          + jnp.dot(a_ref[...], woa_ref[...], preferred_element_type=F32)
          + jnp.dot(hm_ref[...], wom_ref[...], preferred_element_type=F32))
    h = _rms_rows(x1, gmlp_ref[...]).astype(BF16)
    u = jnp.maximum(jnp.dot(h, wup_ref[...], preferred_element_type=F32), 0.0)
    y = x1 + jnp.dot((u * u).astype(BF16), wdn_ref[...], preferred_element_type=F32)
    y_ref[...] = _rms_rows(y, gfin_ref[...])


def _out_mlp(x2d, mix_a, mix_m, wo_a, wo_m, g_mlp, w_up, w_dn, g_fin, tm):
    n = x2d.shape[0]
    row = lambda w: pl.BlockSpec((tm, w), lambda i: (i, 0))
    return pl.pallas_call(
        _out_mlp_kernel,
        grid=(n // tm,),
        in_specs=[row(D_MODEL), row(ATT_W), row(MLSTM_W), _const_spec(wo_a.shape), _const_spec(wo_m.shape),
                  _const_spec((1, D_MODEL)), _const_spec(w_up.shape), _const_spec(w_dn.shape),
                  _const_spec((1, D_MODEL))],
        out_specs=row(D_MODEL),
        out_shape=jax.ShapeDtypeStruct((n, D_MODEL), F32),
        compiler_params=pltpu.CompilerParams(dimension_semantics=("arbitrary",), vmem_limit_bytes=VMEM_LIMIT),
        name="out_mlp",
    )(x2d, mix_a, mix_m, wo_a, wo_m, g_mlp, w_up, w_dn, g_fin)


_CN_ROWS = DV_MLSTM + BF16_ROWS


def _prefix_max_lanes(x):
    n = x.shape[1]
    col = lax.broadcasted_iota(jnp.int32, x.shape, 1)
    k = 1
    while k < n:
        x = jnp.maximum(x, jnp.where(col >= k, pltpu.roll(x, k, axis=1), -jnp.inf))
        k *= 2
    return x


def _mlstm_stages(qT_ref, k_ref, vT_ref, omT_ref, gT_ref, gb_ref, write_mix, state, chunk, nsub):
    L = chunk
    i0 = lax.broadcasted_iota(jnp.int32, (L, L), 0)
    i1 = lax.broadcasted_iota(jnp.int32, (L, L), 1)
    causal_t = i0 <= i1
    triu = causal_t.astype(F32)
    ones_rows = (lax.broadcasted_iota(jnp.int32, (BF16_ROWS, L), 0) == 0).astype(BF16)
    zpad = jnp.zeros((LANES - 2 * N_GATES, L), F32)
    cn, m_r = state['cn'], state['m_r']
    hi = lax.Precision.HIGHEST
    heads = range(N_HEADS_MLSTM)
    subs = range(nsub)
    col_of = lambda j: slice(j * L, (j + 1) * L)
    hv = lambda h: slice(h * DV_MLSTM, (h + 1) * DV_MLSTM)
    hk = lambda h: slice(h * DK_MLSTM, (h + 1) * DK_MLSTM)
    a_r, b_r, pm_r = [], [], []
    for j in subs:
        g_t = gT_ref[:, col_of(j)]
        b_r.append(jnp.dot(g_t, triu, precision=hi, preferred_element_type=F32))
        a_r.append(pltpu.roll(g_t, N_HEADS_MLSTM, axis=0) - b_r[j])
    yield
    kq = [[jnp.dot(k_ref[col_of(j), hk(h)], qT_ref[hk(h), col_of(j)], preferred_element_type=F32)
           for h in heads] for j in subs]
    vT_ext = [[jnp.concatenate([vT_ref[hv(h), col_of(j)], ones_rows], axis=0) for h in heads] for j in subs]
    for j in subs:
        pm_r.append(_prefix_max_lanes(a_r[j]))
    yield
    big_m, inter, em, decay, cols_t = [], [], [], [], []
    for j in subs:
        big_m.append(jnp.maximum(m_r, pm_r[j]))
        inter.append(jnp.exp(m_r - big_m[j]))
        em.append(jnp.exp(-(b_r[j] + big_m[j])))
        m_end = big_m[j][:, L - 1:L]
        w_r = jnp.exp(a_r[j] - m_end)
        decay.append(jnp.exp(m_r[:, 0:1] - m_end))
        m_r = jnp.broadcast_to(b_r[j][:, L - 1:L] + m_end, (N_GATES, L))
        cols_t.append(jnp.concatenate([a_r[j], w_r, zpad], axis=0).T)
    sT = [[None] * N_HEADS_MLSTM for _ in subs]
    kw = [[None] * N_HEADS_MLSTM for _ in subs]
    intra = [[None] * N_HEADS_MLSTM for _ in subs]
    upd = [[None] * N_HEADS_MLSTM for _ in subs]
    for j in subs:
        for h in heads:
            r = _GATE0 + h
            d = jnp.exp(jnp.where(causal_t, cols_t[j][:, r:r + 1] - big_m[j][r:r + 1, :], -jnp.inf))
            sT[j][h] = (kq[j][h] * d).astype(BF16)
            w_col = cols_t[j][:, N_GATES + r:N_GATES + r + 1]
            kw[j][h] = (k_ref[col_of(j), hk(h)].astype(F32) * w_col).astype(BF16)
        for h in heads:
            intra[j][h] = jnp.dot(vT_ext[j][h], sT[j][h], preferred_element_type=F32)
            upd[j][h] = jnp.dot(vT_ext[j][h], kw[j][h], preferred_element_type=F32)
        yield
    for j in subs:
        qc = [jnp.dot(cn[h].astype(BF16), qT_ref[hk(h), col_of(j)], preferred_element_type=F32) for h in heads]
        for h in heads:
            r = _GATE0 + h
            nd = inter[j][r:r + 1, :] * qc[h] + intra[j][h]
            rr = 1.0 / jnp.maximum(jnp.abs(nd[DV_MLSTM:DV_MLSTM + 1, :]), em[j][r:r + 1, :])
            t = omT_ref[hv(h), col_of(j)] * nd[:DV_MLSTM, :]
            msq = jnp.mean(t * t, axis=0, keepdims=True)
            scale = rr * lax.rsqrt(rr * rr * msq + EPS)
            write_mix(hv(h), col_of(j), (t * scale * gb_ref[hv(h), :]).astype(BF16))
            cn[h] = decay[j][r:r + 1, :] * cn[h] + upd[j][h]
        yield
    state['m_r'] = m_r


def _mlstm_mlp_kernel(x_ref, a_ref, qT_ref, k_ref, vT_ref, omT_ref, gT_ref, gb_ref,
                      woa_ref, wom_ref, gmlp_ref, wup_ref, wdn_ref, gfin_ref,
                      y_ref, c_out, n_out, m_out, mix_s, cn_s, mr_s,
                      *, chunk, nsub, fchunk, tiles_per_seq, n_tiles):
    s = pl.program_id(0)
    slot = s % 2

    @pl.when(s == 0)
    def _():
        mix_s[...] = jnp.zeros_like(mix_s)

    @pl.when(s % tiles_per_seq == 0)
    def _():
        cn_s[...] = jnp.zeros_like(cn_s)
        mr_s[...] = jnp.zeros_like(mr_s)

    state = {'cn': [cn_s[h] for h in range(N_HEADS_MLSTM)], 'm_r': mr_s[...]}

    def write_mix(rows, cols, val):
        mix_s[slot, rows, cols] = val

    stages = _mlstm_stages(qT_ref, k_ref, vT_ref, omT_ref, gT_ref, gb_ref, write_mix, state, chunk, nsub)
    next(stages)
    tdims = (((0,), (0,)), ((), ()))
    x1 = (x_ref[...]
          + lax.dot_general(a_ref[...], woa_ref[...], tdims, preferred_element_type=F32)
          + lax.dot_general(mix_s[1 - slot], wom_ref[...], tdims, preferred_element_type=F32))
    h = _rms_rows(x1, gmlp_ref[...]).astype(BF16)
    acc = x1
    n_ff = D_FF // fchunk
    for f in range(n_ff):
        next(stages, None)
        if f % 4 == 3:
            next(stages, None)
        u = jnp.maximum(jnp.dot(h, wup_ref[:, f * fchunk:(f + 1) * fchunk], preferred_element_type=F32), 0.0)
        acc = acc + jnp.dot((u * u).astype(BF16), wdn_ref[f * fchunk:(f + 1) * fchunk, :],
                            preferred_element_type=F32)
    for _ in stages:
        pass
    y_ref[...] = _rms_rows(acc, gfin_ref[...])
    for hh in range(N_HEADS_MLSTM):
        cn_s[hh] = state['cn'][hh]
    mr_s[...] = state['m_r']

    @pl.when((s % tiles_per_seq == tiles_per_seq - 1) & (s < n_tiles))
    def _():
        for hh in range(N_HEADS_MLSTM):
            c_out[0, hh] = cn_s[hh, :DV_MLSTM, :].T
            n_out[0, hh] = cn_s[hh, DV_MLSTM:DV_MLSTM + 1, :]
        m_out[0] = mr_s[...]


def _mlstm_mlp(x2d, mix_aT, qmT, km, vmT, omT, gT, g_b, wo_a, wo_m, g_mlp, w_up, w_dn, g_fin,
               nseq, t, chunk, nsub, fchunk):
    tm = chunk * nsub
    n_tiles = (nseq * t) // tm
    tiles_per_seq = t // tm
    cur = lambda s: jnp.minimum(s, n_tiles - 1)
    prv = lambda s: jnp.maximum(s - 1, 0)
    colb = lambda r: pl.BlockSpec((r, tm), lambda s: (0, cur(s)))
    st = lambda a, d: pl.BlockSpec((1, N_HEADS_MLSTM, a, d), lambda s: (cur(s) // tiles_per_seq, 0, 0, 0))
    sds = jax.ShapeDtypeStruct
    kern = functools.partial(_mlstm_mlp_kernel, chunk=chunk, nsub=nsub, fchunk=fchunk,
                             tiles_per_seq=tiles_per_seq, n_tiles=n_tiles)
    return pl.pallas_call(
        kern,
        grid=(n_tiles + 1,),
        in_specs=[pl.BlockSpec((tm, D_MODEL), lambda s: (prv(s), 0)),
                  pl.BlockSpec((ATT_W, tm), lambda s: (0, prv(s))),
                  colb(QK_M_W), pl.BlockSpec((tm, QK_M_W), lambda s: (cur(s), 0)), colb(MLSTM_W), colb(MLSTM_W),
                  colb(N_GATES),
                  _const_spec(g_b.shape), _const_spec(wo_a.shape), _const_spec(wo_m.shape),
                  _const_spec((1, D_MODEL)), _const_spec(w_up.shape), _const_spec(w_dn.shape),
                  _const_spec((1, D_MODEL))],
        out_specs=[pl.BlockSpec((tm, D_MODEL), lambda s: (prv(s), 0)),
                   st(DK_MLSTM, DV_MLSTM), st(1, DK_MLSTM),
                   pl.BlockSpec((1, N_GATES, chunk), lambda s: (cur(s) // tiles_per_seq, 0, 0))],
        out_shape=[sds((nseq * t, D_MODEL), F32), sds((nseq, N_HEADS_MLSTM, DK_MLSTM, DV_MLSTM), F32),
                   sds((nseq, N_HEADS_MLSTM, 1, DK_MLSTM), F32), sds((nseq, N_GATES, chunk), F32)],
        scratch_shapes=[pltpu.VMEM((2, MLSTM_W, tm), BF16),
                        pltpu.VMEM((N_HEADS_MLSTM, _CN_ROWS, DK_MLSTM), F32),
                        pltpu.VMEM((N_GATES, chunk), F32)],
        compiler_params=pltpu.CompilerParams(dimension_semantics=("arbitrary",), vmem_limit_bytes=VMEM_LIMIT),
        name="mlstm_mlp",
    )(x2d, mix_aT, qmT, km, vmT, omT, gT, g_b, wo_a, wo_m, g_mlp, w_up, w_dn, g_fin)


def kernel(x_prompt, x_sample, cache_k_win, cache_v_win, state_C, state_n, state_m, rel_bias, norm_mix, w_in,
           b_gates, attn_sinks, norm_attn_out, norm_mlstm_out, w_out, norm_mlp, w_up, w_down, norm_final):
    depth = w_in.shape[0]
    assert depth == 1, "single-layer trunk"
    bp, sp = x_prompt.shape[:2]
    bs, ts = x_sample.shape[:2]
    wc = cache_k_win.shape[2]
    l = 0
    chunk = 256

    wl = w_in[l]
    cols = lambda lo, n: wl[:, lo:lo + n]
    w_n = jnp.concatenate([cols(_O_KV, 2 * KV_W), cols(_O_QKM + QK_M_W, QK_M_W),
                           jnp.pad(cols(_O_G, N_GATES), ((0, 0), (0, LANES - N_GATES)))], axis=1).astype(BF16)
    w_t = jnp.concatenate([cols(_O_QA, ATT_W), cols(_O_KV + KV_W, KV_W), cols(_O_QKM, QK_M_W), cols(_O_VM, MLSTM_W),
                           cols(_O_OM, MLSTM_W), jnp.pad(cols(_O_G, N_GATES), ((0, 0), (0, BF16_ROWS - N_GATES)))],
                          axis=1).T.astype(BF16)
    bg_row = jnp.pad(b_gates[l], (0, LANES - N_GATES)).reshape(1, LANES)
    bg_col = b_gates[l].reshape(N_GATES, 1)
    wo_a = w_out[l, :ATT_W].astype(BF16)
    wo_m = w_out[l, ATT_W:].astype(BF16)
    wup = w_up[l].astype(BF16)
    wdn = w_down[l].astype(BF16)
    g_mix = norm_mix[l].reshape(1, D_MODEL)
    g_att = norm_attn_out[l].reshape(1, ATT_W)
    g_m = norm_mlstm_out[l].reshape(1, MLSTM_W)
    g_mlp = norm_mlp[l].reshape(1, D_MODEL)
    g_fin = norm_final.reshape(1, D_MODEL)
    gb_att = jnp.broadcast_to(norm_attn_out[l].reshape(ATT_W, 1), (ATT_W, WINDOW))
    gb_m = jnp.broadcast_to(norm_mlstm_out[l].reshape(MLSTM_W, 1), (MLSTM_W, chunk))
    sinks = attn_sinks[l]

    xp = x_prompt.reshape(bp * sp, D_MODEL)
    bias_t = _bias_table(rel_bias, _prompt_buckets_t())
    far_bucket = int(_t5_bucket_np(np.array(WINDOW)))
    sink_far = jnp.stack([sinks, rel_bias[far_bucket]])
    kvf, km, qmT, vmT, omT, gT, mix_a = _proj_attn(xp, g_mix, w_n, w_t, bg_col, bias_t, sink_far, gb_att, sp, tm=1024)
    y_p, c_p, n_p, m_p = _mlstm_mlp(xp, mix_a, qmT, km, vmT, omT, gT, gb_m, wo_a, wo_m, g_mlp, wup, wdn, g_fin,
                                    bp, sp, chunk=chunk, nsub=512 // chunk, fchunk=D_FF)
    m_p = m_p[:, _GATE0:_GATE0 + N_HEADS_MLSTM, 0]
    win_p = min(WINDOW, sp)
    kv_win = kvf.reshape(bp, sp, 2 * KV_W)[:, sp - win_p:, :]
    k_win_p = kv_win[:, :, :KV_W].reshape(1, bp, win_p, N_KV_HEADS, HEAD_DIM_ATT)
    v_win_p = kv_win[:, :, KV_W:].reshape(1, bp, win_p, N_KV_HEADS, HEAD_DIM_ATT)

    assert math.gcd(ts, 64) == ts and bs % SEQ_PER_BLOCK == 0, "sample group: one mLSTM chunk per sequence"
    kpad = ((wc + ts + BF16_ROWS - 1) // BF16_ROWS) * BF16_ROWS
    bias_s = _bias_table(rel_bias, _sample_buckets(ts, wc, kpad))
    bias_s = bias_s.reshape(N_KV_HEADS, GQA_GROUP * ts, kpad)
    sink_rows = jnp.repeat(sinks, ts).reshape(N_KV_HEADS, GQA_GROUP * ts, 1)
    xs = x_sample.reshape(bs * ts, D_MODEL)
    qa, kvf_s, qkm, vm, om, gt = _in_proj_s(xs, g_mix, w_n, w_t, bg_row, tm=512)
    ck = cache_k_win[l].reshape(bs, wc, KV_W)
    cv = cache_v_win[l].reshape(bs, wc, KV_W)
    mix_a_s, kw_s, vw_s = _attn_sample_b(qa, kvf_s, ck, cv, bias_s, sink_rows, g_att, bs, ts, bb=16)
    n0 = state_n[l].reshape(bs, QK_M_W)
    n0_rows = jnp.repeat(n0, ts, axis=0)
    m0_rows = jnp.pad(jnp.repeat(state_m[l], ts, axis=0),
                      ((0, 0), (_GATE0, LANES - _GATE0 - N_HEADS_MLSTM)))
    mix_m_s, c_s, n_s, m_rows = _mlstm_blk(qkm, vm, om, gt, g_m, state_C[l], n0, n0_rows, m0_rows, bs, ts)
    m_s = m_rows[::ts, _GATE0:_GATE0 + N_HEADS_MLSTM]
    y_s = _out_mlp(xs, mix_a_s, mix_m_s, wo_a, wo_m, g_mlp, wup, wdn, g_fin, tm=512)

    return (y_p.reshape(bp, sp, D_MODEL), y_s.reshape(bs, ts, D_MODEL),
            k_win_p, v_win_p,
            c_p[None], n_p.reshape(1, bp, N_HEADS_MLSTM, DK_MLSTM), m_p.reshape(1, bp, N_HEADS_MLSTM),
            kw_s.reshape(1, bs, wc, N_KV_HEADS, HEAD_DIM_ATT), vw_s.reshape(1, bs, wc, N_KV_HEADS, HEAD_DIM_ATT),
            c_s[None], n_s.reshape(1, bs, N_HEADS_MLSTM, DK_MLSTM), m_s.reshape(1, bs, N_HEADS_MLSTM))
```

```python
import functools
import math

import numpy as np
import jax
import jax.numpy as jnp
from jax import lax
from jax.experimental import pallas as pl
from jax.experimental.pallas import tpu as pltpu

D_MODEL = 1024
N_HEADS_ATT = 8
N_KV_HEADS = 2
GQA_GROUP = N_HEADS_ATT // N_KV_HEADS
HEAD_DIM_ATT = 64
ATT_W = N_HEADS_ATT * HEAD_DIM_ATT
KV_W = N_KV_HEADS * HEAD_DIM_ATT
WINDOW = 128
NUM_BUCKETS = 32
MAX_DISTANCE = 128
N_HEADS_MLSTM = 4
DV_MLSTM = 128
DK_MLSTM = 64
MLSTM_W = N_HEADS_MLSTM * DV_MLSTM
QK_M_W = N_HEADS_MLSTM * DK_MLSTM
D_FF = 4 * D_MODEL
EPS = 1e-6
N_GATES = 2 * N_HEADS_MLSTM
LANES = 128
BF16_ROWS = 16
VMEM_LIMIT = 56 * 1024 * 1024

F32 = jnp.float32
BF16 = jnp.bfloat16

_O_QA = 0
_O_KV = _O_QA + ATT_W
_O_QKM = _O_KV + 2 * KV_W
_O_VM = _O_QKM + 2 * QK_M_W
_O_OM = _O_VM + MLSTM_W
_O_G = _O_OM + MLSTM_W
_T_QA = 0
_T_VA = _T_QA + ATT_W
_T_QM = _T_VA + KV_W
_T_VM = _T_QM + QK_M_W
_T_OM = _T_VM + MLSTM_W
_T_G = _T_OM + MLSTM_W
_T_ROWS = _T_G + BF16_ROWS
_N_KV = 0
_N_KM = _N_KV + 2 * KV_W
_N_G = _N_KM + QK_M_W
_N_COLS = _N_G + LANES
_GATE0 = N_HEADS_MLSTM


def _const_spec(shape):
    nd = len(shape)
    return pl.BlockSpec(shape, lambda *_: (0,) * nd, pipeline_mode=pl.Buffered(1))


def _log_sigmoid(g):
    return jnp.minimum(g, 0.0) - jnp.log1p(jnp.exp(-jnp.abs(g)))


def _rms_rows(x, gain):
    return x * lax.rsqrt(jnp.mean(x * x, axis=-1, keepdims=True) + EPS) * gain


def _in_proj_s_kernel(x_ref, g_ref, wn_ref, wt_ref, bgr_ref, qa_ref, kvf_ref, qkm_ref, vm_ref, om_ref, gt_ref):
    hb = _rms_rows(x_ref[...], g_ref[...]).astype(BF16)
    z_n = jnp.dot(hb, wn_ref[...], preferred_element_type=F32)
    z_t = lax.dot_general(hb, wt_ref[...], (((1,), (1,)), ((), ())), preferred_element_type=F32)
    qa_ref[...] = (z_t[:, _T_QA:_T_QA + ATT_W] * (HEAD_DIM_ATT ** -0.5)).astype(BF16)
    kvf_ref[:, :KV_W] = z_n[:, _N_KV:_N_KV + KV_W]
    kvf_ref[:, KV_W:] = z_t[:, _T_VA:_T_VA + KV_W]
    qkm_ref[:, :QK_M_W] = z_t[:, _T_QM:_T_QM + QK_M_W].astype(BF16)
    qkm_ref[:, QK_M_W:] = (z_n[:, _N_KM:_N_KM + QK_M_W] * (DK_MLSTM ** -0.5)).astype(BF16)
    vm_ref[...] = z_t[:, _T_VM:_T_VM + MLSTM_W].astype(BF16)
    om_ref[...] = jax.nn.sigmoid(z_t[:, _T_OM:_T_OM + MLSTM_W])
    g = z_n[:, _N_G:_N_G + LANES] + bgr_ref[...]
    col = lax.broadcasted_iota(jnp.int32, g.shape, 1)
    gt_ref[...] = jnp.where(col < N_HEADS_MLSTM, g, _log_sigmoid(g))


def _in_proj_s(x2d, g_norm, w_n, w_t, bg_row, tm):
    n = x2d.shape[0]
    row = lambda w: pl.BlockSpec((tm, w), lambda i: (i, 0))
    sds = jax.ShapeDtypeStruct
    return pl.pallas_call(
        _in_proj_s_kernel,
        grid=(n // tm,),
        in_specs=[row(D_MODEL), _const_spec((1, D_MODEL)), _const_spec(w_n.shape), _const_spec(w_t.shape),
                  _const_spec((1, LANES))],
        out_specs=[row(ATT_W), row(2 * KV_W), row(2 * QK_M_W), row(MLSTM_W), row(MLSTM_W), row(LANES)],
        out_shape=[sds((n, ATT_W), BF16), sds((n, 2 * KV_W), F32), sds((n, 2 * QK_M_W), BF16),
                   sds((n, MLSTM_W), BF16), sds((n, MLSTM_W), F32), sds((n, LANES), F32)],
        compiler_params=pltpu.CompilerParams(dimension_semantics=("arbitrary",), vmem_limit_bytes=VMEM_LIMIT),
        name="in_proj_s",
    )(x2d, g_norm, w_n, w_t, bg_row)


def _t5_bucket_np(dist):
    max_exact = NUM_BUCKETS // 2
    d = np.maximum(dist, 0)
    ratio = np.maximum(d, 1).astype(np.float32) / np.float32(max_exact)
    large = max_exact + (np.log(ratio) / np.float32(math.log(MAX_DISTANCE / max_exact))
                         * np.float32(NUM_BUCKETS - max_exact)).astype(np.int32)
    large = np.minimum(large, NUM_BUCKETS - 1)
    return np.where(d < max_exact, d, large).astype(np.int32)


def _bias_table_kernel(rb_ref, bucket_ref, out_ref):
    bucket = bucket_ref[...]
    hit = [bucket == j for j in range(NUM_BUCKETS)]
    for h in range(N_HEADS_ATT):
        acc = jnp.full(bucket.shape, -jnp.inf, F32)
        for j in range(NUM_BUCKETS):
            acc = jnp.where(hit[j], rb_ref[j, h], acc)
        out_ref[h] = acc


def _bias_table(rel_bias, bucket):
    r, c = bucket.shape
    return pl.pallas_call(
        _bias_table_kernel,
        in_specs=[pl.BlockSpec(memory_space=pltpu.SMEM), pl.BlockSpec(memory_space=pltpu.VMEM)],
        out_specs=pl.BlockSpec(memory_space=pltpu.VMEM),
        out_shape=jax.ShapeDtypeStruct((N_HEADS_ATT, r, c), F32),
        name="bias_table",
    )(rel_bias, jnp.asarray(bucket))


def _prompt_buckets_t():
    qi = np.arange(WINDOW)[None, :]
    ci = np.arange(WINDOW)[:, None]
    upper = ci > qi
    general = _t5_bucket_np(np.where(upper, qi + WINDOW - ci, qi - ci))
    first = np.where(upper, -1, general)
    return np.concatenate([general, first], axis=0).astype(np.int32)


def _sample_buckets(t, w, padded):
    dist = (w + np.arange(t))[:, None] - np.arange(padded)[None, :]
    valid = (dist >= 0) & (dist <= WINDOW) & (np.arange(padded)[None, :] < w + t)
    return np.where(valid, _t5_bucket_np(dist), -1).astype(np.int32)


def _attn_block(i, qT, k_all, vT_all, first, bias_ref, sink_ref, gb_ref, write_out):
    W = WINDOW
    gd = lambda g: slice(g * HEAD_DIM_ATT, (g + 1) * HEAD_DIM_ATT)
    cols = slice(i * W, (i + 1) * W)
    is_first = first if i == 0 else False
    off = pl.multiple_of(jnp.where(first, W, 0), W) if i == 0 else 0
    k2 = k_all[i * W:(i + 2) * W, :]
    vT2 = vT_all[:, i * W:(i + 2) * W]
    ci = lax.broadcasted_iota(jnp.int32, (W, W), 0)
    qi = lax.broadcasted_iota(jnp.int32, (W, W), 1)
    upper = ci > qi
    eye = ci == qi
    scores = []
    for g in range(N_KV_HEADS):
        q_grp = jnp.concatenate([qT(slice((g * GQA_GROUP + j) * HEAD_DIM_ATT, (g * GQA_GROUP + j + 1) * HEAD_DIM_ATT),
                                    cols) for j in range(GQA_GROUP)], axis=1)
        scores.append(jnp.dot(k2[:, gd(g)], q_grp, preferred_element_type=F32))
    yield
    probs, rden, pfar = [], [], []
    for h in range(N_HEADS_ATT):
        g, j = divmod(h, GQA_GROUP)
        s_prev = scores[g][:W, j * W:(j + 1) * W]
        s_cur = scores[g][W:, j * W:(j + 1) * W]
        s = jnp.where(upper, s_prev, s_cur) + bias_ref[h, pl.ds(off, W), :]
        far_bias = jnp.where(is_first, -jnp.inf, sink_ref[1, h])
        s_far = jnp.sum(jnp.where(eye, s_prev, 0.0), axis=0, keepdims=True) + far_bias
        sink = sink_ref[0, h]
        m = jnp.maximum(jnp.maximum(jnp.max(s, axis=0, keepdims=True), s_far), sink)
        p = jnp.exp(s - m)
        p_far = jnp.exp(s_far - m)
        rden.append(1.0 / (jnp.sum(p, axis=0, keepdims=True) + p_far + jnp.exp(sink - m)))
        pfar.append(p_far)
        zero = jnp.zeros_like(p)
        probs.append(jnp.concatenate([jnp.where(upper, p, zero), jnp.where(upper, zero, p)], axis=0).astype(BF16))
    yield
    outs, ssq = [], jnp.zeros((1, W), F32)
    for g in range(N_KV_HEADS):
        p_grp = jnp.concatenate(probs[g * GQA_GROUP:(g + 1) * GQA_GROUP], axis=1)
        o_grp = jnp.dot(vT2[gd(g), :], p_grp, preferred_element_type=F32)
        v_far = vT2[gd(g), :W].astype(F32)
        for j in range(GQA_GROUP):
            h = g * GQA_GROUP + j
            o = (o_grp[:, j * W:(j + 1) * W] + v_far * pfar[h]) * rden[h]
            outs.append(o)
            ssq = ssq + jnp.sum(o * o, axis=0, keepdims=True)
    inv = lax.rsqrt(ssq * (1.0 / ATT_W) + EPS)
    for h in range(N_HEADS_ATT):
        rows = slice(h * HEAD_DIM_ATT, (h + 1) * HEAD_DIM_ATT)
        write_out(rows, cols, (outs[h] * inv * gb_ref[rows, :]).astype(BF16))
    yield


def _attn_blocks(qT, k_all, vT_all, first, bias_ref, sink_ref, gb_ref, write_out, qb):
    gens = [_attn_block(i, qT, k_all, vT_all, first, bias_ref, sink_ref, gb_ref, write_out) for i in range(qb)]
    next(gens[0])
    next(gens[0])
    for i in range(qb):
        if i + 1 < qb:
            next(gens[i + 1])
        yield
        next(gens[i])
        if i + 1 < qb:
            next(gens[i + 1])


PROJ_TOKEN_CHUNK = 256
PROJ_ROW_SPLIT = 1024


def _proj_attn_kernel(x_ref, g_ref, wn_ref, wt_ref, bgc_ref, bias_ref, sink_ref, gb_ref,
                      kvf_ref, km_ref, qmT_ref, vmT_ref, omT_ref, gT_ref, mix_ref,
                      q_s, k_s, vT_s, hk_s, hv_s, *, tm, tiles_per_seq):
    s = pl.program_id(0)
    slot = s % 2
    prev = 1 - slot
    qb = tm // WINDOW

    @pl.when(s == 0)
    def _():
        q_s[...] = jnp.zeros_like(q_s)
        k_s[...] = jnp.zeros_like(k_s)
        vT_s[...] = jnp.zeros_like(vT_s)
        hk_s[...] = jnp.zeros_like(hk_s)
        hv_s[...] = jnp.zeros_like(hv_s)

    first = (s + tiles_per_seq - 1) % tiles_per_seq == 0
    k_all = jnp.concatenate([hk_s[...], k_s[prev]], axis=0)
    vT_all = jnp.concatenate([hv_s[...], vT_s[prev]], axis=1)

    def write_out(rows, cols, val):
        mix_ref[rows, cols] = val

    blocks = _attn_blocks(lambda rows, cols: q_s[prev, rows, cols], k_all, vT_all, first, bias_ref, sink_ref, gb_ref,
                          write_out, qb)
    hb = _rms_rows(x_ref[...], g_ref[...]).astype(BF16)

    def put_q(lo, tc, z):
        q_s[slot, lo:lo + z.shape[0], tc] = (z * (HEAD_DIM_ATT ** -0.5)).astype(BF16)

    def put_v(lo, tc, z):
        vT_s[slot, lo:lo + z.shape[0], tc] = z.astype(BF16)

    def put_qm(lo, tc, z):
        qmT_ref[lo:lo + z.shape[0], tc] = z.astype(BF16)

    def put_vm(lo, tc, z):
        vmT_ref[lo:lo + z.shape[0], tc] = z.astype(BF16)

    def put_om(lo, tc, z):
        omT_ref[lo:lo + z.shape[0], tc] = jax.nn.sigmoid(z)

    def put_g(lo, tc, z):
        g_t = z[:N_GATES, :] + bgc_ref[...]
        row = lax.broadcasted_iota(jnp.int32, g_t.shape, 0)
        gT_ref[:, tc] = jnp.where(row < N_HEADS_MLSTM, g_t, _log_sigmoid(g_t))

    segments = [(_T_QA, _T_VA, put_q), (_T_VA, _T_QM, put_v), (_T_QM, _T_VM, put_qm), (_T_VM, _T_OM, put_vm),
                (_T_OM, _T_G, put_om), (_T_G, _T_ROWS, put_g)]

    def proj_piece(lo, hi, tc):
        z = lax.dot_general(wt_ref[lo:hi, :], hb[tc, :], (((1,), (1,)), ((), ())), preferred_element_type=F32)
        for a, b, put in segments:
            if a < hi and lo < b:
                put(max(lo, a) - a, tc, z[max(lo, a) - lo:min(hi, b) - lo, :])

    z_n = jnp.dot(hb, wn_ref[:, :_N_G], preferred_element_type=F32)
    kvf_ref[...] = z_n[:, _N_KV:_N_KV + 2 * KV_W]
    km_ref[...] = (z_n[:, _N_KM:_N_KM + QK_M_W] * (DK_MLSTM ** -0.5)).astype(BF16)
    k_s[slot] = z_n[:, _N_KV:_N_KV + KV_W].astype(BF16)
    pieces = [(lo, hi, slice(c, c + PROJ_TOKEN_CHUNK))
              for c in range(0, tm, PROJ_TOKEN_CHUNK) for lo, hi in ((0, PROJ_ROW_SPLIT), (PROJ_ROW_SPLIT, _T_ROWS))]
    for _ in blocks:
        if pieces:
            proj_piece(*pieces.pop(0))
    for piece in pieces:
        proj_piece(*piece)
    hk_s[...] = k_s[prev, tm - WINDOW:, :]
    hv_s[...] = vT_s[prev, :, tm - WINDOW:]


def _proj_attn(x2d, g_norm, w_n, w_t, bg_col, bias, sinks, g_b, seq, tm):
    n = x2d.shape[0]
    n_tiles = n // tm
    tiles_per_seq = seq // tm
    cur = lambda s: jnp.minimum(s, n_tiles - 1)
    prv = lambda s: jnp.maximum(s - 1, 0)
    row = lambda w: pl.BlockSpec((tm, w), lambda s: (cur(s), 0))
    colb = lambda r: pl.BlockSpec((r, tm), lambda s: (0, cur(s)))
    sds = jax.ShapeDtypeStruct
    return pl.pallas_call(
        functools.partial(_proj_attn_kernel, tm=tm, tiles_per_seq=tiles_per_seq),
        grid=(n_tiles + 1,),
        in_specs=[row(D_MODEL), _const_spec((1, D_MODEL)), _const_spec(w_n.shape), _const_spec(w_t.shape),
                  _const_spec((N_GATES, 1)), _const_spec(bias.shape), pl.BlockSpec(memory_space=pltpu.SMEM),
                  _const_spec(g_b.shape)],
        out_specs=[row(2 * KV_W), row(QK_M_W), colb(QK_M_W), colb(MLSTM_W), colb(MLSTM_W), colb(N_GATES),
                   pl.BlockSpec((ATT_W, tm), lambda s: (0, prv(s)))],
        out_shape=[sds((n, 2 * KV_W), F32), sds((n, QK_M_W), BF16), sds((QK_M_W, n), BF16), sds((MLSTM_W, n), BF16),
                   sds((MLSTM_W, n), F32), sds((N_GATES, n), F32), sds((ATT_W, n), BF16)],
        scratch_shapes=[pltpu.VMEM((2, ATT_W, tm), BF16), pltpu.VMEM((2, tm, KV_W), BF16),
                        pltpu.VMEM((2, KV_W, tm), BF16), pltpu.VMEM((WINDOW, KV_W), BF16),
                        pltpu.VMEM((KV_W, WINDOW), BF16)],
        compiler_params=pltpu.CompilerParams(dimension_semantics=("arbitrary",), vmem_limit_bytes=VMEM_LIMIT),
        name="proj_attn",
    )(x2d, g_norm, w_n, w_t, bg_col, bias, sinks, g_b)


def _attn_sample_b_kernel(q_ref, kvf_ref, ck_ref, cv_ref, bias_ref, sink_ref, g_ref, out_ref, kw_ref, vw_ref,
                          *, bb, t, w, kpad):
    rows = GQA_GROUP * t
    ck = ck_ref[...]
    cv = cv_ref[...]
    k_new = kvf_ref[:, :KV_W].reshape(bb, t, KV_W)
    v_new = kvf_ref[:, KV_W:].reshape(bb, t, KV_W)
    kw_ref[:, :w - t, :] = ck[:, t:, :]
    kw_ref[:, w - t:, :] = k_new
    vw_ref[:, :w - t, :] = cv[:, t:, :]
    vw_ref[:, w - t:, :] = v_new
    zpad = jnp.zeros((bb, kpad - w - t, KV_W), F32)
    kk = jnp.concatenate([ck, k_new, zpad], axis=1).astype(BF16)
    vv = jnp.concatenate([cv, v_new, zpad], axis=1).astype(BF16)
    q3 = q_ref[...].astype(F32).reshape(bb, t, ATT_W)
    outs = []
    ssq = jnp.zeros((bb, 1, t, 1), F32)
    for g in range(N_KV_HEADS):
        gd = slice(g * HEAD_DIM_ATT, (g + 1) * HEAD_DIM_ATT)
        qg = jnp.concatenate([q3[:, :, (g * GQA_GROUP + i) * HEAD_DIM_ATT:(g * GQA_GROUP + i + 1) * HEAD_DIM_ATT]
                              for i in range(GQA_GROUP)], axis=1).astype(BF16)
        s = jnp.einsum('bqd,bkd->bqk', qg, kk[:, :, gd], preferred_element_type=F32) + bias_ref[g]
        sink = sink_ref[g]
        m = jnp.maximum(jnp.max(s, axis=-1, keepdims=True), sink)
        p = jnp.exp(s - m)
        denom = jnp.sum(p, axis=-1, keepdims=True) + jnp.exp(sink - m)
        p = (p * (1.0 / denom)).astype(BF16)
        o = jnp.einsum('bqk,bkd->bqd', p, vv[:, :, gd], preferred_element_type=F32)
        o = o.reshape(bb, GQA_GROUP, t, HEAD_DIM_ATT)
        outs.append(o)
        ssq = ssq + jnp.sum(jnp.sum(o * o, axis=-1, keepdims=True), axis=1, keepdims=True)
    inv = lax.rsqrt(ssq * (1.0 / ATT_W) + EPS)
    for g in range(N_KV_HEADS):
        y = outs[g] * inv
        for i in range(GQA_GROUP):
            sl = slice((g * GQA_GROUP + i) * HEAD_DIM_ATT, (g * GQA_GROUP + i + 1) * HEAD_DIM_ATT)
            out_ref[:, sl] = (y[:, i].reshape(bb * t, HEAD_DIM_ATT) * g_ref[:, sl]).astype(BF16)


def _attn_sample_b(qa, kvf, ck, cv, bias, sink_rows, g_att, nseq, t, bb):
    w = ck.shape[1]
    kpad = bias.shape[-1]
    rows = GQA_GROUP * t
    kern = functools.partial(_attn_sample_b_kernel, bb=bb, t=t, w=w, kpad=kpad)
    tok = lambda wd: pl.BlockSpec((bb * t, wd), lambda i: (i, 0))
    cache = pl.BlockSpec((bb, w, KV_W), lambda i: (i, 0, 0))
    return pl.pallas_call(
        kern,
        grid=(nseq // bb,),
        in_specs=[tok(ATT_W), tok(2 * KV_W), cache, cache,
                  _const_spec((N_KV_HEADS, rows, kpad)), _const_spec((N_KV_HEADS, rows, 1)), _const_spec((1, ATT_W))],
        out_specs=[tok(ATT_W), cache, cache],
        out_shape=[jax.ShapeDtypeStruct((nseq * t, ATT_W), BF16),
                   jax.ShapeDtypeStruct((nseq, w, KV_W), F32), jax.ShapeDtypeStruct((nseq, w, KV_W), F32)],
        compiler_params=pltpu.CompilerParams(dimension_semantics=("arbitrary",), vmem_limit_bytes=VMEM_LIMIT),
        name="attn_sample_b",
    )(qa, kvf, ck, cv, bias, sink_rows, g_att)


SEQ_PER_BLOCK = 16


def _seg_scan_rows(x, seg, forward):
    n = x.shape[0]
    pos = lax.broadcasted_iota(jnp.int32, x.shape, 0) % seg
    k = 1
    while k < seg:
        if forward:
            x = jnp.maximum(x, jnp.where(pos >= k, pltpu.roll(x, k, axis=0), -jnp.inf))
        else:
            x = jnp.maximum(x, jnp.where(pos < seg - k, pltpu.roll(x, n - k, axis=0), -jnp.inf))
        k *= 2
    return x


def _mlstm_blk_kernel(qk_ref, v_ref, om_ref, gt_ref, gm_ref, c0_ref, n0_ref, nrow_ref, m0_ref,
                      mix_ref, c_out, n_out, m_out, *, t):
    S = SEQ_PER_BLOCK
    L = S * t
    hi = lax.Precision.HIGHEST
    ti = lax.broadcasted_iota(jnp.int32, (L, L), 0)
    si = lax.broadcasted_iota(jnp.int32, (L, L), 1)
    same = (ti // t) == (si // t)
    causal = same & (si <= ti)
    tril_seg = causal.astype(F32)
    ones_seg = same.astype(F32)
    ones_col = (lax.broadcasted_iota(jnp.int32, (L, DV_MLSTM), 1) == 0).astype(BF16)
    bd_mask = (lax.broadcasted_iota(jnp.int32, (L, S * DK_MLSTM), 0) // t
               == lax.broadcasted_iota(jnp.int32, (L, S * DK_MLSTM), 1) // DK_MLSTM)
    bd_mask_t = (lax.broadcasted_iota(jnp.int32, (S * DK_MLSTM, L), 0) // DK_MLSTM
                 == lax.broadcasted_iota(jnp.int32, (S * DK_MLSTM, L), 1) // t)
    heads = range(N_HEADS_MLSTM)
    hk = lambda h: slice(h * DK_MLSTM, (h + 1) * DK_MLSTM)
    hkk = lambda h: slice(QK_M_W + h * DK_MLSTM, QK_M_W + (h + 1) * DK_MLSTM)
    hv = lambda h: slice(h * DV_MLSTM, (h + 1) * DV_MLSTM)

    gt = gt_ref[...]
    m_prev = m0_ref[...]
    b = jnp.dot(tril_seg, gt, precision=hi, preferred_element_type=F32)
    b_end = jnp.dot(ones_seg, gt, precision=hi, preferred_element_type=F32)
    a = pltpu.roll(gt, N_HEADS_MLSTM, axis=1) - b
    big_m = jnp.maximum(m_prev, _seg_scan_rows(a, t, True))
    inter = jnp.exp(m_prev - big_m)
    em = jnp.exp(-(b + big_m))
    m_end = _seg_scan_rows(big_m, t, False)
    w = jnp.exp(a - m_end)
    decay = jnp.exp(m_prev - m_end)
    m_out[...] = b_end + m_end
    a_t = a.T
    q = [qk_ref[:, hk(h)] for h in heads]
    k = [qk_ref[:, hkk(h)] for h in heads]
    v_ext = [jnp.concatenate([v_ref[:, hv(h)], ones_col], axis=1) for h in heads]
    qk = [lax.dot_general(q[h], k[h], (((1,), (1,)), ((), ())), preferred_element_type=F32) for h in heads]
    c_all = [c0_ref[:, h].reshape(S * DK_MLSTM, DV_MLSTM) for h in heads]
    q_bd = []
    for h in heads:
        q2 = jnp.concatenate([q[h], q[h]], axis=1)
        q_bd.append(jnp.where(bd_mask, jnp.concatenate([q2] * (S // 2), axis=1), jnp.zeros((), BF16)))
    qc = [jnp.dot(q_bd[h], c_all[h].astype(BF16), preferred_element_type=F32) for h in heads]
    s = []
    for h in heads:
        ln = _GATE0 + h
        d = jnp.exp(jnp.where(causal, a_t[ln:ln + 1, :] - big_m[:, ln:ln + 1], -jnp.inf))
        s.append((qk[h] * d).astype(BF16))
    sv = [jnp.dot(s[h], v_ext[h], preferred_element_type=F32) for h in heads]
    kw, kw_bd = [], []
    for h in heads:
        ln = _GATE0 + h
        kw_f = k[h].astype(F32) * w[:, ln:ln + 1]
        kw.append(kw_f.astype(BF16))
        kw_bd.append(jnp.where(bd_mask_t, jnp.concatenate([kw_f.T] * S, axis=0), 0.0).astype(BF16))
    upd = [jnp.dot(kw_bd[h], v_ref[:, hv(h)], preferred_element_type=F32) for h in heads]
    seq_of_tok = lax.broadcasted_iota(jnp.int32, (S, L), 1) // t == lax.broadcasted_iota(jnp.int32, (S, L), 0)
    first_tok = lax.broadcasted_iota(jnp.int32, (S, L), 1) == lax.broadcasted_iota(jnp.int32, (S, L), 0) * t
    n_upd = [jnp.dot(seq_of_tok.astype(BF16), kw[h], preferred_element_type=F32) for h in heads]
    decay_seq = jnp.dot(first_tok.astype(F32), decay, precision=hi, preferred_element_type=F32)
    for h in heads:
        ln = _GATE0 + h
        inter_h = inter[:, ln:ln + 1]
        qn = jnp.sum(q[h].astype(F32) * nrow_ref[:, hk(h)], axis=1, keepdims=True)
        den = inter_h * qn + sv[h][:, DV_MLSTM:DV_MLSTM + 1]
        r = 1.0 / jnp.maximum(jnp.abs(den), em[:, ln:ln + 1])
        tt = om_ref[:, hv(h)] * (inter_h * qc[h] + sv[h][:, :DV_MLSTM])
        msq = jnp.mean(tt * tt, axis=1, keepdims=True)
        scale = r * lax.rsqrt(r * r * msq + EPS)
        mix_ref[:, hv(h)] = (tt * scale * gm_ref[:, hv(h)]).astype(BF16)
        dec = decay[:, ln:ln + 1].reshape(S, t, 1)[:, 0:1, :]
        c_out[:, h] = dec * c0_ref[:, h] + upd[h].reshape(S, DK_MLSTM, DV_MLSTM)
        n_out[:, hk(h)] = decay_seq[:, ln:ln + 1] * n0_ref[:, hk(h)] + n_upd[h]


def _mlstm_blk(qkm, vm, om, gt, g_m, c0, n0_seq, n0_rows, m0_rows, nseq, t):
    S = SEQ_PER_BLOCK
    L = S * t
    tok = lambda wd: pl.BlockSpec((L, wd), lambda i: (i, 0))
    st_c = pl.BlockSpec((S, N_HEADS_MLSTM, DK_MLSTM, DV_MLSTM), lambda i: (i, 0, 0, 0))
    st_n = pl.BlockSpec((S, QK_M_W), lambda i: (i, 0))
    sds = jax.ShapeDtypeStruct
    return pl.pallas_call(
        functools.partial(_mlstm_blk_kernel, t=t),
        grid=(nseq // S,),
        in_specs=[tok(2 * QK_M_W), tok(MLSTM_W), tok(MLSTM_W), tok(LANES), _const_spec((1, MLSTM_W)),
                  st_c, st_n, tok(QK_M_W), tok(LANES)],
        out_specs=[tok(MLSTM_W), st_c, st_n, tok(LANES)],
        out_shape=[sds((nseq * t, MLSTM_W), BF16), sds((nseq, N_HEADS_MLSTM, DK_MLSTM, DV_MLSTM), F32),
                   sds((nseq, QK_M_W), F32), sds((nseq * t, LANES), F32)],
        compiler_params=pltpu.CompilerParams(dimension_semantics=("arbitrary",), vmem_limit_bytes=VMEM_LIMIT),
        name="mlstm_blk",
    )(qkm, vm, om, gt, g_m, c0, n0_seq, n0_rows, m0_rows)


def _out_mlp_kernel(x_ref, a_ref, hm_ref, woa_ref, wom_ref, gmlp_ref, wup_ref, wdn_ref, gfin_ref, y_ref):
    x1 = (x_ref[...]
          + jnp.dot(a_ref[...], woa_ref[...], preferred_element_type=F32)
          + jnp.dot(hm_ref[...], wom_ref[...], preferred_element_type=F32))
    h = _rms_rows(x1, gmlp_ref[...]).astype(BF16)
    u = jnp.maximum(jnp.dot(h, wup_ref[...], preferred_element_type=F32), 0.0)
    y = x1 + jnp.dot((u * u).astype(BF16), wdn_ref[...], preferred_element_type=F32)
    y_ref[...] = _rms_rows(y, gfin_ref[...])


def _out_mlp(x2d, mix_a, mix_m, wo_a, wo_m, g_mlp, w_up, w_dn, g_fin, tm):
    n = x2d.shape[0]
    row = lambda w: pl.BlockSpec((tm, w), lambda i: (i, 0))
    return pl.pallas_call(
        _out_mlp_kernel,
        grid=(n // tm,),
        in_specs=[row(D_MODEL), row(ATT_W), row(MLSTM_W), _const_spec(wo_a.shape), _const_spec(wo_m.shape),
                  _const_spec((1, D_MODEL)), _const_spec(w_up.shape), _const_spec(w_dn.shape),
                  _const_spec((1, D_MODEL))],
        out_specs=row(D_MODEL),
        out_shape=jax.ShapeDtypeStruct((n, D_MODEL), F32),
        compiler_params=pltpu.CompilerParams(dimension_semantics=("arbitrary",), vmem_limit_bytes=VMEM_LIMIT),
        name="out_mlp",
    )(x2d, mix_a, mix_m, wo_a, wo_m, g_mlp, w_up, w_dn, g_fin)


_CN_ROWS = DV_MLSTM + BF16_ROWS


def _prefix_max_lanes(x):
    n = x.shape[1]
    col = lax.broadcasted_iota(jnp.int32, x.shape, 1)
    k = 1
    while k < n:
        x = jnp.maximum(x, jnp.where(col >= k, pltpu.roll(x, k, axis=1), -jnp.inf))
        k *= 2
    return x


def _mlstm_stages(qT_ref, k_ref, vT_ref, omT_ref, gT_ref, gb_ref, write_mix, state, chunk, nsub):
    L = chunk
    i0 = lax.broadcasted_iota(jnp.int32, (L, L), 0)
    i1 = lax.broadcasted_iota(jnp.int32, (L, L), 1)
    causal_t = i0 <= i1
    triu = causal_t.astype(F32)
    ones_rows = (lax.broadcasted_iota(jnp.int32, (BF16_ROWS, L), 0) == 0).astype(BF16)
    zpad = jnp.zeros((LANES - 2 * N_GATES, L), F32)
    cn, m_r = state['cn'], state['m_r']
    hi = lax.Precision.HIGHEST
    heads = range(N_HEADS_MLSTM)
    subs = range(nsub)
    col_of = lambda j: slice(j * L, (j + 1) * L)
    hv = lambda h: slice(h * DV_MLSTM, (h + 1) * DV_MLSTM)
    hk = lambda h: slice(h * DK_MLSTM, (h + 1) * DK_MLSTM)
    a_r, b_r, pm_r = [], [], []
    for j in subs:
        g_t = gT_ref[:, col_of(j)]
        b_r.append(jnp.dot(g_t, triu, precision=hi, preferred_element_type=F32))
        a_r.append(pltpu.roll(g_t, N_HEADS_MLSTM, axis=0) - b_r[j])
    yield
    kq = [[jnp.dot(k_ref[col_of(j), hk(h)], qT_ref[hk(h), col_of(j)], preferred_element_type=F32)
           for h in heads] for j in subs]
    vT_ext = [[jnp.concatenate([vT_ref[hv(h), col_of(j)], ones_rows], axis=0) for h in heads] for j in subs]
    for j in subs:
        pm_r.append(_prefix_max_lanes(a_r[j]))
    yield
    big_m, inter, em, decay, cols_t = [], [], [], [], []
    for j in subs:
        big_m.append(jnp.maximum(m_r, pm_r[j]))
        inter.append(jnp.exp(m_r - big_m[j]))
        em.append(jnp.exp(-(b_r[j] + big_m[j])))
        m_end = big_m[j][:, L - 1:L]
        w_r = jnp.exp(a_r[j] - m_end)
        decay.append(jnp.exp(m_r[:, 0:1] - m_end))
        m_r = jnp.broadcast_to(b_r[j][:, L - 1:L] + m_end, (N_GATES, L))
        cols_t.append(jnp.concatenate([a_r[j], w_r, zpad], axis=0).T)
    sT = [[None] * N_HEADS_MLSTM for _ in subs]
    kw = [[None] * N_HEADS_MLSTM for _ in subs]
    intra = [[None] * N_HEADS_MLSTM for _ in subs]
    upd = [[None] * N_HEADS_MLSTM for _ in subs]
    for j in subs:
        for h in heads:
            r = _GATE0 + h
            d = jnp.exp(jnp.where(causal_t, cols_t[j][:, r:r + 1] - big_m[j][r:r + 1, :], -jnp.inf))
            sT[j][h] = (kq[j][h] * d).astype(BF16)
            w_col = cols_t[j][:, N_GATES + r:N_GATES + r + 1]
            kw[j][h] = (k_ref[col_of(j), hk(h)].astype(F32) * w_col).astype(BF16)
        for h in heads:
            intra[j][h] = jnp.dot(vT_ext[j][h], sT[j][h], preferred_element_type=F32)
            upd[j][h] = jnp.dot(vT_ext[j][h], kw[j][h], preferred_element_type=F32)
        yield
    for j in subs:
        qc = [jnp.dot(cn[h].astype(BF16), qT_ref[hk(h), col_of(j)], preferred_element_type=F32) for h in heads]
        for h in heads:
            r = _GATE0 + h
            nd = inter[j][r:r + 1, :] * qc[h] + intra[j][h]
            rr = 1.0 / jnp.maximum(jnp.abs(nd[DV_MLSTM:DV_MLSTM + 1, :]), em[j][r:r + 1, :])
            t = omT_ref[hv(h), col_of(j)] * nd[:DV_MLSTM, :]
            msq = jnp.mean(t * t, axis=0, keepdims=True)
            scale = rr * lax.rsqrt(rr * rr * msq + EPS)
            write_mix(hv(h), col_of(j), (t * scale * gb_ref[hv(h), :]).astype(BF16))
            cn[h] = decay[j][r:r + 1, :] * cn[h] + upd[j][h]
        yield
    state['m_r'] = m_r


def _mlstm_mlp_kernel(x_ref, a_ref, qT_ref, k_ref, vT_ref, omT_ref, gT_ref, gb_ref,
                      woa_ref, wom_ref, gmlp_ref, wup_ref, wdn_ref, gfin_ref,
                      y_ref, c_out, n_out, m_out, mix_s, cn_s, mr_s,
                      *, chunk, nsub, tiles_per_seq, n_tiles):
    s = pl.program_id(0)
    slot = s % 2

    @pl.when(s == 0)
    def _():
        mix_s[...] = jnp.zeros_like(mix_s)

    @pl.when(s % tiles_per_seq == 0)
    def _():
        cn_s[...] = jnp.zeros_like(cn_s)
        mr_s[...] = jnp.zeros_like(mr_s)

    state = {'cn': [cn_s[h] for h in range(N_HEADS_MLSTM)], 'm_r': mr_s[...]}

    def write_mix(rows, cols, val):
        mix_s[slot, rows, cols] = val

    stages = _mlstm_stages(qT_ref, k_ref, vT_ref, omT_ref, gT_ref, gb_ref, write_mix, state, chunk, nsub)
    next(stages)
    tdims = (((0,), (0,)), ((), ()))
    x1 = (x_ref[...]
          + lax.dot_general(a_ref[...], woa_ref[...], tdims, preferred_element_type=F32)
          + lax.dot_general(mix_s[1 - slot], wom_ref[...], tdims, preferred_element_type=F32))
    next(stages)
    h = _rms_rows(x1, gmlp_ref[...]).astype(BF16)
    u = jnp.maximum(jnp.dot(h, wup_ref[...], preferred_element_type=F32), 0.0)
    for _ in stages:
        pass
    y = x1 + jnp.dot((u * u).astype(BF16), wdn_ref[...], preferred_element_type=F32)
    y_ref[...] = _rms_rows(y, gfin_ref[...])
    for hh in range(N_HEADS_MLSTM):
        cn_s[hh] = state['cn'][hh]
    mr_s[...] = state['m_r']

    @pl.when((s % tiles_per_seq == tiles_per_seq - 1) & (s < n_tiles))
    def _():
        for hh in range(N_HEADS_MLSTM):
            c_out[0, hh] = cn_s[hh, :DV_MLSTM, :].T
            n_out[0, hh] = cn_s[hh, DV_MLSTM:DV_MLSTM + 1, :]
        m_out[0] = mr_s[...]


def _mlstm_mlp(x2d, mix_aT, qmT, km, vmT, omT, gT, g_b, wo_a, wo_m, g_mlp, w_up, w_dn, g_fin,
               nseq, t, chunk, nsub):
    tm = chunk * nsub
    n_tiles = (nseq * t) // tm
    tiles_per_seq = t // tm
    cur = lambda s: jnp.minimum(s, n_tiles - 1)
    prv = lambda s: jnp.maximum(s - 1, 0)
    colb = lambda r: pl.BlockSpec((r, tm), lambda s: (0, cur(s)))
    st = lambda a, d: pl.BlockSpec((1, N_HEADS_MLSTM, a, d), lambda s: (cur(s) // tiles_per_seq, 0, 0, 0))
    sds = jax.ShapeDtypeStruct
    kern = functools.partial(_mlstm_mlp_kernel, chunk=chunk, nsub=nsub,
                             tiles_per_seq=tiles_per_seq, n_tiles=n_tiles)
    return pl.pallas_call(
        kern,
        grid=(n_tiles + 1,),
        in_specs=[pl.BlockSpec((tm, D_MODEL), lambda s: (prv(s), 0)),
                  pl.BlockSpec((ATT_W, tm), lambda s: (0, prv(s))),
                  colb(QK_M_W), pl.BlockSpec((tm, QK_M_W), lambda s: (cur(s), 0)), colb(MLSTM_W), colb(MLSTM_W),
                  colb(N_GATES),
                  _const_spec(g_b.shape), _const_spec(wo_a.shape), _const_spec(wo_m.shape),
                  _const_spec((1, D_MODEL)), _const_spec(w_up.shape), _const_spec(w_dn.shape),
                  _const_spec((1, D_MODEL))],
        out_specs=[pl.BlockSpec((tm, D_MODEL), lambda s: (prv(s), 0)),
                   st(DK_MLSTM, DV_MLSTM), st(1, DK_MLSTM),
                   pl.BlockSpec((1, N_GATES, chunk), lambda s: (cur(s) // tiles_per_seq, 0, 0))],
        out_shape=[sds((nseq * t, D_MODEL), F32), sds((nseq, N_HEADS_MLSTM, DK_MLSTM, DV_MLSTM), F32),
                   sds((nseq, N_HEADS_MLSTM, 1, DK_MLSTM), F32), sds((nseq, N_GATES, chunk), F32)],
        scratch_shapes=[pltpu.VMEM((2, MLSTM_W, tm), BF16),
                        pltpu.VMEM((N_HEADS_MLSTM, _CN_ROWS, DK_MLSTM), F32),
                        pltpu.VMEM((N_GATES, chunk), F32)],
        compiler_params=pltpu.CompilerParams(dimension_semantics=("arbitrary",), vmem_limit_bytes=VMEM_LIMIT),
        name="mlstm_mlp",
    )(x2d, mix_aT, qmT, km, vmT, omT, gT, g_b, wo_a, wo_m, g_mlp, w_up, w_dn, g_fin)


def kernel(x_prompt, x_sample, cache_k_win, cache_v_win, state_C, state_n, state_m, rel_bias, norm_mix, w_in,
           b_gates, attn_sinks, norm_attn_out, norm_mlstm_out, w_out, norm_mlp, w_up, w_down, norm_final):
    depth = w_in.shape[0]
    assert depth == 1, "single-layer trunk"
    bp, sp = x_prompt.shape[:2]
    bs, ts = x_sample.shape[:2]
    wc = cache_k_win.shape[2]
    l = 0
    chunk = 256

    wl = w_in[l]
    cols = lambda lo, n: wl[:, lo:lo + n]
    w_n = jnp.concatenate([cols(_O_KV, 2 * KV_W), cols(_O_QKM + QK_M_W, QK_M_W),
                           jnp.pad(cols(_O_G, N_GATES), ((0, 0), (0, LANES - N_GATES)))], axis=1).astype(BF16)
    w_t = jnp.concatenate([cols(_O_QA, ATT_W), cols(_O_KV + KV_W, KV_W), cols(_O_QKM, QK_M_W), cols(_O_VM, MLSTM_W),
                           cols(_O_OM, MLSTM_W), jnp.pad(cols(_O_G, N_GATES), ((0, 0), (0, BF16_ROWS - N_GATES)))],
                          axis=1).T.astype(BF16)
    bg_row = jnp.pad(b_gates[l], (0, LANES - N_GATES)).reshape(1, LANES)
    bg_col = b_gates[l].reshape(N_GATES, 1)
    wo_a = w_out[l, :ATT_W].astype(BF16)
    wo_m = w_out[l, ATT_W:].astype(BF16)
    wup = w_up[l].astype(BF16)
    wdn = w_down[l].astype(BF16)
    g_mix = norm_mix[l].reshape(1, D_MODEL)
    g_att = norm_attn_out[l].reshape(1, ATT_W)
    g_m = norm_mlstm_out[l].reshape(1, MLSTM_W)
    g_mlp = norm_mlp[l].reshape(1, D_MODEL)
    g_fin = norm_final.reshape(1, D_MODEL)
    gb_att = jnp.broadcast_to(norm_attn_out[l].reshape(ATT_W, 1), (ATT_W, WINDOW))
    gb_m = jnp.broadcast_to(norm_mlstm_out[l].reshape(MLSTM_W, 1), (MLSTM_W, chunk))
    sinks = attn_sinks[l]

    xp = x_prompt.reshape(bp * sp, D_MODEL)
    bias_t = _bias_table(rel_bias, _prompt_buckets_t())
    far_bucket = int(_t5_bucket_np(np.array(WINDOW)))
    sink_far = jnp.stack([sinks, rel_bias[far_bucket]])
    kvf, km, qmT, vmT, omT, gT, mix_a = _proj_attn(xp, g_mix, w_n, w_t, bg_col, bias_t, sink_far, gb_att, sp, tm=1024)
    y_p, c_p, n_p, m_p = _mlstm_mlp(xp, mix_a, qmT, km, vmT, omT, gT, gb_m, wo_a, wo_m, g_mlp, wup, wdn, g_fin,
                                    bp, sp, chunk=chunk, nsub=512 // chunk)
    m_p = m_p[:, _GATE0:_GATE0 + N_HEADS_MLSTM, 0]
    win_p = min(WINDOW, sp)
    kv_win = kvf.reshape(bp, sp, 2 * KV_W)[:, sp - win_p:, :]
    k_win_p = kv_win[:, :, :KV_W].reshape(1, bp, win_p, N_KV_HEADS, HEAD_DIM_ATT)
    v_win_p = kv_win[:, :, KV_W:].reshape(1, bp, win_p, N_KV_HEADS, HEAD_DIM_ATT)

    assert math.gcd(ts, 64) == ts and bs % SEQ_PER_BLOCK == 0, "sample group: one mLSTM chunk per sequence"
    kpad = ((wc + ts + BF16_ROWS - 1) // BF16_ROWS) * BF16_ROWS
    bias_s = _bias_table(rel_bias, _sample_buckets(ts, wc, kpad))
    bias_s = bias_s.reshape(N_KV_HEADS, GQA_GROUP * ts, kpad)
    sink_rows = jnp.repeat(sinks, ts).reshape(N_KV_HEADS, GQA_GROUP * ts, 1)
    xs = x_sample.reshape(bs * ts, D_MODEL)
    qa, kvf_s, qkm, vm, om, gt = _in_proj_s(xs, g_mix, w_n, w_t, bg_row, tm=512)
    ck = cache_k_win[l].reshape(bs, wc, KV_W)
    cv = cache_v_win[l].reshape(bs, wc, KV_W)
    mix_a_s, kw_s, vw_s = _attn_sample_b(qa, kvf_s, ck, cv, bias_s, sink_rows, g_att, bs, ts, bb=16)
    n0 = state_n[l].reshape(bs, QK_M_W)
    n0_rows = jnp.repeat(n0, ts, axis=0)
    m0_rows = jnp.pad(jnp.repeat(state_m[l], ts, axis=0),
                      ((0, 0), (_GATE0, LANES - _GATE0 - N_HEADS_MLSTM)))
    mix_m_s, c_s, n_s, m_rows = _mlstm_blk(qkm, vm, om, gt, g_m, state_C[l], n0, n0_rows, m0_rows, bs, ts)
    m_s = m_rows[::ts, _GATE0:_GATE0 + N_HEADS_MLSTM]
    y_s = _out_mlp(xs, mix_a_s, mix_m_s, wo_a, wo_m, g_mlp, wup, wdn, g_fin, tm=512)

    return (y_p.reshape(bp, sp, D_MODEL), y_s.reshape(bs, ts, D_MODEL),
            k_win_p, v_win_p,
            c_p[None], n_p.reshape(1, bp, N_HEADS_MLSTM, DK_MLSTM), m_p.reshape(1, bp, N_HEADS_MLSTM),
            kw_s.reshape(1, bs, wc, N_KV_HEADS, HEAD_DIM_ATT), vw_s.reshape(1, bs, wc, N_KV_HEADS, HEAD_DIM_ATT),
            c_s[None], n_s.reshape(1, bs, N_HEADS_MLSTM, DK_MLSTM), m_s.reshape(1, bs, N_HEADS_MLSTM))
```

```python
import functools
import itertools
import math

import numpy as np
import jax
import jax.numpy as jnp
from jax import lax
from jax.experimental import pallas as pl
from jax.experimental.pallas import tpu as pltpu

D_MODEL = 1024
N_HEADS_ATT = 8
N_KV_HEADS = 2
GQA_GROUP = N_HEADS_ATT // N_KV_HEADS
HEAD_DIM_ATT = 64
ATT_W = N_HEADS_ATT * HEAD_DIM_ATT
KV_W = N_KV_HEADS * HEAD_DIM_ATT
WINDOW = 128
NUM_BUCKETS = 32
MAX_DISTANCE = 128
N_HEADS_MLSTM = 4
DV_MLSTM = 128
DK_MLSTM = 64
MLSTM_W = N_HEADS_MLSTM * DV_MLSTM
QK_M_W = N_HEADS_MLSTM * DK_MLSTM
D_FF = 4 * D_MODEL
EPS = 1e-6
N_GATES = 2 * N_HEADS_MLSTM
LANES = 128
BF16_ROWS = 16
VMEM_LIMIT = 56 * 1024 * 1024

F32 = jnp.float32
BF16 = jnp.bfloat16

_O_QA = 0
_O_KV = _O_QA + ATT_W
_O_QKM = _O_KV + 2 * KV_W
_O_VM = _O_QKM + 2 * QK_M_W
_O_OM = _O_VM + MLSTM_W
_O_G = _O_OM + MLSTM_W
_T_QA = 0
_T_VA = _T_QA + ATT_W
_T_QM = _T_VA + KV_W
_T_VM = _T_QM + QK_M_W
_T_OM = _T_VM + MLSTM_W
_T_G = _T_OM + MLSTM_W
_T_ROWS = _T_G + BF16_ROWS
_N_KV = 0
_N_KM = _N_KV + 2 * KV_W
_N_G = _N_KM + QK_M_W
_N_COLS = _N_G + LANES
_GATE0 = N_HEADS_MLSTM


def _const_spec(shape):
    nd = len(shape)
    return pl.BlockSpec(shape, lambda *_: (0,) * nd, pipeline_mode=pl.Buffered(1))


def _log_sigmoid(g):
    return jnp.minimum(g, 0.0) - jnp.log1p(jnp.exp(-jnp.abs(g)))


def _rms_rows(x, gain):
    return x * lax.rsqrt(jnp.mean(x * x, axis=-1, keepdims=True) + EPS) * gain


def _in_proj_s_kernel(x_ref, g_ref, wn_ref, wt_ref, bgr_ref, qa_ref, kvf_ref, qkm_ref, vm_ref, om_ref, gt_ref):
    hb = _rms_rows(x_ref[...], g_ref[...]).astype(BF16)
    z_n = jnp.dot(hb, wn_ref[...], preferred_element_type=F32)
    z_t = lax.dot_general(hb, wt_ref[...], (((1,), (1,)), ((), ())), preferred_element_type=F32)
    qa_ref[...] = (z_t[:, _T_QA:_T_QA + ATT_W] * (HEAD_DIM_ATT ** -0.5)).astype(BF16)
    kvf_ref[:, :KV_W] = z_n[:, _N_KV:_N_KV + KV_W]
    kvf_ref[:, KV_W:] = z_t[:, _T_VA:_T_VA + KV_W]
    qkm_ref[:, :QK_M_W] = z_t[:, _T_QM:_T_QM + QK_M_W].astype(BF16)
    qkm_ref[:, QK_M_W:] = (z_n[:, _N_KM:_N_KM + QK_M_W] * (DK_MLSTM ** -0.5)).astype(BF16)
    vm_ref[...] = z_t[:, _T_VM:_T_VM + MLSTM_W].astype(BF16)
    om_ref[...] = jax.nn.sigmoid(z_t[:, _T_OM:_T_OM + MLSTM_W])
    g = z_n[:, _N_G:_N_G + LANES] + bgr_ref[...]
    col = lax.broadcasted_iota(jnp.int32, g.shape, 1)
    gt_ref[...] = jnp.where(col < N_HEADS_MLSTM, g, _log_sigmoid(g))


def _in_proj_s(x2d, g_norm, w_n, w_t, bg_row, tm):
    n = x2d.shape[0]
    row = lambda w: pl.BlockSpec((tm, w), lambda i: (i, 0))
    sds = jax.ShapeDtypeStruct
    return pl.pallas_call(
        _in_proj_s_kernel,
        grid=(n // tm,),
        in_specs=[row(D_MODEL), _const_spec((1, D_MODEL)), _const_spec(w_n.shape), _const_spec(w_t.shape),
                  _const_spec((1, LANES))],
        out_specs=[row(ATT_W), row(2 * KV_W), row(2 * QK_M_W), row(MLSTM_W), row(MLSTM_W), row(LANES)],
        out_shape=[sds((n, ATT_W), BF16), sds((n, 2 * KV_W), F32), sds((n, 2 * QK_M_W), BF16),
                   sds((n, MLSTM_W), BF16), sds((n, MLSTM_W), F32), sds((n, LANES), F32)],
        compiler_params=pltpu.CompilerParams(dimension_semantics=("arbitrary",), vmem_limit_bytes=VMEM_LIMIT),
        name="in_proj_s",
    )(x2d, g_norm, w_n, w_t, bg_row)


def _t5_bucket_np(dist):
    max_exact = NUM_BUCKETS // 2
    d = np.maximum(dist, 0)
    ratio = np.maximum(d, 1).astype(np.float32) / np.float32(max_exact)
    large = max_exact + (np.log(ratio) / np.float32(math.log(MAX_DISTANCE / max_exact))
                         * np.float32(NUM_BUCKETS - max_exact)).astype(np.int32)
    large = np.minimum(large, NUM_BUCKETS - 1)
    return np.where(d < max_exact, d, large).astype(np.int32)


def _bias_table_kernel(rb_ref, bucket_ref, out_ref):
    bucket = bucket_ref[...]
    hit = [bucket == j for j in range(NUM_BUCKETS)]
    for h in range(N_HEADS_ATT):
        acc = jnp.full(bucket.shape, -jnp.inf, F32)
        for j in range(NUM_BUCKETS):
            acc = jnp.where(hit[j], rb_ref[j, h], acc)
        out_ref[h] = acc


def _bias_table(rel_bias, bucket):
    r, c = bucket.shape
    return pl.pallas_call(
        _bias_table_kernel,
        in_specs=[pl.BlockSpec(memory_space=pltpu.SMEM), pl.BlockSpec(memory_space=pltpu.VMEM)],
        out_specs=pl.BlockSpec(memory_space=pltpu.VMEM),
        out_shape=jax.ShapeDtypeStruct((N_HEADS_ATT, r, c), F32),
        name="bias_table",
    )(rel_bias, jnp.asarray(bucket))


def _prompt_buckets_t():
    qi = np.arange(WINDOW)[None, :]
    ci = np.arange(WINDOW)[:, None]
    upper = ci > qi
    general = _t5_bucket_np(np.where(upper, qi + WINDOW - ci, qi - ci))
    first = np.where(upper, -1, general)
    return np.concatenate([general, first], axis=0).astype(np.int32)


def _sample_buckets(t, w, padded):
    dist = (w + np.arange(t))[:, None] - np.arange(padded)[None, :]
    valid = (dist >= 0) & (dist <= WINDOW) & (np.arange(padded)[None, :] < w + t)
    return np.where(valid, _t5_bucket_np(dist), -1).astype(np.int32)


def _attn_block(i, qT, k_all, vT_all, first, bias_ref, sink_ref, gb_ref, write_out):
    W = WINDOW
    gd = lambda g: slice(g * HEAD_DIM_ATT, (g + 1) * HEAD_DIM_ATT)
    cols = slice(i * W, (i + 1) * W)
    is_first = first if i == 0 else False
    off = pl.multiple_of(jnp.where(first, W, 0), W) if i == 0 else 0
    k2 = k_all[i * W:(i + 2) * W, :]
    vT2 = vT_all[:, i * W:(i + 2) * W]
    ci = lax.broadcasted_iota(jnp.int32, (W, W), 0)
    qi = lax.broadcasted_iota(jnp.int32, (W, W), 1)
    upper = ci > qi
    eye = ci == qi
    scores = []
    for g in range(N_KV_HEADS):
        q_grp = jnp.concatenate([qT(slice((g * GQA_GROUP + j) * HEAD_DIM_ATT, (g * GQA_GROUP + j + 1) * HEAD_DIM_ATT),
                                    cols) for j in range(GQA_GROUP)], axis=1)
        scores.append(jnp.dot(k2[:, gd(g)], q_grp, preferred_element_type=F32))
    yield
    probs, rden, pfar = [], [], []
    for h in range(N_HEADS_ATT):
        g, j = divmod(h, GQA_GROUP)
        s_prev = scores[g][:W, j * W:(j + 1) * W]
        s_cur = scores[g][W:, j * W:(j + 1) * W]
        s = jnp.where(upper, s_prev, s_cur) + bias_ref[h, pl.ds(off, W), :]
        far_bias = jnp.where(is_first, -jnp.inf, sink_ref[1, h])
        s_far = jnp.sum(jnp.where(eye, s_prev, 0.0), axis=0, keepdims=True) + far_bias
        sink = sink_ref[0, h]
        m = jnp.maximum(jnp.maximum(jnp.max(s, axis=0, keepdims=True), s_far), sink)
        p = jnp.exp(s - m)
        p_far = jnp.exp(s_far - m)
        rden.append(1.0 / (jnp.sum(p, axis=0, keepdims=True) + p_far + jnp.exp(sink - m)))
        pfar.append(p_far)
        zero = jnp.zeros_like(p)
        probs.append(jnp.concatenate([jnp.where(upper, p, zero), jnp.where(upper, zero, p)], axis=0).astype(BF16))
    yield
    outs, ssq = [], jnp.zeros((1, W), F32)
    for g in range(N_KV_HEADS):
        p_grp = jnp.concatenate(probs[g * GQA_GROUP:(g + 1) * GQA_GROUP], axis=1)
        o_grp = jnp.dot(vT2[gd(g), :], p_grp, preferred_element_type=F32)
        v_far = vT2[gd(g), :W].astype(F32)
        for j in range(GQA_GROUP):
            h = g * GQA_GROUP + j
            o = (o_grp[:, j * W:(j + 1) * W] + v_far * pfar[h]) * rden[h]
            outs.append(o)
            ssq = ssq + jnp.sum(o * o, axis=0, keepdims=True)
    inv = lax.rsqrt(ssq * (1.0 / ATT_W) + EPS)
    for h in range(N_HEADS_ATT):
        rows = slice(h * HEAD_DIM_ATT, (h + 1) * HEAD_DIM_ATT)
        write_out(rows, cols, (outs[h] * inv * gb_ref[rows, :]).astype(BF16))
    yield


def _attn_blocks(qT, k_all, vT_all, first, bias_ref, sink_ref, gb_ref, write_out, qb):
    gens = [_attn_block(i, qT, k_all, vT_all, first, bias_ref, sink_ref, gb_ref, write_out) for i in range(qb)]
    next(gens[0])
    next(gens[0])
    for i in range(qb):
        if i + 1 < qb:
            next(gens[i + 1])
        yield
        next(gens[i])
        if i + 1 < qb:
            next(gens[i + 1])


PROJ_TOKEN_CHUNK = 256
PROJ_ROW_SPLIT = 1024


def _proj_attn_kernel(x_ref, g_ref, wn_ref, wt_ref, bgc_ref, bias_ref, sink_ref, gb_ref,
                      kvf_ref, km_ref, qmT_ref, vmT_ref, omT_ref, gT_ref, mix_ref,
                      q_s, k_s, vT_s, hk_s, hv_s, *, tm, tiles_per_seq):
    s = pl.program_id(0)
    slot = s % 2
    prev = 1 - slot
    qb = tm // WINDOW

    @pl.when(s == 0)
    def _():
        q_s[...] = jnp.zeros_like(q_s)
        k_s[...] = jnp.zeros_like(k_s)
        vT_s[...] = jnp.zeros_like(vT_s)
        hk_s[...] = jnp.zeros_like(hk_s)
        hv_s[...] = jnp.zeros_like(hv_s)

    first = (s + tiles_per_seq - 1) % tiles_per_seq == 0
    k_all = jnp.concatenate([hk_s[...], k_s[prev]], axis=0)
    vT_all = jnp.concatenate([hv_s[...], vT_s[prev]], axis=1)

    def write_out(rows, cols, val):
        mix_ref[rows, cols] = val

    blocks = _attn_blocks(lambda rows, cols: q_s[prev, rows, cols], k_all, vT_all, first, bias_ref, sink_ref, gb_ref,
                          write_out, qb)
    next(blocks)
    hb = _rms_rows(x_ref[...], g_ref[...]).astype(BF16)

    def put_q(lo, tc, z):
        q_s[slot, lo:lo + z.shape[0], tc] = (z * (HEAD_DIM_ATT ** -0.5)).astype(BF16)

    def put_v(lo, tc, z):
        vT_s[slot, lo:lo + z.shape[0], tc] = z.astype(BF16)

    def put_qm(lo, tc, z):
        qmT_ref[lo:lo + z.shape[0], tc] = z.astype(BF16)

    def put_vm(lo, tc, z):
        vmT_ref[lo:lo + z.shape[0], tc] = z.astype(BF16)

    def put_om(lo, tc, z):
        omT_ref[lo:lo + z.shape[0], tc] = jax.nn.sigmoid(z)

    def put_g(lo, tc, z):
        g_t = z[:N_GATES, :] + bgc_ref[...]
        row = lax.broadcasted_iota(jnp.int32, g_t.shape, 0)
        gT_ref[:, tc] = jnp.where(row < N_HEADS_MLSTM, g_t, _log_sigmoid(g_t))

    segments = [(_T_QA, _T_VA, put_q), (_T_VA, _T_QM, put_v), (_T_QM, _T_VM, put_qm), (_T_VM, _T_OM, put_vm),
                (_T_OM, _T_G, put_om), (_T_G, _T_ROWS, put_g)]

    def proj_piece(lo, hi, tc):
        z = lax.dot_general(wt_ref[lo:hi, :], hb[tc, :], (((1,), (1,)), ((), ())), preferred_element_type=F32)
        for a, b, put in segments:
            if a < hi and lo < b:
                put(max(lo, a) - a, tc, z[max(lo, a) - lo:min(hi, b) - lo, :])

    z_n = jnp.dot(hb, wn_ref[:, :_N_G], preferred_element_type=F32)
    kvf_ref[...] = z_n[:, _N_KV:_N_KV + 2 * KV_W]
    km_ref[...] = (z_n[:, _N_KM:_N_KM + QK_M_W] * (DK_MLSTM ** -0.5)).astype(BF16)
    k_s[slot] = z_n[:, _N_KV:_N_KV + KV_W].astype(BF16)
    pieces = [(lo, hi, slice(c, c + PROJ_TOKEN_CHUNK))
              for c in range(0, tm, PROJ_TOKEN_CHUNK) for lo, hi in ((0, PROJ_ROW_SPLIT), (PROJ_ROW_SPLIT, _T_ROWS))]
    proj_piece(*pieces.pop(0))
    for _ in blocks:
        if pieces:
            proj_piece(*pieces.pop(0))
    for piece in pieces:
        proj_piece(*piece)
    hk_s[...] = k_s[prev, tm - WINDOW:, :]
    hv_s[...] = vT_s[prev, :, tm - WINDOW:]


def _proj_attn(x2d, g_norm, w_n, w_t, bg_col, bias, sinks, g_b, seq, tm):
    n = x2d.shape[0]
    n_tiles = n // tm
    tiles_per_seq = seq // tm
    cur = lambda s: jnp.minimum(s, n_tiles - 1)
    prv = lambda s: jnp.maximum(s - 1, 0)
    row = lambda w: pl.BlockSpec((tm, w), lambda s: (cur(s), 0))
    colb = lambda r: pl.BlockSpec((r, tm), lambda s: (0, cur(s)))
    sds = jax.ShapeDtypeStruct
    return pl.pallas_call(
        functools.partial(_proj_attn_kernel, tm=tm, tiles_per_seq=tiles_per_seq),
        grid=(n_tiles + 1,),
        in_specs=[row(D_MODEL), _const_spec((1, D_MODEL)), _const_spec(w_n.shape), _const_spec(w_t.shape),
                  _const_spec((N_GATES, 1)), _const_spec(bias.shape), pl.BlockSpec(memory_space=pltpu.SMEM),
                  _const_spec(g_b.shape)],
        out_specs=[row(2 * KV_W), row(QK_M_W), colb(QK_M_W), colb(MLSTM_W), colb(MLSTM_W), colb(N_GATES),
                   pl.BlockSpec((ATT_W, tm), lambda s: (0, prv(s)))],
        out_shape=[sds((n, 2 * KV_W), F32), sds((n, QK_M_W), BF16), sds((QK_M_W, n), BF16), sds((MLSTM_W, n), BF16),
                   sds((MLSTM_W, n), F32), sds((N_GATES, n), F32), sds((ATT_W, n), BF16)],
        scratch_shapes=[pltpu.VMEM((2, ATT_W, tm), BF16), pltpu.VMEM((2, tm, KV_W), BF16),
                        pltpu.VMEM((2, KV_W, tm), BF16), pltpu.VMEM((WINDOW, KV_W), BF16),
                        pltpu.VMEM((KV_W, WINDOW), BF16)],
        compiler_params=pltpu.CompilerParams(dimension_semantics=("arbitrary",), vmem_limit_bytes=VMEM_LIMIT),
        name="proj_attn",
    )(x2d, g_norm, w_n, w_t, bg_col, bias, sinks, g_b)


def _attn_sample_stages(q_ref, kvf_ref, ck_ref, cv_ref, bias_ref, sink_ref, g_ref, out_ref, kw_ref, vw_ref,
                        *, bb, t, w, kpad):
    ck = ck_ref[...]
    cv = cv_ref[...]
    k_new = kvf_ref[:, :KV_W].reshape(bb, t, KV_W)
    v_new = kvf_ref[:, KV_W:].reshape(bb, t, KV_W)
    kw_ref[:, :w - t, :] = ck[:, t:, :]
    kw_ref[:, w - t:, :] = k_new
    vw_ref[:, :w - t, :] = cv[:, t:, :]
    vw_ref[:, w - t:, :] = v_new
    zpad = jnp.zeros((bb, kpad - w - t, KV_W), F32)
    kk = jnp.concatenate([ck, k_new, zpad], axis=1).astype(BF16)
    vv = jnp.concatenate([cv, v_new, zpad], axis=1).astype(BF16)
    q3 = q_ref[...].astype(F32).reshape(bb, t, ATT_W)
    yield
    gd = lambda g: slice(g * HEAD_DIM_ATT, (g + 1) * HEAD_DIM_ATT)
    scores = []
    for g in range(N_KV_HEADS):
        qg = jnp.concatenate([q3[:, :, (g * GQA_GROUP + i) * HEAD_DIM_ATT:(g * GQA_GROUP + i + 1) * HEAD_DIM_ATT]
                              for i in range(GQA_GROUP)], axis=1).astype(BF16)
        scores.append(jnp.einsum('bqd,bkd->bqk', qg, kk[:, :, gd(g)], preferred_element_type=F32))
    yield
    probs = []
    for g in range(N_KV_HEADS):
        s = scores[g] + bias_ref[g]
        sink = sink_ref[g]
        m = jnp.maximum(jnp.max(s, axis=-1, keepdims=True), sink)
        p = jnp.exp(s - m)
        denom = jnp.sum(p, axis=-1, keepdims=True) + jnp.exp(sink - m)
        probs.append((p * (1.0 / denom)).astype(BF16))
        yield
    outs = []
    ssq = jnp.zeros((bb, 1, t, 1), F32)
    for g in range(N_KV_HEADS):
        o = jnp.einsum('bqk,bkd->bqd', probs[g], vv[:, :, gd(g)], preferred_element_type=F32)
        o = o.reshape(bb, GQA_GROUP, t, HEAD_DIM_ATT)
        outs.append(o)
        ssq = ssq + jnp.sum(jnp.sum(o * o, axis=-1, keepdims=True), axis=1, keepdims=True)
    yield
    inv = lax.rsqrt(ssq * (1.0 / ATT_W) + EPS)
    for g in range(N_KV_HEADS):
        y = outs[g] * inv
        for i in range(GQA_GROUP):
            sl = slice((g * GQA_GROUP + i) * HEAD_DIM_ATT, (g * GQA_GROUP + i + 1) * HEAD_DIM_ATT)
            out_ref[:, sl] = (y[:, i].reshape(bb * t, HEAD_DIM_ATT) * g_ref[:, sl]).astype(BF16)


SEQ_PER_BLOCK = 16


def _seg_scan_rows(x, seg, forward):
    n = x.shape[0]
    pos = lax.broadcasted_iota(jnp.int32, x.shape, 0) % seg
    k = 1
    while k < seg:
        if forward:
            x = jnp.maximum(x, jnp.where(pos >= k, pltpu.roll(x, k, axis=0), -jnp.inf))
        else:
            x = jnp.maximum(x, jnp.where(pos < seg - k, pltpu.roll(x, n - k, axis=0), -jnp.inf))
        k *= 2
    return x


def _mlstm_blk_stages(qk_ref, v_ref, om_ref, gt_ref, gm_ref, c0_ref, n0_ref, nrow_ref, m0_ref,
                      mix_ref, c_out, n_out, m_out, *, t):
    S = SEQ_PER_BLOCK
    L = S * t
    hi = lax.Precision.HIGHEST
    ti = lax.broadcasted_iota(jnp.int32, (L, L), 0)
    si = lax.broadcasted_iota(jnp.int32, (L, L), 1)
    same = (ti // t) == (si // t)
    causal = same & (si <= ti)
    tril_seg = causal.astype(F32)
    ones_seg = same.astype(F32)
    ones_col = (lax.broadcasted_iota(jnp.int32, (L, DV_MLSTM), 1) == 0).astype(BF16)
    bd_mask = (lax.broadcasted_iota(jnp.int32, (L, S * DK_MLSTM), 0) // t
               == lax.broadcasted_iota(jnp.int32, (L, S * DK_MLSTM), 1) // DK_MLSTM)
    bd_mask_t = (lax.broadcasted_iota(jnp.int32, (S * DK_MLSTM, L), 0) // DK_MLSTM
                 == lax.broadcasted_iota(jnp.int32, (S * DK_MLSTM, L), 1) // t)
    heads = range(N_HEADS_MLSTM)
    hk = lambda h: slice(h * DK_MLSTM, (h + 1) * DK_MLSTM)
    hkk = lambda h: slice(QK_M_W + h * DK_MLSTM, QK_M_W + (h + 1) * DK_MLSTM)
    hv = lambda h: slice(h * DV_MLSTM, (h + 1) * DV_MLSTM)

    gt = gt_ref[...]
    m_prev = m0_ref[...]
    b = jnp.dot(tril_seg, gt, precision=hi, preferred_element_type=F32)
    b_end = jnp.dot(ones_seg, gt, precision=hi, preferred_element_type=F32)
    yield
    a = pltpu.roll(gt, N_HEADS_MLSTM, axis=1) - b
    big_m = jnp.maximum(m_prev, _seg_scan_rows(a, t, True))
    inter = jnp.exp(m_prev - big_m)
    em = jnp.exp(-(b + big_m))
    m_end = _seg_scan_rows(big_m, t, False)
    w = jnp.exp(a - m_end)
    decay = jnp.exp(m_prev - m_end)
    m_out[...] = b_end + m_end
    a_t = a.T
    yield
    q = [qk_ref[:, hk(h)] for h in heads]
    k = [qk_ref[:, hkk(h)] for h in heads]
    v_ext = [jnp.concatenate([v_ref[:, hv(h)], ones_col], axis=1) for h in heads]
    qk = [lax.dot_general(q[h], k[h], (((1,), (1,)), ((), ())), preferred_element_type=F32) for h in heads]
    c_all = [c0_ref[:, h].reshape(S * DK_MLSTM, DV_MLSTM) for h in heads]
    q_bd = []
    for h in heads:
        q2 = jnp.concatenate([q[h], q[h]], axis=1)
        q_bd.append(jnp.where(bd_mask, jnp.concatenate([q2] * (S // 2), axis=1), jnp.zeros((), BF16)))
    qc = [jnp.dot(q_bd[h], c_all[h].astype(BF16), preferred_element_type=F32) for h in heads]
    yield
    s = []
    for h in heads:
        ln = _GATE0 + h
        d = jnp.exp(jnp.where(causal, a_t[ln:ln + 1, :] - big_m[:, ln:ln + 1], -jnp.inf))
        s.append((qk[h] * d).astype(BF16))
    yield
    sv = [jnp.dot(s[h], v_ext[h], preferred_element_type=F32) for h in heads]
    kw, kw_bd = [], []
    for h in heads:
        ln = _GATE0 + h
        kw_f = k[h].astype(F32) * w[:, ln:ln + 1]
        kw.append(kw_f.astype(BF16))
        kw_bd.append(jnp.where(bd_mask_t, jnp.concatenate([kw_f.T] * S, axis=0), 0.0).astype(BF16))
    upd = [jnp.dot(kw_bd[h], v_ref[:, hv(h)], preferred_element_type=F32) for h in heads]
    seq_of_tok = lax.broadcasted_iota(jnp.int32, (S, L), 1) // t == lax.broadcasted_iota(jnp.int32, (S, L), 0)
    first_tok = lax.broadcasted_iota(jnp.int32, (S, L), 1) == lax.broadcasted_iota(jnp.int32, (S, L), 0) * t
    n_upd = [jnp.dot(seq_of_tok.astype(BF16), kw[h], preferred_element_type=F32) for h in heads]
    decay_seq = jnp.dot(first_tok.astype(F32), decay, precision=hi, preferred_element_type=F32)
    yield
    for h in heads:
        ln = _GATE0 + h
        inter_h = inter[:, ln:ln + 1]
        qn = jnp.sum(q[h].astype(F32) * nrow_ref[:, hk(h)], axis=1, keepdims=True)
        den = inter_h * qn + sv[h][:, DV_MLSTM:DV_MLSTM + 1]
        r = 1.0 / jnp.maximum(jnp.abs(den), em[:, ln:ln + 1])
        tt = om_ref[:, hv(h)] * (inter_h * qc[h] + sv[h][:, :DV_MLSTM])
        msq = jnp.mean(tt * tt, axis=1, keepdims=True)
        scale = r * lax.rsqrt(r * r * msq + EPS)
        mix_ref[:, hv(h)] = (tt * scale * gm_ref[:, hv(h)]).astype(BF16)
        dec = decay[:, ln:ln + 1].reshape(S, t, 1)[:, 0:1, :]
        c_out[:, h] = dec * c0_ref[:, h] + upd[h].reshape(S, DK_MLSTM, DV_MLSTM)
        n_out[:, hk(h)] = decay_seq[:, ln:ln + 1] * n0_ref[:, hk(h)] + n_upd[h]


def _sample_mixers_kernel(q_ref, kvf_ref, ck_ref, cv_ref, bias_ref, sink_ref, ga_ref,
                          qk_ref, v_ref, om_ref, gt_ref, gm_ref, c0_ref, n0_ref, nrow_ref, m0_ref,
                          mixa_ref, kw_ref, vw_ref, mixm_ref, c_out, n_out, m_out, *, t, w, kpad):
    attn = _attn_sample_stages(q_ref, kvf_ref, ck_ref, cv_ref, bias_ref, sink_ref, ga_ref, mixa_ref, kw_ref, vw_ref,
                               bb=SEQ_PER_BLOCK, t=t, w=w, kpad=kpad)
    mlstm = _mlstm_blk_stages(qk_ref, v_ref, om_ref, gt_ref, gm_ref, c0_ref, n0_ref, nrow_ref, m0_ref,
                              mixm_ref, c_out, n_out, m_out, t=t)
    for _ in itertools.zip_longest(attn, mlstm):
        pass


def _sample_mixers(qa, kvf, ck, cv, bias, sink_rows, g_att, qkm, vm, om, gt, g_m, c0, n0_seq, n0_rows, m0_rows,
                   nseq, t):
    S = SEQ_PER_BLOCK
    L = S * t
    w = ck.shape[1]
    kpad = bias.shape[-1]
    rows = GQA_GROUP * t
    tok = lambda wd: pl.BlockSpec((L, wd), lambda i: (i, 0))
    cache = pl.BlockSpec((S, w, KV_W), lambda i: (i, 0, 0))
    st_c = pl.BlockSpec((S, N_HEADS_MLSTM, DK_MLSTM, DV_MLSTM), lambda i: (i, 0, 0, 0))
    st_n = pl.BlockSpec((S, QK_M_W), lambda i: (i, 0))
    sds = jax.ShapeDtypeStruct
    return pl.pallas_call(
        functools.partial(_sample_mixers_kernel, t=t, w=w, kpad=kpad),
        grid=(nseq // S,),
        in_specs=[tok(ATT_W), tok(2 * KV_W), cache, cache,
                  _const_spec((N_KV_HEADS, rows, kpad)), _const_spec((N_KV_HEADS, rows, 1)), _const_spec((1, ATT_W)),
                  tok(2 * QK_M_W), tok(MLSTM_W), tok(MLSTM_W), tok(LANES), _const_spec((1, MLSTM_W)),
                  st_c, st_n, tok(QK_M_W), tok(LANES)],
        out_specs=[tok(ATT_W), cache, cache, tok(MLSTM_W), st_c, st_n, tok(LANES)],
        out_shape=[sds((nseq * t, ATT_W), BF16), sds((nseq, w, KV_W), F32), sds((nseq, w, KV_W), F32),
                   sds((nseq * t, MLSTM_W), BF16), sds((nseq, N_HEADS_MLSTM, DK_MLSTM, DV_MLSTM), F32),
                   sds((nseq, QK_M_W), F32), sds((nseq * t, LANES), F32)],
        compiler_params=pltpu.CompilerParams(dimension_semantics=("arbitrary",), vmem_limit_bytes=VMEM_LIMIT),
        name="sample_mixers",
    )(qa, kvf, ck, cv, bias, sink_rows, g_att, qkm, vm, om, gt, g_m, c0, n0_seq, n0_rows, m0_rows)


def _out_mlp_kernel(x_ref, a_ref, hm_ref, woa_ref, wom_ref, gmlp_ref, wup_ref, wdn_ref, gfin_ref, y_ref):
    x1 = (x_ref[...]
          + jnp.dot(a_ref[...], woa_ref[...], preferred_element_type=F32)
          + jnp.dot(hm_ref[...], wom_ref[...], preferred_element_type=F32))
    h = _rms_rows(x1, gmlp_ref[...]).astype(BF16)
    u = jnp.maximum(jnp.dot(h, wup_ref[...], preferred_element_type=F32), 0.0)
    y = x1 + jnp.dot((u * u).astype(BF16), wdn_ref[...], preferred_element_type=F32)
    y_ref[...] = _rms_rows(y, gfin_ref[...])


def _out_mlp(x2d, mix_a, mix_m, wo_a, wo_m, g_mlp, w_up, w_dn, g_fin, tm):
    n = x2d.shape[0]
    row = lambda w: pl.BlockSpec((tm, w), lambda i: (i, 0))
    return pl.pallas_call(
        _out_mlp_kernel,
        grid=(n // tm,),
        in_specs=[row(D_MODEL), row(ATT_W), row(MLSTM_W), _const_spec(wo_a.shape), _const_spec(wo_m.shape),
                  _const_spec((1, D_MODEL)), _const_spec(w_up.shape), _const_spec(w_dn.shape),
                  _const_spec((1, D_MODEL))],
        out_specs=row(D_MODEL),
        out_shape=jax.ShapeDtypeStruct((n, D_MODEL), F32),
        compiler_params=pltpu.CompilerParams(dimension_semantics=("arbitrary",), vmem_limit_bytes=VMEM_LIMIT),
        name="out_mlp",
    )(x2d, mix_a, mix_m, wo_a, wo_m, g_mlp, w_up, w_dn, g_fin)


_CN_ROWS = DV_MLSTM + BF16_ROWS


def _prefix_max_lanes(x):
    n = x.shape[1]
    col = lax.broadcasted_iota(jnp.int32, x.shape, 1)
    k = 1
    while k < n:
        x = jnp.maximum(x, jnp.where(col >= k, pltpu.roll(x, k, axis=1), -jnp.inf))
        k *= 2
    return x


def _mlstm_stages(qT_ref, k_ref, vT_ref, omT_ref, gT_ref, gb_ref, write_mix, state, chunk, nsub):
    L = chunk
    i0 = lax.broadcasted_iota(jnp.int32, (L, L), 0)
    i1 = lax.broadcasted_iota(jnp.int32, (L, L), 1)
    causal_t = i0 <= i1
    triu = causal_t.astype(F32)
    ones_rows = (lax.broadcasted_iota(jnp.int32, (BF16_ROWS, L), 0) == 0).astype(BF16)
    zpad = jnp.zeros((LANES - 2 * N_GATES, L), F32)
    cn, m_r = state['cn'], state['m_r']
    hi = lax.Precision.HIGHEST
    heads = range(N_HEADS_MLSTM)
    subs = range(nsub)
    col_of = lambda j: slice(j * L, (j + 1) * L)
    hv = lambda h: slice(h * DV_MLSTM, (h + 1) * DV_MLSTM)
    hk = lambda h: slice(h * DK_MLSTM, (h + 1) * DK_MLSTM)
    a_r, b_r, pm_r = [], [], []
    for j in subs:
        g_t = gT_ref[:, col_of(j)]
        b_r.append(jnp.dot(g_t, triu, precision=hi, preferred_element_type=F32))
        a_r.append(pltpu.roll(g_t, N_HEADS_MLSTM, axis=0) - b_r[j])
    yield
    kq = [[jnp.dot(k_ref[col_of(j), hk(h)], qT_ref[hk(h), col_of(j)], preferred_element_type=F32)
           for h in heads] for j in subs]
    vT_ext = [[jnp.concatenate([vT_ref[hv(h), col_of(j)], ones_rows], axis=0) for h in heads] for j in subs]
    for j in subs:
        pm_r.append(_prefix_max_lanes(a_r[j]))
    yield
    big_m, inter, em, decay, cols_t = [], [], [], [], []
    for j in subs:
        big_m.append(jnp.maximum(m_r, pm_r[j]))
        inter.append(jnp.exp(m_r - big_m[j]))
        em.append(jnp.exp(-(b_r[j] + big_m[j])))
        m_end = big_m[j][:, L - 1:L]
        w_r = jnp.exp(a_r[j] - m_end)
        decay.append(jnp.exp(m_r[:, 0:1] - m_end))
        m_r = jnp.broadcast_to(b_r[j][:, L - 1:L] + m_end, (N_GATES, L))
        cols_t.append(jnp.concatenate([a_r[j], w_r, zpad], axis=0).T)
    sT = [[None] * N_HEADS_MLSTM for _ in subs]
    kw = [[None] * N_HEADS_MLSTM for _ in subs]
    intra = [[None] * N_HEADS_MLSTM for _ in subs]
    upd = [[None] * N_HEADS_MLSTM for _ in subs]
    for j in subs:
        for h in heads:
            r = _GATE0 + h
            d = jnp.exp(jnp.where(causal_t, cols_t[j][:, r:r + 1] - big_m[j][r:r + 1, :], -jnp.inf))
            sT[j][h] = (kq[j][h] * d).astype(BF16)
            w_col = cols_t[j][:, N_GATES + r:N_GATES + r + 1]
            kw[j][h] = (k_ref[col_of(j), hk(h)].astype(F32) * w_col).astype(BF16)
        for h in heads:
            intra[j][h] = jnp.dot(vT_ext[j][h], sT[j][h], preferred_element_type=F32)
            upd[j][h] = jnp.dot(vT_ext[j][h], kw[j][h], preferred_element_type=F32)
        yield
    for j in subs:
        qc = [jnp.dot(cn[h].astype(BF16), qT_ref[hk(h), col_of(j)], preferred_element_type=F32) for h in heads]
        for h in heads:
            r = _GATE0 + h
            nd = inter[j][r:r + 1, :] * qc[h] + intra[j][h]
            rr = 1.0 / jnp.maximum(jnp.abs(nd[DV_MLSTM:DV_MLSTM + 1, :]), em[j][r:r + 1, :])
            t = omT_ref[hv(h), col_of(j)] * nd[:DV_MLSTM, :]
            msq = jnp.mean(t * t, axis=0, keepdims=True)
            scale = rr * lax.rsqrt(rr * rr * msq + EPS)
            write_mix(hv(h), col_of(j), (t * scale * gb_ref[hv(h), :]).astype(BF16))
            cn[h] = decay[j][r:r + 1, :] * cn[h] + upd[j][h]
        yield
    state['m_r'] = m_r


def _mlstm_mlp_kernel(x_ref, a_ref, qT_ref, k_ref, vT_ref, omT_ref, gT_ref, gb_ref,
                      woa_ref, wom_ref, gmlp_ref, wup_ref, wdn_ref, gfin_ref,
                      y_ref, c_out, n_out, m_out, mix_s, cn_s, mr_s,
                      *, chunk, nsub, tiles_per_seq, n_tiles):
    s = pl.program_id(0)
    slot = s % 2

    @pl.when(s == 0)
    def _():
        mix_s[...] = jnp.zeros_like(mix_s)

    @pl.when(s % tiles_per_seq == 0)
    def _():
        cn_s[...] = jnp.zeros_like(cn_s)
        mr_s[...] = jnp.zeros_like(mr_s)

    state = {'cn': [cn_s[h] for h in range(N_HEADS_MLSTM)], 'm_r': mr_s[...]}

    def write_mix(rows, cols, val):
        mix_s[slot, rows, cols] = val

    stages = _mlstm_stages(qT_ref, k_ref, vT_ref, omT_ref, gT_ref, gb_ref, write_mix, state, chunk, nsub)
    next(stages)
    tdims = (((0,), (0,)), ((), ()))
    x1 = (x_ref[...]
          + lax.dot_general(a_ref[...], woa_ref[...], tdims, preferred_element_type=F32)
          + lax.dot_general(mix_s[1 - slot], wom_ref[...], tdims, preferred_element_type=F32))
    next(stages)
    h = _rms_rows(x1, gmlp_ref[...]).astype(BF16)
    u = jnp.maximum(jnp.dot(h, wup_ref[...], preferred_element_type=F32), 0.0)
    for _ in stages:
        pass
    y = x1 + jnp.dot((u * u).astype(BF16), wdn_ref[...], preferred_element_type=F32)
    y_ref[...] = _rms_rows(y, gfin_ref[...])
    for hh in range(N_HEADS_MLSTM):
        cn_s[hh] = state['cn'][hh]
    mr_s[...] = state['m_r']

    @pl.when((s % tiles_per_seq == tiles_per_seq - 1) & (s < n_tiles))
    def _():
        for hh in range(N_HEADS_MLSTM):
            c_out[0, hh] = cn_s[hh, :DV_MLSTM, :].T
            n_out[0, hh] = cn_s[hh, DV_MLSTM:DV_MLSTM + 1, :]
        m_out[0] = mr_s[...]


def _mlstm_mlp(x2d, mix_aT, qmT, km, vmT, omT, gT, g_b, wo_a, wo_m, g_mlp, w_up, w_dn, g_fin,
               nseq, t, chunk, nsub):
    tm = chunk * nsub
    n_tiles = (nseq * t) // tm
    tiles_per_seq = t // tm
    cur = lambda s: jnp.minimum(s, n_tiles - 1)
    prv = lambda s: jnp.maximum(s - 1, 0)
    colb = lambda r: pl.BlockSpec((r, tm), lambda s: (0, cur(s)))
    st = lambda a, d: pl.BlockSpec((1, N_HEADS_MLSTM, a, d), lambda s: (cur(s) // tiles_per_seq, 0, 0, 0))
    sds = jax.ShapeDtypeStruct
    kern = functools.partial(_mlstm_mlp_kernel, chunk=chunk, nsub=nsub,
                             tiles_per_seq=tiles_per_seq, n_tiles=n_tiles)
    return pl.pallas_call(
        kern,
        grid=(n_tiles + 1,),
        in_specs=[pl.BlockSpec((tm, D_MODEL), lambda s: (prv(s), 0)),
                  pl.BlockSpec((ATT_W, tm), lambda s: (0, prv(s))),
                  colb(QK_M_W), pl.BlockSpec((tm, QK_M_W), lambda s: (cur(s), 0)), colb(MLSTM_W), colb(MLSTM_W),
                  colb(N_GATES),
                  _const_spec(g_b.shape), _const_spec(wo_a.shape), _const_spec(wo_m.shape),
                  _const_spec((1, D_MODEL)), _const_spec(w_up.shape), _const_spec(w_dn.shape),
                  _const_spec((1, D_MODEL))],
        out_specs=[pl.BlockSpec((tm, D_MODEL), lambda s: (prv(s), 0)),
                   st(DK_MLSTM, DV_MLSTM), st(1, DK_MLSTM),
                   pl.BlockSpec((1, N_GATES, chunk), lambda s: (cur(s) // tiles_per_seq, 0, 0))],
        out_shape=[sds((nseq * t, D_MODEL), F32), sds((nseq, N_HEADS_MLSTM, DK_MLSTM, DV_MLSTM), F32),
                   sds((nseq, N_HEADS_MLSTM, 1, DK_MLSTM), F32), sds((nseq, N_GATES, chunk), F32)],
        scratch_shapes=[pltpu.VMEM((2, MLSTM_W, tm), BF16),
                        pltpu.VMEM((N_HEADS_MLSTM, _CN_ROWS, DK_MLSTM), F32),
                        pltpu.VMEM((N_GATES, chunk), F32)],
        compiler_params=pltpu.CompilerParams(dimension_semantics=("arbitrary",), vmem_limit_bytes=VMEM_LIMIT),
        name="mlstm_mlp",
    )(x2d, mix_aT, qmT, km, vmT, omT, gT, g_b, wo_a, wo_m, g_mlp, w_up, w_dn, g_fin)


def kernel(x_prompt, x_sample, cache_k_win, cache_v_win, state_C, state_n, state_m, rel_bias, norm_mix, w_in,
           b_gates, attn_sinks, norm_attn_out, norm_mlstm_out, w_out, norm_mlp, w_up, w_down, norm_final):
    depth = w_in.shape[0]
    assert depth == 1, "single-layer trunk"
    bp, sp = x_prompt.shape[:2]
    bs, ts = x_sample.shape[:2]
    wc = cache_k_win.shape[2]
    l = 0
    chunk = 256

    wl = w_in[l]
    cols = lambda lo, n: wl[:, lo:lo + n]
    w_n = jnp.concatenate([cols(_O_KV, 2 * KV_W), cols(_O_QKM + QK_M_W, QK_M_W),
                           jnp.pad(cols(_O_G, N_GATES), ((0, 0), (0, LANES - N_GATES)))], axis=1).astype(BF16)
    w_t = jnp.concatenate([cols(_O_QA, ATT_W), cols(_O_KV + KV_W, KV_W), cols(_O_QKM, QK_M_W), cols(_O_VM, MLSTM_W),
                           cols(_O_OM, MLSTM_W), jnp.pad(cols(_O_G, N_GATES), ((0, 0), (0, BF16_ROWS - N_GATES)))],
                          axis=1).T.astype(BF16)
    bg_row = jnp.pad(b_gates[l], (0, LANES - N_GATES)).reshape(1, LANES)
    bg_col = b_gates[l].reshape(N_GATES, 1)
    wo_a = w_out[l, :ATT_W].astype(BF16)
    wo_m = w_out[l, ATT_W:].astype(BF16)
    wup = w_up[l].astype(BF16)
    wdn = w_down[l].astype(BF16)
    g_mix = norm_mix[l].reshape(1, D_MODEL)
    g_att = norm_attn_out[l].reshape(1, ATT_W)
    g_m = norm_mlstm_out[l].reshape(1, MLSTM_W)
    g_mlp = norm_mlp[l].reshape(1, D_MODEL)
    g_fin = norm_final.reshape(1, D_MODEL)
    gb_att = jnp.broadcast_to(norm_attn_out[l].reshape(ATT_W, 1), (ATT_W, WINDOW))
    gb_m = jnp.broadcast_to(norm_mlstm_out[l].reshape(MLSTM_W, 1), (MLSTM_W, chunk))
    sinks = attn_sinks[l]

    xp = x_prompt.reshape(bp * sp, D_MODEL)
    bias_t = _bias_table(rel_bias, _prompt_buckets_t())
    far_bucket = int(_t5_bucket_np(np.array(WINDOW)))
    sink_far = jnp.stack([sinks, rel_bias[far_bucket]])
    kvf, km, qmT, vmT, omT, gT, mix_a = _proj_attn(xp, g_mix, w_n, w_t, bg_col, bias_t, sink_far, gb_att, sp, tm=1024)
    y_p, c_p, n_p, m_p = _mlstm_mlp(xp, mix_a, qmT, km, vmT, omT, gT, gb_m, wo_a, wo_m, g_mlp, wup, wdn, g_fin,
                                    bp, sp, chunk=chunk, nsub=512 // chunk)
    m_p = m_p[:, _GATE0:_GATE0 + N_HEADS_MLSTM, 0]
    win_p = min(WINDOW, sp)
    kv_win = kvf.reshape(bp, sp, 2 * KV_W)[:, sp - win_p:, :]
    k_win_p = kv_win[:, :, :KV_W].reshape(1, bp, win_p, N_KV_HEADS, HEAD_DIM_ATT)
    v_win_p = kv_win[:, :, KV_W:].reshape(1, bp, win_p, N_KV_HEADS, HEAD_DIM_ATT)

    assert math.gcd(ts, 64) == ts and bs % SEQ_PER_BLOCK == 0, "sample group: one mLSTM chunk per sequence"
    kpad = ((wc + ts + BF16_ROWS - 1) // BF16_ROWS) * BF16_ROWS
    bias_s = _bias_table(rel_bias, _sample_buckets(ts, wc, kpad))
    bias_s = bias_s.reshape(N_KV_HEADS, GQA_GROUP * ts, kpad)
    sink_rows = jnp.repeat(sinks, ts).reshape(N_KV_HEADS, GQA_GROUP * ts, 1)
    xs = x_sample.reshape(bs * ts, D_MODEL)
    qa, kvf_s, qkm, vm, om, gt = _in_proj_s(xs, g_mix, w_n, w_t, bg_row, tm=512)
    ck = cache_k_win[l].reshape(bs, wc, KV_W)
    cv = cache_v_win[l].reshape(bs, wc, KV_W)
    n0 = state_n[l].reshape(bs, QK_M_W)
    n0_rows = jnp.repeat(n0, ts, axis=0)
    m0_rows = jnp.pad(jnp.repeat(state_m[l], ts, axis=0),
                      ((0, 0), (_GATE0, LANES - _GATE0 - N_HEADS_MLSTM)))
    mix_a_s, kw_s, vw_s, mix_m_s, c_s, n_s, m_rows = _sample_mixers(
        qa, kvf_s, ck, cv, bias_s, sink_rows, g_att, qkm, vm, om, gt, g_m, state_C[l], n0, n0_rows, m0_rows, bs, ts)
    m_s = m_rows[::ts, _GATE0:_GATE0 + N_HEADS_MLSTM]
    y_s = _out_mlp(xs, mix_a_s, mix_m_s, wo_a, wo_m, g_mlp, wup, wdn, g_fin, tm=512)

    return (y_p.reshape(bp, sp, D_MODEL), y_s.reshape(bs, ts, D_MODEL),
            k_win_p, v_win_p,
            c_p[None], n_p.reshape(1, bp, N_HEADS_MLSTM, DK_MLSTM), m_p.reshape(1, bp, N_HEADS_MLSTM),
            kw_s.reshape(1, bs, wc, N_KV_HEADS, HEAD_DIM_ATT), vw_s.reshape(1, bs, wc, N_KV_HEADS, HEAD_DIM_ATT),
            c_s[None], n_s.reshape(1, bs, N_HEADS_MLSTM, DK_MLSTM), m_s.reshape(1, bs, N_HEADS_MLSTM))
```

```python
import functools
import itertools
import math

import numpy as np
import jax
import jax.numpy as jnp
from jax import lax
from jax.experimental import pallas as pl
from jax.experimental.pallas import tpu as pltpu

D_MODEL = 1024
N_HEADS_ATT = 8
N_KV_HEADS = 2
GQA_GROUP = N_HEADS_ATT // N_KV_HEADS
HEAD_DIM_ATT = 64
ATT_W = N_HEADS_ATT * HEAD_DIM_ATT
KV_W = N_KV_HEADS * HEAD_DIM_ATT
WINDOW = 128
NUM_BUCKETS = 32
MAX_DISTANCE = 128
N_HEADS_MLSTM = 4
DV_MLSTM = 128
DK_MLSTM = 64
MLSTM_W = N_HEADS_MLSTM * DV_MLSTM
QK_M_W = N_HEADS_MLSTM * DK_MLSTM
D_FF = 4 * D_MODEL
EPS = 1e-6
N_GATES = 2 * N_HEADS_MLSTM
LANES = 128
BF16_ROWS = 16
VMEM_LIMIT = 56 * 1024 * 1024

F32 = jnp.float32
BF16 = jnp.bfloat16

_O_QA = 0
_O_KV = _O_QA + ATT_W
_O_QKM = _O_KV + 2 * KV_W
_O_VM = _O_QKM + 2 * QK_M_W
_O_OM = _O_VM + MLSTM_W
_O_G = _O_OM + MLSTM_W
_T_QA = 0
_T_VA = _T_QA + ATT_W
_T_QM = _T_VA + KV_W
_T_VM = _T_QM + QK_M_W
_T_OM = _T_VM + MLSTM_W
_T_G = _T_OM + MLSTM_W
_T_ROWS = _T_G + BF16_ROWS
_N_KV = 0
_N_KM = _N_KV + 2 * KV_W
_N_G = _N_KM + QK_M_W
_N_COLS = _N_G + LANES
_GATE0 = N_HEADS_MLSTM


def _const_spec(shape):
    nd = len(shape)
    return pl.BlockSpec(shape, lambda *_: (0,) * nd, pipeline_mode=pl.Buffered(1))


def _log_sigmoid(g):
    return jnp.minimum(g, 0.0) - jnp.log1p(jnp.exp(-jnp.abs(g)))


def _rms_rows(x, gain):
    return x * lax.rsqrt(jnp.mean(x * x, axis=-1, keepdims=True) + EPS) * gain


def _in_proj_s_kernel(x_ref, g_ref, wn_ref, wt_ref, bgr_ref, qa_ref, kvf_ref, qkm_ref, vm_ref, om_ref, gt_ref):
    hb = _rms_rows(x_ref[...], g_ref[...]).astype(BF16)
    z_n = jnp.dot(hb, wn_ref[...], preferred_element_type=F32)
    z_t = lax.dot_general(hb, wt_ref[...], (((1,), (1,)), ((), ())), preferred_element_type=F32)
    qa_ref[...] = (z_t[:, _T_QA:_T_QA + ATT_W] * (HEAD_DIM_ATT ** -0.5)).astype(BF16)
    kvf_ref[:, :KV_W] = z_n[:, _N_KV:_N_KV + KV_W]
    kvf_ref[:, KV_W:] = z_t[:, _T_VA:_T_VA + KV_W]
    qkm_ref[:, :QK_M_W] = z_t[:, _T_QM:_T_QM + QK_M_W].astype(BF16)
    qkm_ref[:, QK_M_W:] = (z_n[:, _N_KM:_N_KM + QK_M_W] * (DK_MLSTM ** -0.5)).astype(BF16)
    vm_ref[...] = z_t[:, _T_VM:_T_VM + MLSTM_W].astype(BF16)
    om_ref[...] = jax.nn.sigmoid(z_t[:, _T_OM:_T_OM + MLSTM_W])
    g = z_n[:, _N_G:_N_G + LANES] + bgr_ref[...]
    col = lax.broadcasted_iota(jnp.int32, g.shape, 1)
    gt_ref[...] = jnp.where(col < N_HEADS_MLSTM, g, _log_sigmoid(g))


def _in_proj_s(x2d, g_norm, w_n, w_t, bg_row, tm):
    n = x2d.shape[0]
    row = lambda w: pl.BlockSpec((tm, w), lambda i: (i, 0))
    sds = jax.ShapeDtypeStruct
    return pl.pallas_call(
        _in_proj_s_kernel,
        grid=(n // tm,),
        in_specs=[row(D_MODEL), _const_spec((1, D_MODEL)), _const_spec(w_n.shape), _const_spec(w_t.shape),
                  _const_spec((1, LANES))],
        out_specs=[row(ATT_W), row(2 * KV_W), row(2 * QK_M_W), row(MLSTM_W), row(MLSTM_W), row(LANES)],
        out_shape=[sds((n, ATT_W), BF16), sds((n, 2 * KV_W), F32), sds((n, 2 * QK_M_W), BF16),
                   sds((n, MLSTM_W), BF16), sds((n, MLSTM_W), F32), sds((n, LANES), F32)],
        compiler_params=pltpu.CompilerParams(dimension_semantics=("arbitrary",), vmem_limit_bytes=VMEM_LIMIT),
        name="in_proj_s",
    )(x2d, g_norm, w_n, w_t, bg_row)


def _t5_bucket_np(dist):
    max_exact = NUM_BUCKETS // 2
    d = np.maximum(dist, 0)
    ratio = np.maximum(d, 1).astype(np.float32) / np.float32(max_exact)
    large = max_exact + (np.log(ratio) / np.float32(math.log(MAX_DISTANCE / max_exact))
                         * np.float32(NUM_BUCKETS - max_exact)).astype(np.int32)
    large = np.minimum(large, NUM_BUCKETS - 1)
    return np.where(d < max_exact, d, large).astype(np.int32)


def _bias_table_kernel(rb_ref, bucket_ref, out_ref):
    bucket = bucket_ref[...]
    hit = [bucket == j for j in range(NUM_BUCKETS)]
    for h in range(N_HEADS_ATT):
        acc = jnp.full(bucket.shape, -jnp.inf, F32)
        for j in range(NUM_BUCKETS):
            acc = jnp.where(hit[j], rb_ref[j, h], acc)
        out_ref[h] = acc


def _bias_table(rel_bias, bucket):
    r, c = bucket.shape
    return pl.pallas_call(
        _bias_table_kernel,
        in_specs=[pl.BlockSpec(memory_space=pltpu.SMEM), pl.BlockSpec(memory_space=pltpu.VMEM)],
        out_specs=pl.BlockSpec(memory_space=pltpu.VMEM),
        out_shape=jax.ShapeDtypeStruct((N_HEADS_ATT, r, c), F32),
        name="bias_table",
    )(rel_bias, jnp.asarray(bucket))


def _prompt_buckets_t():
    qi = np.arange(WINDOW)[None, :]
    ci = np.arange(WINDOW)[:, None]
    upper = ci > qi
    general = _t5_bucket_np(np.where(upper, qi + WINDOW - ci, qi - ci))
    first = np.where(upper, -1, general)
    return np.concatenate([general, first], axis=0).astype(np.int32)


def _sample_buckets(t, w, padded):
    dist = (w + np.arange(t))[:, None] - np.arange(padded)[None, :]
    valid = (dist >= 0) & (dist <= WINDOW) & (np.arange(padded)[None, :] < w + t)
    return np.where(valid, _t5_bucket_np(dist), -1).astype(np.int32)


def _attn_block(i, qT, k_all, vT_all, first, bias_ref, sink_ref, gb_ref, write_out):
    W = WINDOW
    gd = lambda g: slice(g * HEAD_DIM_ATT, (g + 1) * HEAD_DIM_ATT)
    cols = slice(i * W, (i + 1) * W)
    is_first = first if i == 0 else False
    off = pl.multiple_of(jnp.where(first, W, 0), W) if i == 0 else 0
    k2 = k_all[i * W:(i + 2) * W, :]
    vT2 = vT_all[:, i * W:(i + 2) * W]
    ci = lax.broadcasted_iota(jnp.int32, (W, W), 0)
    qi = lax.broadcasted_iota(jnp.int32, (W, W), 1)
    upper = ci > qi
    eye = ci == qi
    scores = []
    for g in range(N_KV_HEADS):
        q_grp = jnp.concatenate([qT(slice((g * GQA_GROUP + j) * HEAD_DIM_ATT, (g * GQA_GROUP + j + 1) * HEAD_DIM_ATT),
                                    cols) for j in range(GQA_GROUP)], axis=1)
        scores.append(jnp.dot(k2[:, gd(g)], q_grp, preferred_element_type=F32))
    yield
    probs, rden, pfar = [], [], []
    for h in range(N_HEADS_ATT):
        g, j = divmod(h, GQA_GROUP)
        s_prev = scores[g][:W, j * W:(j + 1) * W]
        s_cur = scores[g][W:, j * W:(j + 1) * W]
        s = jnp.where(upper, s_prev, s_cur) + bias_ref[h, pl.ds(off, W), :]
        far_bias = jnp.where(is_first, -jnp.inf, sink_ref[1, h])
        s_far = jnp.sum(jnp.where(eye, s_prev, 0.0), axis=0, keepdims=True) + far_bias
        sink = sink_ref[0, h]
        m = jnp.maximum(jnp.maximum(jnp.max(s, axis=0, keepdims=True), s_far), sink)
        p = jnp.exp(s - m)
        p_far = jnp.exp(s_far - m)
        rden.append(1.0 / (jnp.sum(p, axis=0, keepdims=True) + p_far + jnp.exp(sink - m)))
        pfar.append(p_far)
        zero = jnp.zeros_like(p)
        probs.append(jnp.concatenate([jnp.where(upper, p, zero), jnp.where(upper, zero, p)], axis=0).astype(BF16))
    yield
    outs, ssq = [], jnp.zeros((1, W), F32)
    for g in range(N_KV_HEADS):
        p_grp = jnp.concatenate(probs[g * GQA_GROUP:(g + 1) * GQA_GROUP], axis=1)
        o_grp = jnp.dot(vT2[gd(g), :], p_grp, preferred_element_type=F32)
        v_far = vT2[gd(g), :W].astype(F32)
        for j in range(GQA_GROUP):
            h = g * GQA_GROUP + j
            o = (o_grp[:, j * W:(j + 1) * W] + v_far * pfar[h]) * rden[h]
            outs.append(o)
            ssq = ssq + jnp.sum(o * o, axis=0, keepdims=True)
    inv = lax.rsqrt(ssq * (1.0 / ATT_W) + EPS)
    for h in range(N_HEADS_ATT):
        rows = slice(h * HEAD_DIM_ATT, (h + 1) * HEAD_DIM_ATT)
        write_out(rows, cols, (outs[h] * inv * gb_ref[rows, :]).astype(BF16))
    yield


def _attn_blocks(qT, k_all, vT_all, first, bias_ref, sink_ref, gb_ref, write_out, qb):
    gens = [_attn_block(i, qT, k_all, vT_all, first, bias_ref, sink_ref, gb_ref, write_out) for i in range(qb)]
    next(gens[0])
    next(gens[0])
    for i in range(qb):
        if i + 1 < qb:
            next(gens[i + 1])
        yield
        next(gens[i])
        if i + 1 < qb:
            next(gens[i + 1])


PROJ_TOKEN_CHUNK = 256
PROJ_ROW_SPLIT = 1024


def _proj_attn_kernel(x_ref, g_ref, wn_ref, wt_ref, bgc_ref, bias_ref, sink_ref, gb_ref, *refs,
                      tm, tiles_per_seq, n_cast):
    cast_in, refs = refs[:n_cast], refs[n_cast:]
    kvw_ref, km_ref, qmT_ref, vmT_ref, omT_ref, gT_ref, mix_ref = refs[:7]
    cast_out, (q_s, k_s, vT_s, hk_s, hv_s) = refs[7:7 + n_cast], refs[7 + n_cast:]
    for src, dst in zip(cast_in, cast_out):
        dst[...] = src[...].astype(BF16)
    s = pl.program_id(0)
    slot = s % 2
    prev = 1 - slot
    qb = tm // WINDOW

    @pl.when(s == 0)
    def _():
        q_s[...] = jnp.zeros_like(q_s)
        k_s[...] = jnp.zeros_like(k_s)
        vT_s[...] = jnp.zeros_like(vT_s)
        hk_s[...] = jnp.zeros_like(hk_s)
        hv_s[...] = jnp.zeros_like(hv_s)

    first = (s + tiles_per_seq - 1) % tiles_per_seq == 0
    k_all = jnp.concatenate([hk_s[...], k_s[prev]], axis=0)
    vT_all = jnp.concatenate([hv_s[...], vT_s[prev]], axis=1)

    def write_out(rows, cols, val):
        mix_ref[rows, cols] = val

    blocks = _attn_blocks(lambda rows, cols: q_s[prev, rows, cols], k_all, vT_all, first, bias_ref, sink_ref, gb_ref,
                          write_out, qb)
    next(blocks)
    hb = _rms_rows(x_ref[...], g_ref[...]).astype(BF16)

    def put_q(lo, tc, z):
        q_s[slot, lo:lo + z.shape[0], tc] = (z * (HEAD_DIM_ATT ** -0.5)).astype(BF16)

    def put_v(lo, tc, z):
        vT_s[slot, lo:lo + z.shape[0], tc] = z.astype(BF16)

    def put_qm(lo, tc, z):
        qmT_ref[lo:lo + z.shape[0], tc] = z.astype(BF16)

    def put_vm(lo, tc, z):
        vmT_ref[lo:lo + z.shape[0], tc] = z.astype(BF16)

    def put_om(lo, tc, z):
        omT_ref[lo:lo + z.shape[0], tc] = jax.nn.sigmoid(z)

    def put_g(lo, tc, z):
        g_t = z[:N_GATES, :] + bgc_ref[...]
        row = lax.broadcasted_iota(jnp.int32, g_t.shape, 0)
        gT_ref[:, tc] = jnp.where(row < N_HEADS_MLSTM, g_t, _log_sigmoid(g_t))

    segments = [(_T_QA, _T_VA, put_q), (_T_VA, _T_QM, put_v), (_T_QM, _T_VM, put_qm), (_T_VM, _T_OM, put_vm),
                (_T_OM, _T_G, put_om), (_T_G, _T_ROWS, put_g)]

    def proj_piece(lo, hi, tc):
        z = lax.dot_general(wt_ref[lo:hi, :], hb[tc, :], (((1,), (1,)), ((), ())), preferred_element_type=F32)
        for a, b, put in segments:
            if a < hi and lo < b:
                put(max(lo, a) - a, tc, z[max(lo, a) - lo:min(hi, b) - lo, :])

    z_n = jnp.dot(hb, wn_ref[:, :_N_G], preferred_element_type=F32)
    kvw_ref[...] = z_n[tm - WINDOW:, _N_KV:_N_KV + 2 * KV_W]
    km_ref[...] = (z_n[:, _N_KM:_N_KM + QK_M_W] * (DK_MLSTM ** -0.5)).astype(BF16)
    k_s[slot] = z_n[:, _N_KV:_N_KV + KV_W].astype(BF16)
    pieces = [(lo, hi, slice(c, c + PROJ_TOKEN_CHUNK))
              for c in range(0, tm, PROJ_TOKEN_CHUNK) for lo, hi in ((0, PROJ_ROW_SPLIT), (PROJ_ROW_SPLIT, _T_ROWS))]
    proj_piece(*pieces.pop(0))
    for _ in blocks:
        if pieces:
            proj_piece(*pieces.pop(0))
    for piece in pieces:
        proj_piece(*piece)
    hk_s[...] = k_s[prev, tm - WINDOW:, :]
    hv_s[...] = vT_s[prev, :, tm - WINDOW:]


def _proj_attn(x2d, g_norm, w_n, w_t, bg_col, bias, sinks, g_b, casts, seq, tm):
    n = x2d.shape[0]
    n_tiles = n // tm
    tiles_per_seq = seq // tm
    cur = lambda s: jnp.minimum(s, n_tiles - 1)
    prv = lambda s: jnp.maximum(s - 1, 0)
    row = lambda w: pl.BlockSpec((tm, w), lambda s: (cur(s), 0))
    colb = lambda r: pl.BlockSpec((r, tm), lambda s: (0, cur(s)))
    sds = jax.ShapeDtypeStruct
    cast_in, cast_out, cast_shape = [], [], []
    for arr, first, count in casts:
        chunk = count // n_tiles
        assert chunk * n_tiles == count and chunk % BF16_ROWS == 0 and first % chunk == 0
        cols = arr.shape[1]
        cast_in.append(pl.BlockSpec((chunk, cols), lambda s, o=first // chunk: (o + cur(s), 0)))
        cast_out.append(pl.BlockSpec((chunk, cols), lambda s: (cur(s), 0)))
        cast_shape.append(sds((count, cols), BF16))
    return pl.pallas_call(
        functools.partial(_proj_attn_kernel, tm=tm, tiles_per_seq=tiles_per_seq, n_cast=len(casts)),
        grid=(n_tiles + 1,),
        in_specs=[row(D_MODEL), _const_spec((1, D_MODEL)), _const_spec(w_n.shape), _const_spec(w_t.shape),
                  _const_spec((N_GATES, 1)), _const_spec(bias.shape), pl.BlockSpec(memory_space=pltpu.SMEM),
                  _const_spec(g_b.shape)] + cast_in,
        out_specs=[pl.BlockSpec((WINDOW, 2 * KV_W), lambda s: (cur(s) // tiles_per_seq, 0)),
                   row(QK_M_W), colb(QK_M_W), colb(MLSTM_W), colb(MLSTM_W), colb(N_GATES),
                   pl.BlockSpec((ATT_W, tm), lambda s: (0, prv(s)))] + cast_out,
        out_shape=[sds((n // seq * WINDOW, 2 * KV_W), F32), sds((n, QK_M_W), BF16), sds((QK_M_W, n), BF16),
                   sds((MLSTM_W, n), BF16),
                   sds((MLSTM_W, n), F32), sds((N_GATES, n), F32), sds((ATT_W, n), BF16)] + cast_shape,
        scratch_shapes=[pltpu.VMEM((2, ATT_W, tm), BF16), pltpu.VMEM((2, tm, KV_W), BF16),
                        pltpu.VMEM((2, KV_W, tm), BF16), pltpu.VMEM((WINDOW, KV_W), BF16),
                        pltpu.VMEM((KV_W, WINDOW), BF16)],
        compiler_params=pltpu.CompilerParams(dimension_semantics=("arbitrary",), vmem_limit_bytes=VMEM_LIMIT),
        name="proj_attn",
    )(x2d, g_norm, w_n, w_t, bg_col, bias, sinks, g_b, *[arr for arr, _, _ in casts])


def _attn_sample_stages(q_ref, kvf_ref, ck_ref, cv_ref, bias_ref, sink_ref, g_ref, out_ref, kw_ref, vw_ref,
                        *, bb, t, w, kpad):
    ck = ck_ref[...]
    cv = cv_ref[...]
    k_new = kvf_ref[:, :KV_W].reshape(bb, t, KV_W)
    v_new = kvf_ref[:, KV_W:].reshape(bb, t, KV_W)
    kw_ref[:, :w - t, :] = ck[:, t:, :]
    kw_ref[:, w - t:, :] = k_new
    vw_ref[:, :w - t, :] = cv[:, t:, :]
    vw_ref[:, w - t:, :] = v_new
    zpad = jnp.zeros((bb, kpad - w - t, KV_W), F32)
    kk = jnp.concatenate([ck, k_new, zpad], axis=1).astype(BF16)
    vv = jnp.concatenate([cv, v_new, zpad], axis=1).astype(BF16)
    q3 = q_ref[...].astype(F32).reshape(bb, t, ATT_W)
    yield
    gd = lambda g: slice(g * HEAD_DIM_ATT, (g + 1) * HEAD_DIM_ATT)
    scores = []
    for g in range(N_KV_HEADS):
        qg = jnp.concatenate([q3[:, :, (g * GQA_GROUP + i) * HEAD_DIM_ATT:(g * GQA_GROUP + i + 1) * HEAD_DIM_ATT]
                              for i in range(GQA_GROUP)], axis=1).astype(BF16)
        scores.append(jnp.einsum('bqd,bkd->bqk', qg, kk[:, :, gd(g)], preferred_element_type=F32))
    yield
    probs = []
    for g in range(N_KV_HEADS):
        s = scores[g] + bias_ref[g]
        sink = sink_ref[g]
        m = jnp.maximum(jnp.max(s, axis=-1, keepdims=True), sink)
        p = jnp.exp(s - m)
        denom = jnp.sum(p, axis=-1, keepdims=True) + jnp.exp(sink - m)
        probs.append((p * (1.0 / denom)).astype(BF16))
        yield
    outs = []
    ssq = jnp.zeros((bb, 1, t, 1), F32)
    for g in range(N_KV_HEADS):
        o = jnp.einsum('bqk,bkd->bqd', probs[g], vv[:, :, gd(g)], preferred_element_type=F32)
        o = o.reshape(bb, GQA_GROUP, t, HEAD_DIM_ATT)
        outs.append(o)
        ssq = ssq + jnp.sum(jnp.sum(o * o, axis=-1, keepdims=True), axis=1, keepdims=True)
    yield
    inv = lax.rsqrt(ssq * (1.0 / ATT_W) + EPS)
    for g in range(N_KV_HEADS):
        y = outs[g] * inv
        for i in range(GQA_GROUP):
            sl = slice((g * GQA_GROUP + i) * HEAD_DIM_ATT, (g * GQA_GROUP + i + 1) * HEAD_DIM_ATT)
            out_ref[:, sl] = (y[:, i].reshape(bb * t, HEAD_DIM_ATT) * g_ref[:, sl]).astype(BF16)


SEQ_PER_BLOCK = 16


def _seg_scan_rows(x, seg, forward):
    n = x.shape[0]
    pos = lax.broadcasted_iota(jnp.int32, x.shape, 0) % seg
    k = 1
    while k < seg:
        if forward:
            x = jnp.maximum(x, jnp.where(pos >= k, pltpu.roll(x, k, axis=0), -jnp.inf))
        else:
            x = jnp.maximum(x, jnp.where(pos < seg - k, pltpu.roll(x, n - k, axis=0), -jnp.inf))
        k *= 2
    return x


def _mlstm_blk_stages(qk_ref, v_ref, om_ref, gt_ref, gm_ref, c0_ref, n0_ref, m0_ref,
                      mix_ref, c_out, n_out, m_out, *, t):
    S = SEQ_PER_BLOCK
    L = S * t
    hi = lax.Precision.HIGHEST
    ti = lax.broadcasted_iota(jnp.int32, (L, L), 0)
    si = lax.broadcasted_iota(jnp.int32, (L, L), 1)
    same = (ti // t) == (si // t)
    causal = same & (si <= ti)
    tril_seg = causal.astype(F32)
    ones_seg = same.astype(F32)
    ones_col = (lax.broadcasted_iota(jnp.int32, (L, DV_MLSTM), 1) == 0).astype(BF16)
    bd_mask = (lax.broadcasted_iota(jnp.int32, (L, S * DK_MLSTM), 0) // t
               == lax.broadcasted_iota(jnp.int32, (L, S * DK_MLSTM), 1) // DK_MLSTM)
    bd_mask_t = (lax.broadcasted_iota(jnp.int32, (S * DK_MLSTM, L), 0) // DK_MLSTM
                 == lax.broadcasted_iota(jnp.int32, (S * DK_MLSTM, L), 1) // t)
    heads = range(N_HEADS_MLSTM)
    hk = lambda h: slice(h * DK_MLSTM, (h + 1) * DK_MLSTM)
    hkk = lambda h: slice(QK_M_W + h * DK_MLSTM, QK_M_W + (h + 1) * DK_MLSTM)
    hv = lambda h: slice(h * DV_MLSTM, (h + 1) * DV_MLSTM)

    gt = gt_ref[...]
    per_token = lambda x: jnp.broadcast_to(x[:, None, :], (S, t, x.shape[-1])).reshape(L, x.shape[-1])
    m_prev = per_token(m0_ref[...])
    n_rows = per_token(n0_ref[...])
    b = jnp.dot(tril_seg, gt, precision=hi, preferred_element_type=F32)
    b_end = jnp.dot(ones_seg, gt, precision=hi, preferred_element_type=F32)
    yield
    a = pltpu.roll(gt, N_HEADS_MLSTM, axis=1) - b
    big_m = jnp.maximum(m_prev, _seg_scan_rows(a, t, True))
    inter = jnp.exp(m_prev - big_m)
    em = jnp.exp(-(b + big_m))
    m_end = _seg_scan_rows(big_m, t, False)
    w = jnp.exp(a - m_end)
    decay = jnp.exp(m_prev - m_end)
    m_out[...] = (b_end + m_end).reshape(S, t, LANES)[:, 0, :]
    a_t = a.T
    yield
    q = [qk_ref[:, hk(h)] for h in heads]
    k = [qk_ref[:, hkk(h)] for h in heads]
    v_ext = [jnp.concatenate([v_ref[:, hv(h)], ones_col], axis=1) for h in heads]
    qk = [lax.dot_general(q[h], k[h], (((1,), (1,)), ((), ())), preferred_element_type=F32) for h in heads]
    c_all = [c0_ref[:, h].reshape(S * DK_MLSTM, DV_MLSTM) for h in heads]
    q_bd = []
    for h in heads:
        q2 = jnp.concatenate([q[h], q[h]], axis=1)
        q_bd.append(jnp.where(bd_mask, jnp.concatenate([q2] * (S // 2), axis=1), jnp.zeros((), BF16)))
    qc = [jnp.dot(q_bd[h], c_all[h].astype(BF16), preferred_element_type=F32) for h in heads]
    yield
    s = []
    for h in heads:
        ln = _GATE0 + h
        d = jnp.exp(jnp.where(causal, a_t[ln:ln + 1, :] - big_m[:, ln:ln + 1], -jnp.inf))
        s.append((qk[h] * d).astype(BF16))
    yield
    sv = [jnp.dot(s[h], v_ext[h], preferred_element_type=F32) for h in heads]
    kw, kw_bd = [], []
    for h in heads:
        ln = _GATE0 + h
        kw_f = k[h].astype(F32) * w[:, ln:ln + 1]
        kw.append(kw_f.astype(BF16))
        kw_bd.append(jnp.where(bd_mask_t, jnp.concatenate([kw_f.T] * S, axis=0), 0.0).astype(BF16))
    upd = [jnp.dot(kw_bd[h], v_ref[:, hv(h)], preferred_element_type=F32) for h in heads]
    seq_of_tok = lax.broadcasted_iota(jnp.int32, (S, L), 1) // t == lax.broadcasted_iota(jnp.int32, (S, L), 0)
    first_tok = lax.broadcasted_iota(jnp.int32, (S, L), 1) == lax.broadcasted_iota(jnp.int32, (S, L), 0) * t
    n_upd = [jnp.dot(seq_of_tok.astype(BF16), kw[h], preferred_element_type=F32) for h in heads]
    decay_seq = jnp.dot(first_tok.astype(F32), decay, precision=hi, preferred_element_type=F32)
    yield
    for h in heads:
        ln = _GATE0 + h
        inter_h = inter[:, ln:ln + 1]
        qn = jnp.sum(q[h].astype(F32) * n_rows[:, hk(h)], axis=1, keepdims=True)
        den = inter_h * qn + sv[h][:, DV_MLSTM:DV_MLSTM + 1]
        r = 1.0 / jnp.maximum(jnp.abs(den), em[:, ln:ln + 1])
        tt = om_ref[:, hv(h)] * (inter_h * qc[h] + sv[h][:, :DV_MLSTM])
        msq = jnp.mean(tt * tt, axis=1, keepdims=True)
        scale = r * lax.rsqrt(r * r * msq + EPS)
        mix_ref[:, hv(h)] = (tt * scale * gm_ref[:, hv(h)]).astype(BF16)
        dec = decay[:, ln:ln + 1].reshape(S, t, 1)[:, 0:1, :]
        c_out[:, h] = dec * c0_ref[:, h] + upd[h].reshape(S, DK_MLSTM, DV_MLSTM)
        n_out[:, hk(h)] = decay_seq[:, ln:ln + 1] * n0_ref[:, hk(h)] + n_upd[h]


def _sample_mixers_kernel(q_ref, kvf_ref, ck_ref, cv_ref, bias_ref, sink_ref, ga_ref,
                          qk_ref, v_ref, om_ref, gt_ref, gm_ref, c0_ref, n0_ref, m0_ref,
                          mixa_ref, kw_ref, vw_ref, mixm_ref, c_out, n_out, m_out, *, t, w, kpad):
    attn = _attn_sample_stages(q_ref, kvf_ref, ck_ref, cv_ref, bias_ref, sink_ref, ga_ref, mixa_ref, kw_ref, vw_ref,
                               bb=SEQ_PER_BLOCK, t=t, w=w, kpad=kpad)
    mlstm = _mlstm_blk_stages(qk_ref, v_ref, om_ref, gt_ref, gm_ref, c0_ref, n0_ref, m0_ref,
                              mixm_ref, c_out, n_out, m_out, t=t)
    for _ in itertools.zip_longest(attn, mlstm):
        pass


def _sample_mixers(qa, kvf, ck, cv, bias, sink_rows, g_att, qkm, vm, om, gt, g_m, c0, n0_seq, m0_seq, nseq, t):
    S = SEQ_PER_BLOCK
    L = S * t
    w = ck.shape[1]
    kpad = bias.shape[-1]
    rows = GQA_GROUP * t
    tok = lambda wd: pl.BlockSpec((L, wd), lambda i: (i, 0))
    cache = pl.BlockSpec((S, w, KV_W), lambda i: (i, 0, 0))
    st_c = pl.BlockSpec((S, N_HEADS_MLSTM, DK_MLSTM, DV_MLSTM), lambda i: (i, 0, 0, 0))
    st_n = pl.BlockSpec((S, QK_M_W), lambda i: (i, 0))
    st_m = pl.BlockSpec((S, LANES), lambda i: (i, 0))
    sds = jax.ShapeDtypeStruct
    return pl.pallas_call(
        functools.partial(_sample_mixers_kernel, t=t, w=w, kpad=kpad),
        grid=(nseq // S,),
        in_specs=[tok(ATT_W), tok(2 * KV_W), cache, cache,
                  _const_spec((N_KV_HEADS, rows, kpad)), _const_spec((N_KV_HEADS, rows, 1)), _const_spec((1, ATT_W)),
                  tok(2 * QK_M_W), tok(MLSTM_W), tok(MLSTM_W), tok(LANES), _const_spec((1, MLSTM_W)),
                  st_c, st_n, st_m],
        out_specs=[tok(ATT_W), cache, cache, tok(MLSTM_W), st_c, st_n, st_m],
        out_shape=[sds((nseq * t, ATT_W), BF16), sds((nseq, w, KV_W), F32), sds((nseq, w, KV_W), F32),
                   sds((nseq * t, MLSTM_W), BF16), sds((nseq, N_HEADS_MLSTM, DK_MLSTM, DV_MLSTM), F32),
                   sds((nseq, QK_M_W), F32), sds((nseq, LANES), F32)],
        compiler_params=pltpu.CompilerParams(dimension_semantics=("arbitrary",), vmem_limit_bytes=VMEM_LIMIT),
        name="sample_mixers",
    )(qa, kvf, ck, cv, bias, sink_rows, g_att, qkm, vm, om, gt, g_m, c0, n0_seq, m0_seq)


def _out_mlp_kernel(x_ref, a_ref, hm_ref, woa_ref, wom_ref, gmlp_ref, wup_ref, wdn_ref, gfin_ref, y_ref):
    x1 = (x_ref[...]
          + jnp.dot(a_ref[...], woa_ref[...], preferred_element_type=F32)
          + jnp.dot(hm_ref[...], wom_ref[...], preferred_element_type=F32))
    h = _rms_rows(x1, gmlp_ref[...]).astype(BF16)
    u = jnp.maximum(jnp.dot(h, wup_ref[...], preferred_element_type=F32), 0.0)
    y = x1 + jnp.dot((u * u).astype(BF16), wdn_ref[...], preferred_element_type=F32)
    y_ref[...] = _rms_rows(y, gfin_ref[...])


def _out_mlp(x2d, mix_a, mix_m, wo_a, wo_m, g_mlp, w_up, w_dn, g_fin, tm):
    n = x2d.shape[0]
    row = lambda w: pl.BlockSpec((tm, w), lambda i: (i, 0))
    return pl.pallas_call(
        _out_mlp_kernel,
        grid=(n // tm,),
        in_specs=[row(D_MODEL), row(ATT_W), row(MLSTM_W), _const_spec(wo_a.shape), _const_spec(wo_m.shape),
                  _const_spec((1, D_MODEL)), _const_spec(w_up.shape), _const_spec(w_dn.shape),
                  _const_spec((1, D_MODEL))],
        out_specs=row(D_MODEL),
        out_shape=jax.ShapeDtypeStruct((n, D_MODEL), F32),
        compiler_params=pltpu.CompilerParams(dimension_semantics=("arbitrary",), vmem_limit_bytes=VMEM_LIMIT),
        name="out_mlp",
    )(x2d, mix_a, mix_m, wo_a, wo_m, g_mlp, w_up, w_dn, g_fin)


_CN_ROWS = DV_MLSTM + BF16_ROWS


def _prefix_max_lanes(x):
    n = x.shape[1]
    col = lax.broadcasted_iota(jnp.int32, x.shape, 1)
    k = 1
    while k < n:
        x = jnp.maximum(x, jnp.where(col >= k, pltpu.roll(x, k, axis=1), -jnp.inf))
        k *= 2
    return x


def _mlstm_stages(qT_ref, k_ref, vT_ref, omT_ref, gT_ref, gb_ref, write_mix, state, chunk, nsub):
    L = chunk
    i0 = lax.broadcasted_iota(jnp.int32, (L, L), 0)
    i1 = lax.broadcasted_iota(jnp.int32, (L, L), 1)
    causal_t = i0 <= i1
    triu = causal_t.astype(F32)
    ones_rows = (lax.broadcasted_iota(jnp.int32, (BF16_ROWS, L), 0) == 0).astype(BF16)
    zpad = jnp.zeros((LANES - 2 * N_GATES, L), F32)
    cn, m_r = state['cn'], state['m_r']
    hi = lax.Precision.HIGHEST
    heads = range(N_HEADS_MLSTM)
    subs = range(nsub)
    col_of = lambda j: slice(j * L, (j + 1) * L)
    hv = lambda h: slice(h * DV_MLSTM, (h + 1) * DV_MLSTM)
    hk = lambda h: slice(h * DK_MLSTM, (h + 1) * DK_MLSTM)
    a_r, b_r, pm_r = [], [], []
    for j in subs:
        g_t = gT_ref[:, col_of(j)]
        b_r.append(jnp.dot(g_t, triu, precision=hi, preferred_element_type=F32))
        a_r.append(pltpu.roll(g_t, N_HEADS_MLSTM, axis=0) - b_r[j])
    yield
    kq = [[jnp.dot(k_ref[col_of(j), hk(h)], qT_ref[hk(h), col_of(j)], preferred_element_type=F32)
           for h in heads] for j in subs]
    vT_ext = [[jnp.concatenate([vT_ref[hv(h), col_of(j)], ones_rows], axis=0) for h in heads] for j in subs]
    for j in subs:
        pm_r.append(_prefix_max_lanes(a_r[j]))
    yield
    big_m, inter, em, decay, cols_t = [], [], [], [], []
    for j in subs:
        big_m.append(jnp.maximum(m_r, pm_r[j]))
        inter.append(jnp.exp(m_r - big_m[j]))
        em.append(jnp.exp(-(b_r[j] + big_m[j])))
        m_end = big_m[j][:, L - 1:L]
        w_r = jnp.exp(a_r[j] - m_end)
        decay.append(jnp.exp(m_r[:, 0:1] - m_end))
        m_r = jnp.broadcast_to(b_r[j][:, L - 1:L] + m_end, (N_GATES, L))
        cols_t.append(jnp.concatenate([a_r[j], w_r, zpad], axis=0).T)
    sT = [[None] * N_HEADS_MLSTM for _ in subs]
    kw = [[None] * N_HEADS_MLSTM for _ in subs]
    intra = [[None] * N_HEADS_MLSTM for _ in subs]
    upd = [[None] * N_HEADS_MLSTM for _ in subs]
    for j in subs:
        for h in heads:
            r = _GATE0 + h
            d = jnp.exp(jnp.where(causal_t, cols_t[j][:, r:r + 1] - big_m[j][r:r + 1, :], -jnp.inf))
            sT[j][h] = (kq[j][h] * d).astype(BF16)
            w_col = cols_t[j][:, N_GATES + r:N_GATES + r + 1]
            kw[j][h] = (k_ref[col_of(j), hk(h)].astype(F32) * w_col).astype(BF16)
        for h in heads:
            intra[j][h] = jnp.dot(vT_ext[j][h], sT[j][h], preferred_element_type=F32)
            upd[j][h] = jnp.dot(vT_ext[j][h], kw[j][h], preferred_element_type=F32)
        yield
    for j in subs:
        qc = [jnp.dot(cn[h].astype(BF16), qT_ref[hk(h), col_of(j)], preferred_element_type=F32) for h in heads]
        for h in heads:
            r = _GATE0 + h
            nd = inter[j][r:r + 1, :] * qc[h] + intra[j][h]
            rr = 1.0 / jnp.maximum(jnp.abs(nd[DV_MLSTM:DV_MLSTM + 1, :]), em[j][r:r + 1, :])
            t = omT_ref[hv(h), col_of(j)] * nd[:DV_MLSTM, :]
            msq = jnp.mean(t * t, axis=0, keepdims=True)
            scale = rr * lax.rsqrt(rr * rr * msq + EPS)
            write_mix(hv(h), col_of(j), (t * scale * gb_ref[hv(h), :]).astype(BF16))
            cn[h] = decay[j][r:r + 1, :] * cn[h] + upd[j][h]
        yield
    state['m_r'] = m_r


def _mlstm_mlp_kernel(x_ref, a_ref, qT_ref, k_ref, vT_ref, omT_ref, gT_ref, gb_ref,
                      woa_ref, wom_ref, gmlp_ref, wup_ref, wdn_ref, gfin_ref,
                      y_ref, c_out, n_out, m_out, mix_s, cn_s, mr_s,
                      *, chunk, nsub, tiles_per_seq, n_tiles):
    s = pl.program_id(0)
    slot = s % 2

    @pl.when(s == 0)
    def _():
        mix_s[...] = jnp.zeros_like(mix_s)

    @pl.when(s % tiles_per_seq == 0)
    def _():
        cn_s[...] = jnp.zeros_like(cn_s)
        mr_s[...] = jnp.zeros_like(mr_s)

    state = {'cn': [cn_s[h] for h in range(N_HEADS_MLSTM)], 'm_r': mr_s[...]}

    def write_mix(rows, cols, val):
        mix_s[slot, rows, cols] = val

    stages = _mlstm_stages(qT_ref, k_ref, vT_ref, omT_ref, gT_ref, gb_ref, write_mix, state, chunk, nsub)
    next(stages)
    tdims = (((0,), (0,)), ((), ()))
    x1 = (x_ref[...]
          + lax.dot_general(a_ref[...], woa_ref[...], tdims, preferred_element_type=F32)
          + lax.dot_general(mix_s[1 - slot], wom_ref[...], tdims, preferred_element_type=F32))
    next(stages)
    h = _rms_rows(x1, gmlp_ref[...]).astype(BF16)
    u = jnp.maximum(jnp.dot(h, wup_ref[...], preferred_element_type=F32), 0.0)
    for _ in stages:
        pass
    y = x1 + jnp.dot((u * u).astype(BF16), wdn_ref[...], preferred_element_type=F32)
    y_ref[...] = _rms_rows(y, gfin_ref[...])
    for hh in range(N_HEADS_MLSTM):
        cn_s[hh] = state['cn'][hh]
    mr_s[...] = state['m_r']

    @pl.when((s % tiles_per_seq == tiles_per_seq - 1) & (s < n_tiles))
    def _():
        for hh in range(N_HEADS_MLSTM):
            c_out[0, hh] = cn_s[hh, :DV_MLSTM, :].T
            n_out[0, hh] = cn_s[hh, DV_MLSTM:DV_MLSTM + 1, :]
        m_out[0] = mr_s[...]


def _mlstm_mlp(x2d, mix_aT, qmT, km, vmT, omT, gT, g_b, wo_a, wo_m, g_mlp, w_up, w_dn, g_fin,
               nseq, t, chunk, nsub):
    tm = chunk * nsub
    n_tiles = (nseq * t) // tm
    tiles_per_seq = t // tm
    cur = lambda s: jnp.minimum(s, n_tiles - 1)
    prv = lambda s: jnp.maximum(s - 1, 0)
    colb = lambda r: pl.BlockSpec((r, tm), lambda s: (0, cur(s)))
    st = lambda a, d: pl.BlockSpec((1, N_HEADS_MLSTM, a, d), lambda s: (cur(s) // tiles_per_seq, 0, 0, 0))
    sds = jax.ShapeDtypeStruct
    kern = functools.partial(_mlstm_mlp_kernel, chunk=chunk, nsub=nsub,
                             tiles_per_seq=tiles_per_seq, n_tiles=n_tiles)
    return pl.pallas_call(
        kern,
        grid=(n_tiles + 1,),
        in_specs=[pl.BlockSpec((tm, D_MODEL), lambda s: (prv(s), 0)),
                  pl.BlockSpec((ATT_W, tm), lambda s: (0, prv(s))),
                  colb(QK_M_W), pl.BlockSpec((tm, QK_M_W), lambda s: (cur(s), 0)), colb(MLSTM_W), colb(MLSTM_W),
                  colb(N_GATES),
                  _const_spec(g_b.shape), _const_spec(wo_a.shape), _const_spec(wo_m.shape),
                  _const_spec((1, D_MODEL)), _const_spec(w_up.shape), _const_spec(w_dn.shape),
                  _const_spec((1, D_MODEL))],
        out_specs=[pl.BlockSpec((tm, D_MODEL), lambda s: (prv(s), 0)),
                   st(DK_MLSTM, DV_MLSTM), st(1, DK_MLSTM),
                   pl.BlockSpec((1, N_GATES, chunk), lambda s: (cur(s) // tiles_per_seq, 0, 0))],
        out_shape=[sds((nseq * t, D_MODEL), F32), sds((nseq, N_HEADS_MLSTM, DK_MLSTM, DV_MLSTM), F32),
                   sds((nseq, N_HEADS_MLSTM, 1, DK_MLSTM), F32), sds((nseq, N_GATES, chunk), F32)],
        scratch_shapes=[pltpu.VMEM((2, MLSTM_W, tm), BF16),
                        pltpu.VMEM((N_HEADS_MLSTM, _CN_ROWS, DK_MLSTM), F32),
                        pltpu.VMEM((N_GATES, chunk), F32)],
        compiler_params=pltpu.CompilerParams(dimension_semantics=("arbitrary",), vmem_limit_bytes=VMEM_LIMIT),
        name="mlstm_mlp",
    )(x2d, mix_aT, qmT, km, vmT, omT, gT, g_b, wo_a, wo_m, g_mlp, w_up, w_dn, g_fin)


def kernel(x_prompt, x_sample, cache_k_win, cache_v_win, state_C, state_n, state_m, rel_bias, norm_mix, w_in,
           b_gates, attn_sinks, norm_attn_out, norm_mlstm_out, w_out, norm_mlp, w_up, w_down, norm_final):
    depth = w_in.shape[0]
    assert depth == 1, "single-layer trunk"
    bp, sp = x_prompt.shape[:2]
    bs, ts = x_sample.shape[:2]
    wc = cache_k_win.shape[2]
    l = 0
    chunk = 256

    wl = w_in[l]
    cols = lambda lo, n: wl[:, lo:lo + n]
    w_n = jnp.concatenate([cols(_O_KV, 2 * KV_W), cols(_O_QKM + QK_M_W, QK_M_W),
                           jnp.pad(cols(_O_G, N_GATES), ((0, 0), (0, LANES - N_GATES)))], axis=1).astype(BF16)
    w_t = jnp.concatenate([cols(_O_QA, ATT_W), cols(_O_KV + KV_W, KV_W), cols(_O_QKM, QK_M_W), cols(_O_VM, MLSTM_W),
                           cols(_O_OM, MLSTM_W), jnp.pad(cols(_O_G, N_GATES), ((0, 0), (0, BF16_ROWS - N_GATES)))],
                          axis=1).T.astype(BF16)
    bg_row = jnp.pad(b_gates[l], (0, LANES - N_GATES)).reshape(1, LANES)
    bg_col = b_gates[l].reshape(N_GATES, 1)
    g_mix = norm_mix[l].reshape(1, D_MODEL)
    g_att = norm_attn_out[l].reshape(1, ATT_W)
    g_m = norm_mlstm_out[l].reshape(1, MLSTM_W)
    g_mlp = norm_mlp[l].reshape(1, D_MODEL)
    g_fin = norm_final.reshape(1, D_MODEL)
    gb_att = jnp.broadcast_to(norm_attn_out[l].reshape(ATT_W, 1), (ATT_W, WINDOW))
    gb_m = jnp.broadcast_to(norm_mlstm_out[l].reshape(MLSTM_W, 1), (MLSTM_W, chunk))
    sinks = attn_sinks[l]

    xp = x_prompt.reshape(bp * sp, D_MODEL)
    bias_t = _bias_table(rel_bias, _prompt_buckets_t())
    far_bucket = int(_t5_bucket_np(np.array(WINDOW)))
    sink_far = jnp.stack([sinks, rel_bias[far_bucket]])
    casts = [(w_out[l], 0, ATT_W), (w_out[l], ATT_W, MLSTM_W), (w_up[l], 0, D_MODEL), (w_down[l], 0, D_FF)]
    kvw, km, qmT, vmT, omT, gT, mix_a, wo_a, wo_m, wup, wdn = _proj_attn(
        xp, g_mix, w_n, w_t, bg_col, bias_t, sink_far, gb_att, casts, sp, tm=1024)
    y_p, c_p, n_p, m_p = _mlstm_mlp(xp, mix_a, qmT, km, vmT, omT, gT, gb_m, wo_a, wo_m, g_mlp, wup, wdn, g_fin,
                                    bp, sp, chunk=chunk, nsub=512 // chunk)
    m_p = m_p[:, _GATE0:_GATE0 + N_HEADS_MLSTM, 0]
    assert sp >= WINDOW
    k_win_p = kvw[:, :KV_W].reshape(1, bp, WINDOW, N_KV_HEADS, HEAD_DIM_ATT)
    v_win_p = kvw[:, KV_W:].reshape(1, bp, WINDOW, N_KV_HEADS, HEAD_DIM_ATT)

    assert math.gcd(ts, 64) == ts and bs % SEQ_PER_BLOCK == 0, "sample group: one mLSTM chunk per sequence"
    kpad = ((wc + ts + BF16_ROWS - 1) // BF16_ROWS) * BF16_ROWS
    bias_s = _bias_table(rel_bias, _sample_buckets(ts, wc, kpad))
    bias_s = bias_s.reshape(N_KV_HEADS, GQA_GROUP * ts, kpad)
    sink_rows = jnp.repeat(sinks, ts).reshape(N_KV_HEADS, GQA_GROUP * ts, 1)
    xs = x_sample.reshape(bs * ts, D_MODEL)
    qa, kvf_s, qkm, vm, om, gt = _in_proj_s(xs, g_mix, w_n, w_t, bg_row, tm=512)
    ck = cache_k_win[l].reshape(bs, wc, KV_W)
    cv = cache_v_win[l].reshape(bs, wc, KV_W)
    n0 = state_n[l].reshape(bs, QK_M_W)
    m0 = jnp.pad(state_m[l], ((0, 0), (_GATE0, LANES - _GATE0 - N_HEADS_MLSTM)))
    mix_a_s, kw_s, vw_s, mix_m_s, c_s, n_s, m_seq = _sample_mixers(
        qa, kvf_s, ck, cv, bias_s, sink_rows, g_att, qkm, vm, om, gt, g_m, state_C[l], n0, m0, bs, ts)
    m_s = m_seq[:, _GATE0:_GATE0 + N_HEADS_MLSTM]
    y_s = _out_mlp(xs, mix_a_s, mix_m_s, wo_a, wo_m, g_mlp, wup, wdn, g_fin, tm=512)

    return (y_p.reshape(bp, sp, D_MODEL), y_s.reshape(bs, ts, D_MODEL),
            k_win_p, v_win_p,
            c_p[None], n_p.reshape(1, bp, N_HEADS_MLSTM, DK_MLSTM), m_p.reshape(1, bp, N_HEADS_MLSTM),
            kw_s.reshape(1, bs, wc, N_KV_HEADS, HEAD_DIM_ATT), vw_s.reshape(1, bs, wc, N_KV_HEADS, HEAD_DIM_ATT),
            c_s[None], n_s.reshape(1, bs, N_HEADS_MLSTM, DK_MLSTM), m_s.reshape(1, bs, N_HEADS_MLSTM))
```

```python
import functools
import itertools
import math

import numpy as np
import jax
import jax.numpy as jnp
from jax import lax
from jax.experimental import pallas as pl
from jax.experimental.pallas import tpu as pltpu

D_MODEL = 1024
N_HEADS_ATT = 8
N_KV_HEADS = 2
GQA_GROUP = N_HEADS_ATT // N_KV_HEADS
HEAD_DIM_ATT = 64
ATT_W = N_HEADS_ATT * HEAD_DIM_ATT
KV_W = N_KV_HEADS * HEAD_DIM_ATT
WINDOW = 128
NUM_BUCKETS = 32
MAX_DISTANCE = 128
N_HEADS_MLSTM = 4
DV_MLSTM = 128
DK_MLSTM = 64
MLSTM_W = N_HEADS_MLSTM * DV_MLSTM
QK_M_W = N_HEADS_MLSTM * DK_MLSTM
D_FF = 4 * D_MODEL
EPS = 1e-6
N_GATES = 2 * N_HEADS_MLSTM
LANES = 128
BF16_ROWS = 16
VMEM_LIMIT = 56 * 1024 * 1024
PROJ_TILE = 1024
MLP_TILE = 512
MLSTM_CHUNK = 256
SAMPLE_TILE = 512

F32 = jnp.float32
BF16 = jnp.bfloat16

_O_QA = 0
_O_KV = _O_QA + ATT_W
_O_QKM = _O_KV + 2 * KV_W
_O_VM = _O_QKM + 2 * QK_M_W
_O_OM = _O_VM + MLSTM_W
_O_G = _O_OM + MLSTM_W
_T_QA = 0
_T_VA = _T_QA + ATT_W
_T_QM = _T_VA + KV_W
_T_VM = _T_QM + QK_M_W
_T_OM = _T_VM + MLSTM_W
_T_G = _T_OM + MLSTM_W
_T_ROWS = _T_G + BF16_ROWS
_N_KV = 0
_N_KM = _N_KV + 2 * KV_W
_N_G = _N_KM + QK_M_W
_GATE0 = N_HEADS_MLSTM


def _const_spec(shape):
    nd = len(shape)
    return pl.BlockSpec(shape, lambda *_: (0,) * nd, pipeline_mode=pl.Buffered(1))


def _log_sigmoid(g):
    return jnp.minimum(g, 0.0) - jnp.log1p(jnp.exp(-jnp.abs(g)))


def _rms_rows(x, gain):
    return x * lax.rsqrt(jnp.mean(x * x, axis=-1, keepdims=True) + EPS) * gain


def _in_proj_s_kernel(x_ref, g_ref, wn_ref, wt_ref, bgr_ref, qa_ref, kvf_ref, qkm_ref, vm_ref, om_ref, gt_ref):
    hb = _rms_rows(x_ref[...], g_ref[...]).astype(BF16)
    z_n = jnp.dot(hb, wn_ref[...], preferred_element_type=F32)
    z_t = lax.dot_general(hb, wt_ref[...], (((1,), (1,)), ((), ())), preferred_element_type=F32)
    qa_ref[...] = (z_t[:, _T_QA:_T_QA + ATT_W] * (HEAD_DIM_ATT ** -0.5)).astype(BF16)
    kvf_ref[:, :KV_W] = z_n[:, _N_KV:_N_KV + KV_W]
    kvf_ref[:, KV_W:] = z_t[:, _T_VA:_T_VA + KV_W]
    qkm_ref[:, :QK_M_W] = z_t[:, _T_QM:_T_QM + QK_M_W].astype(BF16)
    qkm_ref[:, QK_M_W:] = (z_n[:, _N_KM:_N_KM + QK_M_W] * (DK_MLSTM ** -0.5)).astype(BF16)
    vm_ref[...] = z_t[:, _T_VM:_T_VM + MLSTM_W].astype(BF16)
    om_ref[...] = jax.nn.sigmoid(z_t[:, _T_OM:_T_OM + MLSTM_W])
    g = z_n[:, _N_G:_N_G + LANES] + bgr_ref[...]
    col = lax.broadcasted_iota(jnp.int32, g.shape, 1)
    gt_ref[...] = jnp.where(col < N_HEADS_MLSTM, g, _log_sigmoid(g))


def _in_proj_s(x2d, g_norm, w_n, w_t, bg_row, tm):
    n = x2d.shape[0]
    row = lambda w: pl.BlockSpec((tm, w), lambda i: (i, 0))
    sds = jax.ShapeDtypeStruct
    return pl.pallas_call(
        _in_proj_s_kernel,
        grid=(n // tm,),
        in_specs=[row(D_MODEL), _const_spec((1, D_MODEL)), _const_spec(w_n.shape), _const_spec(w_t.shape),
                  _const_spec((1, LANES))],
        out_specs=[row(ATT_W), row(2 * KV_W), row(2 * QK_M_W), row(MLSTM_W), row(MLSTM_W), row(LANES)],
        out_shape=[sds((n, ATT_W), BF16), sds((n, 2 * KV_W), F32), sds((n, 2 * QK_M_W), BF16),
                   sds((n, MLSTM_W), BF16), sds((n, MLSTM_W), F32), sds((n, LANES), F32)],
        compiler_params=pltpu.CompilerParams(dimension_semantics=("arbitrary",), vmem_limit_bytes=VMEM_LIMIT),
        name="in_proj_s",
    )(x2d, g_norm, w_n, w_t, bg_row)


def _t5_bucket_np(dist):
    max_exact = NUM_BUCKETS // 2
    d = np.maximum(dist, 0)
    ratio = np.maximum(d, 1).astype(np.float32) / np.float32(max_exact)
    large = max_exact + (np.log(ratio) / np.float32(math.log(MAX_DISTANCE / max_exact))
                         * np.float32(NUM_BUCKETS - max_exact)).astype(np.int32)
    large = np.minimum(large, NUM_BUCKETS - 1)
    return np.where(d < max_exact, d, large).astype(np.int32)


def _bias_table_kernel(rb_ref, bucket_ref, out_ref):
    bucket = bucket_ref[...]
    hit = [bucket == j for j in range(NUM_BUCKETS)]
    for h in range(N_HEADS_ATT):
        acc = jnp.full(bucket.shape, -jnp.inf, F32)
        for j in range(NUM_BUCKETS):
            acc = jnp.where(hit[j], rb_ref[j, h], acc)
        out_ref[h] = acc


def _bias_table(rel_bias, bucket):
    r, c = bucket.shape
    return pl.pallas_call(
        _bias_table_kernel,
        in_specs=[pl.BlockSpec(memory_space=pltpu.SMEM), pl.BlockSpec(memory_space=pltpu.VMEM)],
        out_specs=pl.BlockSpec(memory_space=pltpu.VMEM),
        out_shape=jax.ShapeDtypeStruct((N_HEADS_ATT, r, c), F32),
        name="bias_table",
    )(rel_bias, jnp.asarray(bucket))


def _prompt_buckets_t():
    qi = np.arange(WINDOW)[None, :]
    ci = np.arange(WINDOW)[:, None]
    upper = ci > qi
    general = _t5_bucket_np(np.where(upper, qi + WINDOW - ci, qi - ci))
    first = np.where(upper, -1, general)
    return np.concatenate([general, first], axis=0).astype(np.int32)


def _sample_buckets(t, w, padded):
    dist = (w + np.arange(t))[:, None] - np.arange(padded)[None, :]
    valid = (dist >= 0) & (dist <= WINDOW) & (np.arange(padded)[None, :] < w + t)
    return np.where(valid, _t5_bucket_np(dist), -1).astype(np.int32)


def _attn_block(i, qT, k_all, vT_all, first, bias_ref, sink_ref, gb_ref, write_out):
    W = WINDOW
    gd = lambda g: slice(g * HEAD_DIM_ATT, (g + 1) * HEAD_DIM_ATT)
    cols = slice(i * W, (i + 1) * W)
    is_first = first if i == 0 else False
    off = pl.multiple_of(jnp.where(first, W, 0), W) if i == 0 else 0
    k2 = k_all[i * W:(i + 2) * W, :]
    vT2 = vT_all[:, i * W:(i + 2) * W]
    ci = lax.broadcasted_iota(jnp.int32, (W, W), 0)
    qi = lax.broadcasted_iota(jnp.int32, (W, W), 1)
    upper = ci > qi
    eye = ci == qi
    scores = []
    for g in range(N_KV_HEADS):
        q_grp = jnp.concatenate([qT(slice((g * GQA_GROUP + j) * HEAD_DIM_ATT, (g * GQA_GROUP + j + 1) * HEAD_DIM_ATT),
                                    cols) for j in range(GQA_GROUP)], axis=1)
        scores.append(jnp.dot(k2[:, gd(g)], q_grp, preferred_element_type=F32))
    yield
    probs, rden, pfar = [], [], []
    for h in range(N_HEADS_ATT):
        g, j = divmod(h, GQA_GROUP)
        s_prev = scores[g][:W, j * W:(j + 1) * W]
        s_cur = scores[g][W:, j * W:(j + 1) * W]
        s = jnp.where(upper, s_prev, s_cur) + bias_ref[h, pl.ds(off, W), :]
        far_bias = jnp.where(is_first, -jnp.inf, sink_ref[1, h])
        s_far = jnp.sum(jnp.where(eye, s_prev, 0.0), axis=0, keepdims=True) + far_bias
        sink = sink_ref[0, h]
        m = jnp.maximum(jnp.maximum(jnp.max(s, axis=0, keepdims=True), s_far), sink)
        p = jnp.exp(s - m)
        p_far = jnp.exp(s_far - m)
        rden.append(1.0 / (jnp.sum(p, axis=0, keepdims=True) + p_far + jnp.exp(sink - m)))
        pfar.append(p_far)
        zero = jnp.zeros_like(p)
        probs.append(jnp.concatenate([jnp.where(upper, p, zero), jnp.where(upper, zero, p)], axis=0).astype(BF16))
    yield
    outs, ssq = [], jnp.zeros((1, W), F32)
    for g in range(N_KV_HEADS):
        p_grp = jnp.concatenate(probs[g * GQA_GROUP:(g + 1) * GQA_GROUP], axis=1)
        o_grp = jnp.dot(vT2[gd(g), :], p_grp, preferred_element_type=F32)
        v_far = vT2[gd(g), :W].astype(F32)
        for j in range(GQA_GROUP):
            h = g * GQA_GROUP + j
            o = (o_grp[:, j * W:(j + 1) * W] + v_far * pfar[h]) * rden[h]
            outs.append(o)
            ssq = ssq + jnp.sum(o * o, axis=0, keepdims=True)
    inv = lax.rsqrt(ssq * (1.0 / ATT_W) + EPS)
    for h in range(N_HEADS_ATT):
        rows = slice(h * HEAD_DIM_ATT, (h + 1) * HEAD_DIM_ATT)
        write_out(rows, cols, (outs[h] * inv * gb_ref[rows, :]).astype(BF16))
    yield


def _attn_blocks(qT, k_all, vT_all, first, bias_ref, sink_ref, gb_ref, write_out, qb):
    gens = [_attn_block(i, qT, k_all, vT_all, first, bias_ref, sink_ref, gb_ref, write_out) for i in range(qb)]
    next(gens[0])
    next(gens[0])
    for i in range(qb):
        if i + 1 < qb:
            next(gens[i + 1])
        yield
        next(gens[i])
        if i + 1 < qb:
            next(gens[i + 1])


PROJ_TOKEN_CHUNK = 256
PROJ_ROW_SPLIT = 1024


def _proj_attn_kernel(x_ref, g_ref, wn_ref, wt_ref, bgc_ref, rb_ref, bucket_ref, sink_ref, gb_ref, *refs,
                      tm, tiles_per_seq, n_cast):
    cast_in, refs = refs[:n_cast], refs[n_cast:]
    kvw_ref, km_ref, qmT_ref, vmT_ref, omT_ref, gT_ref, mix_ref = refs[:7]
    cast_out, (q_s, k_s, vT_s, hk_s, hv_s, bias_ref) = refs[7:7 + n_cast], refs[7 + n_cast:]
    for src, dst in zip(cast_in, cast_out):
        dst[...] = src[...].astype(BF16)
    s = pl.program_id(0)
    slot = s % 2
    prev = 1 - slot
    qb = tm // WINDOW

    @pl.when(s == 0)
    def _():
        q_s[...] = jnp.zeros_like(q_s)
        k_s[...] = jnp.zeros_like(k_s)
        vT_s[...] = jnp.zeros_like(vT_s)
        hk_s[...] = jnp.zeros_like(hk_s)
        hv_s[...] = jnp.zeros_like(hv_s)
        _bias_table_kernel(rb_ref, bucket_ref, bias_ref)

    first =(s + tiles_per_seq - 1) % tiles_per_seq == 0
    k_all = jnp.concatenate([hk_s[...], k_s[prev]], axis=0)
    vT_all = jnp.concatenate([hv_s[...], vT_s[prev]], axis=1)

    def write_out(rows, cols, val):
        mix_ref[rows, cols] = val

    blocks = _attn_blocks(lambda rows, cols: q_s[prev, rows, cols], k_all, vT_all, first, bias_ref, sink_ref, gb_ref,
                          write_out, qb)
    next(blocks)
    hb = _rms_rows(x_ref[...], g_ref[...]).astype(BF16)

    def put_q(lo, tc, z):
        q_s[slot, lo:lo + z.shape[0], tc] = (z * (HEAD_DIM_ATT ** -0.5)).astype(BF16)

    def put_v(lo, tc, z):
        vT_s[slot, lo:lo + z.shape[0], tc] = z.astype(BF16)

    def put_qm(lo, tc, z):
        qmT_ref[lo:lo + z.shape[0], tc] = z.astype(BF16)

    def put_vm(lo, tc, z):
        vmT_ref[lo:lo + z.shape[0], tc] = z.astype(BF16)

    def put_om(lo, tc, z):
        omT_ref[lo:lo + z.shape[0], tc] = jax.nn.sigmoid(z)

    def put_g(lo, tc, z):
        g_t = z[:N_GATES, :] + bgc_ref[...]
        row = lax.broadcasted_iota(jnp.int32, g_t.shape, 0)
        gT_ref[:, tc] = jnp.where(row < N_HEADS_MLSTM, g_t, _log_sigmoid(g_t))

    segments = [(_T_QA, _T_VA, put_q), (_T_VA, _T_QM, put_v), (_T_QM, _T_VM, put_qm), (_T_VM, _T_OM, put_vm),
                (_T_OM, _T_G, put_om), (_T_G, _T_ROWS, put_g)]

    def proj_piece(lo, hi, tc):
        z = lax.dot_general(wt_ref[lo:hi, :], hb[tc, :], (((1,), (1,)), ((), ())), preferred_element_type=F32)
        for a, b, put in segments:
            if a < hi and lo < b:
                put(max(lo, a) - a, tc, z[max(lo, a) - lo:min(hi, b) - lo, :])

    z_n = jnp.dot(hb, wn_ref[:, :_N_G], preferred_element_type=F32)
    kvw_ref[...] = z_n[tm - WINDOW:, _N_KV:_N_KV + 2 * KV_W]
    km_ref[...] = (z_n[:, _N_KM:_N_KM + QK_M_W] * (DK_MLSTM ** -0.5)).astype(BF16)
    k_s[slot] = z_n[:, _N_KV:_N_KV + KV_W].astype(BF16)
    pieces = [(lo, hi, slice(c, c + PROJ_TOKEN_CHUNK))
              for c in range(0, tm, PROJ_TOKEN_CHUNK) for lo, hi in ((0, PROJ_ROW_SPLIT), (PROJ_ROW_SPLIT, _T_ROWS))]
    proj_piece(*pieces.pop(0))
    for _ in blocks:
        if pieces:
            proj_piece(*pieces.pop(0))
    for piece in pieces:
        proj_piece(*piece)
    hk_s[...] = k_s[prev, tm - WINDOW:, :]
    hv_s[...] = vT_s[prev, :, tm - WINDOW:]


def _proj_attn(x2d, g_norm, w_n, w_t, bg_col, rel_bias, bucket, sinks, g_b, casts, seq, tm):
    n = x2d.shape[0]
    n_tiles = n // tm
    tiles_per_seq = seq // tm
    cur = lambda s: jnp.minimum(s, n_tiles - 1)
    prv = lambda s: jnp.maximum(s - 1, 0)
    row = lambda w: pl.BlockSpec((tm, w), lambda s: (cur(s), 0))
    colb = lambda r: pl.BlockSpec((r, tm), lambda s: (0, cur(s)))
    sds = jax.ShapeDtypeStruct
    cast_in, cast_out, cast_shape = [], [], []
    for arr, first, count in casts:
        chunk = count // n_tiles
        assert chunk * n_tiles == count and chunk % BF16_ROWS == 0 and first % chunk == 0
        cols = arr.shape[1]
        cast_in.append(pl.BlockSpec((chunk, cols), lambda s, o=first // chunk: (o + cur(s), 0)))
        cast_out.append(pl.BlockSpec((chunk, cols), lambda s: (cur(s), 0)))
        cast_shape.append(sds((count, cols), BF16))
    return pl.pallas_call(
        functools.partial(_proj_attn_kernel, tm=tm, tiles_per_seq=tiles_per_seq, n_cast=len(casts)),
        grid=(n_tiles + 1,),
        in_specs=[row(D_MODEL), _const_spec((1, D_MODEL)), _const_spec(w_n.shape), _const_spec(w_t.shape),
                  _const_spec((N_GATES, 1)), pl.BlockSpec(memory_space=pltpu.SMEM), _const_spec(bucket.shape),
                  pl.BlockSpec(memory_space=pltpu.SMEM), _const_spec(g_b.shape)] + cast_in,
        out_specs=[pl.BlockSpec((WINDOW, 2 * KV_W), lambda s: (cur(s) // tiles_per_seq, 0)),
                   row(QK_M_W), colb(QK_M_W), colb(MLSTM_W), colb(MLSTM_W), colb(N_GATES),
                   pl.BlockSpec((ATT_W, tm), lambda s: (0, prv(s)))] + cast_out,
        out_shape=[sds((n // seq * WINDOW, 2 * KV_W), F32), sds((n, QK_M_W), BF16), sds((QK_M_W, n), BF16),
                   sds((MLSTM_W, n), BF16),
                   sds((MLSTM_W, n), F32), sds((N_GATES, n), F32), sds((ATT_W, n), BF16)] + cast_shape,
        scratch_shapes=[pltpu.VMEM((2, ATT_W, tm), BF16), pltpu.VMEM((2, tm, KV_W), BF16),
                        pltpu.VMEM((2, KV_W, tm), BF16), pltpu.VMEM((WINDOW, KV_W), BF16),
                        pltpu.VMEM((KV_W, WINDOW), BF16), pltpu.VMEM((N_HEADS_ATT,) + bucket.shape, F32)],
        compiler_params=pltpu.CompilerParams(dimension_semantics=("arbitrary",), vmem_limit_bytes=VMEM_LIMIT),
        name="proj_attn",
    )(x2d, g_norm, w_n, w_t, bg_col, rel_bias, jnp.asarray(bucket), sinks, g_b, *[arr for arr, _, _ in casts])


def _attn_sample_stages(q_ref, kvf_ref, ck_ref, cv_ref, bias_ref, sink_ref, g_ref, out_ref, kw_ref, vw_ref,
                        *, bb, t, w, kpad):
    ck = ck_ref[...]
    cv = cv_ref[...]
    k_new = kvf_ref[:, :KV_W].reshape(bb, t, KV_W)
    v_new = kvf_ref[:, KV_W:].reshape(bb, t, KV_W)
    kw_ref[:, :w - t, :] = ck[:, t:, :]
    kw_ref[:, w - t:, :] = k_new
    vw_ref[:, :w - t, :] = cv[:, t:, :]
    vw_ref[:, w - t:, :] = v_new
    zpad = jnp.zeros((bb, kpad - w - t, KV_W), F32)
    kk = jnp.concatenate([ck, k_new, zpad], axis=1).astype(BF16)
    vv = jnp.concatenate([cv, v_new, zpad], axis=1).astype(BF16)
    q3 = q_ref[...].astype(F32).reshape(bb, t, ATT_W)
    yield
    gd = lambda g: slice(g * HEAD_DIM_ATT, (g + 1) * HEAD_DIM_ATT)
    scores = []
    for g in range(N_KV_HEADS):
        qg = jnp.concatenate([q3[:, :, (g * GQA_GROUP + i) * HEAD_DIM_ATT:(g * GQA_GROUP + i + 1) * HEAD_DIM_ATT]
                              for i in range(GQA_GROUP)], axis=1).astype(BF16)
        zq = jnp.zeros_like(qg)
        qg = jnp.concatenate([qg, zq] if g == 0 else [zq, qg], axis=-1)
        scores.append(jnp.einsum('bqd,bkd->bqk', qg, kk, preferred_element_type=F32))
    yield
    probs = []
    for g in range(N_KV_HEADS):
        s = scores[g] + bias_ref[g]
        sink = sink_ref[g]
        m = jnp.maximum(jnp.max(s, axis=-1, keepdims=True), sink)
        p = jnp.exp(s - m)
        denom = jnp.sum(p, axis=-1, keepdims=True) + jnp.exp(sink - m)
        probs.append((p * (1.0 / denom)).astype(BF16))
        yield
    outs = []
    ssq = jnp.zeros((bb, 1, t, 1), F32)
    for g in range(N_KV_HEADS):
        o = jnp.einsum('bqk,bkd->bqd', probs[g], vv, preferred_element_type=F32)[:, :, gd(g)]
        o = o.reshape(bb, GQA_GROUP, t, HEAD_DIM_ATT)
        outs.append(o)
        ssq = ssq + jnp.sum(jnp.sum(o * o, axis=-1, keepdims=True), axis=1, keepdims=True)
    yield
    inv = lax.rsqrt(ssq * (1.0 / ATT_W) + EPS)
    for g in range(N_KV_HEADS):
        y = outs[g] * inv
        for i in range(GQA_GROUP):
            sl = slice((g * GQA_GROUP + i) * HEAD_DIM_ATT, (g * GQA_GROUP + i + 1) * HEAD_DIM_ATT)
            out_ref[:, sl] = (y[:, i].reshape(bb * t, HEAD_DIM_ATT) * g_ref[:, sl]).astype(BF16)


SEQ_PER_BLOCK = 16


def _seg_scan_rows(x, seg, forward):
    n = x.shape[0]
    pos = lax.broadcasted_iota(jnp.int32, x.shape, 0) % seg
    k = 1
    while k < seg:
        if forward:
            x = jnp.maximum(x, jnp.where(pos >= k, pltpu.roll(x, k, axis=0), -jnp.inf))
        else:
            x = jnp.maximum(x, jnp.where(pos < seg - k, pltpu.roll(x, n - k, axis=0), -jnp.inf))
        k *= 2
    return x


def _mlstm_blk_stages(qk_ref, v_ref, om_ref, gt_ref, gm_ref, c0_ref, n0_ref, m0_ref,
                      mix_ref, c_out, n_out, m_out, *, t):
    S = SEQ_PER_BLOCK
    L = S * t
    hi = lax.Precision.HIGHEST
    ti = lax.broadcasted_iota(jnp.int32, (L, L), 0)
    si = lax.broadcasted_iota(jnp.int32, (L, L), 1)
    same = (ti // t) == (si // t)
    causal = same & (si <= ti)
    tril_seg = causal.astype(F32)
    ones_seg = same.astype(F32)
    ones_col = (lax.broadcasted_iota(jnp.int32, (L, DV_MLSTM), 1) == 0).astype(BF16)
    bd_mask = (lax.broadcasted_iota(jnp.int32, (L, S * DK_MLSTM), 0) // t
               == lax.broadcasted_iota(jnp.int32, (L, S * DK_MLSTM), 1) // DK_MLSTM)
    bd_mask_t = (lax.broadcasted_iota(jnp.int32, (S * DK_MLSTM, L), 0) // DK_MLSTM
                 == lax.broadcasted_iota(jnp.int32, (S * DK_MLSTM, L), 1) // t)
    heads = range(N_HEADS_MLSTM)
    hk = lambda h: slice(h * DK_MLSTM, (h + 1) * DK_MLSTM)
    hkk = lambda h: slice(QK_M_W + h * DK_MLSTM, QK_M_W + (h + 1) * DK_MLSTM)
    hv = lambda h: slice(h * DV_MLSTM, (h + 1) * DV_MLSTM)

    gt = gt_ref[...]
    per_token = lambda x: jnp.broadcast_to(x[:, None, :], (S, t, x.shape[-1])).reshape(L, x.shape[-1])
    m_prev = per_token(m0_ref[...])
    n_rows = per_token(n0_ref[...])
    b = jnp.dot(tril_seg, gt, precision=hi, preferred_element_type=F32)
    b_end = jnp.dot(ones_seg, gt, precision=hi, preferred_element_type=F32)
    yield
    a = pltpu.roll(gt, N_HEADS_MLSTM, axis=1) - b
    big_m = jnp.maximum(m_prev, _seg_scan_rows(a, t, True))
    inter = jnp.exp(m_prev - big_m)
    em = jnp.exp(-(b + big_m))
    m_end = _seg_scan_rows(big_m, t, False)
    w = jnp.exp(a - m_end)
    decay = jnp.exp(m_prev - m_end)
    m_out[...] = (b_end + m_end).reshape(S, t, LANES)[:, 0, :]
    a_t = a.T
    yield
    q = [qk_ref[:, hk(h)] for h in heads]
    k = [qk_ref[:, hkk(h)] for h in heads]
    v_ext = [jnp.concatenate([v_ref[:, hv(h)], ones_col], axis=1) for h in heads]
    qk = [lax.dot_general(q[h], k[h], (((1,), (1,)), ((), ())), preferred_element_type=F32) for h in heads]
    c_all = [c0_ref[:, h].reshape(S * DK_MLSTM, DV_MLSTM) for h in heads]
    q_bd = []
    for h in heads:
        q2 = jnp.concatenate([q[h], q[h]], axis=1)
        q_bd.append(jnp.where(bd_mask, jnp.concatenate([q2] * (S // 2), axis=1), jnp.zeros((), BF16)))
    qc = [jnp.dot(q_bd[h], c_all[h].astype(BF16), preferred_element_type=F32) for h in heads]
    yield
    s = []
    for h in heads:
        ln = _GATE0 + h
        d = jnp.exp(jnp.where(causal, a_t[ln:ln + 1, :] - big_m[:, ln:ln + 1], -jnp.inf))
        s.append((qk[h] * d).astype(BF16))
    yield
    sv = [jnp.dot(s[h], v_ext[h], preferred_element_type=F32) for h in heads]
    kw, kw_bd = [], []
    for h in heads:
        ln = _GATE0 + h
        kw_f = k[h].astype(F32) * w[:, ln:ln + 1]
        kw.append(kw_f.astype(BF16))
        kw_bd.append(jnp.where(bd_mask_t, jnp.concatenate([kw_f.T] * S, axis=0), 0.0).astype(BF16))
    upd = [jnp.dot(kw_bd[h], v_ref[:, hv(h)], preferred_element_type=F32) for h in heads]
    seq_of_tok = lax.broadcasted_iota(jnp.int32, (S, L), 1) // t == lax.broadcasted_iota(jnp.int32, (S, L), 0)
    first_tok = lax.broadcasted_iota(jnp.int32, (S, L), 1) == lax.broadcasted_iota(jnp.int32, (S, L), 0) * t
    n_upd = [jnp.dot(seq_of_tok.astype(BF16), kw[h], preferred_element_type=F32) for h in heads]
    decay_seq = jnp.dot(first_tok.astype(F32), decay, precision=hi, preferred_element_type=F32)
    yield
    for h in heads:
        ln = _GATE0 + h
        inter_h = inter[:, ln:ln + 1]
        qn = jnp.sum(q[h].astype(F32) * n_rows[:, hk(h)], axis=1, keepdims=True)
        den = inter_h * qn + sv[h][:, DV_MLSTM:DV_MLSTM + 1]
        r = 1.0 / jnp.maximum(jnp.abs(den), em[:, ln:ln + 1])
        tt = om_ref[:, hv(h)] * (inter_h * qc[h] + sv[h][:, :DV_MLSTM])
        msq = jnp.mean(tt * tt, axis=1, keepdims=True)
        scale = r * lax.rsqrt(r * r * msq + EPS)
        mix_ref[:, hv(h)] = (tt * scale * gm_ref[:, hv(h)]).astype(BF16)
        dec = decay[:, ln:ln + 1].reshape(S, t, 1)[:, 0:1, :]
        c_out[:, h] = dec * c0_ref[:, h] + upd[h].reshape(S, DK_MLSTM, DV_MLSTM)
        n_out[:, hk(h)] = decay_seq[:, ln:ln + 1] * n0_ref[:, hk(h)] + n_upd[h]


def _sample_mixers_kernel(q_ref, kvf_ref, ck_ref, cv_ref, bias_ref, sink_ref, ga_ref,
                          qk_ref, v_ref, om_ref, gt_ref, gm_ref, c0_ref, n0_ref, m0_ref,
                          mixa_ref, kw_ref, vw_ref, mixm_ref, c_out, n_out, m_out, *, t, w, kpad):
    attn = _attn_sample_stages(q_ref, kvf_ref, ck_ref, cv_ref, bias_ref, sink_ref, ga_ref, mixa_ref, kw_ref, vw_ref,
                               bb=SEQ_PER_BLOCK, t=t, w=w, kpad=kpad)
    mlstm = _mlstm_blk_stages(qk_ref, v_ref, om_ref, gt_ref, gm_ref, c0_ref, n0_ref, m0_ref,
                              mixm_ref, c_out, n_out, m_out, t=t)
    for _ in itertools.zip_longest(attn, mlstm):
        pass


def _sample_mixers(qa, kvf, ck, cv, bias, sink_rows, g_att, qkm, vm, om, gt, g_m, c0, n0_seq, m0_seq, nseq, t):
    S = SEQ_PER_BLOCK
    L = S * t
    w = ck.shape[1]
    kpad = bias.shape[-1]
    rows = GQA_GROUP * t
    tok = lambda wd: pl.BlockSpec((L, wd), lambda i: (i, 0))
    cache = pl.BlockSpec((S, w, KV_W), lambda i: (i, 0, 0))
    st_c = pl.BlockSpec((S, N_HEADS_MLSTM, DK_MLSTM, DV_MLSTM), lambda i: (i, 0, 0, 0))
    st_n = pl.BlockSpec((S, QK_M_W), lambda i: (i, 0))
    st_m = pl.BlockSpec((S, LANES), lambda i: (i, 0))
    sds = jax.ShapeDtypeStruct
    return pl.pallas_call(
        functools.partial(_sample_mixers_kernel, t=t, w=w, kpad=kpad),
        grid=(nseq // S,),
        in_specs=[tok(ATT_W), tok(2 * KV_W), cache, cache,
                  _const_spec((N_KV_HEADS, rows, kpad)), _const_spec((N_KV_HEADS, rows, 1)), _const_spec((1, ATT_W)),
                  tok(2 * QK_M_W), tok(MLSTM_W), tok(MLSTM_W), tok(LANES), _const_spec((1, MLSTM_W)),
                  st_c, st_n, st_m],
        out_specs=[tok(ATT_W), cache, cache, tok(MLSTM_W), st_c, st_n, st_m],
        out_shape=[sds((nseq * t, ATT_W), BF16), sds((nseq, w, KV_W), F32), sds((nseq, w, KV_W), F32),
                   sds((nseq * t, MLSTM_W), BF16), sds((nseq, N_HEADS_MLSTM, DK_MLSTM, DV_MLSTM), F32),
                   sds((nseq, QK_M_W), F32), sds((nseq, LANES), F32)],
        compiler_params=pltpu.CompilerParams(dimension_semantics=("arbitrary",), vmem_limit_bytes=VMEM_LIMIT),
        name="sample_mixers",
    )(qa, kvf, ck, cv, bias, sink_rows, g_att, qkm, vm, om, gt, g_m, c0, n0_seq, m0_seq)


def _out_mlp_kernel(x_ref, a_ref, hm_ref, woa_ref, wom_ref, gmlp_ref, wup_ref, wdn_ref, gfin_ref, y_ref):
    x1 = (x_ref[...]
          + jnp.dot(a_ref[...], woa_ref[...], preferred_element_type=F32)
          + jnp.dot(hm_ref[...], wom_ref[...], preferred_element_type=F32))
    h = _rms_rows(x1, gmlp_ref[...]).astype(BF16)
    u = jnp.maximum(jnp.dot(h, wup_ref[...], preferred_element_type=F32), 0.0)
    y = x1 + jnp.dot((u * u).astype(BF16), wdn_ref[...], preferred_element_type=F32)
    y_ref[...] = _rms_rows(y, gfin_ref[...])


def _out_mlp(x2d, mix_a, mix_m, wo_a, wo_m, g_mlp, w_up, w_dn, g_fin, tm):
    n = x2d.shape[0]
    row = lambda w: pl.BlockSpec((tm, w), lambda i: (i, 0))
    return pl.pallas_call(
        _out_mlp_kernel,
        grid=(n // tm,),
        in_specs=[row(D_MODEL), row(ATT_W), row(MLSTM_W), _const_spec(wo_a.shape), _const_spec(wo_m.shape),
                  _const_spec((1, D_MODEL)), _const_spec(w_up.shape), _const_spec(w_dn.shape),
                  _const_spec((1, D_MODEL))],
        out_specs=row(D_MODEL),
        out_shape=jax.ShapeDtypeStruct((n, D_MODEL), F32),
        compiler_params=pltpu.CompilerParams(dimension_semantics=("arbitrary",), vmem_limit_bytes=VMEM_LIMIT),
        name="out_mlp",
    )(x2d, mix_a, mix_m, wo_a, wo_m, g_mlp, w_up, w_dn, g_fin)


_CN_ROWS = DV_MLSTM + BF16_ROWS


def _prefix_max_lanes(x):
    n = x.shape[1]
    col = lax.broadcasted_iota(jnp.int32, x.shape, 1)
    k = 1
    while k < n:
        x = jnp.maximum(x, jnp.where(col >= k, pltpu.roll(x, k, axis=1), -jnp.inf))
        k *= 2
    return x


def _mlstm_stages(qT_ref, k_ref, vT_ref, omT_ref, gT_ref, gb_ref, write_mix, state, chunk, nsub):
    L = chunk
    i0 = lax.broadcasted_iota(jnp.int32, (L, L), 0)
    i1 = lax.broadcasted_iota(jnp.int32, (L, L), 1)
    causal_t = i0 <= i1
    triu = causal_t.astype(F32)
    ones_rows = (lax.broadcasted_iota(jnp.int32, (BF16_ROWS, L), 0) == 0).astype(BF16)
    zpad = jnp.zeros((LANES - 2 * N_GATES, L), F32)
    cn, m_r = state['cn'], state['m_r']
    hi = lax.Precision.HIGHEST
    heads = range(N_HEADS_MLSTM)
    subs = range(nsub)
    col_of = lambda j: slice(j * L, (j + 1) * L)
    hv = lambda h: slice(h * DV_MLSTM, (h + 1) * DV_MLSTM)
    hk = lambda h: slice(h * DK_MLSTM, (h + 1) * DK_MLSTM)
    a_r, b_r, pm_r = [], [], []
    for j in subs:
        g_t = gT_ref[:, col_of(j)]
        b_r.append(jnp.dot(g_t, triu, precision=hi, preferred_element_type=F32))
        a_r.append(pltpu.roll(g_t, N_HEADS_MLSTM, axis=0) - b_r[j])
    yield
    kq = [[jnp.dot(k_ref[col_of(j), hk(h)], qT_ref[hk(h), col_of(j)], preferred_element_type=F32)
           for h in heads] for j in subs]
    vT_ext = [[jnp.concatenate([vT_ref[hv(h), col_of(j)], ones_rows], axis=0) for h in heads] for j in subs]
    for j in subs:
        pm_r.append(_prefix_max_lanes(a_r[j]))
    yield
    big_m, inter, em, decay, cols_t = [], [], [], [], []
    for j in subs:
        big_m.append(jnp.maximum(m_r, pm_r[j]))
        inter.append(jnp.exp(m_r - big_m[j]))
        em.append(jnp.exp(-(b_r[j] + big_m[j])))
        m_end = big_m[j][:, L - 1:L]
        w_r = jnp.exp(a_r[j] - m_end)
        decay.append(jnp.exp(m_r[:, 0:1] - m_end))
        m_r = jnp.broadcast_to(b_r[j][:, L - 1:L] + m_end, (N_GATES, L))
        cols_t.append(jnp.concatenate([a_r[j], w_r, zpad], axis=0).T)
    sT = [[None] * N_HEADS_MLSTM for _ in subs]
    kw = [[None] * N_HEADS_MLSTM for _ in subs]
    intra = [[None] * N_HEADS_MLSTM for _ in subs]
    upd = [[None] * N_HEADS_MLSTM for _ in subs]
    for j in subs:
        for h in heads:
            r = _GATE0 + h
            d = jnp.exp(jnp.where(causal_t, cols_t[j][:, r:r + 1] - big_m[j][r:r + 1, :], -jnp.inf))
            sT[j][h] = (kq[j][h] * d).astype(BF16)
            w_col = cols_t[j][:, N_GATES + r:N_GATES + r + 1]
            kw[j][h] = (k_ref[col_of(j), hk(h)].astype(F32) * w_col).astype(BF16)
        for h in heads:
            intra[j][h] = jnp.dot(vT_ext[j][h], sT[j][h], preferred_element_type=F32)
            upd[j][h] = jnp.dot(vT_ext[j][h], kw[j][h], preferred_element_type=F32)
        yield
    for j in subs:
        qc = [jnp.dot(cn[h].astype(BF16), qT_ref[hk(h), col_of(j)], preferred_element_type=F32) for h in heads]
        for h in heads:
            r = _GATE0 + h
            nd = inter[j][r:r + 1, :] * qc[h] + intra[j][h]
            rr = 1.0 / jnp.maximum(jnp.abs(nd[DV_MLSTM:DV_MLSTM + 1, :]), em[j][r:r + 1, :])
            t = omT_ref[hv(h), col_of(j)] * nd[:DV_MLSTM, :]
            msq = jnp.mean(t * t, axis=0, keepdims=True)
            scale = rr * lax.rsqrt(rr * rr * msq + EPS)
            write_mix(hv(h), col_of(j), (t * scale * gb_ref[hv(h), :]).astype(BF16))
            cn[h] = decay[j][r:r + 1, :] * cn[h] + upd[j][h]
        yield
    state['m_r'] = m_r


def _mlstm_mlp_kernel(x_ref, a_ref, qT_ref, k_ref, vT_ref, omT_ref, gT_ref, gb_ref,
                      woa_ref, wom_ref, gmlp_ref, wup_ref, wdn_ref, gfin_ref,
                      y_ref, c_out, n_out, m_out, mix_s, cn_s, mr_s,
                      *, chunk, nsub, tiles_per_seq, n_tiles):
    s = pl.program_id(0)
    slot = s % 2

    @pl.when(s == 0)
    def _():
        mix_s[...] = jnp.zeros_like(mix_s)

    @pl.when(s % tiles_per_seq == 0)
    def _():
        cn_s[...] = jnp.zeros_like(cn_s)
        mr_s[...] = jnp.zeros_like(mr_s)

    state = {'cn': [cn_s[h] for h in range(N_HEADS_MLSTM)], 'm_r': mr_s[...]}

    def write_mix(rows, cols, val):
        mix_s[slot, rows, cols] = val

    stages = _mlstm_stages(qT_ref, k_ref, vT_ref, omT_ref, gT_ref, gb_ref, write_mix, state, chunk, nsub)
    next(stages)
    tdims = (((0,), (0,)), ((), ()))
    x1 = (x_ref[...]
          + lax.dot_general(a_ref[...], woa_ref[...], tdims, preferred_element_type=F32)
          + lax.dot_general(mix_s[1 - slot], wom_ref[...], tdims, preferred_element_type=F32))
    next(stages)
    h = _rms_rows(x1, gmlp_ref[...]).astype(BF16)
    u = jnp.maximum(jnp.dot(h, wup_ref[...], preferred_element_type=F32), 0.0)
    for _ in stages:
        pass
    y = x1 + jnp.dot((u * u).astype(BF16), wdn_ref[...], preferred_element_type=F32)
    y_ref[...] = _rms_rows(y, gfin_ref[...])
    for hh in range(N_HEADS_MLSTM):
        cn_s[hh] = state['cn'][hh]
    mr_s[...] = state['m_r']

    @pl.when((s % tiles_per_seq == tiles_per_seq - 1) & (s < n_tiles))
    def _():
        for hh in range(N_HEADS_MLSTM):
            c_out[0, hh] = cn_s[hh, :DV_MLSTM, :].T
            n_out[0, hh] = cn_s[hh, DV_MLSTM:DV_MLSTM + 1, :]
        m_out[0] = mr_s[...]


def _mlstm_mlp(x2d, mix_aT, qmT, km, vmT, omT, gT, g_b, wo_a, wo_m, g_mlp, w_up, w_dn, g_fin,
               nseq, t, chunk, nsub):
    tm = chunk * nsub
    n_tiles = (nseq * t) // tm
    tiles_per_seq = t // tm
    cur = lambda s: jnp.minimum(s, n_tiles - 1)
    prv = lambda s: jnp.maximum(s - 1, 0)
    colb = lambda r: pl.BlockSpec((r, tm), lambda s: (0, cur(s)))
    st = lambda a, d: pl.BlockSpec((1, N_HEADS_MLSTM, a, d), lambda s: (cur(s) // tiles_per_seq, 0, 0, 0))
    sds = jax.ShapeDtypeStruct
    kern = functools.partial(_mlstm_mlp_kernel, chunk=chunk, nsub=nsub,
                             tiles_per_seq=tiles_per_seq, n_tiles=n_tiles)
    return pl.pallas_call(
        kern,
        grid=(n_tiles + 1,),
        in_specs=[pl.BlockSpec((tm, D_MODEL), lambda s: (prv(s), 0)),
                  pl.BlockSpec((ATT_W, tm), lambda s: (0, prv(s))),
                  colb(QK_M_W), pl.BlockSpec((tm, QK_M_W), lambda s: (cur(s), 0)), colb(MLSTM_W), colb(MLSTM_W),
                  colb(N_GATES),
                  _const_spec(g_b.shape), _const_spec(wo_a.shape), _const_spec(wo_m.shape),
                  _const_spec((1, D_MODEL)), _const_spec(w_up.shape), _const_spec(w_dn.shape),
                  _const_spec((1, D_MODEL))],
        out_specs=[pl.BlockSpec((tm, D_MODEL), lambda s: (prv(s), 0)),
                   st(DK_MLSTM, DV_MLSTM), st(1, DK_MLSTM),
                   pl.BlockSpec((1, N_GATES, chunk), lambda s: (cur(s) // tiles_per_seq, 0, 0))],
        out_shape=[sds((nseq * t, D_MODEL), F32), sds((nseq, N_HEADS_MLSTM, DK_MLSTM, DV_MLSTM), F32),
                   sds((nseq, N_HEADS_MLSTM, 1, DK_MLSTM), F32), sds((nseq, N_GATES, chunk), F32)],
        scratch_shapes=[pltpu.VMEM((2, MLSTM_W, tm), BF16),
                        pltpu.VMEM((N_HEADS_MLSTM, _CN_ROWS, DK_MLSTM), F32),
                        pltpu.VMEM((N_GATES, chunk), F32)],
        compiler_params=pltpu.CompilerParams(dimension_semantics=("arbitrary",), vmem_limit_bytes=VMEM_LIMIT),
        name="mlstm_mlp",
    )(x2d, mix_aT, qmT, km, vmT, omT, gT, g_b, wo_a, wo_m, g_mlp, w_up, w_dn, g_fin)


def kernel(x_prompt, x_sample, cache_k_win, cache_v_win, state_C, state_n, state_m, rel_bias, norm_mix, w_in,
           b_gates, attn_sinks, norm_attn_out, norm_mlstm_out, w_out, norm_mlp, w_up, w_down, norm_final):
    depth = w_in.shape[0]
    assert depth == 1, "single-layer trunk"
    bp, sp = x_prompt.shape[:2]
    bs, ts = x_sample.shape[:2]
    wc = cache_k_win.shape[2]
    l = 0
    assert sp % PROJ_TILE == 0 and sp % MLP_TILE == 0 and sp >= WINDOW and (bs * ts) % SAMPLE_TILE == 0

    wl = w_in[l]
    cols = lambda lo, n: wl[:, lo:lo + n]
    w_n = jnp.concatenate([cols(_O_KV, 2 * KV_W), cols(_O_QKM + QK_M_W, QK_M_W),
                           jnp.pad(cols(_O_G, N_GATES), ((0, 0), (0, LANES - N_GATES)))], axis=1).astype(BF16)
    w_t = jnp.concatenate([cols(_O_QA, ATT_W), cols(_O_KV + KV_W, KV_W), cols(_O_QKM, QK_M_W), cols(_O_VM, MLSTM_W),
                           cols(_O_OM, MLSTM_W), jnp.pad(cols(_O_G, N_GATES), ((0, 0), (0, BF16_ROWS - N_GATES)))],
                          axis=1).T.astype(BF16)
    bg_row = jnp.pad(b_gates[l], (0, LANES - N_GATES)).reshape(1, LANES)
    bg_col = b_gates[l].reshape(N_GATES, 1)
    g_mix = norm_mix[l].reshape(1, D_MODEL)
    g_att = norm_attn_out[l].reshape(1, ATT_W)
    g_m = norm_mlstm_out[l].reshape(1, MLSTM_W)
    g_mlp = norm_mlp[l].reshape(1, D_MODEL)
    g_fin = norm_final.reshape(1, D_MODEL)
    gb_att = jnp.broadcast_to(norm_attn_out[l].reshape(ATT_W, 1), (ATT_W, WINDOW))
    gb_m = jnp.broadcast_to(norm_mlstm_out[l].reshape(MLSTM_W, 1), (MLSTM_W, MLSTM_CHUNK))
    sinks = attn_sinks[l]

    xp = x_prompt.reshape(bp * sp, D_MODEL)
    far_bucket = int(_t5_bucket_np(np.array(WINDOW)))
    sink_far = jnp.stack([sinks, rel_bias[far_bucket]])
    casts = [(w_out[l], 0, ATT_W), (w_out[l], ATT_W, MLSTM_W), (w_up[l], 0, D_MODEL), (w_down[l], 0, D_FF)]
    kvw, km, qmT, vmT, omT, gT, mix_a, wo_a, wo_m, wup, wdn = _proj_attn(
        xp, g_mix, w_n, w_t, bg_col, rel_bias, _prompt_buckets_t(), sink_far, gb_att, casts, sp, tm=PROJ_TILE)
    y_p, c_p, n_p, m_p = _mlstm_mlp(xp, mix_a, qmT, km, vmT, omT, gT, gb_m, wo_a, wo_m, g_mlp, wup, wdn, g_fin,
                                    bp, sp, chunk=MLSTM_CHUNK, nsub=MLP_TILE // MLSTM_CHUNK)
    m_p = m_p[:, _GATE0:_GATE0 + N_HEADS_MLSTM, 0]
    k_win_p = kvw[:, :KV_W].reshape(1, bp, WINDOW, N_KV_HEADS, HEAD_DIM_ATT)
    v_win_p = kvw[:, KV_W:].reshape(1, bp, WINDOW, N_KV_HEADS, HEAD_DIM_ATT)

    assert math.gcd(ts, 64) == ts and bs % SEQ_PER_BLOCK == 0, "sample group: one mLSTM chunk per sequence"
    kpad = ((wc + ts + BF16_ROWS - 1) // BF16_ROWS) * BF16_ROWS
    bias_s = _bias_table(rel_bias, _sample_buckets(ts, wc, kpad))
    bias_s = bias_s.reshape(N_KV_HEADS, GQA_GROUP * ts, kpad)
    sink_rows = jnp.repeat(sinks, ts).reshape(N_KV_HEADS, GQA_GROUP * ts, 1)
    xs = x_sample.reshape(bs * ts, D_MODEL)
    qa, kvf_s, qkm, vm, om, gt = _in_proj_s(xs, g_mix, w_n, w_t, bg_row, tm=SAMPLE_TILE)
    ck = cache_k_win[l].reshape(bs, wc, KV_W)
    cv = cache_v_win[l].reshape(bs, wc, KV_W)
    n0 = state_n[l].reshape(bs, QK_M_W)
    m0 = jnp.pad(state_m[l], ((0, 0), (_GATE0, LANES - _GATE0 - N_HEADS_MLSTM)))
    mix_a_s, kw_s, vw_s, mix_m_s, c_s, n_s, m_seq = _sample_mixers(
        qa, kvf_s, ck, cv, bias_s, sink_rows, g_att, qkm, vm, om, gt, g_m, state_C[l], n0, m0, bs, ts)
    m_s = m_seq[:, _GATE0:_GATE0 + N_HEADS_MLSTM]
    y_s = _out_mlp(xs, mix_a_s, mix_m_s, wo_a, wo_m, g_mlp, wup, wdn, g_fin, tm=SAMPLE_TILE)

    return (y_p.reshape(bp, sp, D_MODEL), y_s.reshape(bs, ts, D_MODEL),
            k_win_p, v_win_p,
            c_p[None], n_p.reshape(1, bp, N_HEADS_MLSTM, DK_MLSTM), m_p.reshape(1, bp, N_HEADS_MLSTM),
            kw_s.reshape(1, bs, wc, N_KV_HEADS, HEAD_DIM_ATT), vw_s.reshape(1, bs, wc, N_KV_HEADS, HEAD_DIM_ATT),
            c_s[None], n_s.reshape(1, bs, N_HEADS_MLSTM, DK_MLSTM), m_s.reshape(1, bs, N_HEADS_MLSTM))
```

```python
import functools
import itertools
import math

import numpy as np
import jax
import jax.numpy as jnp
from jax import lax
from jax.experimental import pallas as pl
from jax.experimental.pallas import tpu as pltpu

D_MODEL = 1024
N_HEADS_ATT = 8
N_KV_HEADS = 2
GQA_GROUP = N_HEADS_ATT // N_KV_HEADS
HEAD_DIM_ATT = 64
ATT_W = N_HEADS_ATT * HEAD_DIM_ATT
KV_W = N_KV_HEADS * HEAD_DIM_ATT
WINDOW = 128
NUM_BUCKETS = 32
MAX_DISTANCE = 128
N_HEADS_MLSTM = 4
DV_MLSTM = 128
DK_MLSTM = 64
MLSTM_W = N_HEADS_MLSTM * DV_MLSTM
QK_M_W = N_HEADS_MLSTM * DK_MLSTM
D_FF = 4 * D_MODEL
EPS = 1e-6
N_GATES = 2 * N_HEADS_MLSTM
LANES = 128
BF16_ROWS = 16
VMEM_LIMIT = 56 * 1024 * 1024
PROJ_TILE = 1024
MLP_TILE = 512
MLSTM_CHUNK = 256
SAMPLE_TILE = 512

F32 = jnp.float32
BF16 = jnp.bfloat16

_O_QA = 0
_O_KV = _O_QA + ATT_W
_O_QKM = _O_KV + 2 * KV_W
_O_VM = _O_QKM + 2 * QK_M_W
_O_OM = _O_VM + MLSTM_W
_O_G = _O_OM + MLSTM_W
_T_QA = 0
_T_VA = _T_QA + ATT_W
_T_QM = _T_VA + KV_W
_T_VM = _T_QM + QK_M_W
_T_OM = _T_VM + MLSTM_W
_T_G = _T_OM + MLSTM_W
_T_ROWS = _T_G + BF16_ROWS
_N_KV = 0
_N_KM = _N_KV + 2 * KV_W
_N_G = _N_KM + QK_M_W
_GATE0 = N_HEADS_MLSTM


def _const_spec(shape):
    nd = len(shape)
    return pl.BlockSpec(shape, lambda *_: (0,) * nd, pipeline_mode=pl.Buffered(1))


def _log_sigmoid(g):
    return jnp.minimum(g, 0.0) - jnp.log1p(jnp.exp(-jnp.abs(g)))


def _rms_rows(x, gain):
    return x * lax.rsqrt(jnp.mean(x * x, axis=-1, keepdims=True) + EPS) * gain


def _in_proj_s_kernel(x_ref, g_ref, wn_ref, wt_ref, bgr_ref, qa_ref, kvf_ref, qkm_ref, vm_ref, om_ref, gt_ref):
    hb = _rms_rows(x_ref[...], g_ref[...]).astype(BF16)
    z_n = jnp.dot(hb, wn_ref[...], preferred_element_type=F32)
    z_t = lax.dot_general(hb, wt_ref[...], (((1,), (1,)), ((), ())), preferred_element_type=F32)
    qa_ref[...] = (z_t[:, _T_QA:_T_QA + ATT_W] * (HEAD_DIM_ATT ** -0.5)).astype(BF16)
    kvf_ref[:, :KV_W] = z_n[:, _N_KV:_N_KV + KV_W]
    kvf_ref[:, KV_W:] = z_t[:, _T_VA:_T_VA + KV_W]
    qkm_ref[:, :QK_M_W] = z_t[:, _T_QM:_T_QM + QK_M_W].astype(BF16)
    qkm_ref[:, QK_M_W:] = (z_n[:, _N_KM:_N_KM + QK_M_W] * (DK_MLSTM ** -0.5)).astype(BF16)
    vm_ref[...] = z_t[:, _T_VM:_T_VM + MLSTM_W].astype(BF16)
    om_ref[...] = jax.nn.sigmoid(z_t[:, _T_OM:_T_OM + MLSTM_W])
    g = z_n[:, _N_G:_N_G + LANES] + bgr_ref[...]
    col = lax.broadcasted_iota(jnp.int32, g.shape, 1)
    gt_ref[...] = jnp.where(col < N_HEADS_MLSTM, g, _log_sigmoid(g))


def _in_proj_s(x2d, g_norm, w_n, w_t, bg_row, tm):
    n = x2d.shape[0]
    row = lambda w: pl.BlockSpec((tm, w), lambda i: (i, 0))
    sds = jax.ShapeDtypeStruct
    return pl.pallas_call(
        _in_proj_s_kernel,
        grid=(n // tm,),
        in_specs=[row(D_MODEL), _const_spec((1, D_MODEL)), _const_spec(w_n.shape), _const_spec(w_t.shape),
                  _const_spec((1, LANES))],
        out_specs=[row(ATT_W), row(2 * KV_W), row(2 * QK_M_W), row(MLSTM_W), row(MLSTM_W), row(LANES)],
        out_shape=[sds((n, ATT_W), BF16), sds((n, 2 * KV_W), F32), sds((n, 2 * QK_M_W), BF16),
                   sds((n, MLSTM_W), BF16), sds((n, MLSTM_W), F32), sds((n, LANES), F32)],
        compiler_params=pltpu.CompilerParams(dimension_semantics=("arbitrary",), vmem_limit_bytes=VMEM_LIMIT),
        name="in_proj_s",
    )(x2d, g_norm, w_n, w_t, bg_row)


def _t5_bucket_np(dist):
    max_exact = NUM_BUCKETS // 2
    d = np.maximum(dist, 0)
    ratio = np.maximum(d, 1).astype(np.float32) / np.float32(max_exact)
    large = max_exact + (np.log(ratio) / np.float32(math.log(MAX_DISTANCE / max_exact))
                         * np.float32(NUM_BUCKETS - max_exact)).astype(np.int32)
    large = np.minimum(large, NUM_BUCKETS - 1)
    return np.where(d < max_exact, d, large).astype(np.int32)


def _bias_table_kernel(rb_ref, bucket_ref, out_ref):
    bucket = bucket_ref[...]
    hit = [bucket == j for j in range(NUM_BUCKETS)]
    for h in range(N_HEADS_ATT):
        acc = jnp.full(bucket.shape, -jnp.inf, F32)
        for j in range(NUM_BUCKETS):
            acc = jnp.where(hit[j], rb_ref[j, h], acc)
        out_ref[h] = acc


def _bias_table(rel_bias, bucket):
    r, c = bucket.shape
    return pl.pallas_call(
        _bias_table_kernel,
        in_specs=[pl.BlockSpec(memory_space=pltpu.SMEM), pl.BlockSpec(memory_space=pltpu.VMEM)],
        out_specs=pl.BlockSpec(memory_space=pltpu.VMEM),
        out_shape=jax.ShapeDtypeStruct((N_HEADS_ATT, r, c), F32),
        name="bias_table",
    )(rel_bias, jnp.asarray(bucket))


def _prompt_buckets_t():
    qi = np.arange(WINDOW)[None, :]
    ci = np.arange(WINDOW)[:, None]
    upper = ci > qi
    general = _t5_bucket_np(np.where(upper, qi + WINDOW - ci, qi - ci))
    first = np.where(upper, -1, general)
    return np.concatenate([general, first], axis=0).astype(np.int32)


def _sample_buckets(t, w, padded):
    dist = (w + np.arange(t))[:, None] - np.arange(padded)[None, :]
    valid = (dist >= 0) & (dist <= WINDOW) & (np.arange(padded)[None, :] < w + t)
    return np.where(valid, _t5_bucket_np(dist), -1).astype(np.int32)


def _attn_block(i, qT, k_all, vT_all, first, bias_ref, sink_ref, gb_ref, write_out):
    W = WINDOW
    gd = lambda g: slice(g * HEAD_DIM_ATT, (g + 1) * HEAD_DIM_ATT)
    cols = slice(i * W, (i + 1) * W)
    is_first = first if i == 0 else False
    off = pl.multiple_of(jnp.where(first, W, 0), W) if i == 0 else 0
    k2 = k_all[i * W:(i + 2) * W, :]
    vT2 = vT_all[:, i * W:(i + 2) * W]
    ci = lax.broadcasted_iota(jnp.int32, (W, W), 0)
    qi = lax.broadcasted_iota(jnp.int32, (W, W), 1)
    upper = ci > qi
    eye = ci == qi
    scores = []
    for g in range(N_KV_HEADS):
        q_grp = jnp.concatenate([qT(slice((g * GQA_GROUP + j) * HEAD_DIM_ATT, (g * GQA_GROUP + j + 1) * HEAD_DIM_ATT),
                                    cols) for j in range(GQA_GROUP)], axis=1)
        scores.append(jnp.dot(k2[:, gd(g)], q_grp, preferred_element_type=F32))
    yield
    probs, rden, pfar = [], [], []
    for h in range(N_HEADS_ATT):
        g, j = divmod(h, GQA_GROUP)
        s_prev = scores[g][:W, j * W:(j + 1) * W]
        s_cur = scores[g][W:, j * W:(j + 1) * W]
        s = jnp.where(upper, s_prev, s_cur) + bias_ref[h, pl.ds(off, W), :]
        far_bias = jnp.where(is_first, -jnp.inf, sink_ref[1, h])
        s_far = jnp.sum(jnp.where(eye, s_prev, 0.0), axis=0, keepdims=True) + far_bias
        sink = sink_ref[0, h]
        m = jnp.maximum(jnp.maximum(jnp.max(s, axis=0, keepdims=True), s_far), sink)
        p = jnp.exp(s - m)
        p_far = jnp.exp(s_far - m)
        rden.append(1.0 / (jnp.sum(p, axis=0, keepdims=True) + p_far + jnp.exp(sink - m)))
        pfar.append(p_far)
        zero = jnp.zeros_like(p)
        probs.append(jnp.concatenate([jnp.where(upper, p, zero), jnp.where(upper, zero, p)], axis=0).astype(BF16))
    yield
    outs, ssq = [], jnp.zeros((1, W), F32)
    for g in range(N_KV_HEADS):
        p_grp = jnp.concatenate(probs[g * GQA_GROUP:(g + 1) * GQA_GROUP], axis=1)
        o_grp = jnp.dot(vT2[gd(g), :], p_grp, preferred_element_type=F32)
        v_far = vT2[gd(g), :W].astype(F32)
        for j in range(GQA_GROUP):
            h = g * GQA_GROUP + j
            o = (o_grp[:, j * W:(j + 1) * W] + v_far * pfar[h]) * rden[h]
            outs.append(o)
            ssq = ssq + jnp.sum(o * o, axis=0, keepdims=True)
    inv = lax.rsqrt(ssq * (1.0 / ATT_W) + EPS)
    for h in range(N_HEADS_ATT):
        rows = slice(h * HEAD_DIM_ATT, (h + 1) * HEAD_DIM_ATT)
        write_out(rows, cols, (outs[h] * inv * gb_ref[rows, :]).astype(BF16))
    yield


def _attn_blocks(qT, k_all, vT_all, first, bias_ref, sink_ref, gb_ref, write_out, qb):
    gens = [_attn_block(i, qT, k_all, vT_all, first, bias_ref, sink_ref, gb_ref, write_out) for i in range(qb)]
    next(gens[0])
    next(gens[0])
    for i in range(qb):
        if i + 1 < qb:
            next(gens[i + 1])
        yield
        next(gens[i])
        if i + 1 < qb:
            next(gens[i + 1])


PROJ_TOKEN_CHUNK = 256
PROJ_ROW_SPLIT = 1024


def _proj_attn_kernel(x_ref, g_ref, wn_ref, wt_ref, bgc_ref, rb_ref, bucket_ref, sink_ref, gb_ref, *refs,
                      tm, tiles_per_seq, n_cast):
    cast_in, refs = refs[:n_cast], refs[n_cast:]
    kvw_ref, km_ref, qmT_ref, vmT_ref, omT_ref, gT_ref, mix_ref = refs[:7]
    cast_out, (q_s, k_s, vT_s, hk_s, hv_s, bias_ref) = refs[7:7 + n_cast], refs[7 + n_cast:]
    for src, dst in zip(cast_in, cast_out):
        dst[...] = src[...].astype(BF16)
    s = pl.program_id(0)
    slot = s % 2
    prev = 1 - slot
    qb = tm // WINDOW

    @pl.when(s == 0)
    def _():
        q_s[...] = jnp.zeros_like(q_s)
        k_s[...] = jnp.zeros_like(k_s)
        vT_s[...] = jnp.zeros_like(vT_s)
        hk_s[...] = jnp.zeros_like(hk_s)
        hv_s[...] = jnp.zeros_like(hv_s)
        _bias_table_kernel(rb_ref, bucket_ref, bias_ref)

    first =(s + tiles_per_seq - 1) % tiles_per_seq == 0
    k_all = jnp.concatenate([hk_s[...], k_s[prev]], axis=0)
    vT_all = jnp.concatenate([hv_s[...], vT_s[prev]], axis=1)

    def write_out(rows, cols, val):
        mix_ref[rows, cols] = val

    blocks = _attn_blocks(lambda rows, cols: q_s[prev, rows, cols], k_all, vT_all, first, bias_ref, sink_ref, gb_ref,
                          write_out, qb)
    next(blocks)
    hb = _rms_rows(x_ref[...], g_ref[...]).astype(BF16)

    def put_q(lo, tc, z):
        q_s[slot, lo:lo + z.shape[0], tc] = (z * (HEAD_DIM_ATT ** -0.5)).astype(BF16)

    def put_v(lo, tc, z):
        vT_s[slot, lo:lo + z.shape[0], tc] = z.astype(BF16)

    def put_qm(lo, tc, z):
        qmT_ref[lo:lo + z.shape[0], tc] = z.astype(BF16)

    def put_vm(lo, tc, z):
        vmT_ref[lo:lo + z.shape[0], tc] = z.astype(BF16)

    def put_om(lo, tc, z):
        omT_ref[lo:lo + z.shape[0], tc] = jax.nn.sigmoid(z)

    def put_g(lo, tc, z):
        g_t = z[:N_GATES, :] + bgc_ref[...]
        row = lax.broadcasted_iota(jnp.int32, g_t.shape, 0)
        gT_ref[:, tc] = jnp.where(row < N_HEADS_MLSTM, g_t, _log_sigmoid(g_t))

    segments = [(_T_QA, _T_VA, put_q), (_T_VA, _T_QM, put_v), (_T_QM, _T_VM, put_qm), (_T_VM, _T_OM, put_vm),
                (_T_OM, _T_G, put_om), (_T_G, _T_ROWS, put_g)]

    def proj_piece(lo, hi, tc):
        z = lax.dot_general(wt_ref[lo:hi, :], hb[tc, :], (((1,), (1,)), ((), ())), preferred_element_type=F32)
        for a, b, put in segments:
            if a < hi and lo < b:
                put(max(lo, a) - a, tc, z[max(lo, a) - lo:min(hi, b) - lo, :])

    z_n = jnp.dot(hb, wn_ref[:, :_N_G], preferred_element_type=F32)
    kvw_ref[...] = z_n[tm - WINDOW:, _N_KV:_N_KV + 2 * KV_W]
    km_ref[...] = (z_n[:, _N_KM:_N_KM + QK_M_W] * (DK_MLSTM ** -0.5)).astype(BF16)
    k_s[slot] = z_n[:, _N_KV:_N_KV + KV_W].astype(BF16)
    pieces = [(lo, hi, slice(c, c + PROJ_TOKEN_CHUNK))
              for c in range(0, tm, PROJ_TOKEN_CHUNK) for lo, hi in ((0, PROJ_ROW_SPLIT), (PROJ_ROW_SPLIT, _T_ROWS))]
    proj_piece(*pieces.pop(0))
    for _ in blocks:
        if pieces:
            proj_piece(*pieces.pop(0))
    for piece in pieces:
        proj_piece(*piece)
    hk_s[...] = k_s[prev, tm - WINDOW:, :]
    hv_s[...] = vT_s[prev, :, tm - WINDOW:]


def _proj_attn(x2d, g_norm, w_n, w_t, bg_col, rel_bias, bucket, sinks, g_b, casts, seq, tm):
    n = x2d.shape[0]
    n_tiles = n // tm
    tiles_per_seq = seq // tm
    cur = lambda s: jnp.minimum(s, n_tiles - 1)
    prv = lambda s: jnp.maximum(s - 1, 0)
    row = lambda w: pl.BlockSpec((tm, w), lambda s: (cur(s), 0))
    colb = lambda r: pl.BlockSpec((r, tm), lambda s: (0, cur(s)))
    sds = jax.ShapeDtypeStruct
    cast_in, cast_out, cast_shape = [], [], []
    for arr, first, count in casts:
        chunk = count // n_tiles
        assert chunk * n_tiles == count and chunk % BF16_ROWS == 0 and first % chunk == 0
        cols = arr.shape[1]
        cast_in.append(pl.BlockSpec((chunk, cols), lambda s, o=first // chunk: (o + cur(s), 0)))
        cast_out.append(pl.BlockSpec((chunk, cols), lambda s: (cur(s), 0)))
        cast_shape.append(sds((count, cols), BF16))
    return pl.pallas_call(
        functools.partial(_proj_attn_kernel, tm=tm, tiles_per_seq=tiles_per_seq, n_cast=len(casts)),
        grid=(n_tiles + 1,),
        in_specs=[row(D_MODEL), _const_spec((1, D_MODEL)), _const_spec(w_n.shape), _const_spec(w_t.shape),
                  _const_spec((N_GATES, 1)), pl.BlockSpec(memory_space=pltpu.SMEM), _const_spec(bucket.shape),
                  pl.BlockSpec(memory_space=pltpu.SMEM), _const_spec(g_b.shape)] + cast_in,
        out_specs=[pl.BlockSpec((WINDOW, 2 * KV_W), lambda s: (cur(s) // tiles_per_seq, 0)),
                   row(QK_M_W), colb(QK_M_W), colb(MLSTM_W), colb(MLSTM_W), colb(N_GATES),
                   pl.BlockSpec((ATT_W, tm), lambda s: (0, prv(s)))] + cast_out,
        out_shape=[sds((n // seq * WINDOW, 2 * KV_W), F32), sds((n, QK_M_W), BF16), sds((QK_M_W, n), BF16),
                   sds((MLSTM_W, n), BF16),
                   sds((MLSTM_W, n), F32), sds((N_GATES, n), F32), sds((ATT_W, n), BF16)] + cast_shape,
        scratch_shapes=[pltpu.VMEM((2, ATT_W, tm), BF16), pltpu.VMEM((2, tm, KV_W), BF16),
                        pltpu.VMEM((2, KV_W, tm), BF16), pltpu.VMEM((WINDOW, KV_W), BF16),
                        pltpu.VMEM((KV_W, WINDOW), BF16), pltpu.VMEM((N_HEADS_ATT,) + bucket.shape, F32)],
        compiler_params=pltpu.CompilerParams(dimension_semantics=("arbitrary",), vmem_limit_bytes=VMEM_LIMIT),
        name="proj_attn",
    )(x2d, g_norm, w_n, w_t, bg_col, rel_bias, jnp.asarray(bucket), sinks, g_b, *[arr for arr, _, _ in casts])


def _attn_sample_stages(q_ref, kvf_ref, ck_ref, cv_ref, bias_ref, sink_ref, g_ref, out_ref, kw_ref, vw_ref,
                        *, bb, t, w, kpad):
    ck = ck_ref[...]
    cv = cv_ref[...]
    k_new = kvf_ref[:, :KV_W].reshape(bb, t, KV_W)
    v_new = kvf_ref[:, KV_W:].reshape(bb, t, KV_W)
    kw_ref[:, :w - t, :] = ck[:, t:, :]
    kw_ref[:, w - t:, :] = k_new
    vw_ref[:, :w - t, :] = cv[:, t:, :]
    vw_ref[:, w - t:, :] = v_new
    zpad = jnp.zeros((bb, kpad - w - t, KV_W), F32)
    kk = jnp.concatenate([ck, k_new, zpad], axis=1).astype(BF16)
    vv = jnp.concatenate([cv, v_new, zpad], axis=1).astype(BF16)
    q3 = q_ref[...].astype(F32).reshape(bb, t, ATT_W)
    yield
    gd = lambda g: slice(g * HEAD_DIM_ATT, (g + 1) * HEAD_DIM_ATT)
    scores = []
    for g in range(N_KV_HEADS):
        qg = jnp.concatenate([q3[:, :, (g * GQA_GROUP + i) * HEAD_DIM_ATT:(g * GQA_GROUP + i + 1) * HEAD_DIM_ATT]
                              for i in range(GQA_GROUP)], axis=1).astype(BF16)
        zq = jnp.zeros_like(qg)
        qg = jnp.concatenate([qg, zq] if g == 0 else [zq, qg], axis=-1)
        scores.append(jnp.einsum('bqd,bkd->bqk', qg, kk, preferred_element_type=F32))
    yield
    probs, rden = [], []
    for g in range(N_KV_HEADS):
        s = scores[g] + bias_ref[g]
        sink = sink_ref[g]
        m = jnp.maximum(jnp.max(s, axis=-1, keepdims=True), sink)
        p = jnp.exp(s - m)
        rden.append(1.0 / (jnp.sum(p, axis=-1, keepdims=True) + jnp.exp(sink - m)))
        probs.append(p.astype(BF16))
        yield
    outs = []
    ssq = jnp.zeros((bb, 1, t, HEAD_DIM_ATT), F32)
    for g in range(N_KV_HEADS):
        o = jnp.einsum('bqk,bkd->bqd', probs[g], vv, preferred_element_type=F32)[:, :, gd(g)] * rden[g]
        o = o.reshape(bb, GQA_GROUP, t, HEAD_DIM_ATT)
        outs.append(o)
        ssq = ssq + jnp.sum(o * o, axis=1, keepdims=True)
    yield
    inv = lax.rsqrt(jnp.sum(ssq, axis=-1, keepdims=True) * (1.0 / ATT_W) + EPS)
    for g in range(N_KV_HEADS):
        y = outs[g] * inv
        for i in range(GQA_GROUP):
            sl = slice((g * GQA_GROUP + i) * HEAD_DIM_ATT, (g * GQA_GROUP + i + 1) * HEAD_DIM_ATT)
            out_ref[:, sl] = (y[:, i].reshape(bb * t, HEAD_DIM_ATT) * g_ref[:, sl]).astype(BF16)


SEQ_PER_BLOCK = 16


def _seg_scan_rows(x, seg, forward):
    n = x.shape[0]
    pos = lax.broadcasted_iota(jnp.int32, x.shape, 0) % seg
    k = 1
    while k < seg:
        if forward:
            x = jnp.maximum(x, jnp.where(pos >= k, pltpu.roll(x, k, axis=0), -jnp.inf))
        else:
            x = jnp.maximum(x, jnp.where(pos < seg - k, pltpu.roll(x, n - k, axis=0), -jnp.inf))
        k *= 2
    return x


def _mlstm_blk_stages(qk_ref, v_ref, om_ref, gt_ref, gm_ref, c0_ref, n0_ref, m0_ref,
                      mix_ref, c_out, n_out, m_out, *, t):
    S = SEQ_PER_BLOCK
    L = S * t
    hi = lax.Precision.HIGHEST
    ti = lax.broadcasted_iota(jnp.int32, (L, L), 0)
    si = lax.broadcasted_iota(jnp.int32, (L, L), 1)
    same = (ti // t) == (si // t)
    causal = same & (si <= ti)
    tril_seg = causal.astype(F32)
    ones_seg = same.astype(F32)
    ones_col = (lax.broadcasted_iota(jnp.int32, (L, DV_MLSTM), 1) == 0).astype(BF16)
    bd_mask = (lax.broadcasted_iota(jnp.int32, (L, S * DK_MLSTM), 0) // t
               == lax.broadcasted_iota(jnp.int32, (L, S * DK_MLSTM), 1) // DK_MLSTM)
    bd_mask_t = (lax.broadcasted_iota(jnp.int32, (S * DK_MLSTM, L), 0) // DK_MLSTM
                 == lax.broadcasted_iota(jnp.int32, (S * DK_MLSTM, L), 1) // t)
    heads = range(N_HEADS_MLSTM)
    hk = lambda h: slice(h * DK_MLSTM, (h + 1) * DK_MLSTM)
    hkk = lambda h: slice(QK_M_W + h * DK_MLSTM, QK_M_W + (h + 1) * DK_MLSTM)
    hv = lambda h: slice(h * DV_MLSTM, (h + 1) * DV_MLSTM)

    gt = gt_ref[...]
    per_token = lambda x: jnp.broadcast_to(x[:, None, :], (S, t, x.shape[-1])).reshape(L, x.shape[-1])
    m_prev = per_token(m0_ref[...])
    n_rows = per_token(n0_ref[...])
    b = jnp.dot(tril_seg, gt, precision=hi, preferred_element_type=F32)
    b_end = jnp.dot(ones_seg, gt, precision=hi, preferred_element_type=F32)
    yield
    a = pltpu.roll(gt, N_HEADS_MLSTM, axis=1) - b
    big_m = jnp.maximum(m_prev, _seg_scan_rows(a, t, True))
    inter = jnp.exp(m_prev - big_m)
    em = jnp.exp(-(b + big_m))
    m_end = _seg_scan_rows(big_m, t, False)
    w = jnp.exp(a - m_end)
    decay = jnp.exp(m_prev - m_end)
    m_out[...] = (b_end + m_end).reshape(S, t, LANES)[:, 0, :]
    a_t = a.T
    yield
    q = [qk_ref[:, hk(h)] for h in heads]
    k = [qk_ref[:, hkk(h)] for h in heads]
    v_ext = [jnp.concatenate([v_ref[:, hv(h)], ones_col], axis=1) for h in heads]
    qk = [lax.dot_general(q[h], k[h], (((1,), (1,)), ((), ())), preferred_element_type=F32) for h in heads]
    c_all = [c0_ref[:, h].reshape(S * DK_MLSTM, DV_MLSTM) for h in heads]
    q_bd = []
    for h in heads:
        q2 = jnp.concatenate([q[h], q[h]], axis=1)
        q_bd.append(jnp.where(bd_mask, jnp.concatenate([q2] * (S // 2), axis=1), jnp.zeros((), BF16)))
    qc = [jnp.dot(q_bd[h], c_all[h].astype(BF16), preferred_element_type=F32) for h in heads]
    yield
    s = []
    for h in heads:
        ln = _GATE0 + h
        d = jnp.exp(jnp.where(causal, a_t[ln:ln + 1, :] - big_m[:, ln:ln + 1], -jnp.inf))
        s.append((qk[h] * d).astype(BF16))
    yield
    sv = [jnp.dot(s[h], v_ext[h], preferred_element_type=F32) for h in heads]
    kw, kw_bd = [], []
    for h in heads:
        ln = _GATE0 + h
        kw_f = k[h].astype(F32) * w[:, ln:ln + 1]
        kw.append(kw_f.astype(BF16))
        kw_bd.append(jnp.where(bd_mask_t, jnp.concatenate([kw_f.T] * S, axis=0), 0.0).astype(BF16))
    upd = [jnp.dot(kw_bd[h], v_ref[:, hv(h)], preferred_element_type=F32) for h in heads]
    seq_of_tok = lax.broadcasted_iota(jnp.int32, (S, L), 1) // t == lax.broadcasted_iota(jnp.int32, (S, L), 0)
    first_tok = lax.broadcasted_iota(jnp.int32, (S, L), 1) == lax.broadcasted_iota(jnp.int32, (S, L), 0) * t
    n_upd = [jnp.dot(seq_of_tok.astype(BF16), kw[h], preferred_element_type=F32) for h in heads]
    decay_seq = jnp.dot(first_tok.astype(F32), decay, precision=hi, preferred_element_type=F32)
    yield
    for h in heads:
        ln = _GATE0 + h
        inter_h = inter[:, ln:ln + 1]
        qn = jnp.sum(q[h].astype(F32) * n_rows[:, hk(h)], axis=1, keepdims=True)
        den = inter_h * qn + sv[h][:, DV_MLSTM:DV_MLSTM + 1]
        r = 1.0 / jnp.maximum(jnp.abs(den), em[:, ln:ln + 1])
        tt = om_ref[:, hv(h)] * (inter_h * qc[h] + sv[h][:, :DV_MLSTM])
        msq = jnp.mean(tt * tt, axis=1, keepdims=True)
        scale = r * lax.rsqrt(r * r * msq + EPS)
        mix_ref[:, hv(h)] = (tt * scale * gm_ref[:, hv(h)]).astype(BF16)
        dec = decay[:, ln:ln + 1].reshape(S, t, 1)[:, 0:1, :]
        c_out[:, h] = dec * c0_ref[:, h] + upd[h].reshape(S, DK_MLSTM, DV_MLSTM)
        n_out[:, hk(h)] = decay_seq[:, ln:ln + 1] * n0_ref[:, hk(h)] + n_upd[h]


def _sample_mixers_kernel(q_ref, kvf_ref, ck_ref, cv_ref, bias_ref, sink_ref, ga_ref,
                          qk_ref, v_ref, om_ref, gt_ref, gm_ref, c0_ref, n0_ref, m0_ref,
                          mixa_ref, kw_ref, vw_ref, mixm_ref, c_out, n_out, m_out, *, t, w, kpad):
    attn = _attn_sample_stages(q_ref, kvf_ref, ck_ref, cv_ref, bias_ref, sink_ref, ga_ref, mixa_ref, kw_ref, vw_ref,
                               bb=SEQ_PER_BLOCK, t=t, w=w, kpad=kpad)
    mlstm = _mlstm_blk_stages(qk_ref, v_ref, om_ref, gt_ref, gm_ref, c0_ref, n0_ref, m0_ref,
                              mixm_ref, c_out, n_out, m_out, t=t)
    for _ in itertools.zip_longest(attn, mlstm):
        pass


def _sample_mixers(qa, kvf, ck, cv, bias, sink_rows, g_att, qkm, vm, om, gt, g_m, c0, n0_seq, m0_seq, nseq, t):
    S = SEQ_PER_BLOCK
    L = S * t
    w = ck.shape[1]
    kpad = bias.shape[-1]
    rows = GQA_GROUP * t
    tok = lambda wd: pl.BlockSpec((L, wd), lambda i: (i, 0))
    cache = pl.BlockSpec((S, w, KV_W), lambda i: (i, 0, 0))
    st_c = pl.BlockSpec((S, N_HEADS_MLSTM, DK_MLSTM, DV_MLSTM), lambda i: (i, 0, 0, 0))
    st_n = pl.BlockSpec((S, QK_M_W), lambda i: (i, 0))
    st_m = pl.BlockSpec((S, LANES), lambda i: (i, 0))
    sds = jax.ShapeDtypeStruct
    return pl.pallas_call(
        functools.partial(_sample_mixers_kernel, t=t, w=w, kpad=kpad),
        grid=(nseq // S,),
        in_specs=[tok(ATT_W), tok(2 * KV_W), cache, cache,
                  _const_spec((N_KV_HEADS, rows, kpad)), _const_spec((N_KV_HEADS, rows, 1)), _const_spec((1, ATT_W)),
                  tok(2 * QK_M_W), tok(MLSTM_W), tok(MLSTM_W), tok(LANES), _const_spec((1, MLSTM_W)),
                  st_c, st_n, st_m],
        out_specs=[tok(ATT_W), cache, cache, tok(MLSTM_W), st_c, st_n, st_m],
        out_shape=[sds((nseq * t, ATT_W), BF16), sds((nseq, w, KV_W), F32), sds((nseq, w, KV_W), F32),
                   sds((nseq * t, MLSTM_W), BF16), sds((nseq, N_HEADS_MLSTM, DK_MLSTM, DV_MLSTM), F32),
                   sds((nseq, QK_M_W), F32), sds((nseq, LANES), F32)],
        compiler_params=pltpu.CompilerParams(dimension_semantics=("arbitrary",), vmem_limit_bytes=VMEM_LIMIT),
        name="sample_mixers",
    )(qa, kvf, ck, cv, bias, sink_rows, g_att, qkm, vm, om, gt, g_m, c0, n0_seq, m0_seq)


def _out_mlp_kernel(x_ref, a_ref, hm_ref, woa_ref, wom_ref, gmlp_ref, wup_ref, wdn_ref, gfin_ref, y_ref):
    x1 = (x_ref[...]
          + jnp.dot(a_ref[...], woa_ref[...], preferred_element_type=F32)
          + jnp.dot(hm_ref[...], wom_ref[...], preferred_element_type=F32))
    h = _rms_rows(x1, gmlp_ref[...]).astype(BF16)
    u = jnp.maximum(jnp.dot(h, wup_ref[...], preferred_element_type=F32), 0.0)
    y = x1 + jnp.dot((u * u).astype(BF16), wdn_ref[...], preferred_element_type=F32)
    y_ref[...] = _rms_rows(y, gfin_ref[...])


def _out_mlp(x2d, mix_a, mix_m, wo_a, wo_m, g_mlp, w_up, w_dn, g_fin, tm):
    n = x2d.shape[0]
    row = lambda w: pl.BlockSpec((tm, w), lambda i: (i, 0))
    return pl.pallas_call(
        _out_mlp_kernel,
        grid=(n // tm,),
        in_specs=[row(D_MODEL), row(ATT_W), row(MLSTM_W), _const_spec(wo_a.shape), _const_spec(wo_m.shape),
                  _const_spec((1, D_MODEL)), _const_spec(w_up.shape), _const_spec(w_dn.shape),
                  _const_spec((1, D_MODEL))],
        out_specs=row(D_MODEL),
        out_shape=jax.ShapeDtypeStruct((n, D_MODEL), F32),
        compiler_params=pltpu.CompilerParams(dimension_semantics=("arbitrary",), vmem_limit_bytes=VMEM_LIMIT),
        name="out_mlp",
    )(x2d, mix_a, mix_m, wo_a, wo_m, g_mlp, w_up, w_dn, g_fin)


_CN_ROWS = DV_MLSTM + BF16_ROWS


def _prefix_max_lanes(x):
    n = x.shape[1]
    col = lax.broadcasted_iota(jnp.int32, x.shape, 1)
    k = 1
    while k < n:
        x = jnp.maximum(x, jnp.where(col >= k, pltpu.roll(x, k, axis=1), -jnp.inf))
        k *= 2
    return x


def _mlstm_stages(qT_ref, k_ref, vT_ref, omT_ref, gT_ref, gb_ref, write_mix, state, chunk, nsub):
    L = chunk
    i0 = lax.broadcasted_iota(jnp.int32, (L, L), 0)
    i1 = lax.broadcasted_iota(jnp.int32, (L, L), 1)
    causal_t = i0 <= i1
    triu = causal_t.astype(F32)
    ones_rows = (lax.broadcasted_iota(jnp.int32, (BF16_ROWS, L), 0) == 0).astype(BF16)
    zpad = jnp.zeros((LANES - 2 * N_GATES, L), F32)
    cn, m_r = state['cn'], state['m_r']
    hi = lax.Precision.HIGHEST
    heads = range(N_HEADS_MLSTM)
    subs = range(nsub)
    col_of = lambda j: slice(j * L, (j + 1) * L)
    hv = lambda h: slice(h * DV_MLSTM, (h + 1) * DV_MLSTM)
    hk = lambda h: slice(h * DK_MLSTM, (h + 1) * DK_MLSTM)
    a_r, b_r, pm_r = [], [], []
    for j in subs:
        g_t = gT_ref[:, col_of(j)]
        b_r.append(jnp.dot(g_t, triu, precision=hi, preferred_element_type=F32))
        a_r.append(pltpu.roll(g_t, N_HEADS_MLSTM, axis=0) - b_r[j])
    yield
    kq = [[jnp.dot(k_ref[col_of(j), hk(h)], qT_ref[hk(h), col_of(j)], preferred_element_type=F32)
           for h in heads] for j in subs]
    vT_ext = [[jnp.concatenate([vT_ref[hv(h), col_of(j)], ones_rows], axis=0) for h in heads] for j in subs]
    for j in subs:
        pm_r.append(_prefix_max_lanes(a_r[j]))
    yield
    big_m, inter, em, decay, cols_t = [], [], [], [], []
    for j in subs:
        big_m.append(jnp.maximum(m_r, pm_r[j]))
        inter.append(jnp.exp(m_r - big_m[j]))
        em.append(jnp.exp(-(b_r[j] + big_m[j])))
        m_end = big_m[j][:, L - 1:L]
        w_r = jnp.exp(a_r[j] - m_end)
        decay.append(jnp.exp(m_r[:, 0:1] - m_end))
        m_r = jnp.broadcast_to(b_r[j][:, L - 1:L] + m_end, (N_GATES, L))
        cols_t.append(jnp.concatenate([a_r[j], w_r, zpad], axis=0).T)
    sT = [[None] * N_HEADS_MLSTM for _ in subs]
    kw = [[None] * N_HEADS_MLSTM for _ in subs]
    intra = [[None] * N_HEADS_MLSTM for _ in subs]
    upd = [[None] * N_HEADS_MLSTM for _ in subs]
    for j in subs:
        for h in heads:
            r = _GATE0 + h
            d = jnp.exp(jnp.where(causal_t, cols_t[j][:, r:r + 1] - big_m[j][r:r + 1, :], -jnp.inf))
            sT[j][h] = (kq[j][h] * d).astype(BF16)
            w_col = cols_t[j][:, N_GATES + r:N_GATES + r + 1]
            kw[j][h] = (k_ref[col_of(j), hk(h)].astype(F32) * w_col).astype(BF16)
        for h in heads:
            intra[j][h] = jnp.dot(vT_ext[j][h], sT[j][h], preferred_element_type=F32)
            upd[j][h] = jnp.dot(vT_ext[j][h], kw[j][h], preferred_element_type=F32)
        yield
    for j in subs:
        qc = [jnp.dot(cn[h].astype(BF16), qT_ref[hk(h), col_of(j)], preferred_element_type=F32) for h in heads]
        for h in heads:
            r = _GATE0 + h
            nd = inter[j][r:r + 1, :] * qc[h] + intra[j][h]
            rr = 1.0 / jnp.maximum(jnp.abs(nd[DV_MLSTM:DV_MLSTM + 1, :]), em[j][r:r + 1, :])
            t = omT_ref[hv(h), col_of(j)] * nd[:DV_MLSTM, :]
            msq = jnp.mean(t * t, axis=0, keepdims=True)
            scale = rr * lax.rsqrt(rr * rr * msq + EPS)
            write_mix(hv(h), col_of(j), (t * scale * gb_ref[hv(h), :]).astype(BF16))
            cn[h] = decay[j][r:r + 1, :] * cn[h] + upd[j][h]
        yield
    state['m_r'] = m_r


def _mlstm_mlp_kernel(x_ref, a_ref, qT_ref, k_ref, vT_ref, omT_ref, gT_ref, gb_ref,
                      woa_ref, wom_ref, gmlp_ref, wup_ref, wdn_ref, gfin_ref,
                      y_ref, c_out, n_out, m_out, mix_s, cn_s, mr_s,
                      *, chunk, nsub, tiles_per_seq, n_tiles):
    s = pl.program_id(0)
    slot = s % 2

    @pl.when(s == 0)
    def _():
        mix_s[...] = jnp.zeros_like(mix_s)

    @pl.when(s % tiles_per_seq == 0)
    def _():
        cn_s[...] = jnp.zeros_like(cn_s)
        mr_s[...] = jnp.zeros_like(mr_s)

    state = {'cn': [cn_s[h] for h in range(N_HEADS_MLSTM)], 'm_r': mr_s[...]}

    def write_mix(rows, cols, val):
        mix_s[slot, rows, cols] = val

    stages = _mlstm_stages(qT_ref, k_ref, vT_ref, omT_ref, gT_ref, gb_ref, write_mix, state, chunk, nsub)
    next(stages)
    tdims = (((0,), (0,)), ((), ()))
    x1 = (x_ref[...]
          + lax.dot_general(a_ref[...], woa_ref[...], tdims, preferred_element_type=F32)
          + lax.dot_general(mix_s[1 - slot], wom_ref[...], tdims, preferred_element_type=F32))
    next(stages)
    h = _rms_rows(x1, gmlp_ref[...]).astype(BF16)
    u = jnp.maximum(jnp.dot(h, wup_ref[...], preferred_element_type=F32), 0.0)
    for _ in stages:
        pass
    y = x1 + jnp.dot((u * u).astype(BF16), wdn_ref[...], preferred_element_type=F32)
    y_ref[...] = _rms_rows(y, gfin_ref[...])
    for hh in range(N_HEADS_MLSTM):
        cn_s[hh] = state['cn'][hh]
    mr_s[...] = state['m_r']

    @pl.when((s % tiles_per_seq == tiles_per_seq - 1) & (s < n_tiles))
    def _():
        for hh in range(N_HEADS_MLSTM):
            c_out[0, hh] = cn_s[hh, :DV_MLSTM, :].T
            n_out[0, hh] = cn_s[hh, DV_MLSTM:DV_MLSTM + 1, :]
        m_out[0] = mr_s[...]


def _mlstm_mlp(x2d, mix_aT, qmT, km, vmT, omT, gT, g_b, wo_a, wo_m, g_mlp, w_up, w_dn, g_fin,
               nseq, t, chunk, nsub):
    tm = chunk * nsub
    n_tiles = (nseq * t) // tm
    tiles_per_seq = t // tm
    cur = lambda s: jnp.minimum(s, n_tiles - 1)
    prv = lambda s: jnp.maximum(s - 1, 0)
    colb = lambda r: pl.BlockSpec((r, tm), lambda s: (0, cur(s)))
    st = lambda a, d: pl.BlockSpec((1, N_HEADS_MLSTM, a, d), lambda s: (cur(s) // tiles_per_seq, 0, 0, 0))
    sds = jax.ShapeDtypeStruct
    kern = functools.partial(_mlstm_mlp_kernel, chunk=chunk, nsub=nsub,
                             tiles_per_seq=tiles_per_seq, n_tiles=n_tiles)
    return pl.pallas_call(
        kern,
        grid=(n_tiles + 1,),
        in_specs=[pl.BlockSpec((tm, D_MODEL), lambda s: (prv(s), 0)),
                  pl.BlockSpec((ATT_W, tm), lambda s: (0, prv(s))),
                  colb(QK_M_W), pl.BlockSpec((tm, QK_M_W), lambda s: (cur(s), 0)), colb(MLSTM_W), colb(MLSTM_W),
                  colb(N_GATES),
                  _const_spec(g_b.shape), _const_spec(wo_a.shape), _const_spec(wo_m.shape),
                  _const_spec((1, D_MODEL)), _const_spec(w_up.shape), _const_spec(w_dn.shape),
                  _const_spec((1, D_MODEL))],
        out_specs=[pl.BlockSpec((tm, D_MODEL), lambda s: (prv(s), 0)),
                   st(DK_MLSTM, DV_MLSTM), st(1, DK_MLSTM),
                   pl.BlockSpec((1, N_GATES, chunk), lambda s: (cur(s) // tiles_per_seq, 0, 0))],
        out_shape=[sds((nseq * t, D_MODEL), F32), sds((nseq, N_HEADS_MLSTM, DK_MLSTM, DV_MLSTM), F32),
                   sds((nseq, N_HEADS_MLSTM, 1, DK_MLSTM), F32), sds((nseq, N_GATES, chunk), F32)],
        scratch_shapes=[pltpu.VMEM((2, MLSTM_W, tm), BF16),
                        pltpu.VMEM((N_HEADS_MLSTM, _CN_ROWS, DK_MLSTM), F32),
                        pltpu.VMEM((N_GATES, chunk), F32)],
        compiler_params=pltpu.CompilerParams(dimension_semantics=("arbitrary",), vmem_limit_bytes=VMEM_LIMIT),
        name="mlstm_mlp",
    )(x2d, mix_aT, qmT, km, vmT, omT, gT, g_b, wo_a, wo_m, g_mlp, w_up, w_dn, g_fin)


def kernel(x_prompt, x_sample, cache_k_win, cache_v_win, state_C, state_n, state_m, rel_bias, norm_mix, w_in,
           b_gates, attn_sinks, norm_attn_out, norm_mlstm_out, w_out, norm_mlp, w_up, w_down, norm_final):
    depth = w_in.shape[0]
    assert depth == 1, "single-layer trunk"
    bp, sp = x_prompt.shape[:2]
    bs, ts = x_sample.shape[:2]
    wc = cache_k_win.shape[2]
    l = 0
    assert sp % PROJ_TILE == 0 and sp % MLP_TILE == 0 and sp >= WINDOW and (bs * ts) % SAMPLE_TILE == 0

    wl = w_in[l]
    cols = lambda lo, n: wl[:, lo:lo + n]
    w_n = jnp.concatenate([cols(_O_KV, 2 * KV_W), cols(_O_QKM + QK_M_W, QK_M_W),
                           jnp.pad(cols(_O_G, N_GATES), ((0, 0), (0, LANES - N_GATES)))], axis=1).astype(BF16)
    w_t = jnp.concatenate([cols(_O_QA, ATT_W), cols(_O_KV + KV_W, KV_W), cols(_O_QKM, QK_M_W), cols(_O_VM, MLSTM_W),
                           cols(_O_OM, MLSTM_W), jnp.pad(cols(_O_G, N_GATES), ((0, 0), (0, BF16_ROWS - N_GATES)))],
                          axis=1).T.astype(BF16)
    bg_row = jnp.pad(b_gates[l], (0, LANES - N_GATES)).reshape(1, LANES)
    bg_col = b_gates[l].reshape(N_GATES, 1)
    g_mix = norm_mix[l].reshape(1, D_MODEL)
    g_att = norm_attn_out[l].reshape(1, ATT_W)
    g_m = norm_mlstm_out[l].reshape(1, MLSTM_W)
    g_mlp = norm_mlp[l].reshape(1, D_MODEL)
    g_fin = norm_final.reshape(1, D_MODEL)
    gb_att = jnp.broadcast_to(norm_attn_out[l].reshape(ATT_W, 1), (ATT_W, WINDOW))
    gb_m = jnp.broadcast_to(norm_mlstm_out[l].reshape(MLSTM_W, 1), (MLSTM_W, MLSTM_CHUNK))
    sinks = attn_sinks[l]

    xp = x_prompt.reshape(bp * sp, D_MODEL)
    far_bucket = int(_t5_bucket_np(np.array(WINDOW)))
    sink_far = jnp.stack([sinks, rel_bias[far_bucket]])
    casts = [(w_out[l], 0, ATT_W), (w_out[l], ATT_W, MLSTM_W), (w_up[l], 0, D_MODEL), (w_down[l], 0, D_FF)]
    kvw, km, qmT, vmT, omT, gT, mix_a, wo_a, wo_m, wup, wdn = _proj_attn(
        xp, g_mix, w_n, w_t, bg_col, rel_bias, _prompt_buckets_t(), sink_far, gb_att, casts, sp, tm=PROJ_TILE)
    y_p, c_p, n_p, m_p = _mlstm_mlp(xp, mix_a, qmT, km, vmT, omT, gT, gb_m, wo_a, wo_m, g_mlp, wup, wdn, g_fin,
                                    bp, sp, chunk=MLSTM_CHUNK, nsub=MLP_TILE // MLSTM_CHUNK)
    m_p = m_p[:, _GATE0:_GATE0 + N_HEADS_MLSTM, 0]
    k_win_p = kvw[:, :KV_W].reshape(1, bp, WINDOW, N_KV_HEADS, HEAD_DIM_ATT)
    v_win_p = kvw[:, KV_W:].reshape(1, bp, WINDOW, N_KV_HEADS, HEAD_DIM_ATT)

    assert math.gcd(ts, 64) == ts and bs % SEQ_PER_BLOCK == 0, "sample group: one mLSTM chunk per sequence"
    kpad = ((wc + ts + BF16_ROWS - 1) // BF16_ROWS) * BF16_ROWS
    bias_s = _bias_table(rel_bias, _sample_buckets(ts, wc, kpad))
    bias_s = bias_s.reshape(N_KV_HEADS, GQA_GROUP * ts, kpad)
    sink_rows = jnp.repeat(sinks, ts).reshape(N_KV_HEADS, GQA_GROUP * ts, 1)
    xs = x_sample.reshape(bs * ts, D_MODEL)
    qa, kvf_s, qkm, vm, om, gt = _in_proj_s(xs, g_mix, w_n, w_t, bg_row, tm=SAMPLE_TILE)
    ck = cache_k_win[l].reshape(bs, wc, KV_W)
    cv = cache_v_win[l].reshape(bs, wc, KV_W)
    n0 = state_n[l].reshape(bs, QK_M_W)
    m0 = jnp.pad(state_m[l], ((0, 0), (_GATE0, LANES - _GATE0 - N_HEADS_MLSTM)))
    mix_a_s, kw_s, vw_s, mix_m_s, c_s, n_s, m_seq = _sample_mixers(
        qa, kvf_s, ck, cv, bias_s, sink_rows, g_att, qkm, vm, om, gt, g_m, state_C[l], n0, m0, bs, ts)
    m_s = m_seq[:, _GATE0:_GATE0 + N_HEADS_MLSTM]
    y_s = _out_mlp(xs, mix_a_s, mix_m_s, wo_a, wo_m, g_mlp, wup, wdn, g_fin, tm=SAMPLE_TILE)

    return (y_p.reshape(bp, sp, D_MODEL), y_s.reshape(bs, ts, D_MODEL),
            k_win_p, v_win_p,
            c_p[None], n_p.reshape(1, bp, N_HEADS_MLSTM, DK_MLSTM), m_p.reshape(1, bp, N_HEADS_MLSTM),
            kw_s.reshape(1, bs, wc, N_KV_HEADS, HEAD_DIM_ATT), vw_s.reshape(1, bs, wc, N_KV_HEADS, HEAD_DIM_ATT),
            c_s[None], n_s.reshape(1, bs, N_HEADS_MLSTM, DK_MLSTM), m_s.reshape(1, bs, N_HEADS_MLSTM))
```

```python
import functools
import itertools
import math

import numpy as np
import jax
import jax.numpy as jnp
from jax import lax
from jax.experimental import pallas as pl
from jax.experimental.pallas import tpu as pltpu

D_MODEL = 1024
N_HEADS_ATT = 8
N_KV_HEADS = 2
GQA_GROUP = N_HEADS_ATT // N_KV_HEADS
HEAD_DIM_ATT = 64
ATT_W = N_HEADS_ATT * HEAD_DIM_ATT
KV_W = N_KV_HEADS * HEAD_DIM_ATT
WINDOW = 128
NUM_BUCKETS = 32
MAX_DISTANCE = 128
N_HEADS_MLSTM = 4
DV_MLSTM = 128
DK_MLSTM = 64
MLSTM_W = N_HEADS_MLSTM * DV_MLSTM
QK_M_W = N_HEADS_MLSTM * DK_MLSTM
D_FF = 4 * D_MODEL
EPS = 1e-6
N_GATES = 2 * N_HEADS_MLSTM
LANES = 128
BF16_ROWS = 16
VMEM_LIMIT = 56 * 1024 * 1024
PROJ_TILE = 1024
MLP_TILE = 512
MLSTM_CHUNK = 256
SAMPLE_TILE = 512

F32 = jnp.float32
BF16 = jnp.bfloat16

_O_QA = 0
_O_KV = _O_QA + ATT_W
_O_QKM = _O_KV + 2 * KV_W
_O_VM = _O_QKM + 2 * QK_M_W
_O_OM = _O_VM + MLSTM_W
_O_G = _O_OM + MLSTM_W
_T_QA = 0
_T_VA = _T_QA + ATT_W
_T_QM = _T_VA + KV_W
_T_VM = _T_QM + QK_M_W
_T_OM = _T_VM + MLSTM_W
_T_G = _T_OM + MLSTM_W
_T_ROWS = _T_G + BF16_ROWS
_N_KV = 0
_N_KM = _N_KV + 2 * KV_W
_N_G = _N_KM + QK_M_W
_GATE0 = N_HEADS_MLSTM


def _const_spec(shape):
    nd = len(shape)
    return pl.BlockSpec(shape, lambda *_: (0,) * nd, pipeline_mode=pl.Buffered(1))


def _log_sigmoid(g):
    return jnp.minimum(g, 0.0) - jnp.log1p(jnp.exp(-jnp.abs(g)))


def _rms_rows(x, gain):
    return x * lax.rsqrt(jnp.mean(x * x, axis=-1, keepdims=True) + EPS) * gain


def _in_proj_s_kernel(x_ref, g_ref, wn_ref, wt_ref, bgr_ref, qa_ref, kvf_ref, qkm_ref, vm_ref, om_ref, gt_ref):
    hb = _rms_rows(x_ref[...], g_ref[...]).astype(BF16)
    z_n = jnp.dot(hb, wn_ref[...], preferred_element_type=F32)
    z_t = lax.dot_general(hb, wt_ref[...], (((1,), (1,)), ((), ())), preferred_element_type=F32)
    qa_ref[...] = (z_t[:, _T_QA:_T_QA + ATT_W] * (HEAD_DIM_ATT ** -0.5)).astype(BF16)
    kvf_ref[:, :KV_W] = z_n[:, _N_KV:_N_KV + KV_W]
    kvf_ref[:, KV_W:] = z_t[:, _T_VA:_T_VA + KV_W]
    qkm_ref[:, :QK_M_W] = z_t[:, _T_QM:_T_QM + QK_M_W].astype(BF16)
    qkm_ref[:, QK_M_W:] = (z_n[:, _N_KM:_N_KM + QK_M_W] * (DK_MLSTM ** -0.5)).astype(BF16)
    vm_ref[...] = z_t[:, _T_VM:_T_VM + MLSTM_W].astype(BF16)
    om_ref[...] = jax.nn.sigmoid(z_t[:, _T_OM:_T_OM + MLSTM_W])
    g = z_n[:, _N_G:_N_G + LANES] + bgr_ref[...]
    col = lax.broadcasted_iota(jnp.int32, g.shape, 1)
    gt_ref[...] = jnp.where(col < N_HEADS_MLSTM, g, _log_sigmoid(g))


def _in_proj_s(x2d, g_norm, w_n, w_t, bg_row, tm):
    n = x2d.shape[0]
    row = lambda w: pl.BlockSpec((tm, w), lambda i: (i, 0))
    sds = jax.ShapeDtypeStruct
    return pl.pallas_call(
        _in_proj_s_kernel,
        grid=(n // tm,),
        in_specs=[row(D_MODEL), _const_spec((1, D_MODEL)), _const_spec(w_n.shape), _const_spec(w_t.shape),
                  _const_spec((1, LANES))],
        out_specs=[row(ATT_W), row(2 * KV_W), row(2 * QK_M_W), row(MLSTM_W), row(MLSTM_W), row(LANES)],
        out_shape=[sds((n, ATT_W), BF16), sds((n, 2 * KV_W), F32), sds((n, 2 * QK_M_W), BF16),
                   sds((n, MLSTM_W), BF16), sds((n, MLSTM_W), F32), sds((n, LANES), F32)],
        compiler_params=pltpu.CompilerParams(dimension_semantics=("arbitrary",), vmem_limit_bytes=VMEM_LIMIT),
        name="in_proj_s",
    )(x2d, g_norm, w_n, w_t, bg_row)


def _t5_bucket_np(dist):
    max_exact = NUM_BUCKETS // 2
    d = np.maximum(dist, 0)
    ratio = np.maximum(d, 1).astype(np.float32) / np.float32(max_exact)
    large = max_exact + (np.log(ratio) / np.float32(math.log(MAX_DISTANCE / max_exact))
                         * np.float32(NUM_BUCKETS - max_exact)).astype(np.int32)
    large = np.minimum(large, NUM_BUCKETS - 1)
    return np.where(d < max_exact, d, large).astype(np.int32)


def _bias_table_kernel(rb_ref, bucket_ref, out_ref):
    bucket = bucket_ref[...]
    hit = [bucket == j for j in range(NUM_BUCKETS)]
    for h in range(N_HEADS_ATT):
        acc = jnp.full(bucket.shape, -jnp.inf, F32)
        for j in range(NUM_BUCKETS):
            acc = jnp.where(hit[j], rb_ref[j, h], acc)
        out_ref[h] = acc


def _prompt_buckets_t():
    qi = np.arange(WINDOW)[None, :]
    ci = np.arange(WINDOW)[:, None]
    upper = ci > qi
    general = _t5_bucket_np(np.where(upper, qi + WINDOW - ci, qi - ci))
    first = np.where(upper, -1, general)
    return np.concatenate([general, first], axis=0).astype(np.int32)


def _sample_buckets(t, w, padded):
    dist = (w + np.arange(t))[:, None] - np.arange(padded)[None, :]
    valid = (dist >= 0) & (dist <= WINDOW) & (np.arange(padded)[None, :] < w + t)
    return np.where(valid, _t5_bucket_np(dist), -1).astype(np.int32)


def _attn_block(i, qT, k_all, vT_all, first, bias_ref, sink_ref, gb_ref, write_out):
    W = WINDOW
    gd = lambda g: slice(g * HEAD_DIM_ATT, (g + 1) * HEAD_DIM_ATT)
    cols = slice(i * W, (i + 1) * W)
    is_first = first if i == 0 else False
    off = pl.multiple_of(jnp.where(first, W, 0), W) if i == 0 else 0
    k2 = k_all[i * W:(i + 2) * W, :]
    vT2 = vT_all[:, i * W:(i + 2) * W]
    ci = lax.broadcasted_iota(jnp.int32, (W, W), 0)
    qi = lax.broadcasted_iota(jnp.int32, (W, W), 1)
    upper = ci > qi
    eye = ci == qi
    scores = []
    for g in range(N_KV_HEADS):
        q_grp = jnp.concatenate([qT(slice((g * GQA_GROUP + j) * HEAD_DIM_ATT, (g * GQA_GROUP + j + 1) * HEAD_DIM_ATT),
                                    cols) for j in range(GQA_GROUP)], axis=1)
        scores.append(jnp.dot(k2[:, gd(g)], q_grp, preferred_element_type=F32))
    yield
    probs, rden, pfar = [], [], []
    for h in range(N_HEADS_ATT):
        g, j = divmod(h, GQA_GROUP)
        s_prev = scores[g][:W, j * W:(j + 1) * W]
        s_cur = scores[g][W:, j * W:(j + 1) * W]
        s = jnp.where(upper, s_prev, s_cur) + bias_ref[h, pl.ds(off, W), :]
        far_bias = jnp.where(is_first, -jnp.inf, sink_ref[1, h])
        s_far = jnp.sum(jnp.where(eye, s_prev, 0.0), axis=0, keepdims=True) + far_bias
        sink = sink_ref[0, h]
        m = jnp.maximum(jnp.maximum(jnp.max(s, axis=0, keepdims=True), s_far), sink)
        p = jnp.exp(s - m)
        p_far = jnp.exp(s_far - m)
        rden.append(1.0 / (jnp.sum(p, axis=0, keepdims=True) + p_far + jnp.exp(sink - m)))
        pfar.append(p_far)
        zero = jnp.zeros_like(p)
        probs.append(jnp.concatenate([jnp.where(upper, p, zero), jnp.where(upper, zero, p)], axis=0).astype(BF16))
    yield
    outs, ssq = [], jnp.zeros((1, W), F32)
    for g in range(N_KV_HEADS):
        p_grp = jnp.concatenate(probs[g * GQA_GROUP:(g + 1) * GQA_GROUP], axis=1)
        o_grp = jnp.dot(vT2[gd(g), :], p_grp, preferred_element_type=F32)
        v_far = vT2[gd(g), :W].astype(F32)
        for j in range(GQA_GROUP):
            h = g * GQA_GROUP + j
            o = (o_grp[:, j * W:(j + 1) * W] + v_far * pfar[h]) * rden[h]
            outs.append(o)
            ssq = ssq + jnp.sum(o * o, axis=0, keepdims=True)
    inv = lax.rsqrt(ssq * (1.0 / ATT_W) + EPS)
    for h in range(N_HEADS_ATT):
        rows = slice(h * HEAD_DIM_ATT, (h + 1) * HEAD_DIM_ATT)
        write_out(rows, cols, (outs[h] * inv * gb_ref[rows, :]).astype(BF16))
    yield


def _attn_blocks(qT, k_all, vT_all, first, bias_ref, sink_ref, gb_ref, write_out, qb):
    gens = [_attn_block(i, qT, k_all, vT_all, first, bias_ref, sink_ref, gb_ref, write_out) for i in range(qb)]
    next(gens[0])
    next(gens[0])
    for i in range(qb):
        if i + 1 < qb:
            next(gens[i + 1])
        yield
        next(gens[i])
        if i + 1 < qb:
            next(gens[i + 1])


PROJ_TOKEN_CHUNK = 256
PROJ_ROW_SPLIT = 1024


def _proj_attn_kernel(x_ref, g_ref, wn_ref, wt_ref, bgc_ref, rb_ref, bucket_ref, sink_ref, gb_ref, *refs,
                      tm, tiles_per_seq, n_cast):
    cast_in, refs = refs[:n_cast], refs[n_cast:]
    kvw_ref, km_ref, qmT_ref, vmT_ref, omT_ref, gT_ref, mix_ref = refs[:7]
    cast_out, (q_s, k_s, vT_s, hk_s, hv_s, bias_ref) = refs[7:7 + n_cast], refs[7 + n_cast:]
    for src, dst in zip(cast_in, cast_out):
        dst[...] = src[...].astype(BF16)
    s = pl.program_id(0)
    slot = s % 2
    prev = 1 - slot
    qb = tm // WINDOW

    @pl.when(s == 0)
    def _():
        q_s[...] = jnp.zeros_like(q_s)
        k_s[...] = jnp.zeros_like(k_s)
        vT_s[...] = jnp.zeros_like(vT_s)
        hk_s[...] = jnp.zeros_like(hk_s)
        hv_s[...] = jnp.zeros_like(hv_s)
        _bias_table_kernel(rb_ref, bucket_ref, bias_ref)

    first =(s + tiles_per_seq - 1) % tiles_per_seq == 0
    k_all = jnp.concatenate([hk_s[...], k_s[prev]], axis=0)
    vT_all = jnp.concatenate([hv_s[...], vT_s[prev]], axis=1)

    def write_out(rows, cols, val):
        mix_ref[rows, cols] = val

    blocks = _attn_blocks(lambda rows, cols: q_s[prev, rows, cols], k_all, vT_all, first, bias_ref, sink_ref, gb_ref,
                          write_out, qb)
    next(blocks)
    hb = _rms_rows(x_ref[...], g_ref[...]).astype(BF16)

    def put_q(lo, tc, z):
        q_s[slot, lo:lo + z.shape[0], tc] = (z * (HEAD_DIM_ATT ** -0.5)).astype(BF16)

    def put_v(lo, tc, z):
        vT_s[slot, lo:lo + z.shape[0], tc] = z.astype(BF16)

    def put_qm(lo, tc, z):
        qmT_ref[lo:lo + z.shape[0], tc] = z.astype(BF16)

    def put_vm(lo, tc, z):
        vmT_ref[lo:lo + z.shape[0], tc] = z.astype(BF16)

    def put_om(lo, tc, z):
        omT_ref[lo:lo + z.shape[0], tc] = jax.nn.sigmoid(z)

    def put_g(lo, tc, z):
        g_t = z[:N_GATES, :] + bgc_ref[...]
        row = lax.broadcasted_iota(jnp.int32, g_t.shape, 0)
        gT_ref[:, tc] = jnp.where(row < N_HEADS_MLSTM, g_t, _log_sigmoid(g_t))

    segments = [(_T_QA, _T_VA, put_q), (_T_VA, _T_QM, put_v), (_T_QM, _T_VM, put_qm), (_T_VM, _T_OM, put_vm),
                (_T_OM, _T_G, put_om), (_T_G, _T_ROWS, put_g)]

    def proj_piece(lo, hi, tc):
        z = lax.dot_general(wt_ref[lo:hi, :], hb[tc, :], (((1,), (1,)), ((), ())), preferred_element_type=F32)
        for a, b, put in segments:
            if a < hi and lo < b:
                put(max(lo, a) - a, tc, z[max(lo, a) - lo:min(hi, b) - lo, :])

    z_n = jnp.dot(hb, wn_ref[:, :_N_G], preferred_element_type=F32)
    kvw_ref[...] = z_n[tm - WINDOW:, _N_KV:_N_KV + 2 * KV_W]
    km_ref[...] = (z_n[:, _N_KM:_N_KM + QK_M_W] * (DK_MLSTM ** -0.5)).astype(BF16)
    k_s[slot] = z_n[:, _N_KV:_N_KV + KV_W].astype(BF16)
    pieces = [(lo, hi, slice(c, c + PROJ_TOKEN_CHUNK))
              for c in range(0, tm, PROJ_TOKEN_CHUNK) for lo, hi in ((0, PROJ_ROW_SPLIT), (PROJ_ROW_SPLIT, _T_ROWS))]
    proj_piece(*pieces.pop(0))
    for _ in blocks:
        if pieces:
            proj_piece(*pieces.pop(0))
    for piece in pieces:
        proj_piece(*piece)
    hk_s[...] = k_s[prev, tm - WINDOW:, :]
    hv_s[...] = vT_s[prev, :, tm - WINDOW:]


def _proj_attn(x2d, g_norm, w_n, w_t, bg_col, rel_bias, bucket, sinks, g_b, casts, seq, tm):
    n = x2d.shape[0]
    n_tiles = n // tm
    tiles_per_seq = seq // tm
    cur = lambda s: jnp.minimum(s, n_tiles - 1)
    prv = lambda s: jnp.maximum(s - 1, 0)
    row = lambda w: pl.BlockSpec((tm, w), lambda s: (cur(s), 0))
    colb = lambda r: pl.BlockSpec((r, tm), lambda s: (0, cur(s)))
    sds = jax.ShapeDtypeStruct
    cast_in, cast_out, cast_shape = [], [], []
    for arr, first, count in casts:
        chunk = count // n_tiles
        assert chunk * n_tiles == count and chunk % BF16_ROWS == 0 and first % chunk == 0
        cols = arr.shape[1]
        cast_in.append(pl.BlockSpec((chunk, cols), lambda s, o=first // chunk: (o + cur(s), 0)))
        cast_out.append(pl.BlockSpec((chunk, cols), lambda s: (cur(s), 0)))
        cast_shape.append(sds((count, cols), BF16))
    return pl.pallas_call(
        functools.partial(_proj_attn_kernel, tm=tm, tiles_per_seq=tiles_per_seq, n_cast=len(casts)),
        grid=(n_tiles + 1,),
        in_specs=[row(D_MODEL), _const_spec((1, D_MODEL)), _const_spec(w_n.shape), _const_spec(w_t.shape),
                  _const_spec((N_GATES, 1)), pl.BlockSpec(memory_space=pltpu.SMEM), _const_spec(bucket.shape),
                  pl.BlockSpec(memory_space=pltpu.SMEM), _const_spec(g_b.shape)] + cast_in,
        out_specs=[pl.BlockSpec((WINDOW, 2 * KV_W), lambda s: (cur(s) // tiles_per_seq, 0)),
                   row(QK_M_W), colb(QK_M_W), colb(MLSTM_W), colb(MLSTM_W), colb(N_GATES),
                   pl.BlockSpec((ATT_W, tm), lambda s: (0, prv(s)))] + cast_out,
        out_shape=[sds((n // seq * WINDOW, 2 * KV_W), F32), sds((n, QK_M_W), BF16), sds((QK_M_W, n), BF16),
                   sds((MLSTM_W, n), BF16),
                   sds((MLSTM_W, n), F32), sds((N_GATES, n), F32), sds((ATT_W, n), BF16)] + cast_shape,
        scratch_shapes=[pltpu.VMEM((2, ATT_W, tm), BF16), pltpu.VMEM((2, tm, KV_W), BF16),
                        pltpu.VMEM((2, KV_W, tm), BF16), pltpu.VMEM((WINDOW, KV_W), BF16),
                        pltpu.VMEM((KV_W, WINDOW), BF16), pltpu.VMEM((N_HEADS_ATT,) + bucket.shape, F32)],
        compiler_params=pltpu.CompilerParams(dimension_semantics=("arbitrary",), vmem_limit_bytes=VMEM_LIMIT),
        name="proj_attn",
    )(x2d, g_norm, w_n, w_t, bg_col, rel_bias, jnp.asarray(bucket), sinks, g_b, *[arr for arr, _, _ in casts])


def _attn_sample_stages(q_ref, kvf_ref, ck_ref, cv_ref, bias_ref, sink_ref, g_ref, out_ref, kw_ref, vw_ref,
                        *, bb, t, w, kpad):
    ck = ck_ref[...]
    cv = cv_ref[...]
    k_new = kvf_ref[:, :KV_W].reshape(bb, t, KV_W)
    v_new = kvf_ref[:, KV_W:].reshape(bb, t, KV_W)
    kw_ref[:, :w - t, :] = ck[:, t:, :]
    kw_ref[:, w - t:, :] = k_new
    vw_ref[:, :w - t, :] = cv[:, t:, :]
    vw_ref[:, w - t:, :] = v_new
    zpad = jnp.zeros((bb, kpad - w - t, KV_W), F32)
    kk = jnp.concatenate([ck, k_new, zpad], axis=1).astype(BF16)
    vv = jnp.concatenate([cv, v_new, zpad], axis=1).astype(BF16)
    q3 = q_ref[...].astype(F32).reshape(bb, t, ATT_W)
    yield
    gd = lambda g: slice(g * HEAD_DIM_ATT, (g + 1) * HEAD_DIM_ATT)
    scores = []
    for g in range(N_KV_HEADS):
        qg = jnp.concatenate([q3[:, :, (g * GQA_GROUP + i) * HEAD_DIM_ATT:(g * GQA_GROUP + i + 1) * HEAD_DIM_ATT]
                              for i in range(GQA_GROUP)], axis=1).astype(BF16)
        zq = jnp.zeros_like(qg)
        qg = jnp.concatenate([qg, zq] if g == 0 else [zq, qg], axis=-1)
        scores.append(jnp.einsum('bqd,bkd->bqk', qg, kk, preferred_element_type=F32))
    yield
    probs, rden = [], []
    for g in range(N_KV_HEADS):
        s = scores[g] + bias_ref[g * GQA_GROUP:(g + 1) * GQA_GROUP].reshape(GQA_GROUP * t, kpad)
        sink = sink_ref[g]
        m = jnp.maximum(jnp.max(s, axis=-1, keepdims=True), sink)
        p = jnp.exp(s - m)
        rden.append(1.0 / (jnp.sum(p, axis=-1, keepdims=True) + jnp.exp(sink - m)))
        probs.append(p.astype(BF16))
        yield
    outs = []
    ssq = jnp.zeros((bb, 1, t, HEAD_DIM_ATT), F32)
    for g in range(N_KV_HEADS):
        o = jnp.einsum('bqk,bkd->bqd', probs[g], vv, preferred_element_type=F32)[:, :, gd(g)] * rden[g]
        o = o.reshape(bb, GQA_GROUP, t, HEAD_DIM_ATT)
        outs.append(o)
        ssq = ssq + jnp.sum(o * o, axis=1, keepdims=True)
    yield
    inv = lax.rsqrt(jnp.sum(ssq, axis=-1, keepdims=True) * (1.0 / ATT_W) + EPS)
    for g in range(N_KV_HEADS):
        y = outs[g] * inv
        for i in range(GQA_GROUP):
            sl = slice((g * GQA_GROUP + i) * HEAD_DIM_ATT, (g * GQA_GROUP + i + 1) * HEAD_DIM_ATT)
            out_ref[:, sl] = (y[:, i].reshape(bb * t, HEAD_DIM_ATT) * g_ref[:, sl]).astype(BF16)


SEQ_PER_BLOCK = 16


def _seg_scan_rows(x, seg, forward):
    n = x.shape[0]
    pos = lax.broadcasted_iota(jnp.int32, x.shape, 0) % seg
    k = 1
    while k < seg:
        if forward:
            x = jnp.maximum(x, jnp.where(pos >= k, pltpu.roll(x, k, axis=0), -jnp.inf))
        else:
            x = jnp.maximum(x, jnp.where(pos < seg - k, pltpu.roll(x, n - k, axis=0), -jnp.inf))
        k *= 2
    return x


def _mlstm_blk_stages(qk_ref, v_ref, om_ref, gt_ref, gm_ref, c0_ref, n0_ref, m0_ref,
                      mix_ref, c_out, n_out, m_out, *, t):
    S = SEQ_PER_BLOCK
    L = S * t
    hi = lax.Precision.HIGHEST
    ti = lax.broadcasted_iota(jnp.int32, (L, L), 0)
    si = lax.broadcasted_iota(jnp.int32, (L, L), 1)
    same = (ti // t) == (si // t)
    causal = same & (si <= ti)
    tril_seg = causal.astype(F32)
    ones_seg = same.astype(F32)
    ones_col = (lax.broadcasted_iota(jnp.int32, (L, DV_MLSTM), 1) == 0).astype(BF16)
    bd_mask = (lax.broadcasted_iota(jnp.int32, (L, S * DK_MLSTM), 0) // t
               == lax.broadcasted_iota(jnp.int32, (L, S * DK_MLSTM), 1) // DK_MLSTM)
    bd_mask_t = (lax.broadcasted_iota(jnp.int32, (S * DK_MLSTM, L), 0) // DK_MLSTM
                 == lax.broadcasted_iota(jnp.int32, (S * DK_MLSTM, L), 1) // t)
    heads = range(N_HEADS_MLSTM)
    hk = lambda h: slice(h * DK_MLSTM, (h + 1) * DK_MLSTM)
    hkk = lambda h: slice(QK_M_W + h * DK_MLSTM, QK_M_W + (h + 1) * DK_MLSTM)
    hv = lambda h: slice(h * DV_MLSTM, (h + 1) * DV_MLSTM)

    gt = gt_ref[...]
    per_token = lambda x: jnp.broadcast_to(x[:, None, :], (S, t, x.shape[-1])).reshape(L, x.shape[-1])
    m_prev = per_token(m0_ref[...])
    n_rows = per_token(n0_ref[...])
    b = jnp.dot(tril_seg, gt, precision=hi, preferred_element_type=F32)
    b_end = jnp.dot(ones_seg, gt, precision=hi, preferred_element_type=F32)
    yield
    a = pltpu.roll(gt, N_HEADS_MLSTM, axis=1) - b
    big_m = jnp.maximum(m_prev, _seg_scan_rows(a, t, True))
    inter = jnp.exp(m_prev - big_m)
    em = jnp.exp(-(b + big_m))
    m_end = _seg_scan_rows(big_m, t, False)
    w = jnp.exp(a - m_end)
    decay = jnp.exp(m_prev - m_end)
    m_out[...] = (b_end + m_end).reshape(S, t, LANES)[:, 0, :]
    a_t = a.T
    yield
    q = [qk_ref[:, hk(h)] for h in heads]
    k = [qk_ref[:, hkk(h)] for h in heads]
    v_ext = [jnp.concatenate([v_ref[:, hv(h)], ones_col], axis=1) for h in heads]
    qk = [lax.dot_general(q[h], k[h], (((1,), (1,)), ((), ())), preferred_element_type=F32) for h in heads]
    c_all = [c0_ref[:, h].reshape(S * DK_MLSTM, DV_MLSTM) for h in heads]
    q_bd = []
    for h in heads:
        q2 = jnp.concatenate([q[h], q[h]], axis=1)
        q_bd.append(jnp.where(bd_mask, jnp.concatenate([q2] * (S // 2), axis=1), jnp.zeros((), BF16)))
    qc = [jnp.dot(q_bd[h], c_all[h].astype(BF16), preferred_element_type=F32) for h in heads]
    yield
    s = []
    for h in heads:
        ln = _GATE0 + h
        d = jnp.exp(jnp.where(causal, a_t[ln:ln + 1, :] - big_m[:, ln:ln + 1], -jnp.inf))
        s.append((qk[h] * d).astype(BF16))
    yield
    sv = [jnp.dot(s[h], v_ext[h], preferred_element_type=F32) for h in heads]
    kw, kw_bd = [], []
    for h in heads:
        ln = _GATE0 + h
        kw_f = k[h].astype(F32) * w[:, ln:ln + 1]
        kw.append(kw_f.astype(BF16))
        kw_bd.append(jnp.where(bd_mask_t, jnp.concatenate([kw_f.T] * S, axis=0), 0.0).astype(BF16))
    upd = [jnp.dot(kw_bd[h], v_ref[:, hv(h)], preferred_element_type=F32) for h in heads]
    seq_of_tok = lax.broadcasted_iota(jnp.int32, (S, L), 1) // t == lax.broadcasted_iota(jnp.int32, (S, L), 0)
    first_tok = lax.broadcasted_iota(jnp.int32, (S, L), 1) == lax.broadcasted_iota(jnp.int32, (S, L), 0) * t
    n_upd = [jnp.dot(seq_of_tok.astype(BF16), kw[h], preferred_element_type=F32) for h in heads]
    decay_seq = jnp.dot(first_tok.astype(F32), decay, precision=hi, preferred_element_type=F32)
    yield
    for h in heads:
        ln = _GATE0 + h
        inter_h = inter[:, ln:ln + 1]
        qn = jnp.sum(q[h].astype(F32) * n_rows[:, hk(h)], axis=1, keepdims=True)
        den = inter_h * qn + sv[h][:, DV_MLSTM:DV_MLSTM + 1]
        r = 1.0 / jnp.maximum(jnp.abs(den), em[:, ln:ln + 1])
        tt = om_ref[:, hv(h)] * (inter_h * qc[h] + sv[h][:, :DV_MLSTM])
        msq = jnp.mean(tt * tt, axis=1, keepdims=True)
        scale = r * lax.rsqrt(r * r * msq + EPS)
        mix_ref[:, hv(h)] = (tt * scale * gm_ref[:, hv(h)]).astype(BF16)
        dec = decay[:, ln:ln + 1].reshape(S, t, 1)[:, 0:1, :]
        c_out[:, h] = dec * c0_ref[:, h] + upd[h].reshape(S, DK_MLSTM, DV_MLSTM)
        n_out[:, hk(h)] = decay_seq[:, ln:ln + 1] * n0_ref[:, hk(h)] + n_upd[h]


def _sample_mixers_kernel(q_ref, kvf_ref, ck_ref, cv_ref, rb_ref, bucket_ref, sink_ref, ga_ref,
                          qk_ref, v_ref, om_ref, gt_ref, gm_ref, c0_ref, n0_ref, m0_ref,
                          mixa_ref, kw_ref, vw_ref, mixm_ref, c_out, n_out, m_out, bias_ref, *, t, w, kpad):
    @pl.when(pl.program_id(0) == 0)
    def _():
        _bias_table_kernel(rb_ref, bucket_ref, bias_ref)

    attn = _attn_sample_stages(q_ref, kvf_ref, ck_ref, cv_ref, bias_ref, sink_ref, ga_ref, mixa_ref, kw_ref, vw_ref,
                               bb=SEQ_PER_BLOCK, t=t, w=w, kpad=kpad)
    mlstm = _mlstm_blk_stages(qk_ref, v_ref, om_ref, gt_ref, gm_ref, c0_ref, n0_ref, m0_ref,
                              mixm_ref, c_out, n_out, m_out, t=t)
    for _ in itertools.zip_longest(attn, mlstm):
        pass


def _sample_mixers(qa, kvf, ck, cv, rel_bias, bucket, sink_rows, g_att, qkm, vm, om, gt, g_m, c0, n0_seq, m0_seq,
                   nseq, t):
    S = SEQ_PER_BLOCK
    L = S * t
    w = ck.shape[1]
    kpad = bucket.shape[-1]
    rows = GQA_GROUP * t
    tok = lambda wd: pl.BlockSpec((L, wd), lambda i: (i, 0))
    cache = pl.BlockSpec((S, w, KV_W), lambda i: (i, 0, 0))
    st_c = pl.BlockSpec((S, N_HEADS_MLSTM, DK_MLSTM, DV_MLSTM), lambda i: (i, 0, 0, 0))
    st_n = pl.BlockSpec((S, QK_M_W), lambda i: (i, 0))
    st_m = pl.BlockSpec((S, LANES), lambda i: (i, 0))
    sds = jax.ShapeDtypeStruct
    return pl.pallas_call(
        functools.partial(_sample_mixers_kernel, t=t, w=w, kpad=kpad),
        grid=(nseq // S,),
        in_specs=[tok(ATT_W), tok(2 * KV_W), cache, cache,
                  pl.BlockSpec(memory_space=pltpu.SMEM), _const_spec(bucket.shape),
                  _const_spec((N_KV_HEADS, rows, 1)), _const_spec((1, ATT_W)),
                  tok(2 * QK_M_W), tok(MLSTM_W), tok(MLSTM_W), tok(LANES), _const_spec((1, MLSTM_W)),
                  st_c, st_n, st_m],
        out_specs=[tok(ATT_W), cache, cache, tok(MLSTM_W), st_c, st_n, st_m],
        out_shape=[sds((nseq * t, ATT_W), BF16), sds((nseq, w, KV_W), F32), sds((nseq, w, KV_W), F32),
                   sds((nseq * t, MLSTM_W), BF16), sds((nseq, N_HEADS_MLSTM, DK_MLSTM, DV_MLSTM), F32),
                   sds((nseq, QK_M_W), F32), sds((nseq, LANES), F32)],
        scratch_shapes=[pltpu.VMEM((N_HEADS_ATT,) + bucket.shape, F32)],
        compiler_params=pltpu.CompilerParams(dimension_semantics=("arbitrary",), vmem_limit_bytes=VMEM_LIMIT),
        name="sample_mixers",
    )(qa, kvf, ck, cv, rel_bias, jnp.asarray(bucket), sink_rows, g_att, qkm, vm, om, gt, g_m, c0, n0_seq, m0_seq)


def _out_mlp_kernel(x_ref, a_ref, hm_ref, woa_ref, wom_ref, gmlp_ref, wup_ref, wdn_ref, gfin_ref, y_ref):
    x1 = (x_ref[...]
          + jnp.dot(a_ref[...], woa_ref[...], preferred_element_type=F32)
          + jnp.dot(hm_ref[...], wom_ref[...], preferred_element_type=F32))
    h = _rms_rows(x1, gmlp_ref[...]).astype(BF16)
    u = jnp.maximum(jnp.dot(h, wup_ref[...], preferred_element_type=F32), 0.0)
    y = x1 + jnp.dot((u * u).astype(BF16), wdn_ref[...], preferred_element_type=F32)
    y_ref[...] = _rms_rows(y, gfin_ref[...])


def _out_mlp(x2d, mix_a, mix_m, wo_a, wo_m, g_mlp, w_up, w_dn, g_fin, tm):
    n = x2d.shape[0]
    row = lambda w: pl.BlockSpec((tm, w), lambda i: (i, 0))
    return pl.pallas_call(
        _out_mlp_kernel,
        grid=(n // tm,),
        in_specs=[row(D_MODEL), row(ATT_W), row(MLSTM_W), _const_spec(wo_a.shape), _const_spec(wo_m.shape),
                  _const_spec((1, D_MODEL)), _const_spec(w_up.shape), _const_spec(w_dn.shape),
                  _const_spec((1, D_MODEL))],
        out_specs=row(D_MODEL),
        out_shape=jax.ShapeDtypeStruct((n, D_MODEL), F32),
        compiler_params=pltpu.CompilerParams(dimension_semantics=("arbitrary",), vmem_limit_bytes=VMEM_LIMIT),
        name="out_mlp",
    )(x2d, mix_a, mix_m, wo_a, wo_m, g_mlp, w_up, w_dn, g_fin)


_CN_ROWS = DV_MLSTM + BF16_ROWS


def _prefix_max_lanes(x):
    n = x.shape[1]
    col = lax.broadcasted_iota(jnp.int32, x.shape, 1)
    k = 1
    while k < n:
        x = jnp.maximum(x, jnp.where(col >= k, pltpu.roll(x, k, axis=1), -jnp.inf))
        k *= 2
    return x


def _mlstm_stages(qT_ref, k_ref, vT_ref, omT_ref, gT_ref, gb_ref, write_mix, state, chunk, nsub):
    L = chunk
    i0 = lax.broadcasted_iota(jnp.int32, (L, L), 0)
    i1 = lax.broadcasted_iota(jnp.int32, (L, L), 1)
    causal_t = i0 <= i1
    triu = causal_t.astype(F32)
    ones_rows = (lax.broadcasted_iota(jnp.int32, (BF16_ROWS, L), 0) == 0).astype(BF16)
    zpad = jnp.zeros((LANES - 2 * N_GATES, L), F32)
    cn, m_r = state['cn'], state['m_r']
    hi = lax.Precision.HIGHEST
    heads = range(N_HEADS_MLSTM)
    subs = range(nsub)
    col_of = lambda j: slice(j * L, (j + 1) * L)
    hv = lambda h: slice(h * DV_MLSTM, (h + 1) * DV_MLSTM)
    hk = lambda h: slice(h * DK_MLSTM, (h + 1) * DK_MLSTM)
    a_r, b_r, pm_r = [], [], []
    for j in subs:
        g_t = gT_ref[:, col_of(j)]
        b_r.append(jnp.dot(g_t, triu, precision=hi, preferred_element_type=F32))
        a_r.append(pltpu.roll(g_t, N_HEADS_MLSTM, axis=0) - b_r[j])
    yield
    kq = [[jnp.dot(k_ref[col_of(j), hk(h)], qT_ref[hk(h), col_of(j)], preferred_element_type=F32)
           for h in heads] for j in subs]
    vT_ext = [[jnp.concatenate([vT_ref[hv(h), col_of(j)], ones_rows], axis=0) for h in heads] for j in subs]
    for j in subs:
        pm_r.append(_prefix_max_lanes(a_r[j]))
    yield
    big_m, inter, em, decay, cols_t = [], [], [], [], []
    for j in subs:
        big_m.append(jnp.maximum(m_r, pm_r[j]))
        inter.append(jnp.exp(m_r - big_m[j]))
        em.append(jnp.exp(-(b_r[j] + big_m[j])))
        m_end = big_m[j][:, L - 1:L]
        w_r = jnp.exp(a_r[j] - m_end)
        decay.append(jnp.exp(m_r[:, 0:1] - m_end))
        m_r = jnp.broadcast_to(b_r[j][:, L - 1:L] + m_end, (N_GATES, L))
        cols_t.append(jnp.concatenate([a_r[j], w_r, zpad], axis=0).T)
    sT = [[None] * N_HEADS_MLSTM for _ in subs]
    kw = [[None] * N_HEADS_MLSTM for _ in subs]
    intra = [[None] * N_HEADS_MLSTM for _ in subs]
    upd = [[None] * N_HEADS_MLSTM for _ in subs]
    for j in subs:
        for h in heads:
            r = _GATE0 + h
            d = jnp.exp(jnp.where(causal_t, cols_t[j][:, r:r + 1] - big_m[j][r:r + 1, :], -jnp.inf))
            sT[j][h] = (kq[j][h] * d).astype(BF16)
            w_col = cols_t[j][:, N_GATES + r:N_GATES + r + 1]
            kw[j][h] = (k_ref[col_of(j), hk(h)].astype(F32) * w_col).astype(BF16)
        for h in heads:
            intra[j][h] = jnp.dot(vT_ext[j][h], sT[j][h], preferred_element_type=F32)
            upd[j][h] = jnp.dot(vT_ext[j][h], kw[j][h], preferred_element_type=F32)
        yield
    for j in subs:
        qc = [jnp.dot(cn[h].astype(BF16), qT_ref[hk(h), col_of(j)], preferred_element_type=F32) for h in heads]
        for h in heads:
            r = _GATE0 + h
            nd = inter[j][r:r + 1, :] * qc[h] + intra[j][h]
            rr = 1.0 / jnp.maximum(jnp.abs(nd[DV_MLSTM:DV_MLSTM + 1, :]), em[j][r:r + 1, :])
            t = omT_ref[hv(h), col_of(j)] * nd[:DV_MLSTM, :]
            msq = jnp.mean(t * t, axis=0, keepdims=True)
            scale = rr * lax.rsqrt(rr * rr * msq + EPS)
            write_mix(hv(h), col_of(j), (t * scale * gb_ref[hv(h), :]).astype(BF16))
            cn[h] = decay[j][r:r + 1, :] * cn[h] + upd[j][h]
        yield
    state['m_r'] = m_r


def _mlstm_mlp_kernel(x_ref, a_ref, qT_ref, k_ref, vT_ref, omT_ref, gT_ref, gb_ref,
                      woa_ref, wom_ref, gmlp_ref, wup_ref, wdn_ref, gfin_ref,
                      y_ref, c_out, n_out, m_out, mix_s, cn_s, mr_s,
                      *, chunk, nsub, tiles_per_seq, n_tiles):
    s = pl.program_id(0)
    slot = s % 2

    @pl.when(s == 0)
    def _():
        mix_s[...] = jnp.zeros_like(mix_s)

    @pl.when(s % tiles_per_seq == 0)
    def _():
        cn_s[...] = jnp.zeros_like(cn_s)
        mr_s[...] = jnp.zeros_like(mr_s)

    state = {'cn': [cn_s[h] for h in range(N_HEADS_MLSTM)], 'm_r': mr_s[...]}

    def write_mix(rows, cols, val):
        mix_s[slot, rows, cols] = val

    stages = _mlstm_stages(qT_ref, k_ref, vT_ref, omT_ref, gT_ref, gb_ref, write_mix, state, chunk, nsub)
    next(stages)
    tdims = (((0,), (0,)), ((), ()))
    x1 = (x_ref[...]
          + lax.dot_general(a_ref[...], woa_ref[...], tdims, preferred_element_type=F32)
          + lax.dot_general(mix_s[1 - slot], wom_ref[...], tdims, preferred_element_type=F32))
    next(stages)
    h = _rms_rows(x1, gmlp_ref[...]).astype(BF16)
    u = jnp.maximum(jnp.dot(h, wup_ref[...], preferred_element_type=F32), 0.0)
    for _ in stages:
        pass
    y = x1 + jnp.dot((u * u).astype(BF16), wdn_ref[...], preferred_element_type=F32)
    y_ref[...] = _rms_rows(y, gfin_ref[...])
    for hh in range(N_HEADS_MLSTM):
        cn_s[hh] = state['cn'][hh]
    mr_s[...] = state['m_r']

    @pl.when((s % tiles_per_seq == tiles_per_seq - 1) & (s < n_tiles))
    def _():
        for hh in range(N_HEADS_MLSTM):
            c_out[0, hh] = cn_s[hh, :DV_MLSTM, :].T
            n_out[0, hh] = cn_s[hh, DV_MLSTM:DV_MLSTM + 1, :]
        m_out[0] = mr_s[...]


def _mlstm_mlp(x2d, mix_aT, qmT, km, vmT, omT, gT, g_b, wo_a, wo_m, g_mlp, w_up, w_dn, g_fin,
               nseq, t, chunk, nsub):
    tm = chunk * nsub
    n_tiles = (nseq * t) // tm
    tiles_per_seq = t // tm
    cur = lambda s: jnp.minimum(s, n_tiles - 1)
    prv = lambda s: jnp.maximum(s - 1, 0)
    colb = lambda r: pl.BlockSpec((r, tm), lambda s: (0, cur(s)))
    st = lambda a, d: pl.BlockSpec((1, N_HEADS_MLSTM, a, d), lambda s: (cur(s) // tiles_per_seq, 0, 0, 0))
    sds = jax.ShapeDtypeStruct
    kern = functools.partial(_mlstm_mlp_kernel, chunk=chunk, nsub=nsub,
                             tiles_per_seq=tiles_per_seq, n_tiles=n_tiles)
    return pl.pallas_call(
        kern,
        grid=(n_tiles + 1,),
        in_specs=[pl.BlockSpec((tm, D_MODEL), lambda s: (prv(s), 0)),
                  pl.BlockSpec((ATT_W, tm), lambda s: (0, prv(s))),
                  colb(QK_M_W), pl.BlockSpec((tm, QK_M_W), lambda s: (cur(s), 0)), colb(MLSTM_W), colb(MLSTM_W),
                  colb(N_GATES),
                  _const_spec(g_b.shape), _const_spec(wo_a.shape), _const_spec(wo_m.shape),
                  _const_spec((1, D_MODEL)), _const_spec(w_up.shape), _const_spec(w_dn.shape),
                  _const_spec((1, D_MODEL))],
        out_specs=[pl.BlockSpec((tm, D_MODEL), lambda s: (prv(s), 0)),
                   st(DK_MLSTM, DV_MLSTM), st(1, DK_MLSTM),
                   pl.BlockSpec((1, N_GATES, chunk), lambda s: (cur(s) // tiles_per_seq, 0, 0))],
        out_shape=[sds((nseq * t, D_MODEL), F32), sds((nseq, N_HEADS_MLSTM, DK_MLSTM, DV_MLSTM), F32),
                   sds((nseq, N_HEADS_MLSTM, 1, DK_MLSTM), F32), sds((nseq, N_GATES, chunk), F32)],
        scratch_shapes=[pltpu.VMEM((2, MLSTM_W, tm), BF16),
                        pltpu.VMEM((N_HEADS_MLSTM, _CN_ROWS, DK_MLSTM), F32),
                        pltpu.VMEM((N_GATES, chunk), F32)],
        compiler_params=pltpu.CompilerParams(dimension_semantics=("arbitrary",), vmem_limit_bytes=VMEM_LIMIT),
        name="mlstm_mlp",
    )(x2d, mix_aT, qmT, km, vmT, omT, gT, g_b, wo_a, wo_m, g_mlp, w_up, w_dn, g_fin)


def kernel(x_prompt, x_sample, cache_k_win, cache_v_win, state_C, state_n, state_m, rel_bias, norm_mix, w_in,
           b_gates, attn_sinks, norm_attn_out, norm_mlstm_out, w_out, norm_mlp, w_up, w_down, norm_final):
    depth = w_in.shape[0]
    assert depth == 1, "single-layer trunk"
    bp, sp = x_prompt.shape[:2]
    bs, ts = x_sample.shape[:2]
    wc = cache_k_win.shape[2]
    l = 0
    assert sp % PROJ_TILE == 0 and sp % MLP_TILE == 0 and sp >= WINDOW and (bs * ts) % SAMPLE_TILE == 0

    wl = w_in[l]
    cols = lambda lo, n: wl[:, lo:lo + n]
    w_n = jnp.concatenate([cols(_O_KV, 2 * KV_W), cols(_O_QKM + QK_M_W, QK_M_W),
                           jnp.pad(cols(_O_G, N_GATES), ((0, 0), (0, LANES - N_GATES)))], axis=1).astype(BF16)
    w_t = jnp.concatenate([cols(_O_QA, ATT_W), cols(_O_KV + KV_W, KV_W), cols(_O_QKM, QK_M_W), cols(_O_VM, MLSTM_W),
                           cols(_O_OM, MLSTM_W), jnp.pad(cols(_O_G, N_GATES), ((0, 0), (0, BF16_ROWS - N_GATES)))],
                          axis=1).T.astype(BF16)
    bg_row = jnp.pad(b_gates[l], (0, LANES - N_GATES)).reshape(1, LANES)
    bg_col = b_gates[l].reshape(N_GATES, 1)
    g_mix = norm_mix[l].reshape(1, D_MODEL)
    g_att = norm_attn_out[l].reshape(1, ATT_W)
    g_m = norm_mlstm_out[l].reshape(1, MLSTM_W)
    g_mlp = norm_mlp[l].reshape(1, D_MODEL)
    g_fin = norm_final.reshape(1, D_MODEL)
    gb_att = jnp.broadcast_to(norm_attn_out[l].reshape(ATT_W, 1), (ATT_W, WINDOW))
    gb_m = jnp.broadcast_to(norm_mlstm_out[l].reshape(MLSTM_W, 1), (MLSTM_W, MLSTM_CHUNK))
    sinks = attn_sinks[l]

    xp = x_prompt.reshape(bp * sp, D_MODEL)
    far_bucket = int(_t5_bucket_np(np.array(WINDOW)))
    sink_far = jnp.stack([sinks, rel_bias[far_bucket]])
    casts = [(w_out[l], 0, ATT_W), (w_out[l], ATT_W, MLSTM_W), (w_up[l], 0, D_MODEL), (w_down[l], 0, D_FF)]
    kvw, km, qmT, vmT, omT, gT, mix_a, wo_a, wo_m, wup, wdn = _proj_attn(
        xp, g_mix, w_n, w_t, bg_col, rel_bias, _prompt_buckets_t(), sink_far, gb_att, casts, sp, tm=PROJ_TILE)
    y_p, c_p, n_p, m_p = _mlstm_mlp(xp, mix_a, qmT, km, vmT, omT, gT, gb_m, wo_a, wo_m, g_mlp, wup, wdn, g_fin,
                                    bp, sp, chunk=MLSTM_CHUNK, nsub=MLP_TILE // MLSTM_CHUNK)
    m_p = m_p[:, _GATE0:_GATE0 + N_HEADS_MLSTM, 0]
    k_win_p = kvw[:, :KV_W].reshape(1, bp, WINDOW, N_KV_HEADS, HEAD_DIM_ATT)
    v_win_p = kvw[:, KV_W:].reshape(1, bp, WINDOW, N_KV_HEADS, HEAD_DIM_ATT)

    assert math.gcd(ts, 64) == ts and bs % SEQ_PER_BLOCK == 0, "sample group: one mLSTM chunk per sequence"
    kpad = ((wc + ts + BF16_ROWS - 1) // BF16_ROWS) * BF16_ROWS
    sink_rows = jnp.repeat(sinks, ts).reshape(N_KV_HEADS, GQA_GROUP * ts, 1)
    xs = x_sample.reshape(bs * ts, D_MODEL)
    qa, kvf_s, qkm, vm, om, gt = _in_proj_s(xs, g_mix, w_n, w_t, bg_row, tm=SAMPLE_TILE)
    ck = cache_k_win[l].reshape(bs, wc, KV_W)
    cv = cache_v_win[l].reshape(bs, wc, KV_W)
    n0 = state_n[l].reshape(bs, QK_M_W)
    m0 = jnp.pad(state_m[l], ((0, 0), (_GATE0, LANES - _GATE0 - N_HEADS_MLSTM)))
    mix_a_s, kw_s, vw_s, mix_m_s, c_s, n_s, m_seq = _sample_mixers(
        qa, kvf_s, ck, cv, rel_bias, _sample_buckets(ts, wc, kpad), sink_rows, g_att, qkm, vm, om, gt, g_m,
        state_C[l], n0, m0, bs, ts)
    m_s = m_seq[:, _GATE0:_GATE0 + N_HEADS_MLSTM]
    y_s = _out_mlp(xs, mix_a_s, mix_m_s, wo_a, wo_m, g_mlp, wup, wdn, g_fin, tm=SAMPLE_TILE)

    return (y_p.reshape(bp, sp, D_MODEL), y_s.reshape(bs, ts, D_MODEL),
            k_win_p, v_win_p,
            c_p[None], n_p.reshape(1, bp, N_HEADS_MLSTM, DK_MLSTM), m_p.reshape(1, bp, N_HEADS_MLSTM),
            kw_s.reshape(1, bs, wc, N_KV_HEADS, HEAD_DIM_ATT), vw_s.reshape(1, bs, wc, N_KV_HEADS, HEAD_DIM_ATT),
            c_s[None], n_s.reshape(1, bs, N_HEADS_MLSTM, DK_MLSTM), m_s.reshape(1, bs, N_HEADS_MLSTM))
```

```python
import functools
import itertools
import math

import numpy as np
import jax
import jax.numpy as jnp
from jax import lax
from jax.experimental import pallas as pl
from jax.experimental.pallas import tpu as pltpu

D_MODEL = 1024
N_HEADS_ATT = 8
N_KV_HEADS = 2
GQA_GROUP = N_HEADS_ATT // N_KV_HEADS
HEAD_DIM_ATT = 64
ATT_W = N_HEADS_ATT * HEAD_DIM_ATT
KV_W = N_KV_HEADS * HEAD_DIM_ATT
WINDOW = 128
NUM_BUCKETS = 32
MAX_DISTANCE = 128
N_HEADS_MLSTM = 4
DV_MLSTM = 128
DK_MLSTM = 64
MLSTM_W = N_HEADS_MLSTM * DV_MLSTM
QK_M_W = N_HEADS_MLSTM * DK_MLSTM
D_FF = 4 * D_MODEL
EPS = 1e-6
N_GATES = 2 * N_HEADS_MLSTM
LANES = 128
BF16_ROWS = 16
VMEM_LIMIT = 56 * 1024 * 1024
PROJ_TILE = 1024
MLP_TILE = 512
MLSTM_CHUNK = 256
SAMPLE_TILE = 512

F32 = jnp.float32
BF16 = jnp.bfloat16

_O_QA = 0
_O_KV = _O_QA + ATT_W
_O_QKM = _O_KV + 2 * KV_W
_O_VM = _O_QKM + 2 * QK_M_W
_O_OM = _O_VM + MLSTM_W
_O_G = _O_OM + MLSTM_W
_T_QA = 0
_T_VA = _T_QA + ATT_W
_T_QM = _T_VA + KV_W
_T_VM = _T_QM + QK_M_W
_T_OM = _T_VM + MLSTM_W
_T_G = _T_OM + MLSTM_W
_T_ROWS = _T_G + BF16_ROWS
_N_KV = 0
_N_KM = _N_KV + 2 * KV_W
_N_G = _N_KM + QK_M_W
_GATE0 = N_HEADS_MLSTM


def _const_spec(shape):
    nd = len(shape)
    return pl.BlockSpec(shape, lambda *_: (0,) * nd, pipeline_mode=pl.Buffered(1))


def _log_sigmoid(g):
    return jnp.minimum(g, 0.0) - jnp.log1p(jnp.exp(-jnp.abs(g)))


def _rms_rows(x, gain):
    return x * lax.rsqrt(jnp.mean(x * x, axis=-1, keepdims=True) + EPS) * gain


def _in_proj_s_kernel(x_ref, g_ref, wn_ref, wt_ref, bgr_ref, qa_ref, kvf_ref, qkm_ref, vm_ref, om_ref, gt_ref):
    hb = _rms_rows(x_ref[...], g_ref[...]).astype(BF16)
    z_n = jnp.dot(hb, wn_ref[...], preferred_element_type=F32)
    z_t = lax.dot_general(hb, wt_ref[...], (((1,), (1,)), ((), ())), preferred_element_type=F32)
    qa_ref[...] = (z_t[:, _T_QA:_T_QA + ATT_W] * (HEAD_DIM_ATT ** -0.5)).astype(BF16)
    kvf_ref[:, :KV_W] = z_n[:, _N_KV:_N_KV + KV_W]
    kvf_ref[:, KV_W:] = z_t[:, _T_VA:_T_VA + KV_W]
    qkm_ref[:, :QK_M_W] = z_t[:, _T_QM:_T_QM + QK_M_W].astype(BF16)
    qkm_ref[:, QK_M_W:] = (z_n[:, _N_KM:_N_KM + QK_M_W] * (DK_MLSTM ** -0.5)).astype(BF16)
    vm_ref[...] = z_t[:, _T_VM:_T_VM + MLSTM_W].astype(BF16)
    om_ref[...] = jax.nn.sigmoid(z_t[:, _T_OM:_T_OM + MLSTM_W])
    g = z_n[:, _N_G:_N_G + LANES] + bgr_ref[...]
    col = lax.broadcasted_iota(jnp.int32, g.shape, 1)
    gt_ref[...] = jnp.where(col < N_HEADS_MLSTM, g, _log_sigmoid(g))


def _in_proj_s(x2d, g_norm, w_n, w_t, bg_row, tm):
    n = x2d.shape[0]
    row = lambda w: pl.BlockSpec((tm, w), lambda i: (i, 0))
    sds = jax.ShapeDtypeStruct
    return pl.pallas_call(
        _in_proj_s_kernel,
        grid=(n // tm,),
        in_specs=[row(D_MODEL), _const_spec((1, D_MODEL)), _const_spec(w_n.shape), _const_spec(w_t.shape),
                  _const_spec((1, LANES))],
        out_specs=[row(ATT_W), row(2 * KV_W), row(2 * QK_M_W), row(MLSTM_W), row(MLSTM_W), row(LANES)],
        out_shape=[sds((n, ATT_W), BF16), sds((n, 2 * KV_W), F32), sds((n, 2 * QK_M_W), BF16),
                   sds((n, MLSTM_W), BF16), sds((n, MLSTM_W), F32), sds((n, LANES), F32)],
        compiler_params=pltpu.CompilerParams(dimension_semantics=("parallel",), vmem_limit_bytes=VMEM_LIMIT),
        name="in_proj_s",
    )(x2d, g_norm, w_n, w_t, bg_row)


def _t5_bucket_np(dist):
    max_exact = NUM_BUCKETS // 2
    d = np.maximum(dist, 0)
    ratio = np.maximum(d, 1).astype(np.float32) / np.float32(max_exact)
    large = max_exact + (np.log(ratio) / np.float32(math.log(MAX_DISTANCE / max_exact))
                         * np.float32(NUM_BUCKETS - max_exact)).astype(np.int32)
    large = np.minimum(large, NUM_BUCKETS - 1)
    return np.where(d < max_exact, d, large).astype(np.int32)


def _bias_table_kernel(rb_ref, bucket_ref, out_ref):
    bucket = bucket_ref[...]
    hit = [bucket == j for j in range(NUM_BUCKETS)]
    for h in range(N_HEADS_ATT):
        acc = jnp.full(bucket.shape, -jnp.inf, F32)
        for j in range(NUM_BUCKETS):
            acc = jnp.where(hit[j], rb_ref[j, h], acc)
        out_ref[h] = acc


def _bias_table(rel_bias, bucket):
    r, c = bucket.shape
    return pl.pallas_call(
        _bias_table_kernel,
        in_specs=[pl.BlockSpec(memory_space=pltpu.SMEM), pl.BlockSpec(memory_space=pltpu.VMEM)],
        out_specs=pl.BlockSpec(memory_space=pltpu.VMEM),
        out_shape=jax.ShapeDtypeStruct((N_HEADS_ATT, r, c), F32),
        name="bias_table",
    )(rel_bias, jnp.asarray(bucket))


def _prompt_buckets_t():
    qi = np.arange(WINDOW)[None, :]
    ci = np.arange(WINDOW)[:, None]
    upper = ci > qi
    general = _t5_bucket_np(np.where(upper, qi + WINDOW - ci, qi - ci))
    first = np.where(upper, -1, general)
    return np.concatenate([general, first], axis=0).astype(np.int32)


def _sample_buckets(t, w, padded):
    dist = (w + np.arange(t))[:, None] - np.arange(padded)[None, :]
    valid = (dist >= 0) & (dist <= WINDOW) & (np.arange(padded)[None, :] < w + t)
    return np.where(valid, _t5_bucket_np(dist), -1).astype(np.int32)


def _attn_block(i, qT, k_all, vT_all, first, bias_ref, sink_ref, gb_ref, write_out):
    W = WINDOW
    gd = lambda g: slice(g * HEAD_DIM_ATT, (g + 1) * HEAD_DIM_ATT)
    cols = slice(i * W, (i + 1) * W)
    is_first = first if i == 0 else False
    off = pl.multiple_of(jnp.where(first, W, 0), W) if i == 0 else 0
    k2 = k_all[i * W:(i + 2) * W, :]
    vT2 = vT_all[:, i * W:(i + 2) * W]
    ci = lax.broadcasted_iota(jnp.int32, (W, W), 0)
    qi = lax.broadcasted_iota(jnp.int32, (W, W), 1)
    upper = ci > qi
    eye = ci == qi
    scores = []
    for g in range(N_KV_HEADS):
        q_grp = jnp.concatenate([qT(slice((g * GQA_GROUP + j) * HEAD_DIM_ATT, (g * GQA_GROUP + j + 1) * HEAD_DIM_ATT),
                                    cols) for j in range(GQA_GROUP)], axis=1)
        scores.append(jnp.dot(k2[:, gd(g)], q_grp, preferred_element_type=F32))
    yield
    probs, rden, pfar = [], [], []
    for h in range(N_HEADS_ATT):
        g, j = divmod(h, GQA_GROUP)
        s_prev = scores[g][:W, j * W:(j + 1) * W]
        s_cur = scores[g][W:, j * W:(j + 1) * W]
        s = jnp.where(upper, s_prev, s_cur) + bias_ref[h, pl.ds(off, W), :]
        far_bias = jnp.where(is_first, -jnp.inf, sink_ref[1, h])
        s_far = jnp.sum(jnp.where(eye, s_prev, 0.0), axis=0, keepdims=True) + far_bias
        sink = sink_ref[0, h]
        m = jnp.maximum(jnp.maximum(jnp.max(s, axis=0, keepdims=True), s_far), sink)
        p = jnp.exp(s - m)
        p_far = jnp.exp(s_far - m)
        rden.append(1.0 / (jnp.sum(p, axis=0, keepdims=True) + p_far + jnp.exp(sink - m)))
        pfar.append(p_far)
        zero = jnp.zeros_like(p)
        probs.append(jnp.concatenate([jnp.where(upper, p, zero), jnp.where(upper, zero, p)], axis=0).astype(BF16))
    yield
    outs, ssq = [], jnp.zeros((1, W), F32)
    for g in range(N_KV_HEADS):
        p_grp = jnp.concatenate(probs[g * GQA_GROUP:(g + 1) * GQA_GROUP], axis=1)
        o_grp = jnp.dot(vT2[gd(g), :], p_grp, preferred_element_type=F32)
        v_far = vT2[gd(g), :W].astype(F32)
        for j in range(GQA_GROUP):
            h = g * GQA_GROUP + j
            o = (o_grp[:, j * W:(j + 1) * W] + v_far * pfar[h]) * rden[h]
            outs.append(o)
            ssq = ssq + jnp.sum(o * o, axis=0, keepdims=True)
    inv = lax.rsqrt(ssq * (1.0 / ATT_W) + EPS)
    for h in range(N_HEADS_ATT):
        rows = slice(h * HEAD_DIM_ATT, (h + 1) * HEAD_DIM_ATT)
        write_out(rows, cols, (outs[h] * inv * gb_ref[rows, :]).astype(BF16))
    yield


def _attn_blocks(qT, k_all, vT_all, first, bias_ref, sink_ref, gb_ref, write_out, qb):
    gens = [_attn_block(i, qT, k_all, vT_all, first, bias_ref, sink_ref, gb_ref, write_out) for i in range(qb)]
    next(gens[0])
    next(gens[0])
    for i in range(qb):
        if i + 1 < qb:
            next(gens[i + 1])
        yield
        next(gens[i])
        if i + 1 < qb:
            next(gens[i + 1])


PROJ_TOKEN_CHUNK = 256
PROJ_ROW_SPLIT = 1024


def _proj_attn_kernel(x_ref, g_ref, wn_ref, wt_ref, bgc_ref, rb_ref, bucket_ref, sink_ref, gb_ref, *refs,
                      tm, tiles_per_seq, n_cast):
    cast_in, refs = refs[:n_cast], refs[n_cast:]
    kvw_ref, km_ref, qmT_ref, vmT_ref, omT_ref, gT_ref, mix_ref = refs[:7]
    cast_out, (q_s, k_s, vT_s, hk_s, hv_s, bias_ref) = refs[7:7 + n_cast], refs[7 + n_cast:]
    for src, dst in zip(cast_in, cast_out):
        dst[...] = src[...].astype(BF16)
    s = pl.program_id(0)
    slot = s % 2
    prev = 1 - slot
    qb = tm // WINDOW

    @pl.when(s == 0)
    def _():
        q_s[...] = jnp.zeros_like(q_s)
        k_s[...] = jnp.zeros_like(k_s)
        vT_s[...] = jnp.zeros_like(vT_s)
        hk_s[...] = jnp.zeros_like(hk_s)
        hv_s[...] = jnp.zeros_like(hv_s)
        _bias_table_kernel(rb_ref, bucket_ref, bias_ref)

    first =(s + tiles_per_seq - 1) % tiles_per_seq == 0
    k_all = jnp.concatenate([hk_s[...], k_s[prev]], axis=0)
    vT_all = jnp.concatenate([hv_s[...], vT_s[prev]], axis=1)

    def write_out(rows, cols, val):
        mix_ref[rows, cols] = val

    blocks = _attn_blocks(lambda rows, cols: q_s[prev, rows, cols], k_all, vT_all, first, bias_ref, sink_ref, gb_ref,
                          write_out, qb)
    next(blocks)
    hb = _rms_rows(x_ref[...], g_ref[...]).astype(BF16)

    def put_q(lo, tc, z):
        q_s[slot, lo:lo + z.shape[0], tc] = (z * (HEAD_DIM_ATT ** -0.5)).astype(BF16)

    def put_v(lo, tc, z):
        vT_s[slot, lo:lo + z.shape[0], tc] = z.astype(BF16)

    def put_qm(lo, tc, z):
        qmT_ref[lo:lo + z.shape[0], tc] = z.astype(BF16)

    def put_vm(lo, tc, z):
        vmT_ref[lo:lo + z.shape[0], tc] = z.astype(BF16)

    def put_om(lo, tc, z):
        omT_ref[lo:lo + z.shape[0], tc] = jax.nn.sigmoid(z)

    def put_g(lo, tc, z):
        g_t = z[:N_GATES, :] + bgc_ref[...]
        row = lax.broadcasted_iota(jnp.int32, g_t.shape, 0)
        gT_ref[:, tc] = jnp.where(row < N_HEADS_MLSTM, g_t, _log_sigmoid(g_t))

    segments = [(_T_QA, _T_VA, put_q), (_T_VA, _T_QM, put_v), (_T_QM, _T_VM, put_qm), (_T_VM, _T_OM, put_vm),
                (_T_OM, _T_G, put_om), (_T_G, _T_ROWS, put_g)]

    def proj_piece(lo, hi, tc):
        z = lax.dot_general(wt_ref[lo:hi, :], hb[tc, :], (((1,), (1,)), ((), ())), preferred_element_type=F32)
        for a, b, put in segments:
            if a < hi and lo < b:
                put(max(lo, a) - a, tc, z[max(lo, a) - lo:min(hi, b) - lo, :])

    z_n = jnp.dot(hb, wn_ref[:, :_N_G], preferred_element_type=F32)
    kvw_ref[...] = z_n[tm - WINDOW:, _N_KV:_N_KV + 2 * KV_W]
    km_ref[...] = (z_n[:, _N_KM:_N_KM + QK_M_W] * (DK_MLSTM ** -0.5)).astype(BF16)
    k_s[slot] = z_n[:, _N_KV:_N_KV + KV_W].astype(BF16)
    pieces = [(lo, hi, slice(c, c + PROJ_TOKEN_CHUNK))
              for c in range(0, tm, PROJ_TOKEN_CHUNK) for lo, hi in ((0, PROJ_ROW_SPLIT), (PROJ_ROW_SPLIT, _T_ROWS))]
    proj_piece(*pieces.pop(0))
    for _ in blocks:
        if pieces:
            proj_piece(*pieces.pop(0))
    for piece in pieces:
        proj_piece(*piece)
    hk_s[...] = k_s[prev, tm - WINDOW:, :]
    hv_s[...] = vT_s[prev, :, tm - WINDOW:]


def _proj_attn(x2d, g_norm, w_n, w_t, bg_col, rel_bias, bucket, sinks, g_b, casts, seq, tm):
    n = x2d.shape[0]
    n_tiles = n // tm
    tiles_per_seq = seq // tm
    cur = lambda s: jnp.minimum(s, n_tiles - 1)
    prv = lambda s: jnp.maximum(s - 1, 0)
    row = lambda w: pl.BlockSpec((tm, w), lambda s: (cur(s), 0))
    colb = lambda r: pl.BlockSpec((r, tm), lambda s: (0, cur(s)))
    sds = jax.ShapeDtypeStruct
    cast_in, cast_out, cast_shape = [], [], []
    for arr, first, count in casts:
        chunk = count // n_tiles
        assert chunk * n_tiles == count and chunk % BF16_ROWS == 0 and first % chunk == 0
        cols = arr.shape[1]
        cast_in.append(pl.BlockSpec((chunk, cols), lambda s, o=first // chunk: (o + cur(s), 0)))
        cast_out.append(pl.BlockSpec((chunk, cols), lambda s: (cur(s), 0)))
        cast_shape.append(sds((count, cols), BF16))
    return pl.pallas_call(
        functools.partial(_proj_attn_kernel, tm=tm, tiles_per_seq=tiles_per_seq, n_cast=len(casts)),
        grid=(n_tiles + 1,),
        in_specs=[row(D_MODEL), _const_spec((1, D_MODEL)), _const_spec(w_n.shape), _const_spec(w_t.shape),
                  _const_spec((N_GATES, 1)), pl.BlockSpec(memory_space=pltpu.SMEM), _const_spec(bucket.shape),
                  pl.BlockSpec(memory_space=pltpu.SMEM), _const_spec(g_b.shape)] + cast_in,
        out_specs=[pl.BlockSpec((WINDOW, 2 * KV_W), lambda s: (cur(s) // tiles_per_seq, 0)),
                   row(QK_M_W), colb(QK_M_W), colb(MLSTM_W), colb(MLSTM_W), colb(N_GATES),
                   pl.BlockSpec((ATT_W, tm), lambda s: (0, prv(s)))] + cast_out,
        out_shape=[sds((n // seq * WINDOW, 2 * KV_W), F32), sds((n, QK_M_W), BF16), sds((QK_M_W, n), BF16),
                   sds((MLSTM_W, n), BF16),
                   sds((MLSTM_W, n), F32), sds((N_GATES, n), F32), sds((ATT_W, n), BF16)] + cast_shape,
        scratch_shapes=[pltpu.VMEM((2, ATT_W, tm), BF16), pltpu.VMEM((2, tm, KV_W), BF16),
                        pltpu.VMEM((2, KV_W, tm), BF16), pltpu.VMEM((WINDOW, KV_W), BF16),
                        pltpu.VMEM((KV_W, WINDOW), BF16), pltpu.VMEM((N_HEADS_ATT,) + bucket.shape, F32)],
        compiler_params=pltpu.CompilerParams(dimension_semantics=("arbitrary",), vmem_limit_bytes=VMEM_LIMIT),
        name="proj_attn",
    )(x2d, g_norm, w_n, w_t, bg_col, rel_bias, jnp.asarray(bucket), sinks, g_b, *[arr for arr, _, _ in casts])


def _attn_sample_stages(q_ref, kvf_ref, ck_ref, cv_ref, bias_ref, sink_ref, g_ref, out_ref, kw_ref, vw_ref,
                        *, bb, t, w, kpad):
    ck = ck_ref[...]
    cv = cv_ref[...]
    k_new = kvf_ref[:, :KV_W].reshape(bb, t, KV_W)
    v_new = kvf_ref[:, KV_W:].reshape(bb, t, KV_W)
    kw_ref[:, :w - t, :] = ck[:, t:, :]
    kw_ref[:, w - t:, :] = k_new
    vw_ref[:, :w - t, :] = cv[:, t:, :]
    vw_ref[:, w - t:, :] = v_new
    zpad = jnp.zeros((bb, kpad - w - t, KV_W), F32)
    kk = jnp.concatenate([ck, k_new, zpad], axis=1).astype(BF16)
    vv = jnp.concatenate([cv, v_new, zpad], axis=1).astype(BF16)
    q3 = q_ref[...].astype(F32).reshape(bb, t, ATT_W)
    yield
    gd = lambda g: slice(g * HEAD_DIM_ATT, (g + 1) * HEAD_DIM_ATT)
    scores = []
    for g in range(N_KV_HEADS):
        qg = jnp.concatenate([q3[:, :, (g * GQA_GROUP + i) * HEAD_DIM_ATT:(g * GQA_GROUP + i + 1) * HEAD_DIM_ATT]
                              for i in range(GQA_GROUP)], axis=1).astype(BF16)
        zq = jnp.zeros_like(qg)
        qg = jnp.concatenate([qg, zq] if g == 0 else [zq, qg], axis=-1)
        scores.append(jnp.einsum('bqd,bkd->bqk', qg, kk, preferred_element_type=F32))
    yield
    probs, rden = [], []
    for g in range(N_KV_HEADS):
        s = scores[g] + bias_ref[g]
        sink = sink_ref[g]
        m = jnp.maximum(jnp.max(s, axis=-1, keepdims=True), sink)
        p = jnp.exp(s - m)
        rden.append(1.0 / (jnp.sum(p, axis=-1, keepdims=True) + jnp.exp(sink - m)))
        probs.append(p.astype(BF16))
        yield
    outs = []
    ssq = jnp.zeros((bb, 1, t, HEAD_DIM_ATT), F32)
    for g in range(N_KV_HEADS):
        o = jnp.einsum('bqk,bkd->bqd', probs[g], vv, preferred_element_type=F32)[:, :, gd(g)] * rden[g]
        o = o.reshape(bb, GQA_GROUP, t, HEAD_DIM_ATT)
        outs.append(o)
        ssq = ssq + jnp.sum(o * o, axis=1, keepdims=True)
    yield
    inv = lax.rsqrt(jnp.sum(ssq, axis=-1, keepdims=True) * (1.0 / ATT_W) + EPS)
    for g in range(N_KV_HEADS):
        y = outs[g] * inv
        for i in range(GQA_GROUP):
            sl = slice((g * GQA_GROUP + i) * HEAD_DIM_ATT, (g * GQA_GROUP + i + 1) * HEAD_DIM_ATT)
            out_ref[:, sl] = (y[:, i].reshape(bb * t, HEAD_DIM_ATT) * g_ref[:, sl]).astype(BF16)


SEQ_PER_BLOCK = 16


def _seg_scan_rows(x, seg, forward):
    n = x.shape[0]
    pos = lax.broadcasted_iota(jnp.int32, x.shape, 0) % seg
    k = 1
    while k < seg:
        if forward:
            x = jnp.maximum(x, jnp.where(pos >= k, pltpu.roll(x, k, axis=0), -jnp.inf))
        else:
            x = jnp.maximum(x, jnp.where(pos < seg - k, pltpu.roll(x, n - k, axis=0), -jnp.inf))
        k *= 2
    return x


def _mlstm_blk_stages(qk_ref, v_ref, om_ref, gt_ref, gm_ref, c0_ref, n0_ref, m0_ref,
                      mix_ref, c_out, n_out, m_out, *, t):
    S = SEQ_PER_BLOCK
    L = S * t
    hi = lax.Precision.HIGHEST
    ti = lax.broadcasted_iota(jnp.int32, (L, L), 0)
    si = lax.broadcasted_iota(jnp.int32, (L, L), 1)
    same = (ti // t) == (si // t)
    causal = same & (si <= ti)
    tril_seg = causal.astype(F32)
    ones_seg = same.astype(F32)
    ones_col = (lax.broadcasted_iota(jnp.int32, (L, DV_MLSTM), 1) == 0).astype(BF16)
    bd_mask = (lax.broadcasted_iota(jnp.int32, (L, S * DK_MLSTM), 0) // t
               == lax.broadcasted_iota(jnp.int32, (L, S * DK_MLSTM), 1) // DK_MLSTM)
    bd_mask_t = (lax.broadcasted_iota(jnp.int32, (S * DK_MLSTM, L), 0) // DK_MLSTM
                 == lax.broadcasted_iota(jnp.int32, (S * DK_MLSTM, L), 1) // t)
    heads = range(N_HEADS_MLSTM)
    hk = lambda h: slice(h * DK_MLSTM, (h + 1) * DK_MLSTM)
    hkk = lambda h: slice(QK_M_W + h * DK_MLSTM, QK_M_W + (h + 1) * DK_MLSTM)
    hv = lambda h: slice(h * DV_MLSTM, (h + 1) * DV_MLSTM)

    gt = gt_ref[...]
    per_token = lambda x: jnp.broadcast_to(x[:, None, :], (S, t, x.shape[-1])).reshape(L, x.shape[-1])
    m_prev = per_token(m0_ref[...])
    n_rows = per_token(n0_ref[...])
    b = jnp.dot(tril_seg, gt, precision=hi, preferred_element_type=F32)
    b_end = jnp.dot(ones_seg, gt, precision=hi, preferred_element_type=F32)
    yield
    a = pltpu.roll(gt, N_HEADS_MLSTM, axis=1) - b
    big_m = jnp.maximum(m_prev, _seg_scan_rows(a, t, True))
    inter = jnp.exp(m_prev - big_m)
    em = jnp.exp(-(b + big_m))
    m_end = _seg_scan_rows(big_m, t, False)
    w = jnp.exp(a - m_end)
    decay = jnp.exp(m_prev - m_end)
    m_out[...] = (b_end + m_end).reshape(S, t, LANES)[:, 0, :]
    a_t = a.T
    yield
    q = [qk_ref[:, hk(h)] for h in heads]
    k = [qk_ref[:, hkk(h)] for h in heads]
    v_ext = [jnp.concatenate([v_ref[:, hv(h)], ones_col], axis=1) for h in heads]
    qk = [lax.dot_general(q[h], k[h], (((1,), (1,)), ((), ())), preferred_element_type=F32) for h in heads]
    c_all = [c0_ref[:, h].reshape(S * DK_MLSTM, DV_MLSTM) for h in heads]
    q_bd = []
    for h in heads:
        q2 = jnp.concatenate([q[h], q[h]], axis=1)
        q_bd.append(jnp.where(bd_mask, jnp.concatenate([q2] * (S // 2), axis=1), jnp.zeros((), BF16)))
    qc = [jnp.dot(q_bd[h], c_all[h].astype(BF16), preferred_element_type=F32) for h in heads]
    yield
    s = []
    for h in heads:
        ln = _GATE0 + h
        d = jnp.exp(jnp.where(causal, a_t[ln:ln + 1, :] - big_m[:, ln:ln + 1], -jnp.inf))
        s.append((qk[h] * d).astype(BF16))
    yield
    sv = [jnp.dot(s[h], v_ext[h], preferred_element_type=F32) for h in heads]
    kw, kw_bd = [], []
    for h in heads:
        ln = _GATE0 + h
        kw_f = k[h].astype(F32) * w[:, ln:ln + 1]
        kw.append(kw_f.astype(BF16))
        kw_bd.append(jnp.where(bd_mask_t, jnp.concatenate([kw_f.T] * S, axis=0), 0.0).astype(BF16))
    upd = [jnp.dot(kw_bd[h], v_ref[:, hv(h)], preferred_element_type=F32) for h in heads]
    seq_of_tok = lax.broadcasted_iota(jnp.int32, (S, L), 1) // t == lax.broadcasted_iota(jnp.int32, (S, L), 0)
    first_tok = lax.broadcasted_iota(jnp.int32, (S, L), 1) == lax.broadcasted_iota(jnp.int32, (S, L), 0) * t
    n_upd = [jnp.dot(seq_of_tok.astype(BF16), kw[h], preferred_element_type=F32) for h in heads]
    decay_seq = jnp.dot(first_tok.astype(F32), decay, precision=hi, preferred_element_type=F32)
    yield
    for h in heads:
        ln = _GATE0 + h
        inter_h = inter[:, ln:ln + 1]
        qn = jnp.sum(q[h].astype(F32) * n_rows[:, hk(h)], axis=1, keepdims=True)
        den = inter_h * qn + sv[h][:, DV_MLSTM:DV_MLSTM + 1]
        r = 1.0 / jnp.maximum(jnp.abs(den), em[:, ln:ln + 1])
        tt = om_ref[:, hv(h)] * (inter_h * qc[h] + sv[h][:, :DV_MLSTM])
        msq = jnp.mean(tt * tt, axis=1, keepdims=True)
        scale = r * lax.rsqrt(r * r * msq + EPS)
        mix_ref[:, hv(h)] = (tt * scale * gm_ref[:, hv(h)]).astype(BF16)
        dec = decay[:, ln:ln + 1].reshape(S, t, 1)[:, 0:1, :]
        c_out[:, h] = dec * c0_ref[:, h] + upd[h].reshape(S, DK_MLSTM, DV_MLSTM)
        n_out[:, hk(h)] = decay_seq[:, ln:ln + 1] * n0_ref[:, hk(h)] + n_upd[h]


def _sample_mixers_kernel(q_ref, kvf_ref, ck_ref, cv_ref, bias_ref, sink_ref, ga_ref,
                          qk_ref, v_ref, om_ref, gt_ref, gm_ref, c0_ref, n0_ref, m0_ref,
                          mixa_ref, kw_ref, vw_ref, mixm_ref, c_out, n_out, m_out, *, t, w, kpad):
    attn = _attn_sample_stages(q_ref, kvf_ref, ck_ref, cv_ref, bias_ref, sink_ref, ga_ref, mixa_ref, kw_ref, vw_ref,
                               bb=SEQ_PER_BLOCK, t=t, w=w, kpad=kpad)
    mlstm = _mlstm_blk_stages(qk_ref, v_ref, om_ref, gt_ref, gm_ref, c0_ref, n0_ref, m0_ref,
                              mixm_ref, c_out, n_out, m_out, t=t)
    for _ in itertools.zip_longest(attn, mlstm):
        pass


def _sample_mixers(qa, kvf, ck, cv, bias, sink_rows, g_att, qkm, vm, om, gt, g_m, c0, n0_seq, m0_seq, nseq, t):
    S = SEQ_PER_BLOCK
    L = S * t
    w = ck.shape[1]
    kpad = bias.shape[-1]
    rows = GQA_GROUP * t
    tok = lambda wd: pl.BlockSpec((L, wd), lambda i: (i, 0))
    cache = pl.BlockSpec((S, w, KV_W), lambda i: (i, 0, 0))
    st_c = pl.BlockSpec((S, N_HEADS_MLSTM, DK_MLSTM, DV_MLSTM), lambda i: (i, 0, 0, 0))
    st_n = pl.BlockSpec((S, QK_M_W), lambda i: (i, 0))
    st_m = pl.BlockSpec((S, LANES), lambda i: (i, 0))
    sds = jax.ShapeDtypeStruct
    return pl.pallas_call(
        functools.partial(_sample_mixers_kernel, t=t, w=w, kpad=kpad),
        grid=(nseq // S,),
        in_specs=[tok(ATT_W), tok(2 * KV_W), cache, cache,
                  _const_spec((N_KV_HEADS, rows, kpad)), _const_spec((N_KV_HEADS, rows, 1)), _const_spec((1, ATT_W)),
                  tok(2 * QK_M_W), tok(MLSTM_W), tok(MLSTM_W), tok(LANES), _const_spec((1, MLSTM_W)),
                  st_c, st_n, st_m],
        out_specs=[tok(ATT_W), cache, cache, tok(MLSTM_W), st_c, st_n, st_m],
        out_shape=[sds((nseq * t, ATT_W), BF16), sds((nseq, w, KV_W), F32), sds((nseq, w, KV_W), F32),
                   sds((nseq * t, MLSTM_W), BF16), sds((nseq, N_HEADS_MLSTM, DK_MLSTM, DV_MLSTM), F32),
                   sds((nseq, QK_M_W), F32), sds((nseq, LANES), F32)],
        compiler_params=pltpu.CompilerParams(dimension_semantics=("arbitrary",), vmem_limit_bytes=VMEM_LIMIT),
        name="sample_mixers",
    )(qa, kvf, ck, cv, bias, sink_rows, g_att, qkm, vm, om, gt, g_m, c0, n0_seq, m0_seq)


def _out_mlp_kernel(x_ref, a_ref, hm_ref, woa_ref, wom_ref, gmlp_ref, wup_ref, wdn_ref, gfin_ref, y_ref):
    x1 = (x_ref[...]
          + jnp.dot(a_ref[...], woa_ref[...], preferred_element_type=F32)
          + jnp.dot(hm_ref[...], wom_ref[...], preferred_element_type=F32))
    h = _rms_rows(x1, gmlp_ref[...]).astype(BF16)
    u = jnp.maximum(jnp.dot(h, wup_ref[...], preferred_element_type=F32), 0.0)
    y = x1 + jnp.dot((u * u).astype(BF16), wdn_ref[...], preferred_element_type=F32)
    y_ref[...] = _rms_rows(y, gfin_ref[...])


def _out_mlp(x2d, mix_a, mix_m, wo_a, wo_m, g_mlp, w_up, w_dn, g_fin, tm):
    n = x2d.shape[0]
    row = lambda w: pl.BlockSpec((tm, w), lambda i: (i, 0))
    return pl.pallas_call(
        _out_mlp_kernel,
        grid=(n // tm,),
        in_specs=[row(D_MODEL), row(ATT_W), row(MLSTM_W), _const_spec(wo_a.shape), _const_spec(wo_m.shape),
                  _const_spec((1, D_MODEL)), _const_spec(w_up.shape), _const_spec(w_dn.shape),
                  _const_spec((1, D_MODEL))],
        out_specs=row(D_MODEL),
        out_shape=jax.ShapeDtypeStruct((n, D_MODEL), F32),
        compiler_params=pltpu.CompilerParams(dimension_semantics=("parallel",), vmem_limit_bytes=VMEM_LIMIT),
        name="out_mlp",
    )(x2d, mix_a, mix_m, wo_a, wo_m, g_mlp, w_up, w_dn, g_fin)


_CN_ROWS = DV_MLSTM + BF16_ROWS


def _prefix_max_lanes(x):
    n = x.shape[1]
    col = lax.broadcasted_iota(jnp.int32, x.shape, 1)
    k = 1
    while k < n:
        x = jnp.maximum(x, jnp.where(col >= k, pltpu.roll(x, k, axis=1), -jnp.inf))
        k *= 2
    return x


def _mlstm_stages(qT_ref, k_ref, vT_ref, omT_ref, gT_ref, gb_ref, write_mix, state, chunk, nsub):
    L = chunk
    i0 = lax.broadcasted_iota(jnp.int32, (L, L), 0)
    i1 = lax.broadcasted_iota(jnp.int32, (L, L), 1)
    causal_t = i0 <= i1
    triu = causal_t.astype(F32)
    ones_rows = (lax.broadcasted_iota(jnp.int32, (BF16_ROWS, L), 0) == 0).astype(BF16)
    zpad = jnp.zeros((LANES - 2 * N_GATES, L), F32)
    cn, m_r = state['cn'], state['m_r']
    hi = lax.Precision.HIGHEST
    heads = range(N_HEADS_MLSTM)
    subs = range(nsub)
    col_of = lambda j: slice(j * L, (j + 1) * L)
    hv = lambda h: slice(h * DV_MLSTM, (h + 1) * DV_MLSTM)
    hk = lambda h: slice(h * DK_MLSTM, (h + 1) * DK_MLSTM)
    a_r, b_r, pm_r = [], [], []
    for j in subs:
        g_t = gT_ref[:, col_of(j)]
        b_r.append(jnp.dot(g_t, triu, precision=hi, preferred_element_type=F32))
        a_r.append(pltpu.roll(g_t, N_HEADS_MLSTM, axis=0) - b_r[j])
    yield
    kq = [[jnp.dot(k_ref[col_of(j), hk(h)], qT_ref[hk(h), col_of(j)], preferred_element_type=F32)
           for h in heads] for j in subs]
    vT_ext = [[jnp.concatenate([vT_ref[hv(h), col_of(j)], ones_rows], axis=0) for h in heads] for j in subs]
    for j in subs:
        pm_r.append(_prefix_max_lanes(a_r[j]))
    yield
    big_m, inter, em, decay, cols_t = [], [], [], [], []
    for j in subs:
        big_m.append(jnp.maximum(m_r, pm_r[j]))
        inter.append(jnp.exp(m_r - big_m[j]))
        em.append(jnp.exp(-(b_r[j] + big_m[j])))
        m_end = big_m[j][:, L - 1:L]
        w_r = jnp.exp(a_r[j] - m_end)
        decay.append(jnp.exp(m_r[:, 0:1] - m_end))
        m_r = jnp.broadcast_to(b_r[j][:, L - 1:L] + m_end, (N_GATES, L))
        cols_t.append(jnp.concatenate([a_r[j], w_r, zpad], axis=0).T)
    sT = [[None] * N_HEADS_MLSTM for _ in subs]
    kw = [[None] * N_HEADS_MLSTM for _ in subs]
    intra = [[None] * N_HEADS_MLSTM for _ in subs]
    upd = [[None] * N_HEADS_MLSTM for _ in subs]
    for j in subs:
        for h in heads:
            r = _GATE0 + h
            d = jnp.exp(jnp.where(causal_t, cols_t[j][:, r:r + 1] - big_m[j][r:r + 1, :], -jnp.inf))
            sT[j][h] = (kq[j][h] * d).astype(BF16)
            w_col = cols_t[j][:, N_GATES + r:N_GATES + r + 1]
            kw[j][h] = (k_ref[col_of(j), hk(h)].astype(F32) * w_col).astype(BF16)
        for h in heads:
            intra[j][h] = jnp.dot(vT_ext[j][h], sT[j][h], preferred_element_type=F32)
            upd[j][h] = jnp.dot(vT_ext[j][h], kw[j][h], preferred_element_type=F32)
        yield
    for j in subs:
        qc = [jnp.dot(cn[h].astype(BF16), qT_ref[hk(h), col_of(j)], preferred_element_type=F32) for h in heads]
        for h in heads:
            r = _GATE0 + h
            nd = inter[j][r:r + 1, :] * qc[h] + intra[j][h]
            rr = 1.0 / jnp.maximum(jnp.abs(nd[DV_MLSTM:DV_MLSTM + 1, :]), em[j][r:r + 1, :])
            t = omT_ref[hv(h), col_of(j)] * nd[:DV_MLSTM, :]
            msq = jnp.mean(t * t, axis=0, keepdims=True)
            scale = rr * lax.rsqrt(rr * rr * msq + EPS)
            write_mix(hv(h), col_of(j), (t * scale * gb_ref[hv(h), :]).astype(BF16))
            cn[h] = decay[j][r:r + 1, :] * cn[h] + upd[j][h]
        yield
    state['m_r'] = m_r


def _mlstm_mlp_kernel(x_ref, a_ref, qT_ref, k_ref, vT_ref, omT_ref, gT_ref, gb_ref,
                      woa_ref, wom_ref, gmlp_ref, wup_ref, wdn_ref, gfin_ref,
                      y_ref, c_out, n_out, m_out, mix_s, cn_s, mr_s,
                      *, chunk, nsub, tiles_per_seq, n_tiles):
    s = pl.program_id(0)
    slot = s % 2

    @pl.when(s == 0)
    def _():
        mix_s[...] = jnp.zeros_like(mix_s)

    @pl.when(s % tiles_per_seq == 0)
    def _():
        cn_s[...] = jnp.zeros_like(cn_s)
        mr_s[...] = jnp.zeros_like(mr_s)

    state = {'cn': [cn_s[h] for h in range(N_HEADS_MLSTM)], 'm_r': mr_s[...]}

    def write_mix(rows, cols, val):
        mix_s[slot, rows, cols] = val

    stages = _mlstm_stages(qT_ref, k_ref, vT_ref, omT_ref, gT_ref, gb_ref, write_mix, state, chunk, nsub)
    next(stages)
    tdims = (((0,), (0,)), ((), ()))
    x1 = (x_ref[...]
          + lax.dot_general(a_ref[...], woa_ref[...], tdims, preferred_element_type=F32)
          + lax.dot_general(mix_s[1 - slot], wom_ref[...], tdims, preferred_element_type=F32))
    next(stages)
    h = _rms_rows(x1, gmlp_ref[...]).astype(BF16)
    u = jnp.maximum(jnp.dot(h, wup_ref[...], preferred_element_type=F32), 0.0)
    for _ in stages:
        pass
    y = x1 + jnp.dot((u * u).astype(BF16), wdn_ref[...], preferred_element_type=F32)
    y_ref[...] = _rms_rows(y, gfin_ref[...])
    for hh in range(N_HEADS_MLSTM):
        cn_s[hh] = state['cn'][hh]
    mr_s[...] = state['m_r']

    @pl.when((s % tiles_per_seq == tiles_per_seq - 1) & (s < n_tiles))
    def _():
        for hh in range(N_HEADS_MLSTM):
            c_out[0, hh] = cn_s[hh, :DV_MLSTM, :].T
            n_out[0, hh] = cn_s[hh, DV_MLSTM:DV_MLSTM + 1, :]
        m_out[0] = mr_s[...]


def _mlstm_mlp(x2d, mix_aT, qmT, km, vmT, omT, gT, g_b, wo_a, wo_m, g_mlp, w_up, w_dn, g_fin,
               nseq, t, chunk, nsub):
    tm = chunk * nsub
    n_tiles = (nseq * t) // tm
    tiles_per_seq = t // tm
    cur = lambda s: jnp.minimum(s, n_tiles - 1)
    prv = lambda s: jnp.maximum(s - 1, 0)
    colb = lambda r: pl.BlockSpec((r, tm), lambda s: (0, cur(s)))
    st = lambda a, d: pl.BlockSpec((1, N_HEADS_MLSTM, a, d), lambda s: (cur(s) // tiles_per_seq, 0, 0, 0))
    sds = jax.ShapeDtypeStruct
    kern = functools.partial(_mlstm_mlp_kernel, chunk=chunk, nsub=nsub,
                             tiles_per_seq=tiles_per_seq, n_tiles=n_tiles)
    return pl.pallas_call(
        kern,
        grid=(n_tiles + 1,),
        in_specs=[pl.BlockSpec((tm, D_MODEL), lambda s: (prv(s), 0)),
                  pl.BlockSpec((ATT_W, tm), lambda s: (0, prv(s))),
                  colb(QK_M_W), pl.BlockSpec((tm, QK_M_W), lambda s: (cur(s), 0)), colb(MLSTM_W), colb(MLSTM_W),
                  colb(N_GATES),
                  _const_spec(g_b.shape), _const_spec(wo_a.shape), _const_spec(wo_m.shape),
                  _const_spec((1, D_MODEL)), _const_spec(w_up.shape), _const_spec(w_dn.shape),
                  _const_spec((1, D_MODEL))],
        out_specs=[pl.BlockSpec((tm, D_MODEL), lambda s: (prv(s), 0)),
                   st(DK_MLSTM, DV_MLSTM), st(1, DK_MLSTM),
                   pl.BlockSpec((1, N_GATES, chunk), lambda s: (cur(s) // tiles_per_seq, 0, 0))],
        out_shape=[sds((nseq * t, D_MODEL), F32), sds((nseq, N_HEADS_MLSTM, DK_MLSTM, DV_MLSTM), F32),
                   sds((nseq, N_HEADS_MLSTM, 1, DK_MLSTM), F32), sds((nseq, N_GATES, chunk), F32)],
        scratch_shapes=[pltpu.VMEM((2, MLSTM_W, tm), BF16),
                        pltpu.VMEM((N_HEADS_MLSTM, _CN_ROWS, DK_MLSTM), F32),
                        pltpu.VMEM((N_GATES, chunk), F32)],
        compiler_params=pltpu.CompilerParams(dimension_semantics=("arbitrary",), vmem_limit_bytes=VMEM_LIMIT),
        name="mlstm_mlp",
    )(x2d, mix_aT, qmT, km, vmT, omT, gT, g_b, wo_a, wo_m, g_mlp, w_up, w_dn, g_fin)


def kernel(x_prompt, x_sample, cache_k_win, cache_v_win, state_C, state_n, state_m, rel_bias, norm_mix, w_in,
           b_gates, attn_sinks, norm_attn_out, norm_mlstm_out, w_out, norm_mlp, w_up, w_down, norm_final):
    depth = w_in.shape[0]
    assert depth == 1, "single-layer trunk"
    bp, sp = x_prompt.shape[:2]
    bs, ts = x_sample.shape[:2]
    wc = cache_k_win.shape[2]
    l = 0
    assert sp % PROJ_TILE == 0 and sp % MLP_TILE == 0 and sp >= WINDOW and (bs * ts) % SAMPLE_TILE == 0

    wl = w_in[l]
    cols = lambda lo, n: wl[:, lo:lo + n]
    w_n = jnp.concatenate([cols(_O_KV, 2 * KV_W), cols(_O_QKM + QK_M_W, QK_M_W),
                           jnp.pad(cols(_O_G, N_GATES), ((0, 0), (0, LANES - N_GATES)))], axis=1).astype(BF16)
    w_t = jnp.concatenate([cols(_O_QA, ATT_W), cols(_O_KV + KV_W, KV_W), cols(_O_QKM, QK_M_W), cols(_O_VM, MLSTM_W),
                           cols(_O_OM, MLSTM_W), jnp.pad(cols(_O_G, N_GATES), ((0, 0), (0, BF16_ROWS - N_GATES)))],
                          axis=1).T.astype(BF16)
    bg_row = jnp.pad(b_gates[l], (0, LANES - N_GATES)).reshape(1, LANES)
    bg_col = b_gates[l].reshape(N_GATES, 1)
    g_mix = norm_mix[l].reshape(1, D_MODEL)
    g_att = norm_attn_out[l].reshape(1, ATT_W)
    g_m = norm_mlstm_out[l].reshape(1, MLSTM_W)
    g_mlp = norm_mlp[l].reshape(1, D_MODEL)
    g_fin = norm_final.reshape(1, D_MODEL)
    gb_att = jnp.broadcast_to(norm_attn_out[l].reshape(ATT_W, 1), (ATT_W, WINDOW))
    gb_m = jnp.broadcast_to(norm_mlstm_out[l].reshape(MLSTM_W, 1), (MLSTM_W, MLSTM_CHUNK))
    sinks = attn_sinks[l]

    xp = x_prompt.reshape(bp * sp, D_MODEL)
    far_bucket = int(_t5_bucket_np(np.array(WINDOW)))
    sink_far = jnp.stack([sinks, rel_bias[far_bucket]])
    casts = [(w_out[l], 0, ATT_W), (w_out[l], ATT_W, MLSTM_W), (w_up[l], 0, D_MODEL), (w_down[l], 0, D_FF)]
    kvw, km, qmT, vmT, omT, gT, mix_a, wo_a, wo_m, wup, wdn = _proj_attn(
        xp, g_mix, w_n, w_t, bg_col, rel_bias, _prompt_buckets_t(), sink_far, gb_att, casts, sp, tm=PROJ_TILE)
    y_p, c_p, n_p, m_p = _mlstm_mlp(xp, mix_a, qmT, km, vmT, omT, gT, gb_m, wo_a, wo_m, g_mlp, wup, wdn, g_fin,
                                    bp, sp, chunk=MLSTM_CHUNK, nsub=MLP_TILE // MLSTM_CHUNK)
    m_p = m_p[:, _GATE0:_GATE0 + N_HEADS_MLSTM, 0]
    k_win_p = kvw[:, :KV_W].reshape(1, bp, WINDOW, N_KV_HEADS, HEAD_DIM_ATT)
    v_win_p = kvw[:, KV_W:].reshape(1, bp, WINDOW, N_KV_HEADS, HEAD_DIM_ATT)

    assert math.gcd(ts, 64) == ts and bs % SEQ_PER_BLOCK == 0, "sample group: one mLSTM chunk per sequence"
    kpad = ((wc + ts + BF16_ROWS - 1) // BF16_ROWS) * BF16_ROWS
    bias_s = _bias_table(rel_bias, _sample_buckets(ts, wc, kpad))
    bias_s = bias_s.reshape(N_KV_HEADS, GQA_GROUP * ts, kpad)
    sink_rows = jnp.repeat(sinks, ts).reshape(N_KV_HEADS, GQA_GROUP * ts, 1)
    xs = x_sample.reshape(bs * ts, D_MODEL)
    qa, kvf_s, qkm, vm, om, gt = _in_proj_s(xs, g_mix, w_n, w_t, bg_row, tm=SAMPLE_TILE)
    ck = cache_k_win[l].reshape(bs, wc, KV_W)
    cv = cache_v_win[l].reshape(bs, wc, KV_W)
    n0 = state_n[l].reshape(bs, QK_M_W)
    m0 = jnp.pad(state_m[l], ((0, 0), (_GATE0, LANES - _GATE0 - N_HEADS_MLSTM)))
    mix_a_s, kw_s, vw_s, mix_m_s, c_s, n_s, m_seq = _sample_mixers(
        qa, kvf_s, ck, cv, bias_s, sink_rows, g_att, qkm, vm, om, gt, g_m, state_C[l], n0, m0, bs, ts)
    m_s = m_seq[:, _GATE0:_GATE0 + N_HEADS_MLSTM]
    y_s = _out_mlp(xs, mix_a_s, mix_m_s, wo_a, wo_m, g_mlp, wup, wdn, g_fin, tm=SAMPLE_TILE)

    return (y_p.reshape(bp, sp, D_MODEL), y_s.reshape(bs, ts, D_MODEL),
            k_win_p, v_win_p,
            c_p[None], n_p.reshape(1, bp, N_HEADS_MLSTM, DK_MLSTM), m_p.reshape(1, bp, N_HEADS_MLSTM),
            kw_s.reshape(1, bs, wc, N_KV_HEADS, HEAD_DIM_ATT), vw_s.reshape(1, bs, wc, N_KV_HEADS, HEAD_DIM_ATT),
            c_s[None], n_s.reshape(1, bs, N_HEADS_MLSTM, DK_MLSTM), m_s.reshape(1, bs, N_HEADS_MLSTM))
```
